```python
import math
import numpy as np
import jax
import jax.numpy as jnp
from jax import lax

D_MODEL = 1024
BATCH = 16
SEQ = 4096
DEPTH = 4

HEAD_DIM = 64
NSA_Q_HEADS = 8
NSA_KV_HEADS = 2
NSA_GROUP = NSA_Q_HEADS // NSA_KV_HEADS
NSA_WIDTH = NSA_Q_HEADS * HEAD_DIM
CMP_LEN = 32
CMP_STRIDE = 16
CMP_HIDDEN = 256
SEL_LEN = 64
SEL_TOPK = 16
WINDOW = 512
NSA_Q_BLOCK = 64
ROPE_THETA = 10000.0
HALF_DIM = HEAD_DIM // 2
FORCED_SCORE = 1e9
NEG_INF = -1e30
CONV_WIDTH = 256
CONV_LEN = 31
S5_WIDTH = 256
S5_GROUP_CH = 16
S5_GROUPS = S5_WIDTH // S5_GROUP_CH
S5_STATE = 64
DT_MIN = 1e-3
DT_MAX = 1e-1
MIX_WIDTH = NSA_WIDTH + CONV_WIDTH + S5_WIDTH
KV_WIDTH = 6 * NSA_KV_HEADS * HEAD_DIM
GATE_WIDTH = 3 * NSA_Q_HEADS
IN_WIDTH = NSA_WIDTH + KV_WIDTH + GATE_WIDTH + 2 * CONV_WIDTH + S5_WIDTH
SPLIT_POINTS = (NSA_WIDTH,
                NSA_WIDTH + KV_WIDTH,
                NSA_WIDTH + KV_WIDTH + GATE_WIDTH,
                NSA_WIDTH + KV_WIDTH + GATE_WIDTH + 2 * CONV_WIDTH)
D_FF = 2816
N_EXPERTS = 8
TOP_K = 2
D_FF_EXPERT = 3584
N_DENSE = (DEPTH + 1) // 2
N_MOE = DEPTH // 2
PLE_DIM = 256
POS_OFFSET_MAX = 4096
ALPHA = (2 * DEPTH) ** 0.25
BETA = (8 * DEPTH) ** -0.25
LN_EPS = 1e-5

kernel_name = "hybrid_nsa_conformer_s5_moe_deepnorm"


def layer_norm(x, g, b):
    xf = x.astype(jnp.float32)
    mu = xf.mean(-1, keepdims=True)
    var = jnp.square(xf - mu).mean(-1, keepdims=True)
    return ((xf - mu) * lax.rsqrt(var + LN_EPS) * g + b).astype(x.dtype)


def masked_softmax(s, mask):
    s = jnp.where(mask, s, NEG_INF)
    m = s.max(-1, keepdims=True)
    e = jnp.where(mask, jnp.exp(s - m), 0.0)
    return e / jnp.maximum(e.sum(-1, keepdims=True), 1e-30)


def rope_tables(positions):
    inv_freq = ROPE_THETA ** (-jnp.arange(0, HEAD_DIM, 2, dtype=jnp.float32) / HEAD_DIM)
    ang = positions[..., None].astype(jnp.float32) * inv_freq
    return jnp.cos(ang)[:, :, None, :], jnp.sin(ang)[:, :, None, :]


def apply_rope(t, cos, sin):
    tf = t.astype(jnp.float32)
    t1, t2 = tf[..., :HALF_DIM], tf[..., HALF_DIM:]
    return jnp.concatenate([t1 * cos - t2 * sin, t2 * cos + t1 * sin], -1).astype(t.dtype)


def nsa_attention(q, k_cmp, v_cmp, k_sel, v_sel, k_win, v_win, gates,
                  cmp_pe, cmp_w1, cmp_b1, cmp_w2, cmp_b2):
    B, L = q.shape[0], q.shape[1]
    scale = HEAD_DIM ** -0.5
    qg = q.reshape(B, L, NSA_KV_HEADS, NSA_GROUP, HEAD_DIM).transpose(0, 2, 3, 1, 4)
    gg = gates.reshape(B, L, NSA_KV_HEADS, NSA_GROUP, 3).transpose(0, 2, 3, 1, 4)
    to_kv = lambda t: t.transpose(0, 2, 1, 3)

    n_cmp = (L - CMP_LEN) // CMP_STRIDE + 1
    blk_idx = np.arange(n_cmp)[:, None] * CMP_STRIDE + np.arange(CMP_LEN)[None, :]

    def compress(t, j):
        blocks = t[:, :, blk_idx] + cmp_pe[j]
        flat = blocks.reshape(B, NSA_KV_HEADS, n_cmp, CMP_LEN * HEAD_DIM)
        return jax.nn.gelu(flat @ cmp_w1[j] + cmp_b1[j]) @ cmp_w2[j] + cmp_b2[j]

    kc = compress(to_kv(k_cmp), 0)
    vc = compress(to_kv(v_cmp), 1)
    cmp_end = jnp.asarray(np.arange(n_cmp) * CMP_STRIDE + CMP_LEN - 1)

    n_sel = L // SEL_LEN
    topk = min(SEL_TOPK, n_sel)
    c_start = np.arange(n_cmp)[:, None] * CMP_STRIDE
    s_start = np.arange(n_sel)[None, :] * SEL_LEN
    overlap = jnp.asarray(((c_start < s_start + SEL_LEN) & (c_start + CMP_LEN > s_start)).astype(np.float32))
    ks_blocks = to_kv(k_sel).reshape(B, NSA_KV_HEADS, n_sel, SEL_LEN, HEAD_DIM)
    vs_blocks = to_kv(v_sel).reshape(B, NSA_KV_HEADS, n_sel, SEL_LEN, HEAD_DIM)
    gather = jax.vmap(jax.vmap(lambda blk, ix: blk[ix]))
    sel_offsets = jnp.arange(SEL_LEN)
    blk_ids = jnp.arange(n_sel)

    pad = ((0, 0), (0, 0), (WINDOW, 0), (0, 0))
    kw = jnp.pad(to_kv(k_win), pad)
    vw = jnp.pad(to_kv(v_win), pad)

    def block_fn(c):
        q0 = c * NSA_Q_BLOCK
        qc = lax.dynamic_slice_in_dim(qg, q0, NSA_Q_BLOCK, axis=3)
        gc = lax.dynamic_slice_in_dim(gg, q0, NSA_Q_BLOCK, axis=3)
        t = q0 + jnp.arange(NSA_Q_BLOCK)

        s = jnp.einsum('bkgqd,bknd->bkgqn', qc, kc).astype(jnp.float32) * scale
        p_cmp = masked_softmax(s, cmp_end[None, :] <= t[:, None])
        o_cmp = jnp.einsum('bkgqn,bknd->bkgqd', p_cmp, vc)

        imp = jnp.einsum('bkgqn,ns->bkqs', p_cmp, overlap)
        qblk = t // SEL_LEN
        forced = (blk_ids[None, :] == 0) | (blk_ids[None, :] == qblk[:, None]) | (blk_ids[None, :] == qblk[:, None] - 1)
        score = jnp.where(forced, FORCED_SCORE, jnp.where(blk_ids[None, :] <= qblk[:, None], imp, NEG_INF))
        _, idx = lax.top_k(score, topk)
        ks = gather(ks_blocks, idx).reshape(B, NSA_KV_HEADS, NSA_Q_BLOCK, topk * SEL_LEN, HEAD_DIM)
        vs = gather(vs_blocks, idx).reshape(B, NSA_KV_HEADS, NSA_Q_BLOCK, topk * SEL_LEN, HEAD_DIM)
        pos = (idx[..., None] * SEL_LEN + sel_offsets).reshape(B, NSA_KV_HEADS, NSA_Q_BLOCK, topk * SEL_LEN)
        s = jnp.einsum('bkgqd,bkqsd->bkgqs', qc, ks).astype(jnp.float32) * scale
        p_sel = masked_softmax(s, (pos <= t[:, None])[:, :, None])
        o_sel = jnp.einsum('bkgqs,bkqsd->bkgqd', p_sel, vs)

        kwc = lax.dynamic_slice_in_dim(kw, q0, NSA_Q_BLOCK + WINDOW, axis=2)
        vwc = lax.dynamic_slice_in_dim(vw, q0, NSA_Q_BLOCK + WINDOW, axis=2)
        wpos = q0 - WINDOW + jnp.arange(NSA_Q_BLOCK + WINDOW)
        wmask = (wpos[None, :] <= t[:, None]) & (wpos[None, :] > t[:, None] - WINDOW) & (wpos[None, :] >= 0)
        s = jnp.einsum('bkgqd,bksd->bkgqs', qc, kwc).astype(jnp.float32) * scale
        p_win = masked_softmax(s, wmask)
        o_win = jnp.einsum('bkgqs,bksd->bkgqd', p_win, vwc)

        return (gc[..., 0:1] * o_cmp + gc[..., 1:2] * o_sel + gc[..., 2:3] * o_win).astype(q.dtype)

    out = lax.map(block_fn, jnp.arange(L // NSA_Q_BLOCK))
    return out.transpose(1, 0, 4, 2, 3, 5).reshape(B, L, NSA_WIDTH)


def conformer_conv(zc, conv_w, conv_b, ln_g, ln_b):
    u = zc[..., :CONV_WIDTH] * jax.nn.sigmoid(zc[..., CONV_WIDTH:])
    y = lax.conv_general_dilated(u, conv_w[:, None, :].astype(u.dtype), (1,), [(CONV_LEN - 1, 0)],
                                 dimension_numbers=('NWC', 'WIO', 'NWC'),
                                 feature_group_count=CONV_WIDTH) + conv_b
    return jax.nn.silu(layer_norm(y, ln_g, ln_b))


def _complex_linear_combine(e1, e2):
    a1r, a1i, b1r, b1i = e1
    a2r, a2i, b2r, b2i = e2
    return (a2r * a1r - a2i * a1i,
            a2r * a1i + a2i * a1r,
            a2r * b1r - a2i * b1i + b2r,
            a2r * b1i + a2i * b1r + b2i)


def s5_layer(u, a_re, a_im, log_dt, b_re, b_im, c_re, c_im, d_skip, glu_w, glu_b):
    B, L, _ = u.shape
    uf = u.astype(jnp.float32)
    ug = uf.reshape(B, L, S5_GROUPS, S5_GROUP_CH)
    a_re = a_re.astype(jnp.float32)
    a_im = a_im.astype(jnp.float32)
    dt = jnp.exp(log_dt.astype(jnp.float32))[:, None]
    mag = jnp.exp(a_re * dt)
    ab_re = mag * jnp.cos(a_im * dt)
    ab_im = mag * jnp.sin(a_im * dt)
    den = a_re * a_re + a_im * a_im
    coef_re = ((ab_re - 1.0) * a_re + ab_im * a_im) / den
    coef_im = (ab_im * a_re - (ab_re - 1.0) * a_im) / den
    b_re = b_re.astype(jnp.float32)
    b_im = b_im.astype(jnp.float32)
    bb_re = coef_re[..., None] * b_re - coef_im[..., None] * b_im
    bb_im = coef_re[..., None] * b_im + coef_im[..., None] * b_re
    bu_re = jnp.einsum('blgc,gnc->blgn', ug, bb_re)
    bu_im = jnp.einsum('blgc,gnc->blgn', ug, bb_im)
    shape = bu_re.shape
    elems = (jnp.broadcast_to(ab_re, shape), jnp.broadcast_to(ab_im, shape), bu_re, bu_im)
    _, _, h_re, h_im = lax.associative_scan(_complex_linear_combine, elems, axis=1)
    y = (jnp.einsum('blgn,gcn->blgc', h_re, c_re.astype(jnp.float32))
         - jnp.einsum('blgn,gcn->blgc', h_im, c_im.astype(jnp.float32)))
    y = y.reshape(B, L, S5_WIDTH) + d_skip * uf
    z = jax.nn.gelu(y)
    return (z * jax.nn.sigmoid(z @ glu_w + glu_b)).astype(u.dtype)


def hybrid_mixer(x, cos, sin, w_in, w_out, cmp_pe, cmp_w1, cmp_b1, cmp_w2, cmp_b2,
                 conv_w, conv_b, conv_ln_g, conv_ln_b,
                 s5_a_re, s5_a_im, s5_log_dt, s5_b_re, s5_b_im, s5_c_re, s5_c_im,
                 s5_d, s5_glu_w, s5_glu_b):
    B, L, _ = x.shape
    z = x @ w_in
    zq, zkv, zg, zc, zs = jnp.split(z, SPLIT_POINTS, axis=-1)
    q = apply_rope(zq.reshape(B, L, NSA_Q_HEADS, HEAD_DIM), cos, sin)
    kv = zkv.reshape(B, L, 6, NSA_KV_HEADS, HEAD_DIM)
    k_cmp = apply_rope(kv[:, :, 0], cos, sin)
    k_sel = apply_rope(kv[:, :, 2], cos, sin)
    k_win = apply_rope(kv[:, :, 4], cos, sin)
    gates = jax.nn.sigmoid(zg).reshape(B, L, NSA_Q_HEADS, 3)
    o_nsa = nsa_attention(q, k_cmp, kv[:, :, 1], k_sel, kv[:, :, 3], k_win, kv[:, :, 5], gates,
                          cmp_pe, cmp_w1, cmp_b1, cmp_w2, cmp_b2)
    o_conv = conformer_conv(zc, conv_w, conv_b, conv_ln_g, conv_ln_b)
    o_s5 = s5_layer(zs, s5_a_re, s5_a_im, s5_log_dt, s5_b_re, s5_b_im, s5_c_re, s5_c_im,
                    s5_d, s5_glu_w, s5_glu_b)
    return jnp.concatenate([o_nsa, o_conv.astype(o_nsa.dtype), o_s5.astype(o_nsa.dtype)], -1) @ w_out


def swiglu(x, w1, w3, w2):
    return (jax.nn.silu(x @ w1) * (x @ w3)) @ w2


def moe_swiglu(x, router, w1, w3, w2):
    logits = (x @ router).astype(jnp.float32)
    top_vals, top_idx = lax.top_k(logits, TOP_K)
    wts = jax.nn.softmax(top_vals, axis=-1)
    comb = jnp.einsum('blk,blke->ble', wts, jax.nn.one_hot(top_idx, N_EXPERTS, dtype=jnp.float32))
    y = jnp.zeros(x.shape, jnp.float32)
    for e in range(N_EXPERTS):
        y = y + comb[..., e:e + 1] * swiglu(x, w1[e], w3[e], w2[e])
    return y.astype(x.dtype)


def setup_inputs(seed: int = 0) -> dict:
    key = jax.random.key(seed)
    keys = iter(jax.random.split(key, 48))
    f32 = jnp.float32

    def nrm(shape, scale):
        return jax.random.normal(next(keys), shape, f32) * scale

    x = nrm((BATCH, SEQ, D_MODEL), 1.0)
    p = nrm((DEPTH, BATCH, SEQ, PLE_DIM), 1.0)
    offsets = jax.random.randint(next(keys), (BATCH, 1), 0, POS_OFFSET_MAX, dtype=jnp.int32)
    positions = offsets + jnp.arange(SEQ, dtype=jnp.int32)[None, :]
    w_in = nrm((DEPTH, D_MODEL, IN_WIDTH), D_MODEL ** -0.5)
    w_out = nrm((DEPTH, MIX_WIDTH, D_MODEL), MIX_WIDTH ** -0.5 * BETA)
    cmp_pe = nrm((DEPTH, 2, CMP_LEN, HEAD_DIM), 0.1)
    cmp_w1 = nrm((DEPTH, 2, CMP_LEN * HEAD_DIM, CMP_HIDDEN), (CMP_LEN * HEAD_DIM) ** -0.5)
    cmp_b1 = nrm((DEPTH, 2, CMP_HIDDEN), 0.02)
    cmp_w2 = nrm((DEPTH, 2, CMP_HIDDEN, HEAD_DIM), CMP_HIDDEN ** -0.5)
    cmp_b2 = nrm((DEPTH, 2, HEAD_DIM), 0.02)
    conv_w = nrm((DEPTH, CONV_LEN, CONV_WIDTH), CONV_LEN ** -0.5)
    conv_b = nrm((DEPTH, CONV_WIDTH), 0.02)
    conv_ln_g = 1.0 + nrm((DEPTH, CONV_WIDTH), 0.02)
    conv_ln_b = nrm((DEPTH, CONV_WIDTH), 0.02)
    s5_a_re = -0.5 + nrm((DEPTH, S5_GROUPS, S5_STATE), 0.01)
    s5_a_im = math.pi * jnp.arange(S5_STATE, dtype=f32) + nrm((DEPTH, S5_GROUPS, S5_STATE), 0.01)
    s5_log_dt = jax.random.uniform(next(keys), (DEPTH, S5_GROUPS), f32,
                                   math.log(DT_MIN), math.log(DT_MAX))
    s5_b_re = nrm((DEPTH, S5_GROUPS, S5_STATE, S5_GROUP_CH), (2 * S5_GROUP_CH) ** -0.5)
    s5_b_im = nrm((DEPTH, S5_GROUPS, S5_STATE, S5_GROUP_CH), (2 * S5_GROUP_CH) ** -0.5)
    s5_c_re = nrm((DEPTH, S5_GROUPS, S5_GROUP_CH, S5_STATE), S5_STATE ** -0.5)
    s5_c_im = nrm((DEPTH, S5_GROUPS, S5_GROUP_CH, S5_STATE), S5_STATE ** -0.5)
    s5_d = nrm((DEPTH, S5_WIDTH), 1.0)
    s5_glu_w = nrm((DEPTH, S5_WIDTH, S5_WIDTH), S5_WIDTH ** -0.5)
    s5_glu_b = nrm((DEPTH, S5_WIDTH), 0.02)
    ffn_w1 = nrm((N_DENSE, D_MODEL, D_FF), D_MODEL ** -0.5)
    ffn_w3 = nrm((N_DENSE, D_MODEL, D_FF), D_MODEL ** -0.5)
    ffn_w2 = nrm((N_DENSE, D_FF, D_MODEL), D_FF ** -0.5 * BETA)
    moe_router = nrm((N_MOE, D_MODEL, N_EXPERTS), D_MODEL ** -0.5)
    moe_w1 = nrm((N_MOE, N_EXPERTS, D_MODEL, D_FF_EXPERT), D_MODEL ** -0.5)
    moe_w3 = nrm((N_MOE, N_EXPERTS, D_MODEL, D_FF_EXPERT), D_MODEL ** -0.5)
    moe_w2 = nrm((N_MOE, N_EXPERTS, D_FF_EXPERT, D_MODEL), D_FF_EXPERT ** -0.5 * BETA)
    ple_gate_w = nrm((DEPTH, D_MODEL, D_MODEL), D_MODEL ** -0.5)
    ple_gate_b = nrm((DEPTH, D_MODEL), 0.02)
    ple_proj = nrm((DEPTH, PLE_DIM, D_MODEL), PLE_DIM ** -0.5 * BETA)
    ln_g = 1.0 + nrm((DEPTH, 3, D_MODEL), 0.02)
    ln_b = nrm((DEPTH, 3, D_MODEL), 0.02)
    return {"x": x, "p": p, "positions": positions, "w_in": w_in, "w_out": w_out,
            "cmp_pe": cmp_pe, "cmp_w1": cmp_w1, "cmp_b1": cmp_b1, "cmp_w2": cmp_w2, "cmp_b2": cmp_b2,
            "conv_w": conv_w, "conv_b": conv_b, "conv_ln_g": conv_ln_g, "conv_ln_b": conv_ln_b,
            "s5_a_re": s5_a_re, "s5_a_im": s5_a_im, "s5_log_dt": s5_log_dt,
            "s5_b_re": s5_b_re, "s5_b_im": s5_b_im, "s5_c_re": s5_c_re, "s5_c_im": s5_c_im,
            "s5_d": s5_d, "s5_glu_w": s5_glu_w, "s5_glu_b": s5_glu_b,
            "ffn_w1": ffn_w1, "ffn_w3": ffn_w3, "ffn_w2": ffn_w2,
            "moe_router": moe_router, "moe_w1": moe_w1, "moe_w3": moe_w3, "moe_w2": moe_w2,
            "ple_gate_w": ple_gate_w, "ple_gate_b": ple_gate_b, "ple_proj": ple_proj,
            "ln_g": ln_g, "ln_b": ln_b}


def reference(x, p, positions, w_in, w_out, cmp_pe, cmp_w1, cmp_b1, cmp_w2, cmp_b2,
              conv_w, conv_b, conv_ln_g, conv_ln_b,
              s5_a_re, s5_a_im, s5_log_dt, s5_b_re, s5_b_im, s5_c_re, s5_c_im,
              s5_d, s5_glu_w, s5_glu_b,
              ffn_w1, ffn_w3, ffn_w2, moe_router, moe_w1, moe_w3, moe_w2,
              ple_gate_w, ple_gate_b, ple_proj, ln_g, ln_b):
    cos, sin = rope_tables(positions)
    for i in range(DEPTH):
        h = hybrid_mixer(x, cos, sin, w_in[i], w_out[i],
                         cmp_pe[i], cmp_w1[i], cmp_b1[i], cmp_w2[i], cmp_b2[i],
                         conv_w[i], conv_b[i], conv_ln_g[i], conv_ln_b[i],
                         s5_a_re[i], s5_a_im[i], s5_log_dt[i], s5_b_re[i], s5_b_im[i],
                         s5_c_re[i], s5_c_im[i], s5_d[i], s5_glu_w[i], s5_glu_b[i])
        x = layer_norm(ALPHA * x + h.astype(x.dtype), ln_g[i, 0], ln_b[i, 0])
        j = i // 2
        if i % 2 == 0:
            f = swiglu(x, ffn_w1[j], ffn_w3[j], ffn_w2[j])
        else:
            f = moe_swiglu(x, moe_router[j], moe_w1[j], moe_w3[j], moe_w2[j])
        x = layer_norm(ALPHA * x + f.astype(x.dtype), ln_g[i, 1], ln_b[i, 1])
        e = (p[i] @ ple_proj[i]) * jax.nn.sigmoid(x @ ple_gate_w[i] + ple_gate_b[i])
        x = layer_norm(ALPHA * x + e.astype(x.dtype), ln_g[i, 2], ln_b[i, 2])
    return x
```

```python
import functools
import math

import numpy as np
import jax
import jax.numpy as jnp
from jax import lax
from jax.experimental import pallas as pl
from jax.experimental.pallas import tpu as pltpu

F32 = jnp.float32
BF16 = jnp.bfloat16
HIGHEST = lax.Precision.HIGHEST

LANES = 128
VMEM_LIMIT = 56 * 1024 * 1024

HEAD_DIM = 64
HALF_DIM = HEAD_DIM // 2
N_Q_HEADS = 8
N_KV_HEADS = 2
GQA = N_Q_HEADS // N_KV_HEADS
CMP_LEN = 32
CMP_STRIDE = 16
CMP_HIDDEN = 256
SEL_LEN = 64
SEL_TOPK = 16
WINDOW = 512
ROPE_THETA = 10000.0
FORCED_SCORE = 1e9
NEG = -1e30
CONV_WIDTH = 256
CONV_LEN = 31
S5_WIDTH = 256
S5_GROUP_CH = 16
S5_GROUPS = 16
S5_STATE = 64
S5_CHUNK = 16
N_EXPERTS = 8
LN_EPS = 1e-5
NSA_WIDTH = N_Q_HEADS * HEAD_DIM

C_Q = 0
C_KSEL = 1024
C_KWIN = 1280
C_VSEL = 1536
C_VWIN = 1792
C_KCMP = 2048
C_VCMP = 2176
C_GATE = 2304
C_CONVA = 2560
C_CONVB = 2816
C_S5 = 3072
C_TOTAL = 3328


def _cparams(*sem):
    return pltpu.CompilerParams(dimension_semantics=sem, vmem_limit_bytes=VMEM_LIMIT)


def _layer_norm(v, g, b):
    mu = jnp.mean(v, axis=-1, keepdims=True)
    d = v - mu
    var = jnp.mean(d * d, axis=-1, keepdims=True)
    return d * lax.rsqrt(var + LN_EPS) * g + b


def _gelu_tanh(x):
    return 0.5 * x * (1.0 + jnp.tanh(math.sqrt(2.0 / math.pi) * (x + 0.044715 * (x * x * x))))


def _sigmoid(x):
    return 1.0 / (1.0 + jnp.exp(-x))


def _rope_table_kernel(pos_ref, freq_ref, tab_ref):
    ang = pos_ref[...] * freq_ref[...]
    c = jnp.cos(ang)
    s = jnp.sin(ang)
    lane = lax.broadcasted_iota(jnp.int32, ang.shape, 1)
    first_half = (lane % HEAD_DIM) < HALF_DIM
    tab_ref[:, 0:LANES] = c
    tab_ref[:, LANES:2 * LANES] = jnp.where(first_half, -s, 0.0)
    tab_ref[:, 2 * LANES:3 * LANES] = jnp.where(first_half, 0.0, s)


def rope_tables(positions, tm=512):
    t = positions.size
    pos = positions.reshape(t, 1).astype(F32)
    inv_freq = ROPE_THETA ** (-jnp.arange(0, HEAD_DIM, 2, dtype=F32) / HEAD_DIM)
    freq = jnp.tile(inv_freq, LANES // HALF_DIM).reshape(1, LANES)
    return pl.pallas_call(
        _rope_table_kernel,
        out_shape=jax.ShapeDtypeStruct((t, 3 * LANES), F32),
        grid=(t // tm,),
        in_specs=[pl.BlockSpec((tm, 1), lambda i: (i, 0)),
                  pl.BlockSpec((1, LANES), lambda i: (0, 0))],
        out_specs=pl.BlockSpec((tm, 3 * LANES), lambda i: (i, 0)),
        compiler_params=_cparams("parallel"),
        name="rope_tables",
    )(pos, freq)


def _in_proj_kernel(x_ref, w_ref, tab_ref, q_ref, ksel_ref, kwin_ref, vsel_ref, vwin_ref,
                    kcmp_ref, vcmp_ref, gate_ref, u_ref, zs_ref, *, seq_len):
    tm = x_ref.shape[0]
    xb = x_ref[...].astype(BF16)
    cos = tab_ref[:, 0:LANES]
    sin_a = tab_ref[:, LANES:2 * LANES]
    sin_b = tab_ref[:, 2 * LANES:3 * LANES]

    def proj(c0, width):
        return jnp.dot(xb, w_ref[:, c0:c0 + width], preferred_element_type=F32)

    def rope(z):
        return (z * cos + pltpu.roll(z, LANES - HALF_DIM, 1) * sin_a
                + pltpu.roll(z, HALF_DIM, 1) * sin_b)

    lane = lax.broadcasted_iota(jnp.int32, (tm, LANES), 1)
    row = lax.broadcasted_iota(jnp.int32, (tm, LANES), 0)
    t_seq = (pl.program_id(0) * tm) % seq_len + row
    blk_onehot = jnp.where(lane == HEAD_DIM + t_seq // SEL_LEN, 1.0, 0.0)
    ones_lane = jnp.where(lane == HEAD_DIM, 1.0, 0.0)

    scale = HEAD_DIM ** -0.5
    for h in range(N_Q_HEADS):
        z = proj(C_Q + h * LANES, LANES)
        q_ref[:, h * LANES:(h + 1) * LANES] = (rope(z) * scale).astype(BF16)
    for h in range(N_KV_HEADS):
        sl = slice(h * LANES, (h + 1) * LANES)
        ksel_ref[:, sl] = (rope(proj(C_KSEL + h * LANES, LANES)) + blk_onehot).astype(BF16)
        kwin_ref[:, sl] = rope(proj(C_KWIN + h * LANES, LANES)).astype(BF16)
        vsel_ref[:, sl] = (proj(C_VSEL + h * LANES, LANES) + ones_lane).astype(BF16)
        vwin_ref[:, sl] = (proj(C_VWIN + h * LANES, LANES) + ones_lane).astype(BF16)
    kcmp_ref[...] = rope(proj(C_KCMP, LANES))
    vcmp_ref[...] = proj(C_VCMP, LANES)
    gate_ref[...] = _sigmoid(proj(C_GATE, 2 * LANES))
    a = proj(C_CONVA, CONV_WIDTH)
    b = proj(C_CONVB, CONV_WIDTH)
    u_ref[...] = a * _sigmoid(b)
    zs_ref[...] = proj(C_S5, S5_WIDTH)


def _widen_w_in(w_in):
    d = w_in.shape[0]
    o_kv = NSA_WIDTH
    o_gate = o_kv + 6 * N_KV_HEADS * HEAD_DIM
    o_conv = o_gate + 3 * N_Q_HEADS
    o_s5 = o_conv + 2 * CONV_WIDTH
    zpad = jnp.zeros((d, HEAD_DIM), w_in.dtype)
    cols = []
    for h in range(N_Q_HEADS):
        cols += [w_in[:, h * HEAD_DIM:(h + 1) * HEAD_DIM], zpad]

    def kv(j, h):
        c0 = o_kv + (j * N_KV_HEADS + h) * HEAD_DIM
        return w_in[:, c0:c0 + HEAD_DIM]

    for j in (2, 4, 3, 5):
        for h in range(N_KV_HEADS):
            cols += [kv(j, h), zpad]
    cols += [kv(0, 0), kv(0, 1), kv(1, 0), kv(1, 1)]
    per_group = 3 * GQA
    for h in range(N_KV_HEADS):
        cols += [w_in[:, o_gate + h * per_group:o_gate + (h + 1) * per_group],
                 jnp.zeros((d, LANES - per_group), w_in.dtype)]
    cols += [w_in[:, o_conv:o_conv + 2 * CONV_WIDTH], w_in[:, o_s5:o_s5 + S5_WIDTH]]
    w = jnp.concatenate(cols, axis=1)
    assert w.shape[1] == C_TOTAL
    return w.astype(BF16)


def in_proj(x2, w_wide, tables, seq_len, tm=512):
    t, d = x2.shape
    row = lambda width: pl.BlockSpec((tm, width), lambda i: (i, 0))
    out_shape = (
        jax.ShapeDtypeStruct((t, N_Q_HEADS * LANES), BF16),
        jax.ShapeDtypeStruct((t, N_KV_HEADS * LANES), BF16),
        jax.ShapeDtypeStruct((t, N_KV_HEADS * LANES), BF16),
        jax.ShapeDtypeStruct((t, N_KV_HEADS * LANES), BF16),
        jax.ShapeDtypeStruct((t, N_KV_HEADS * LANES), BF16),
        jax.ShapeDtypeStruct((t, LANES), F32),
        jax.ShapeDtypeStruct((t, LANES), F32),
        jax.ShapeDtypeStruct((t, 2 * LANES), F32),
        jax.ShapeDtypeStruct((t, CONV_WIDTH), F32),
        jax.ShapeDtypeStruct((t, S5_WIDTH), F32),
    )
    return pl.pallas_call(
        functools.partial(_in_proj_kernel, seq_len=seq_len),
        out_shape=out_shape,
        grid=(t // tm,),
        in_specs=[row(d),
                  pl.BlockSpec((d, C_TOTAL), lambda i: (0, 0)),
                  row(3 * LANES)],
        out_specs=tuple(row(s.shape[1]) for s in out_shape),
        compiler_params=_cparams("parallel"),
        name="in_proj",
    )(x2, w_wide, tables)


def _compress_kernel(x_ref, pe_ref, w1_ref, b1_ref, w2_ref, b2_ref, o_ref, shift_ref):
    nb = x_ref.shape[2]
    x = x_ref[0, 0]
    xa = (x + pe_ref[0, 0:1, :]).astype(BF16)
    xb = (x + pe_ref[0, 1:2, :]).astype(BF16)
    ha = jnp.dot(xa, w1_ref[0, 0], preferred_element_type=F32)
    hb = jnp.dot(xb, w1_ref[0, 1], preferred_element_type=F32)
    shift_ref[0:nb, :] = hb
    shift_ref[nb:nb + 8, :] = jnp.zeros((8, hb.shape[1]), F32)
    h = ha + shift_ref[1:nb + 1, :] + b1_ref[0]
    g = _gelu_tanh(h).astype(BF16)
    for hd in range(N_KV_HEADS):
        gh = g[:, hd * CMP_HIDDEN:(hd + 1) * CMP_HIDDEN]
        o_ref[0, 0, hd] = (jnp.dot(gh, w2_ref[0], preferred_element_type=F32) + b2_ref[0]).astype(BF16)


def _compress_weights(cmp_pe, cmp_w1, cmp_b1, cmp_w2, cmp_b2):
    half = CMP_LEN // 2
    pe = cmp_pe.reshape(2, 2, half, 1, HEAD_DIM)
    pe = jnp.broadcast_to(pe, (2, 2, half, N_KV_HEADS, HEAD_DIM)).reshape(2, 2, half * N_KV_HEADS * HEAD_DIM)
    w1 = cmp_w1.reshape(2, 2, half, HEAD_DIM, CMP_HIDDEN)
    z = jnp.zeros_like(w1)
    w_h0 = jnp.stack([w1, z], axis=3)
    w_h1 = jnp.stack([z, w1], axis=3)
    w1w = jnp.concatenate([w_h0, w_h1], axis=-1)
    w1w = w1w.reshape(2, 2, half * N_KV_HEADS * HEAD_DIM, N_KV_HEADS * CMP_HIDDEN).astype(BF16)
    b1 = jnp.tile(cmp_b1, (1, N_KV_HEADS)).reshape(2, 1, N_KV_HEADS * CMP_HIDDEN)
    w2 = jnp.pad(cmp_w2, ((0, 0), (0, 0), (0, LANES - HEAD_DIM))).astype(BF16)
    b2 = jnp.pad(cmp_b2, ((0, 0), (0, LANES - HEAD_DIM))).reshape(2, 1, LANES)
    return pe, w1w, b1, w2, b2


def compress(kv_cmp, weights):
    pe, w1w, b1, w2, b2 = weights
    _, bsz, seq_len, _ = kv_cmp.shape
    nb = seq_len // CMP_STRIDE
    cw = CMP_STRIDE * LANES
    x = kv_cmp.reshape(2, bsz, nb, cw)
    hid = N_KV_HEADS * CMP_HIDDEN
    return pl.pallas_call(
        _compress_kernel,
        out_shape=jax.ShapeDtypeStruct((2, bsz, N_KV_HEADS, nb, LANES), BF16),
        grid=(2, bsz),
        in_specs=[pl.BlockSpec((1, 1, nb, cw), lambda j, b: (j, b, 0, 0)),
                  pl.BlockSpec((1, 2, cw), lambda j, b: (j, 0, 0)),
                  pl.BlockSpec((1, 2, cw, hid), lambda j, b: (j, 0, 0, 0)),
                  pl.BlockSpec((1, 1, hid), lambda j, b: (j, 0, 0)),
                  pl.BlockSpec((1, CMP_HIDDEN, LANES), lambda j, b: (j, 0, 0)),
                  pl.BlockSpec((1, 1, LANES), lambda j, b: (j, 0, 0))],
        out_specs=pl.BlockSpec((1, 1, N_KV_HEADS, nb, LANES), lambda j, b: (j, b, 0, 0, 0)),
        scratch_shapes=[pltpu.VMEM((nb + 8, hid), F32)],
        compiler_params=_cparams("parallel", "parallel"),
        name="compress",
    )(x, pe, w1w, b1, w2, b2)


def _nsa_kernel(q_ref, ksel_ref, vsel_ref, kwin_ref, vwin_ref, kc_ref, vc_ref, gate_ref, ov_ref,
                o_ref, m_ref, acc_ref):
    tq = q_ref.shape[1]
    rows = GQA * tq
    i = pl.program_id(2)
    q0 = i * tq
    qt = q_ref[0]
    qs = jnp.concatenate([qt[:, g * LANES:(g + 1) * LANES] for g in range(GQA)], axis=0)
    t_q = q0 + lax.broadcasted_iota(jnp.int32, (tq, 1), 0)
    t_row = jnp.concatenate([t_q] * GQA, axis=0)
    contract_last = (((1,), (1,)), ((), ()))

    ncb = kc_ref.shape[3]
    kc = kc_ref[0, 0, 0]
    vc = vc_ref[0, 0, 0]
    s = lax.dot_general(qs, kc, contract_last, preferred_element_type=F32)
    cmp_end = lax.broadcasted_iota(jnp.int32, (1, ncb), 1) * CMP_STRIDE + (CMP_LEN - 1)
    valid = (cmp_end <= t_row) & (cmp_end < ncb * CMP_STRIDE)
    s = jnp.where(valid, s, NEG)
    mx = jnp.max(s, axis=-1, keepdims=True)
    e = jnp.where(valid, jnp.exp(s - mx), 0.0)
    p_cmp = e / jnp.maximum(jnp.sum(e, axis=-1, keepdims=True), 1e-30)
    o_cmp = jnp.dot(p_cmp.astype(BF16), vc, preferred_element_type=F32)

    p_sum = p_cmp[0:tq]
    for g in range(1, GQA):
        p_sum = p_sum + p_cmp[g * tq:(g + 1) * tq]
    ov = ov_ref[...]
    p1 = p_sum.astype(BF16)
    r1 = p_sum - p1.astype(F32)
    p2 = r1.astype(BF16)
    p3 = (r1 - p2.astype(F32)).astype(BF16)
    imp = (jnp.dot(p1, ov, preferred_element_type=F32) + jnp.dot(p2, ov, preferred_element_type=F32)
           + jnp.dot(p3, ov, preferred_element_type=F32))

    lane = lax.broadcasted_iota(jnp.int32, (tq, LANES), 1)
    blk = lane - HEAD_DIM
    qblk = t_q // SEL_LEN
    is_blk = lane >= HEAD_DIM
    causal_blk = blk <= qblk
    forced = (blk == 0) | (blk == qblk) | (blk == qblk - 1)
    score = jnp.where(forced, FORCED_SCORE, jnp.where(causal_blk, imp, NEG))
    score = jnp.where(is_blk, score, -jnp.inf)
    lane_f = lane.astype(F32)
    chosen = jnp.zeros((tq, LANES), F32)
    for _ in range(SEL_TOPK):
        top = jnp.max(score, axis=-1, keepdims=True)
        first = jnp.min(jnp.where(score == top, lane_f, 1e9), axis=-1, keepdims=True)
        pick = lane_f == first
        chosen = jnp.where(pick, 1.0, chosen)
        score = jnp.where(pick, -jnp.inf, score)
    keep = (chosen > 0.5) & causal_blk
    bias = jnp.where(is_blk & jnp.logical_not(keep), NEG, 0.0).astype(BF16)
    qs_sel = qs + jnp.concatenate([bias] * GQA, axis=0)

    tk = tq
    m_ref[...] = jnp.full(m_ref.shape, NEG, F32)
    acc_ref[...] = jnp.zeros(acc_ref.shape, F32)

    def sel_tile(j, diagonal):
        k0 = pl.multiple_of(j * tk, tk)
        k = ksel_ref[0, pl.ds(k0, tk), :]
        v = vsel_ref[0, pl.ds(k0, tk), :]
        sc = lax.dot_general(qs_sel, k, contract_last, preferred_element_type=F32)
        if diagonal:
            kpos = k0 + lax.broadcasted_iota(jnp.int32, (1, tk), 1)
            sc = jnp.where(kpos <= t_row, sc, NEG)
        m_old = m_ref[...]
        m_new = jnp.maximum(m_old, jnp.max(sc, axis=-1, keepdims=True))
        p = jnp.exp(sc - m_new[:, 0:1])
        acc_ref[...] = (jnp.exp(m_old - m_new) * acc_ref[...]
                        + jnp.dot(p.astype(BF16), v, preferred_element_type=F32))
        m_ref[...] = m_new

    def body(j, carry):
        sel_tile(j, False)
        return carry

    lax.fori_loop(0, i, body, 0)
    sel_tile(i, True)
    acc = acc_ref[...]
    o_sel = acc / acc[:, HEAD_DIM:HEAD_DIM + 1]

    wn = tq + WINDOW
    ks = pl.multiple_of(jnp.maximum(q0 - WINDOW, 0), tq)
    kw = kwin_ref[0, pl.ds(ks, wn), :]
    vw = vwin_ref[0, pl.ds(ks, wn), :]
    sw = lax.dot_general(qs, kw, contract_last, preferred_element_type=F32)
    wpos = ks + lax.broadcasted_iota(jnp.int32, (1, wn), 1)
    wmask = (wpos <= t_row) & (wpos > t_row - WINDOW)
    sw = jnp.where(wmask, sw, NEG)
    pw = jnp.exp(sw - jnp.max(sw, axis=-1, keepdims=True))
    accw = jnp.dot(pw.astype(BF16), vw, preferred_element_type=F32)
    o_win = accw / accw[:, HEAD_DIM:HEAD_DIM + 1]

    gates = gate_ref[0]
    heads = []
    for g in range(GQA):
        rs = slice(g * tq, (g + 1) * tq)
        heads.append(gates[:, 3 * g:3 * g + 1] * o_cmp[rs] + gates[:, 3 * g + 1:3 * g + 2] * o_sel[rs]
                     + gates[:, 3 * g + 2:3 * g + 3] * o_win[rs])
    low = lane < HEAD_DIM
    pair0 = jnp.where(low, heads[0], pltpu.roll(heads[1], HEAD_DIM, 1))
    pair1 = jnp.where(low, heads[2], pltpu.roll(heads[3], HEAD_DIM, 1))
    o_ref[0] = jnp.concatenate([pair0, pair1], axis=1).astype(BF16)


def _overlap_matrix(n_rows):
    n = np.arange(n_rows)[:, None]
    s = np.arange(LANES - HEAD_DIM)[None, :]
    c_start = n * CMP_STRIDE
    s_start = s * SEL_LEN
    ov = (c_start < s_start + SEL_LEN) & (c_start + CMP_LEN > s_start)
    out = np.zeros((n_rows, LANES), np.float32)
    out[:, HEAD_DIM:] = ov
    return jnp.asarray(out, BF16)


def nsa_attention(q, ksel, vsel, kwin, vwin, kvc, gates, tq=256):
    bsz, seq_len, _ = q.shape
    ncb = kvc.shape[3]
    assert seq_len // SEL_LEN <= LANES - HEAD_DIM and seq_len % tq == 0 and seq_len >= tq + WINDOW
    kv_spec = pl.BlockSpec((1, seq_len, LANES), lambda b, h, i: (b, 0, h))
    rows = GQA * tq
    return pl.pallas_call(
        _nsa_kernel,
        out_shape=jax.ShapeDtypeStruct((bsz, seq_len, NSA_WIDTH), BF16),
        grid=(bsz, N_KV_HEADS, seq_len // tq),
        in_specs=[pl.BlockSpec((1, tq, GQA * LANES), lambda b, h, i: (b, i, h)),
                  kv_spec, kv_spec, kv_spec, kv_spec,
                  pl.BlockSpec((1, 1, 1, ncb, LANES), lambda b, h, i: (0, b, h, 0, 0)),
                  pl.BlockSpec((1, 1, 1, ncb, LANES), lambda b, h, i: (1, b, h, 0, 0)),
                  pl.BlockSpec((1, tq, LANES), lambda b, h, i: (b, i, h)),
                  pl.BlockSpec((ncb, LANES), lambda b, h, i: (0, 0))],
        out_specs=pl.BlockSpec((1, tq, GQA * HEAD_DIM), lambda b, h, i: (b, i, h)),
        scratch_shapes=[pltpu.VMEM((rows, LANES), F32), pltpu.VMEM((rows, LANES), F32)],
        compiler_params=_cparams("parallel", "parallel", "arbitrary"),
        name="nsa_attention",
    )(q, ksel, vsel, kwin, vwin, kvc, kvc, gates, _overlap_matrix(ncb))


def _conv_kernel(u_ref, w_ref, cb_ref, g_ref, b_ref, o_ref, pad_ref, *, rows):
    seq_len = u_ref.shape[1]
    halo = 32
    pad_ref[0:halo, :] = jnp.zeros((halo, CONV_WIDTH), F32)
    pad_ref[halo:halo + seq_len, :] = u_ref[0]

    first = halo - CONV_LEN + 1

    def body(c, carry):
        r0 = pl.multiple_of(c * rows, rows)
        win = pad_ref[pl.ds(r0, rows + halo), :]
        acc = jnp.zeros((rows, CONV_WIDTH), F32)
        for sub in range(8):
            shifted = win if sub == 0 else pltpu.roll(win, rows + halo - sub, 0)
            for j in range(CONV_LEN):
                if (first + j) % 8 == sub:
                    a0 = first + j - sub
                    acc = acc + w_ref[j:j + 1, :] * shifted[a0:a0 + rows]
        y = _layer_norm(acc + cb_ref[...], g_ref[...], b_ref[...])
        o_ref[0, pl.ds(r0, rows), :] = (y * _sigmoid(y)).astype(BF16)
        return carry

    lax.fori_loop(0, seq_len // rows, body, 0)


def conformer_conv(u, conv_w, conv_b, ln_g, ln_b, rows=64):
    bsz, seq_len, _ = u.shape
    vec = pl.BlockSpec((1, CONV_WIDTH), lambda b: (0, 0))
    return pl.pallas_call(
        functools.partial(_conv_kernel, rows=rows),
        out_shape=jax.ShapeDtypeStruct((bsz, seq_len, CONV_WIDTH), BF16),
        grid=(bsz,),
        in_specs=[pl.BlockSpec((1, seq_len, CONV_WIDTH), lambda b: (b, 0, 0)),
                  pl.BlockSpec((CONV_LEN, CONV_WIDTH), lambda b: (0, 0)), vec, vec, vec],
        out_specs=pl.BlockSpec((1, seq_len, CONV_WIDTH), lambda b: (b, 0, 0)),
        scratch_shapes=[pltpu.VMEM((seq_len + 32, CONV_WIDTH), F32)],
        compiler_params=_cparams("parallel"),
        name="conformer_conv",
    )(u, conv_w, conv_b.reshape(1, -1), ln_g.reshape(1, -1), ln_b.reshape(1, -1))


def _s5_operators(a_re, a_im, log_dt, b_re, b_im, c_re, c_im, n_chunks):
    tc = S5_CHUNK
    dt = jnp.exp(log_dt.astype(F32))[:, None]
    lr = a_re.astype(F32) * dt
    li = a_im.astype(F32) * dt
    mag = jnp.exp(lr)
    ab_re = mag * jnp.cos(li)
    ab_im = mag * jnp.sin(li)
    den = a_re * a_re + a_im * a_im
    coef_re = ((ab_re - 1.0) * a_re + ab_im * a_im) / den
    coef_im = (ab_im * a_re - (ab_re - 1.0) * a_im) / den
    bb_re = coef_re[..., None] * b_re - coef_im[..., None] * b_im
    bb_im = coef_re[..., None] * b_im + coef_im[..., None] * b_re

    def power(tau):
        tau = jnp.asarray(tau, F32)
        m = jnp.exp(lr[..., None] * tau)
        return m * jnp.cos(li[..., None] * tau), m * jnp.sin(li[..., None] * tau)

    lag_re, lag_im = power(jnp.arange(tc + 1))
    ca_re = c_re[..., None] * lag_re[:, None] - c_im[..., None] * lag_im[:, None]
    ca_im = c_re[..., None] * lag_im[:, None] + c_im[..., None] * lag_re[:, None]
    kern = (jnp.einsum('gcnt,gnd->gtcd', ca_re[..., :tc], bb_re, precision=HIGHEST)
            - jnp.einsum('gcnt,gnd->gtcd', ca_im[..., :tc], bb_im, precision=HIGHEST))
    s_idx = np.arange(tc)[:, None]
    i_idx = np.arange(tc)[None, :]
    lag = np.clip(i_idx - s_idx, 0, tc - 1)
    toep = kern[:, lag]
    toep = jnp.where(jnp.asarray(i_idx >= s_idx)[None, :, :, None, None], toep, 0.0)
    m_op = toep.transpose(0, 1, 4, 2, 3).reshape(S5_GROUPS, tc * S5_GROUP_CH, tc * S5_GROUP_CH)
    rev_re, rev_im = lag_re[..., tc - 1::-1][..., :tc], lag_im[..., tc - 1::-1][..., :tc]
    bop_re = (rev_re[..., None] * bb_re[:, :, None] - rev_im[..., None] * bb_im[:, :, None])
    bop_im = (rev_re[..., None] * bb_im[:, :, None] + rev_im[..., None] * bb_re[:, :, None])
    bop_re = bop_re.transpose(0, 2, 3, 1).reshape(S5_GROUPS, tc * S5_GROUP_CH, S5_STATE)
    bop_im = bop_im.transpose(0, 2, 3, 1).reshape(S5_GROUPS, tc * S5_GROUP_CH, S5_STATE)
    cop_re = ca_re[..., 1:].transpose(0, 2, 3, 1).reshape(S5_GROUPS, S5_STATE, tc * S5_GROUP_CH)
    cop_im = -ca_im[..., 1:].transpose(0, 2, 3, 1).reshape(S5_GROUPS, S5_STATE, tc * S5_GROUP_CH)

    n_pairs = S5_GROUPS // 2
    gw = tc * S5_GROUP_CH

    def pair_blockdiag(x):
        r, c = x.shape[1:]
        x = x.reshape(n_pairs, 2, r, c)
        z = jnp.zeros((n_pairs, r, c), x.dtype)
        top = jnp.concatenate([x[:, 0], z], axis=2)
        bot = jnp.concatenate([z, x[:, 1]], axis=2)
        return jnp.concatenate([top, bot], axis=1)

    m_pair = pair_blockdiag(m_op)
    b_pair = jnp.concatenate([pair_blockdiag(bop_re), pair_blockdiag(bop_im)], axis=2)
    c_pair = jnp.concatenate([pair_blockdiag(cop_re), pair_blockdiag(cop_im)], axis=1)
    levels = max(1, int(math.log2(n_chunks)))
    lv_re, lv_im = power(tc * (2.0 ** jnp.arange(levels)))
    lv_re = lv_re.transpose(0, 2, 1).reshape(n_pairs, 2, levels, S5_STATE)
    lv_im = lv_im.transpose(0, 2, 1).reshape(n_pairs, 2, levels, S5_STATE)
    a_lv = jnp.concatenate([lv_re[:, 0], lv_re[:, 1], lv_im[:, 0], lv_im[:, 1]], axis=-1)
    return m_pair, b_pair, c_pair, a_lv


def _s5_kernel(u_ref, m_ref, b_ref, c_ref, a_ref, y_ref, sre_ref, sim_ref):
    nc = u_ref.shape[1]
    half = 2 * S5_STATE
    u = u_ref[0]
    v = jnp.dot(u, b_ref[0], preferred_element_type=F32, precision=HIGHEST)
    zeros = jnp.zeros((nc, half), F32)
    sre_ref[0:nc, :] = zeros
    sim_ref[0:nc, :] = zeros
    sre_ref[nc:2 * nc, :] = v[:, 0:half]
    sim_ref[nc:2 * nc, :] = v[:, half:2 * half]
    levels = a_ref.shape[1]
    for lv in range(levels):
        d = 1 << lv
        ar = a_ref[0, lv:lv + 1, 0:half]
        ai = a_ref[0, lv:lv + 1, half:2 * half]
        pr = sre_ref[nc - d:2 * nc - d, :]
        pi = sim_ref[nc - d:2 * nc - d, :]
        cr = sre_ref[nc:2 * nc, :]
        ci = sim_ref[nc:2 * nc, :]
        sre_ref[nc:2 * nc, :] = cr + ar * pr - ai * pi
        sim_ref[nc:2 * nc, :] = ci + ar * pi + ai * pr
    prev_re = sre_ref[nc - 1:2 * nc - 1, :]
    prev_im = sim_ref[nc - 1:2 * nc - 1, :]
    y = (jnp.dot(u, m_ref[0], preferred_element_type=F32, precision=HIGHEST)
         + jnp.dot(prev_re, c_ref[0, 0:half, :], preferred_element_type=F32, precision=HIGHEST)
         + jnp.dot(prev_im, c_ref[0, half:2 * half, :], preferred_element_type=F32, precision=HIGHEST))
    y_ref[0] = y


def s5_scan(zs, operators):
    m_pair, b_pair, c_pair, a_lv = operators
    bsz, seq_len, _ = zs.shape
    tc = S5_CHUNK
    nc = seq_len // tc
    assert nc & (nc - 1) == 0 and a_lv.shape[1] == int(math.log2(nc))
    n_pairs = S5_GROUPS // 2
    pw = 2 * tc * S5_GROUP_CH
    u = zs.reshape(bsz, nc, tc, S5_GROUPS, S5_GROUP_CH).transpose(0, 1, 3, 2, 4).reshape(bsz, nc, n_pairs * pw)
    y = pl.pallas_call(
        _s5_kernel,
        out_shape=jax.ShapeDtypeStruct((bsz, nc, n_pairs * pw), F32),
        grid=(bsz, n_pairs),
        in_specs=[pl.BlockSpec((1, nc, pw), lambda b, p: (b, 0, p)),
                  pl.BlockSpec((1, pw, pw), lambda b, p: (p, 0, 0)),
                  pl.BlockSpec((1, pw, 4 * S5_STATE), lambda b, p: (p, 0, 0)),
                  pl.BlockSpec((1, 4 * S5_STATE, pw), lambda b, p: (p, 0, 0)),
                  pl.BlockSpec((1, a_lv.shape[1], 4 * S5_STATE), lambda b, p: (p, 0, 0))],
        out_specs=pl.BlockSpec((1, nc, pw), lambda b, p: (b, 0, p)),
        scratch_shapes=[pltpu.VMEM((2 * nc, 2 * S5_STATE), F32), pltpu.VMEM((2 * nc, 2 * S5_STATE), F32)],
        compiler_params=_cparams("parallel", "parallel"),
        name="s5_scan",
    )(u, m_pair, b_pair, c_pair, a_lv)
    return y.reshape(bsz, nc, S5_GROUPS, tc, S5_GROUP_CH).transpose(0, 1, 3, 2, 4).reshape(bsz, seq_len, S5_WIDTH)


def _s5_tail_kernel(y_ref, u_ref, d_ref, w_ref, b_ref, o_ref):
    z = _gelu_tanh(y_ref[...] + d_ref[...] * u_ref[...])
    gate = jnp.dot(z.astype(BF16), w_ref[...], preferred_element_type=F32) + b_ref[...]
    o_ref[...] = (z * _sigmoid(gate)).astype(BF16)


def s5_tail(y2, u2, d_skip, glu_w, glu_b, tm=1024):
    t = y2.shape[0]
    row = pl.BlockSpec((tm, S5_WIDTH), lambda i: (i, 0))
    vec = pl.BlockSpec((1, S5_WIDTH), lambda i: (0, 0))
    return pl.pallas_call(
        _s5_tail_kernel,
        out_shape=jax.ShapeDtypeStruct((t, S5_WIDTH), BF16),
        grid=(t // tm,),
        in_specs=[row, row, vec, pl.BlockSpec((S5_WIDTH, S5_WIDTH), lambda i: (0, 0)), vec],
        out_specs=row,
        compiler_params=_cparams("parallel"),
        name="s5_tail",
    )(y2, u2, d_skip.reshape(1, -1), glu_w.astype(BF16), glu_b.reshape(1, -1))


def _out_proj_kernel(x_ref, a_ref, c_ref, s_ref, w_ref, g_ref, b_ref, o_ref, *, alpha):
    h = jnp.dot(a_ref[...], w_ref[0:NSA_WIDTH, :], preferred_element_type=F32)
    h = h + jnp.dot(c_ref[...], w_ref[NSA_WIDTH:NSA_WIDTH + CONV_WIDTH, :], preferred_element_type=F32)
    h = h + jnp.dot(s_ref[...], w_ref[NSA_WIDTH + CONV_WIDTH:, :], preferred_element_type=F32)
    o_ref[...] = _layer_norm(alpha * x_ref[...] + h, g_ref[...], b_ref[...])


def out_proj_ln(x2, o_nsa, o_conv, o_s5, w_out, g, b, alpha, tm=512):
    t, d = x2.shape
    row = lambda width: pl.BlockSpec((tm, width), lambda i: (i, 0))
    vec = pl.BlockSpec((1, d), lambda i: (0, 0))
    return pl.pallas_call(
        functools.partial(_out_proj_kernel, alpha=alpha),
        out_shape=jax.ShapeDtypeStruct((t, d), F32),
        grid=(t // tm,),
        in_specs=[row(d), row(NSA_WIDTH), row(CONV_WIDTH), row(S5_WIDTH),
                  pl.BlockSpec(w_out.shape, lambda i: (0, 0)), vec, vec],
        out_specs=row(d),
        compiler_params=_cparams("parallel"),
        name="out_proj_ln",
    )(x2, o_nsa, o_conv, o_s5, w_out.astype(BF16), g.reshape(1, -1), b.reshape(1, -1))


def _ffn_kernel(te_ref, nu_ref, x_ref, w1_ref, w3_ref, w2_ref, o_ref, acc_ref):
    i = pl.program_id(0)
    f = pl.program_id(1)
    last = pl.num_programs(1) - 1
    used = i < nu_ref[0]

    @pl.when(used)
    def _():
        xb = x_ref[...].astype(BF16)
        h1 = jnp.dot(xb, w1_ref[0], preferred_element_type=F32)
        h3 = jnp.dot(xb, w3_ref[0], preferred_element_type=F32)
        h = (h1 * _sigmoid(h1) * h3).astype(BF16)
        part = jnp.dot(h, w2_ref[0], preferred_element_type=F32)

        @pl.when(f == 0)
        def _():
            acc_ref[...] = part

        @pl.when(f > 0)
        def _():
            acc_ref[...] = acc_ref[...] + part

        @pl.when(f == last)
        def _():
            o_ref[...] = acc_ref[...]

    @pl.when(jnp.logical_not(used) & (f == last))
    def _():
        o_ref[...] = jnp.zeros(o_ref.shape, o_ref.dtype)


def grouped_swiglu(xs, tile_expert, n_used, w1, w3, w2, tm, tf):
    p, d = xs.shape
    ff = w1.shape[2]
    assert p % tm == 0 and ff % tf == 0
    grid_spec = pltpu.PrefetchScalarGridSpec(
        num_scalar_prefetch=2,
        grid=(p // tm, ff // tf),
        in_specs=[pl.BlockSpec((tm, d), lambda i, f, te, nu: (i, 0)),
                  pl.BlockSpec((1, d, tf), lambda i, f, te, nu: (te[i], 0, f)),
                  pl.BlockSpec((1, d, tf), lambda i, f, te, nu: (te[i], 0, f)),
                  pl.BlockSpec((1, tf, d), lambda i, f, te, nu: (te[i], f, 0))],
        out_specs=pl.BlockSpec((tm, d), lambda i, f, te, nu: (i, 0)),
        scratch_shapes=[pltpu.VMEM((tm, d), F32)],
    )
    return pl.pallas_call(
        _ffn_kernel,
        out_shape=jax.ShapeDtypeStruct((p, d), F32),
        grid_spec=grid_spec,
        compiler_params=_cparams("arbitrary", "arbitrary"),
        name="grouped_swiglu",
    )(tile_expert, n_used, xs, w1, w3, w2)


def _pick_tf(ff, target=1024):
    best = LANES
    for cand in range(LANES, ff + 1, LANES):
        if ff % cand == 0 and cand <= target:
            best = cand
    return best


def _residual_ln_kernel(x_ref, a_ref, b_ref, wa_ref, wb_ref, g_ref, beta_ref, o_ref, *, alpha):
    f = wa_ref[...] * a_ref[...] + wb_ref[...] * b_ref[...]
    o_ref[...] = _layer_norm(alpha * x_ref[...] + f, g_ref[...], beta_ref[...])


def residual_ln(x2, a, b, wa, wb, g, beta, alpha, tm=512):
    t, d = x2.shape
    row = pl.BlockSpec((tm, d), lambda i: (i, 0))
    col = pl.BlockSpec((tm, 1), lambda i: (i, 0))
    vec = pl.BlockSpec((1, d), lambda i: (0, 0))
    return pl.pallas_call(
        functools.partial(_residual_ln_kernel, alpha=alpha),
        out_shape=jax.ShapeDtypeStruct((t, d), F32),
        grid=(t // tm,),
        in_specs=[row, row, row, col, col, vec, vec],
        out_specs=row,
        compiler_params=_cparams("parallel"),
        name="residual_ln",
    )(x2, a, b, wa, wb, g.reshape(1, -1), beta.reshape(1, -1))


def _residual_ln1_kernel(x_ref, a_ref, g_ref, beta_ref, o_ref, *, alpha):
    o_ref[...] = _layer_norm(alpha * x_ref[...] + a_ref[...], g_ref[...], beta_ref[...])


def residual_ln1(x2, a, g, beta, alpha, tm=512):
    t, d = x2.shape
    row = pl.BlockSpec((tm, d), lambda i: (i, 0))
    vec = pl.BlockSpec((1, d), lambda i: (0, 0))
    return pl.pallas_call(
        functools.partial(_residual_ln1_kernel, alpha=alpha),
        out_shape=jax.ShapeDtypeStruct((t, d), F32),
        grid=(t // tm,),
        in_specs=[row, row, vec, vec],
        out_specs=row,
        compiler_params=_cparams("parallel"),
        name="residual_ln1",
    )(x2, a, g.reshape(1, -1), beta.reshape(1, -1))


def _router_kernel(x_ref, w_ref, o_ref):
    logits = jnp.dot(x_ref[...], w_ref[...], preferred_element_type=F32, precision=HIGHEST)
    lane = lax.broadcasted_iota(jnp.int32, logits.shape, 1)
    lane_f = lane.astype(F32)
    logits = jnp.where(lane < N_EXPERTS, logits, -jnp.inf)
    v1 = jnp.max(logits, axis=-1, keepdims=True)
    i1 = jnp.min(jnp.where(logits == v1, lane_f, 1e9), axis=-1, keepdims=True)
    rest = jnp.where(lane_f == i1, -jnp.inf, logits)
    v2 = jnp.max(rest, axis=-1, keepdims=True)
    i2 = jnp.min(jnp.where(rest == v2, lane_f, 1e9), axis=-1, keepdims=True)
    e2 = jnp.exp(v2 - v1)
    den = 1.0 + e2
    out = jnp.where(lane == 0, i1, jnp.where(lane == 1, i2, jnp.where(lane == 2, 1.0 / den, e2 / den)))
    o_ref[...] = jnp.where(lane < 4, out, 0.0)


def moe_route(x2, router, tm=512):
    t, d = x2.shape
    w = jnp.pad(router, ((0, 0), (0, LANES - router.shape[1])))
    return pl.pallas_call(
        _router_kernel,
        out_shape=jax.ShapeDtypeStruct((t, LANES), F32),
        grid=(t // tm,),
        in_specs=[pl.BlockSpec((tm, d), lambda i: (i, 0)), pl.BlockSpec((d, LANES), lambda i: (0, 0))],
        out_specs=pl.BlockSpec((tm, LANES), lambda i: (i, 0)),
        compiler_params=_cparams("parallel"),
        name="moe_route",
    )(x2, w)


def moe_swiglu_ln(x2, router, w1, w3, w2, g, beta, alpha, tm):
    t, d = x2.shape
    n_exp = w1.shape[0]
    routed = moe_route(x2, router)
    idx = routed[:, 0:2].astype(jnp.int32)
    wts = routed[:, 2:4]
    e_flat = idx.T.reshape(-1)
    order = jnp.argsort(e_flat, stable=True)
    sorted_e = e_flat[order]
    counts = jnp.zeros((n_exp,), jnp.int32).at[e_flat].add(1)
    padded = ((counts + tm - 1) // tm) * tm
    pad_start = jnp.cumsum(padded) - padded
    start = jnp.cumsum(counts) - counts
    n_rows = 2 * t + n_exp * tm
    dest = pad_start[sorted_e] + (jnp.arange(2 * t, dtype=jnp.int32) - start[sorted_e])
    tok = order % t
    src = jnp.zeros((n_rows,), jnp.int32).at[dest].set(tok)
    pos = jnp.zeros((2 * t,), jnp.int32).at[order].set(dest)
    n_tiles = n_rows // tm
    tile_start = jnp.arange(n_tiles, dtype=jnp.int32) * tm
    pad_end = jnp.cumsum(padded)
    tile_expert = jnp.minimum(jnp.sum(tile_start[:, None] >= pad_end[None, :], axis=1), n_exp - 1).astype(jnp.int32)
    n_used = (pad_end[-1] // tm).astype(jnp.int32).reshape(1)
    xs = x2.astype(BF16)[src]
    ys = grouped_swiglu(xs, tile_expert, n_used, w1, w3, w2, tm, _pick_tf(w1.shape[2]))
    return residual_ln(x2, ys[pos[:t]], ys[pos[t:]], wts[:, 0:1], wts[:, 1:2], g, beta, alpha)


def dense_swiglu_ln(x2, w1, w3, w2, g, beta, alpha, tm):
    t = x2.shape[0]
    tile_expert = jnp.zeros((t // tm,), jnp.int32)
    n_used = jnp.full((1,), t // tm, jnp.int32)
    ys = grouped_swiglu(x2, tile_expert, n_used, w1[None], w3[None], w2[None], tm, _pick_tf(w1.shape[1], 1408))
    return residual_ln1(x2, ys, g, beta, alpha)


def _ple_kernel(x_ref, p_ref, wg_ref, bg_ref, wp_ref, g_ref, beta_ref, o_ref, *, alpha):
    x = x_ref[...]
    gate = _sigmoid(jnp.dot(x.astype(BF16), wg_ref[...], preferred_element_type=F32) + bg_ref[...])
    e = jnp.dot(p_ref[...].astype(BF16), wp_ref[...], preferred_element_type=F32) * gate
    o_ref[...] = _layer_norm(alpha * x + e, g_ref[...], beta_ref[...])


def ple_ln(x2, p2, gate_w, gate_b, proj, g, beta, alpha, tm=512):
    t, d = x2.shape
    pd = p2.shape[1]
    vec = pl.BlockSpec((1, d), lambda i: (0, 0))
    return pl.pallas_call(
        functools.partial(_ple_kernel, alpha=alpha),
        out_shape=jax.ShapeDtypeStruct((t, d), F32),
        grid=(t // tm,),
        in_specs=[pl.BlockSpec((tm, d), lambda i: (i, 0)), pl.BlockSpec((tm, pd), lambda i: (i, 0)),
                  pl.BlockSpec((d, d), lambda i: (0, 0)), vec,
                  pl.BlockSpec((pd, d), lambda i: (0, 0)), vec, vec],
        out_specs=pl.BlockSpec((tm, d), lambda i: (i, 0)),
        compiler_params=_cparams("parallel"),
        name="ple_ln",
    )(x2, p2, gate_w.astype(BF16), gate_b.reshape(1, -1), proj.astype(BF16), g.reshape(1, -1), beta.reshape(1, -1))


def hybrid_mixer_ln(x2, bsz, seq_len, tables, w_in, w_out, cmp_weights, conv_params, s5_params,
                    g, beta, alpha):
    (q, ksel, kwin, vsel, vwin, kcmp, vcmp, gates, u_conv, zs) = in_proj(x2, _widen_w_in(w_in), tables, seq_len)
    b3 = lambda a: a.reshape(bsz, seq_len, a.shape[1])
    kv_cmp = jnp.stack([b3(kcmp), b3(vcmp)])
    kvc = compress(kv_cmp, _compress_weights(*cmp_weights))
    o_nsa = nsa_attention(b3(q), b3(ksel), b3(vsel), b3(kwin), b3(vwin), kvc, b3(gates))
    o_conv = conformer_conv(b3(u_conv), *conv_params)
    (a_re, a_im, log_dt, b_re, b_im, c_re, c_im, d_skip, glu_w, glu_b) = s5_params
    ops = _s5_operators(a_re, a_im, log_dt, b_re, b_im, c_re, c_im, seq_len // S5_CHUNK)
    y_s5 = s5_scan(b3(zs), ops)
    o_s5 = s5_tail(y_s5.reshape(-1, S5_WIDTH), zs, d_skip, glu_w, glu_b)
    return out_proj_ln(x2, o_nsa.reshape(-1, NSA_WIDTH), o_conv.reshape(-1, CONV_WIDTH), o_s5, w_out,
                       g, beta, alpha)


def kernel(x, p, positions, w_in, w_out, cmp_pe, cmp_w1, cmp_b1, cmp_w2, cmp_b2, conv_w, conv_b, conv_ln_g, conv_ln_b, s5_a_re, s5_a_im, s5_log_dt, s5_b_re, s5_b_im, s5_c_re, s5_c_im, s5_d, s5_glu_w, s5_glu_b, ffn_w1, ffn_w3, ffn_w2, moe_router, moe_w1, moe_w3, moe_w2, ple_gate_w, ple_gate_b, ple_proj, ln_g, ln_b):
    bsz, seq_len, d_model = x.shape
    depth = w_in.shape[0]
    alpha = (2 * depth) ** 0.25
    t = bsz * seq_len
    tables = rope_tables(positions)
    x2 = x.reshape(t, d_model)
    dense_tm = 512
    moe_tm = 512 if t % 1024 else 1024
    for i in range(depth):
        x2 = hybrid_mixer_ln(
            x2, bsz, seq_len, tables, w_in[i], w_out[i],
            (cmp_pe[i], cmp_w1[i], cmp_b1[i], cmp_w2[i], cmp_b2[i]),
            (conv_w[i], conv_b[i], conv_ln_g[i], conv_ln_b[i]),
            (s5_a_re[i], s5_a_im[i], s5_log_dt[i], s5_b_re[i], s5_b_im[i], s5_c_re[i], s5_c_im[i],
             s5_d[i], s5_glu_w[i], s5_glu_b[i]),
            ln_g[i, 0], ln_b[i, 0], alpha)
        j = i // 2
        if i % 2 == 0:
            x2 = dense_swiglu_ln(x2, ffn_w1[j].astype(BF16), ffn_w3[j].astype(BF16), ffn_w2[j].astype(BF16),
                                 ln_g[i, 1], ln_b[i, 1], alpha, dense_tm)
        else:
            x2 = moe_swiglu_ln(x2, moe_router[j], moe_w1[j].astype(BF16), moe_w3[j].astype(BF16),
                               moe_w2[j].astype(BF16), ln_g[i, 1], ln_b[i, 1], alpha, moe_tm)
        x2 = ple_ln(x2, p[i].reshape(t, -1), ple_gate_w[i], ple_gate_b[i], ple_proj[i],
                    ln_g[i, 2], ln_b[i, 2], alpha)
    return x2.reshape(bsz, seq_len, d_model)
```

```python
import functools
import math

import numpy as np
import jax
import jax.numpy as jnp
from jax import lax
from jax.experimental import pallas as pl
from jax.experimental.pallas import tpu as pltpu

F32 = jnp.float32
BF16 = jnp.bfloat16
HIGHEST = lax.Precision.HIGHEST

LANES = 128
VMEM_LIMIT = 56 * 1024 * 1024

HEAD_DIM = 64
HALF_DIM = HEAD_DIM // 2
N_Q_HEADS = 8
N_KV_HEADS = 2
GQA = N_Q_HEADS // N_KV_HEADS
CMP_LEN = 32
CMP_STRIDE = 16
CMP_HIDDEN = 256
SEL_LEN = 64
SEL_TOPK = 16
WINDOW = 512
ROPE_THETA = 10000.0
FORCED_SCORE = 1e9
NEG = -1e30
CONV_WIDTH = 256
CONV_LEN = 31
S5_WIDTH = 256
S5_GROUP_CH = 16
S5_GROUPS = 16
S5_STATE = 64
S5_CHUNK = 16
N_EXPERTS = 8
LN_EPS = 1e-5
NSA_WIDTH = N_Q_HEADS * HEAD_DIM

C_Q = 0
C_KSEL = 1024
C_KWIN = 1280
C_VSEL = 1536
C_VWIN = 1792
C_KCMP = 2048
C_VCMP = 2176
C_GATE = 2304
C_CONVA = 2560
C_CONVB = 2816
C_S5 = 3072
C_TOTAL = 3328


def _cparams(*sem):
    return pltpu.CompilerParams(dimension_semantics=sem, vmem_limit_bytes=VMEM_LIMIT)


def _layer_norm(v, g, b):
    mu = jnp.mean(v, axis=-1, keepdims=True)
    d = v - mu
    var = jnp.mean(d * d, axis=-1, keepdims=True)
    return d * lax.rsqrt(var + LN_EPS) * g + b


def _gelu_tanh(x):
    return 0.5 * x * (1.0 + jnp.tanh(math.sqrt(2.0 / math.pi) * (x + 0.044715 * (x * x * x))))


def _sigmoid(x):
    return 1.0 / (1.0 + jnp.exp(-x))


def _rope_table_kernel(pos_ref, freq_ref, tab_ref):
    ang = pos_ref[...] * freq_ref[...]
    c = jnp.cos(ang)
    s = jnp.sin(ang)
    lane = lax.broadcasted_iota(jnp.int32, ang.shape, 1)
    first_half = (lane % HEAD_DIM) < HALF_DIM
    tab_ref[:, 0:LANES] = c
    tab_ref[:, LANES:2 * LANES] = jnp.where(first_half, -s, 0.0)
    tab_ref[:, 2 * LANES:3 * LANES] = jnp.where(first_half, 0.0, s)


def rope_tables(positions, tm=512):
    t = positions.size
    pos = positions.reshape(t, 1).astype(F32)
    inv_freq = ROPE_THETA ** (-jnp.arange(0, HEAD_DIM, 2, dtype=F32) / HEAD_DIM)
    freq = jnp.tile(inv_freq, LANES // HALF_DIM).reshape(1, LANES)
    return pl.pallas_call(
        _rope_table_kernel,
        out_shape=jax.ShapeDtypeStruct((t, 3 * LANES), F32),
        grid=(t // tm,),
        in_specs=[pl.BlockSpec((tm, 1), lambda i: (i, 0)),
                  pl.BlockSpec((1, LANES), lambda i: (0, 0))],
        out_specs=pl.BlockSpec((tm, 3 * LANES), lambda i: (i, 0)),
        compiler_params=_cparams("parallel"),
        name="rope_tables",
    )(pos, freq)


def _in_proj_kernel(x_ref, w_ref, tab_ref, q_ref, ksel_ref, kwin_ref, vsel_ref, vwin_ref,
                    kcmp_ref, vcmp_ref, gate_ref, u_ref, zs_ref, *, seq_len):
    tm = x_ref.shape[0]
    xb = x_ref[...].astype(BF16)
    cos = tab_ref[:, 0:LANES]
    sin_a = tab_ref[:, LANES:2 * LANES]
    sin_b = tab_ref[:, 2 * LANES:3 * LANES]

    def proj(c0, width):
        return jnp.dot(xb, w_ref[:, c0:c0 + width], preferred_element_type=F32)

    def rope(z):
        return (z * cos + pltpu.roll(z, LANES - HALF_DIM, 1) * sin_a
                + pltpu.roll(z, HALF_DIM, 1) * sin_b)

    lane = lax.broadcasted_iota(jnp.int32, (tm, LANES), 1)
    row = lax.broadcasted_iota(jnp.int32, (tm, LANES), 0)
    t_seq = (pl.program_id(0) * tm) % seq_len + row
    blk_onehot = jnp.where(lane == HEAD_DIM + t_seq // SEL_LEN, 1.0, 0.0)
    ones_lane = jnp.where(lane == HEAD_DIM, 1.0, 0.0)

    scale = HEAD_DIM ** -0.5
    for h in range(N_Q_HEADS):
        z = proj(C_Q + h * LANES, LANES)
        q_ref[:, h * LANES:(h + 1) * LANES] = (rope(z) * scale).astype(BF16)
    for h in range(N_KV_HEADS):
        sl = slice(h * LANES, (h + 1) * LANES)
        ksel_ref[:, sl] = (rope(proj(C_KSEL + h * LANES, LANES)) + blk_onehot).astype(BF16)
        kwin_ref[:, sl] = rope(proj(C_KWIN + h * LANES, LANES)).astype(BF16)
        vsel_ref[:, sl] = (proj(C_VSEL + h * LANES, LANES) + ones_lane).astype(BF16)
        vwin_ref[:, sl] = (proj(C_VWIN + h * LANES, LANES) + ones_lane).astype(BF16)
    kcmp_ref[...] = rope(proj(C_KCMP, LANES))
    vcmp_ref[...] = proj(C_VCMP, LANES)
    gate_ref[...] = _sigmoid(proj(C_GATE, 2 * LANES))
    a = proj(C_CONVA, CONV_WIDTH)
    b = proj(C_CONVB, CONV_WIDTH)
    u_ref[...] = a * _sigmoid(b)
    zs_ref[...] = proj(C_S5, S5_WIDTH)


def _widen_w_in(w_in):
    d = w_in.shape[0]
    o_kv = NSA_WIDTH
    o_gate = o_kv + 6 * N_KV_HEADS * HEAD_DIM
    o_conv = o_gate + 3 * N_Q_HEADS
    o_s5 = o_conv + 2 * CONV_WIDTH
    zpad = jnp.zeros((d, HEAD_DIM), w_in.dtype)
    cols = []
    for h in range(N_Q_HEADS):
        cols += [w_in[:, h * HEAD_DIM:(h + 1) * HEAD_DIM], zpad]

    def kv(j, h):
        c0 = o_kv + (j * N_KV_HEADS + h) * HEAD_DIM
        return w_in[:, c0:c0 + HEAD_DIM]

    for j in (2, 4, 3, 5):
        for h in range(N_KV_HEADS):
            cols += [kv(j, h), zpad]
    cols += [kv(0, 0), kv(0, 1), kv(1, 0), kv(1, 1)]
    per_group = 3 * GQA
    for h in range(N_KV_HEADS):
        cols += [w_in[:, o_gate + h * per_group:o_gate + (h + 1) * per_group],
                 jnp.zeros((d, LANES - per_group), w_in.dtype)]
    cols += [w_in[:, o_conv:o_conv + 2 * CONV_WIDTH], w_in[:, o_s5:o_s5 + S5_WIDTH]]
    w = jnp.concatenate(cols, axis=1)
    assert w.shape[1] == C_TOTAL
    return w.astype(BF16)


def in_proj(x2, w_wide, tables, seq_len, tm=512):
    t, d = x2.shape
    row = lambda width: pl.BlockSpec((tm, width), lambda i: (i, 0))
    out_shape = (
        jax.ShapeDtypeStruct((t, N_Q_HEADS * LANES), BF16),
        jax.ShapeDtypeStruct((t, N_KV_HEADS * LANES), BF16),
        jax.ShapeDtypeStruct((t, N_KV_HEADS * LANES), BF16),
        jax.ShapeDtypeStruct((t, N_KV_HEADS * LANES), BF16),
        jax.ShapeDtypeStruct((t, N_KV_HEADS * LANES), BF16),
        jax.ShapeDtypeStruct((t, LANES), F32),
        jax.ShapeDtypeStruct((t, LANES), F32),
        jax.ShapeDtypeStruct((t, 2 * LANES), F32),
        jax.ShapeDtypeStruct((t, CONV_WIDTH), F32),
        jax.ShapeDtypeStruct((t, S5_WIDTH), F32),
    )
    return pl.pallas_call(
        functools.partial(_in_proj_kernel, seq_len=seq_len),
        out_shape=out_shape,
        grid=(t // tm,),
        in_specs=[row(d),
                  pl.BlockSpec((d, C_TOTAL), lambda i: (0, 0)),
                  row(3 * LANES)],
        out_specs=tuple(row(s.shape[1]) for s in out_shape),
        compiler_params=_cparams("parallel"),
        name="in_proj",
    )(x2, w_wide, tables)


def _compress_kernel(x_ref, pe_ref, w1_ref, b1_ref, w2_ref, b2_ref, o_ref, shift_ref):
    nb = x_ref.shape[2]
    x = x_ref[0, 0]
    xa = (x + pe_ref[0, 0:1, :]).astype(BF16)
    xb = (x + pe_ref[0, 1:2, :]).astype(BF16)
    ha = jnp.dot(xa, w1_ref[0, 0], preferred_element_type=F32)
    hb = jnp.dot(xb, w1_ref[0, 1], preferred_element_type=F32)
    shift_ref[0:nb, :] = hb
    shift_ref[nb:nb + 8, :] = jnp.zeros((8, hb.shape[1]), F32)
    h = ha + shift_ref[1:nb + 1, :] + b1_ref[0]
    g = _gelu_tanh(h).astype(BF16)
    for hd in range(N_KV_HEADS):
        gh = g[:, hd * CMP_HIDDEN:(hd + 1) * CMP_HIDDEN]
        o_ref[0, 0, hd] = (jnp.dot(gh, w2_ref[0], preferred_element_type=F32) + b2_ref[0]).astype(BF16)


def _compress_weights(cmp_pe, cmp_w1, cmp_b1, cmp_w2, cmp_b2):
    half = CMP_LEN // 2
    pe = cmp_pe.reshape(2, 2, half, 1, HEAD_DIM)
    pe = jnp.broadcast_to(pe, (2, 2, half, N_KV_HEADS, HEAD_DIM)).reshape(2, 2, half * N_KV_HEADS * HEAD_DIM)
    w1 = cmp_w1.reshape(2, 2, half, HEAD_DIM, CMP_HIDDEN)
    z = jnp.zeros_like(w1)
    w_h0 = jnp.stack([w1, z], axis=3)
    w_h1 = jnp.stack([z, w1], axis=3)
    w1w = jnp.concatenate([w_h0, w_h1], axis=-1)
    w1w = w1w.reshape(2, 2, half * N_KV_HEADS * HEAD_DIM, N_KV_HEADS * CMP_HIDDEN).astype(BF16)
    b1 = jnp.tile(cmp_b1, (1, N_KV_HEADS)).reshape(2, 1, N_KV_HEADS * CMP_HIDDEN)
    w2 = jnp.pad(cmp_w2, ((0, 0), (0, 0), (0, LANES - HEAD_DIM))).astype(BF16)
    b2 = jnp.pad(cmp_b2, ((0, 0), (0, LANES - HEAD_DIM))).reshape(2, 1, LANES)
    return pe, w1w, b1, w2, b2


def compress(kv_cmp, weights):
    pe, w1w, b1, w2, b2 = weights
    _, bsz, seq_len, _ = kv_cmp.shape
    nb = seq_len // CMP_STRIDE
    cw = CMP_STRIDE * LANES
    x = kv_cmp.reshape(2, bsz, nb, cw)
    hid = N_KV_HEADS * CMP_HIDDEN
    return pl.pallas_call(
        _compress_kernel,
        out_shape=jax.ShapeDtypeStruct((2, bsz, N_KV_HEADS, nb, LANES), BF16),
        grid=(2, bsz),
        in_specs=[pl.BlockSpec((1, 1, nb, cw), lambda j, b: (j, b, 0, 0)),
                  pl.BlockSpec((1, 2, cw), lambda j, b: (j, 0, 0)),
                  pl.BlockSpec((1, 2, cw, hid), lambda j, b: (j, 0, 0, 0)),
                  pl.BlockSpec((1, 1, hid), lambda j, b: (j, 0, 0)),
                  pl.BlockSpec((1, CMP_HIDDEN, LANES), lambda j, b: (j, 0, 0)),
                  pl.BlockSpec((1, 1, LANES), lambda j, b: (j, 0, 0))],
        out_specs=pl.BlockSpec((1, 1, N_KV_HEADS, nb, LANES), lambda j, b: (j, b, 0, 0, 0)),
        scratch_shapes=[pltpu.VMEM((nb + 8, hid), F32)],
        compiler_params=_cparams("parallel", "parallel"),
        name="compress",
    )(x, pe, w1w, b1, w2, b2)


SEL_TK = 512
N_SEL_BLOCKS = LANES - HEAD_DIM


def _nsa_kernel(q_ref, ksel_ref, vsel_ref, kwin_ref, vwin_ref, kc_ref, vc_ref, gate_ref, ovt_ref,
                o_ref, m_ref, acc_ref):
    tq = q_ref.shape[1]
    i = pl.program_id(2)
    q0 = i * tq
    qh = [q_ref[0, :, g * LANES:(g + 1) * LANES] for g in range(GQA)]
    t_col = q0 + lax.broadcasted_iota(jnp.int32, (tq, 1), 0)
    t_row = q0 + lax.broadcasted_iota(jnp.int32, (1, tq), 1)
    contract_last = (((1,), (1,)), ((), ()))

    ncb = kc_ref.shape[3]
    kc = kc_ref[0, 0, 0]
    vc = vc_ref[0, 0, 0]
    cmp_end = lax.broadcasted_iota(jnp.int32, (1, ncb), 1) * CMP_STRIDE + (CMP_LEN - 1)
    valid = cmp_end <= t_col
    s_cmp = [lax.dot_general(qh[g], kc, contract_last, preferred_element_type=F32) for g in range(GQA)]
    wn = tq + WINDOW
    ks = pl.multiple_of(jnp.maximum(q0 - WINDOW, 0), tq)
    kw = kwin_ref[0, pl.ds(ks, wn), :]
    vw = vwin_ref[0, pl.ds(ks, wn), :]
    s_win = [lax.dot_general(qh[g], kw, contract_last, preferred_element_type=F32) for g in range(GQA)]
    o_cmp = []
    p_sum = None
    for g in range(GQA):
        s = jnp.where(valid, s_cmp[g], NEG)
        e = jnp.where(valid, jnp.exp(s - jnp.max(s, axis=-1, keepdims=True)), 0.0)
        p = e / jnp.maximum(jnp.sum(e, axis=-1, keepdims=True), 1e-30)
        o_cmp.append(jnp.dot(p.astype(BF16), vc, preferred_element_type=F32))
        p_sum = p if p_sum is None else p_sum + p

    wpos = ks + lax.broadcasted_iota(jnp.int32, (1, wn), 1)
    wmask = (wpos <= t_col) & (wpos > t_col - WINDOW)
    o_win = []
    for g in range(GQA):
        sw = jnp.where(wmask, s_win[g], NEG)
        pw = jnp.exp(sw - jnp.max(sw, axis=-1, keepdims=True))
        accw = jnp.dot(pw.astype(BF16), vw, preferred_element_type=F32)
        o_win.append(accw / accw[:, HEAD_DIM:HEAD_DIM + 1])

    ovt = ovt_ref[...]
    p1 = p_sum.astype(BF16)
    r1 = p_sum - p1.astype(F32)
    p2 = r1.astype(BF16)
    p3 = (r1 - p2.astype(F32)).astype(BF16)
    imp_t = (lax.dot_general(ovt, p1, contract_last, preferred_element_type=F32)
             + lax.dot_general(ovt, p2, contract_last, preferred_element_type=F32)
             + lax.dot_general(ovt, p3, contract_last, preferred_element_type=F32))

    nblk = N_SEL_BLOCKS
    blk = lax.broadcasted_iota(jnp.int32, (nblk, tq), 0)
    qblk = t_row // SEL_LEN
    causal_blk = blk <= qblk
    forced = (blk == 0) | (blk == qblk) | (blk == qblk - 1)
    score_t = jnp.where(forced, FORCED_SCORE, jnp.where(causal_blk, imp_t, NEG))
    groups = [score_t[8 * r:8 * r + 8] for r in range(nblk // 8)]
    sub = lax.broadcasted_iota(jnp.int32, (8, tq), 0)
    counts = [jnp.zeros((8, tq), F32) for _ in groups]
    for b in range(nblk):
        row = score_t[b:b + 1]
        for r, grp in enumerate(groups):
            if 8 * r > b:
                beats = row >= grp
            elif 8 * r + 7 <= b:
                beats = row > grp
            else:
                beats = (row > grp) | ((row == grp) & (sub > b - 8 * r))
            counts[r] = counts[r] + jnp.where(beats, 1.0, 0.0)
    keep_t = (jnp.concatenate(counts, axis=0) < float(SEL_TOPK)) & causal_blk
    bias_t = jnp.where(keep_t, 0.0, NEG)
    bias = jnp.concatenate([jnp.zeros((LANES - nblk, tq), F32), bias_t], axis=0).T.astype(BF16)
    q_sel = [qh[g] + bias for g in range(GQA)]

    tk = SEL_TK
    m_ref[...] = jnp.full(m_ref.shape, NEG, F32)
    acc_ref[...] = jnp.zeros(acc_ref.shape, F32)

    def sel_tile(j, diagonal):
        k0 = pl.multiple_of(j * tk, tk)
        k = ksel_ref[0, pl.ds(k0, tk), :]
        v = vsel_ref[0, pl.ds(k0, tk), :]
        if diagonal:
            causal = (k0 + lax.broadcasted_iota(jnp.int32, (1, tk), 1)) <= t_col
        scores = [lax.dot_general(q_sel[g], k, contract_last, preferred_element_type=F32)
                  for g in range(GQA)]
        for g in range(GQA):
            rs = slice(g * tq, (g + 1) * tq)
            sc = scores[g]
            if diagonal:
                sc = jnp.where(causal, sc, NEG)
            m_old = m_ref[rs, :]
            m_new = jnp.maximum(m_old, jnp.max(sc, axis=-1, keepdims=True))
            p = jnp.exp(sc - jnp.concatenate([m_new] * (tk // LANES), axis=1))
            acc_ref[rs, :] = (jnp.exp(m_old - m_new) * acc_ref[rs, :]
                              + jnp.dot(p.astype(BF16), v, preferred_element_type=F32))
            m_ref[rs, :] = m_new

    def body(j, carry):
        sel_tile(j, False)
        return carry

    n_full = q0 // tk
    lax.fori_loop(0, n_full, body, 0)
    sel_tile(n_full, True)

    gates = gate_ref[0]
    heads = []
    for g in range(GQA):
        acc = acc_ref[g * tq:(g + 1) * tq, :]
        o_sel = acc / acc[:, HEAD_DIM:HEAD_DIM + 1]
        heads.append(gates[:, 3 * g:3 * g + 1] * o_cmp[g] + gates[:, 3 * g + 1:3 * g + 2] * o_sel
                     + gates[:, 3 * g + 2:3 * g + 3] * o_win[g])
    lane = lax.broadcasted_iota(jnp.int32, (tq, LANES), 1)
    low = lane < HEAD_DIM
    pair0 = jnp.where(low, heads[0], pltpu.roll(heads[1], HEAD_DIM, 1))
    pair1 = jnp.where(low, heads[2], pltpu.roll(heads[3], HEAD_DIM, 1))
    o_ref[0] = jnp.concatenate([pair0, pair1], axis=1).astype(BF16)


def _overlap_matrix_t(n_cmp):
    n = np.arange(n_cmp)[None, :]
    s = np.arange(N_SEL_BLOCKS)[:, None]
    c_start = n * CMP_STRIDE
    s_start = s * SEL_LEN
    ov = (c_start < s_start + SEL_LEN) & (c_start + CMP_LEN > s_start)
    return jnp.asarray(ov.astype(np.float32), BF16)


def nsa_attention(q, ksel, vsel, kwin, vwin, kvc, gates, tq=128):
    bsz, seq_len, _ = q.shape
    ncb = kvc.shape[3]
    assert seq_len // SEL_LEN <= N_SEL_BLOCKS and seq_len % SEL_TK == 0 and SEL_TK % tq == 0
    assert seq_len >= tq + WINDOW
    kv_spec = pl.BlockSpec((1, seq_len, LANES), lambda b, h, i: (b, 0, h))
    rows = GQA * tq
    return pl.pallas_call(
        _nsa_kernel,
        out_shape=jax.ShapeDtypeStruct((bsz, seq_len, NSA_WIDTH), BF16),
        grid=(bsz, N_KV_HEADS, seq_len // tq),
        in_specs=[pl.BlockSpec((1, tq, GQA * LANES), lambda b, h, i: (b, i, h)),
                  kv_spec, kv_spec, kv_spec, kv_spec,
                  pl.BlockSpec((1, 1, 1, ncb, LANES), lambda b, h, i: (0, b, h, 0, 0)),
                  pl.BlockSpec((1, 1, 1, ncb, LANES), lambda b, h, i: (1, b, h, 0, 0)),
                  pl.BlockSpec((1, tq, LANES), lambda b, h, i: (b, i, h)),
                  pl.BlockSpec((N_SEL_BLOCKS, ncb), lambda b, h, i: (0, 0))],
        out_specs=pl.BlockSpec((1, tq, GQA * HEAD_DIM), lambda b, h, i: (b, i, h)),
        scratch_shapes=[pltpu.VMEM((rows, LANES), F32), pltpu.VMEM((rows, LANES), F32)],
        compiler_params=_cparams("parallel", "parallel", "arbitrary"),
        name="nsa_attention",
    )(q, ksel, vsel, kwin, vwin, kvc, kvc, gates, _overlap_matrix_t(ncb))


def _conv_kernel(u_ref, w_ref, cb_ref, g_ref, b_ref, o_ref, pad_ref, *, rows):
    seq_len = u_ref.shape[1]
    halo = 32
    pad_ref[0:halo, :] = jnp.zeros((halo, CONV_WIDTH), F32)
    pad_ref[halo:halo + seq_len, :] = u_ref[0]

    first = halo - CONV_LEN + 1

    def body(c, carry):
        r0 = pl.multiple_of(c * rows, rows)
        win = pad_ref[pl.ds(r0, rows + halo), :]
        acc = jnp.zeros((rows, CONV_WIDTH), F32)
        for sub in range(8):
            shifted = win if sub == 0 else pltpu.roll(win, rows + halo - sub, 0)
            for j in range(CONV_LEN):
                if (first + j) % 8 == sub:
                    a0 = first + j - sub
                    acc = acc + w_ref[j:j + 1, :] * shifted[a0:a0 + rows]
        y = _layer_norm(acc + cb_ref[...], g_ref[...], b_ref[...])
        o_ref[0, pl.ds(r0, rows), :] = (y * _sigmoid(y)).astype(BF16)
        return carry

    lax.fori_loop(0, seq_len // rows, body, 0)


def conformer_conv(u, conv_w, conv_b, ln_g, ln_b, rows=64):
    bsz, seq_len, _ = u.shape
    vec = pl.BlockSpec((1, CONV_WIDTH), lambda b: (0, 0))
    return pl.pallas_call(
        functools.partial(_conv_kernel, rows=rows),
        out_shape=jax.ShapeDtypeStruct((bsz, seq_len, CONV_WIDTH), BF16),
        grid=(bsz,),
        in_specs=[pl.BlockSpec((1, seq_len, CONV_WIDTH), lambda b: (b, 0, 0)),
                  pl.BlockSpec((CONV_LEN, CONV_WIDTH), lambda b: (0, 0)), vec, vec, vec],
        out_specs=pl.BlockSpec((1, seq_len, CONV_WIDTH), lambda b: (b, 0, 0)),
        scratch_shapes=[pltpu.VMEM((seq_len + 32, CONV_WIDTH), F32)],
        compiler_params=_cparams("parallel"),
        name="conformer_conv",
    )(u, conv_w, conv_b.reshape(1, -1), ln_g.reshape(1, -1), ln_b.reshape(1, -1))


def _s5_operators(a_re, a_im, log_dt, b_re, b_im, c_re, c_im, n_chunks):
    tc = S5_CHUNK
    dt = jnp.exp(log_dt.astype(F32))[:, None]
    lr = a_re.astype(F32) * dt
    li = a_im.astype(F32) * dt
    mag = jnp.exp(lr)
    ab_re = mag * jnp.cos(li)
    ab_im = mag * jnp.sin(li)
    den = a_re * a_re + a_im * a_im
    coef_re = ((ab_re - 1.0) * a_re + ab_im * a_im) / den
    coef_im = (ab_im * a_re - (ab_re - 1.0) * a_im) / den
    bb_re = coef_re[..., None] * b_re - coef_im[..., None] * b_im
    bb_im = coef_re[..., None] * b_im + coef_im[..., None] * b_re

    def power(tau):
        tau = jnp.asarray(tau, F32)
        m = jnp.exp(lr[..., None] * tau)
        return m * jnp.cos(li[..., None] * tau), m * jnp.sin(li[..., None] * tau)

    lag_re, lag_im = power(jnp.arange(tc + 1))
    ca_re = c_re[..., None] * lag_re[:, None] - c_im[..., None] * lag_im[:, None]
    ca_im = c_re[..., None] * lag_im[:, None] + c_im[..., None] * lag_re[:, None]
    kern = (jnp.einsum('gcnt,gnd->gtcd', ca_re[..., :tc], bb_re, precision=HIGHEST)
            - jnp.einsum('gcnt,gnd->gtcd', ca_im[..., :tc], bb_im, precision=HIGHEST))
    s_idx = np.arange(tc)[:, None]
    i_idx = np.arange(tc)[None, :]
    lag = np.clip(i_idx - s_idx, 0, tc - 1)
    toep = kern[:, lag]
    toep = jnp.where(jnp.asarray(i_idx >= s_idx)[None, :, :, None, None], toep, 0.0)
    m_op = toep.transpose(0, 1, 4, 2, 3).reshape(S5_GROUPS, tc * S5_GROUP_CH, tc * S5_GROUP_CH)
    rev_re, rev_im = lag_re[..., tc - 1::-1][..., :tc], lag_im[..., tc - 1::-1][..., :tc]
    bop_re = (rev_re[..., None] * bb_re[:, :, None] - rev_im[..., None] * bb_im[:, :, None])
    bop_im = (rev_re[..., None] * bb_im[:, :, None] + rev_im[..., None] * bb_re[:, :, None])
    bop_re = bop_re.transpose(0, 2, 3, 1).reshape(S5_GROUPS, tc * S5_GROUP_CH, S5_STATE)
    bop_im = bop_im.transpose(0, 2, 3, 1).reshape(S5_GROUPS, tc * S5_GROUP_CH, S5_STATE)
    cop_re = ca_re[..., 1:].transpose(0, 2, 3, 1).reshape(S5_GROUPS, S5_STATE, tc * S5_GROUP_CH)
    cop_im = -ca_im[..., 1:].transpose(0, 2, 3, 1).reshape(S5_GROUPS, S5_STATE, tc * S5_GROUP_CH)

    n_pairs = S5_GROUPS // 2
    gw = tc * S5_GROUP_CH

    def pair_blockdiag(x):
        r, c = x.shape[1:]
        x = x.reshape(n_pairs, 2, r, c)
        z = jnp.zeros((n_pairs, r, c), x.dtype)
        top = jnp.concatenate([x[:, 0], z], axis=2)
        bot = jnp.concatenate([z, x[:, 1]], axis=2)
        return jnp.concatenate([top, bot], axis=1)

    m_pair = pair_blockdiag(m_op)
    b_pair = jnp.concatenate([pair_blockdiag(bop_re), pair_blockdiag(bop_im)], axis=2)
    c_pair = jnp.concatenate([pair_blockdiag(cop_re), pair_blockdiag(cop_im)], axis=1)
    levels = max(1, int(math.log2(n_chunks)))
    lv_re, lv_im = power(tc * (2.0 ** jnp.arange(levels)))
    lv_re = lv_re.transpose(0, 2, 1).reshape(n_pairs, 2, levels, S5_STATE)
    lv_im = lv_im.transpose(0, 2, 1).reshape(n_pairs, 2, levels, S5_STATE)
    a_lv = jnp.concatenate([lv_re[:, 0], lv_re[:, 1], lv_im[:, 0], lv_im[:, 1]], axis=-1)
    return m_pair, b_pair, c_pair, a_lv


def _s5_kernel(u_ref, m_ref, b_ref, c_ref, a_ref, y_ref, sre_ref, sim_ref):
    nc = u_ref.shape[1]
    half = 2 * S5_STATE
    u = u_ref[0]
    v = jnp.dot(u, b_ref[0], preferred_element_type=F32, precision=HIGHEST)
    zeros = jnp.zeros((nc, half), F32)
    sre_ref[0:nc, :] = zeros
    sim_ref[0:nc, :] = zeros
    sre_ref[nc:2 * nc, :] = v[:, 0:half]
    sim_ref[nc:2 * nc, :] = v[:, half:2 * half]
    levels = a_ref.shape[1]
    for lv in range(levels):
        d = 1 << lv
        ar = a_ref[0, lv:lv + 1, 0:half]
        ai = a_ref[0, lv:lv + 1, half:2 * half]
        pr = sre_ref[nc - d:2 * nc - d, :]
        pi = sim_ref[nc - d:2 * nc - d, :]
        cr = sre_ref[nc:2 * nc, :]
        ci = sim_ref[nc:2 * nc, :]
        sre_ref[nc:2 * nc, :] = cr + ar * pr - ai * pi
        sim_ref[nc:2 * nc, :] = ci + ar * pi + ai * pr
    prev_re = sre_ref[nc - 1:2 * nc - 1, :]
    prev_im = sim_ref[nc - 1:2 * nc - 1, :]
    y = (jnp.dot(u, m_ref[0], preferred_element_type=F32, precision=HIGHEST)
         + jnp.dot(prev_re, c_ref[0, 0:half, :], preferred_element_type=F32, precision=HIGHEST)
         + jnp.dot(prev_im, c_ref[0, half:2 * half, :], preferred_element_type=F32, precision=HIGHEST))
    y_ref[0] = y


def s5_scan(zs, operators):
    m_pair, b_pair, c_pair, a_lv = operators
    bsz, seq_len, _ = zs.shape
    tc = S5_CHUNK
    nc = seq_len // tc
    assert nc & (nc - 1) == 0 and a_lv.shape[1] == int(math.log2(nc))
    n_pairs = S5_GROUPS // 2
    pw = 2 * tc * S5_GROUP_CH
    u = zs.reshape(bsz, nc, tc, S5_GROUPS, S5_GROUP_CH).transpose(0, 1, 3, 2, 4).reshape(bsz, nc, n_pairs * pw)
    y = pl.pallas_call(
        _s5_kernel,
        out_shape=jax.ShapeDtypeStruct((bsz, nc, n_pairs * pw), F32),
        grid=(bsz, n_pairs),
        in_specs=[pl.BlockSpec((1, nc, pw), lambda b, p: (b, 0, p)),
                  pl.BlockSpec((1, pw, pw), lambda b, p: (p, 0, 0)),
                  pl.BlockSpec((1, pw, 4 * S5_STATE), lambda b, p: (p, 0, 0)),
                  pl.BlockSpec((1, 4 * S5_STATE, pw), lambda b, p: (p, 0, 0)),
                  pl.BlockSpec((1, a_lv.shape[1], 4 * S5_STATE), lambda b, p: (p, 0, 0))],
        out_specs=pl.BlockSpec((1, nc, pw), lambda b, p: (b, 0, p)),
        scratch_shapes=[pltpu.VMEM((2 * nc, 2 * S5_STATE), F32), pltpu.VMEM((2 * nc, 2 * S5_STATE), F32)],
        compiler_params=_cparams("parallel", "parallel"),
        name="s5_scan",
    )(u, m_pair, b_pair, c_pair, a_lv)
    return y.reshape(bsz, nc, S5_GROUPS, tc, S5_GROUP_CH).transpose(0, 1, 3, 2, 4).reshape(bsz, seq_len, S5_WIDTH)


def _s5_tail_kernel(y_ref, u_ref, d_ref, w_ref, b_ref, o_ref):
    z = _gelu_tanh(y_ref[...] + d_ref[...] * u_ref[...])
    gate = jnp.dot(z.astype(BF16), w_ref[...], preferred_element_type=F32) + b_ref[...]
    o_ref[...] = (z * _sigmoid(gate)).astype(BF16)


def s5_tail(y2, u2, d_skip, glu_w, glu_b, tm=1024):
    t = y2.shape[0]
    row = pl.BlockSpec((tm, S5_WIDTH), lambda i: (i, 0))
    vec = pl.BlockSpec((1, S5_WIDTH), lambda i: (0, 0))
    return pl.pallas_call(
        _s5_tail_kernel,
        out_shape=jax.ShapeDtypeStruct((t, S5_WIDTH), BF16),
        grid=(t // tm,),
        in_specs=[row, row, vec, pl.BlockSpec((S5_WIDTH, S5_WIDTH), lambda i: (0, 0)), vec],
        out_specs=row,
        compiler_params=_cparams("parallel"),
        name="s5_tail",
    )(y2, u2, d_skip.reshape(1, -1), glu_w.astype(BF16), glu_b.reshape(1, -1))


def _out_proj_kernel(x_ref, a_ref, c_ref, s_ref, w_ref, g_ref, b_ref, o_ref, *, alpha):
    h = jnp.dot(a_ref[...], w_ref[0:NSA_WIDTH, :], preferred_element_type=F32)
    h = h + jnp.dot(c_ref[...], w_ref[NSA_WIDTH:NSA_WIDTH + CONV_WIDTH, :], preferred_element_type=F32)
    h = h + jnp.dot(s_ref[...], w_ref[NSA_WIDTH + CONV_WIDTH:, :], preferred_element_type=F32)
    o_ref[...] = _layer_norm(alpha * x_ref[...] + h, g_ref[...], b_ref[...])


def out_proj_ln(x2, o_nsa, o_conv, o_s5, w_out, g, b, alpha, tm=512):
    t, d = x2.shape
    row = lambda width: pl.BlockSpec((tm, width), lambda i: (i, 0))
    vec = pl.BlockSpec((1, d), lambda i: (0, 0))
    return pl.pallas_call(
        functools.partial(_out_proj_kernel, alpha=alpha),
        out_shape=jax.ShapeDtypeStruct((t, d), F32),
        grid=(t // tm,),
        in_specs=[row(d), row(NSA_WIDTH), row(CONV_WIDTH), row(S5_WIDTH),
                  pl.BlockSpec(w_out.shape, lambda i: (0, 0)), vec, vec],
        out_specs=row(d),
        compiler_params=_cparams("parallel"),
        name="out_proj_ln",
    )(x2, o_nsa, o_conv, o_s5, w_out.astype(BF16), g.reshape(1, -1), b.reshape(1, -1))


def _ffn_kernel(te_ref, nu_ref, x_ref, w1_ref, w3_ref, w2_ref, o_ref, acc_ref):
    i = pl.program_id(0)
    f = pl.program_id(1)
    last = pl.num_programs(1) - 1
    used = i < nu_ref[0]

    @pl.when(used)
    def _():
        xb = x_ref[...].astype(BF16)
        h1 = jnp.dot(xb, w1_ref[0], preferred_element_type=F32)
        h3 = jnp.dot(xb, w3_ref[0], preferred_element_type=F32)
        h = (h1 * _sigmoid(h1) * h3).astype(BF16)
        part = jnp.dot(h, w2_ref[0], preferred_element_type=F32)

        @pl.when(f == 0)
        def _():
            acc_ref[...] = part

        @pl.when(f > 0)
        def _():
            acc_ref[...] = acc_ref[...] + part

        @pl.when(f == last)
        def _():
            o_ref[...] = acc_ref[...]

    @pl.when(jnp.logical_not(used) & (f == last))
    def _():
        o_ref[...] = jnp.zeros(o_ref.shape, o_ref.dtype)


def grouped_swiglu(xs, tile_expert, n_used, w1, w3, w2, tm, tf):
    p, d = xs.shape
    ff = w1.shape[2]
    assert p % tm == 0 and ff % tf == 0
    grid_spec = pltpu.PrefetchScalarGridSpec(
        num_scalar_prefetch=2,
        grid=(p // tm, ff // tf),
        in_specs=[pl.BlockSpec((tm, d), lambda i, f, te, nu: (i, 0)),
                  pl.BlockSpec((1, d, tf), lambda i, f, te, nu: (te[i], 0, f)),
                  pl.BlockSpec((1, d, tf), lambda i, f, te, nu: (te[i], 0, f)),
                  pl.BlockSpec((1, tf, d), lambda i, f, te, nu: (te[i], f, 0))],
        out_specs=pl.BlockSpec((tm, d), lambda i, f, te, nu: (i, 0)),
        scratch_shapes=[pltpu.VMEM((tm, d), F32)],
    )
    return pl.pallas_call(
        _ffn_kernel,
        out_shape=jax.ShapeDtypeStruct((p, d), F32),
        grid_spec=grid_spec,
        compiler_params=_cparams("arbitrary", "arbitrary"),
        name="grouped_swiglu",
    )(tile_expert, n_used, xs, w1, w3, w2)


def _pick_tf(ff, target=1024):
    best = LANES
    for cand in range(LANES, ff + 1, LANES):
        if ff % cand == 0 and cand <= target:
            best = cand
    return best


def _residual_ln_kernel(x_ref, a_ref, b_ref, wa_ref, wb_ref, g_ref, beta_ref, o_ref, *, alpha):
    f = wa_ref[...] * a_ref[...] + wb_ref[...] * b_ref[...]
    o_ref[...] = _layer_norm(alpha * x_ref[...] + f, g_ref[...], beta_ref[...])


def residual_ln(x2, a, b, wa, wb, g, beta, alpha, tm=512):
    t, d = x2.shape
    row = pl.BlockSpec((tm, d), lambda i: (i, 0))
    col = pl.BlockSpec((tm, 1), lambda i: (i, 0))
    vec = pl.BlockSpec((1, d), lambda i: (0, 0))
    return pl.pallas_call(
        functools.partial(_residual_ln_kernel, alpha=alpha),
        out_shape=jax.ShapeDtypeStruct((t, d), F32),
        grid=(t // tm,),
        in_specs=[row, row, row, col, col, vec, vec],
        out_specs=row,
        compiler_params=_cparams("parallel"),
        name="residual_ln",
    )(x2, a, b, wa, wb, g.reshape(1, -1), beta.reshape(1, -1))


def _residual_ln1_kernel(x_ref, a_ref, g_ref, beta_ref, o_ref, *, alpha):
    o_ref[...] = _layer_norm(alpha * x_ref[...] + a_ref[...], g_ref[...], beta_ref[...])


def residual_ln1(x2, a, g, beta, alpha, tm=512):
    t, d = x2.shape
    row = pl.BlockSpec((tm, d), lambda i: (i, 0))
    vec = pl.BlockSpec((1, d), lambda i: (0, 0))
    return pl.pallas_call(
        functools.partial(_residual_ln1_kernel, alpha=alpha),
        out_shape=jax.ShapeDtypeStruct((t, d), F32),
        grid=(t // tm,),
        in_specs=[row, row, vec, vec],
        out_specs=row,
        compiler_params=_cparams("parallel"),
        name="residual_ln1",
    )(x2, a, g.reshape(1, -1), beta.reshape(1, -1))


def _router_kernel(x_ref, w_ref, o_ref):
    logits = jnp.dot(x_ref[...], w_ref[...], preferred_element_type=F32, precision=HIGHEST)
    lane = lax.broadcasted_iota(jnp.int32, logits.shape, 1)
    lane_f = lane.astype(F32)
    logits = jnp.where(lane < N_EXPERTS, logits, -jnp.inf)
    v1 = jnp.max(logits, axis=-1, keepdims=True)
    i1 = jnp.min(jnp.where(logits == v1, lane_f, 1e9), axis=-1, keepdims=True)
    rest = jnp.where(lane_f == i1, -jnp.inf, logits)
    v2 = jnp.max(rest, axis=-1, keepdims=True)
    i2 = jnp.min(jnp.where(rest == v2, lane_f, 1e9), axis=-1, keepdims=True)
    e2 = jnp.exp(v2 - v1)
    den = 1.0 + e2
    out = jnp.where(lane == 0, i1, jnp.where(lane == 1, i2, jnp.where(lane == 2, 1.0 / den, e2 / den)))
    o_ref[...] = jnp.where(lane < 4, out, 0.0)


def moe_route(x2, router, tm=512):
    t, d = x2.shape
    w = jnp.pad(router, ((0, 0), (0, LANES - router.shape[1])))
    return pl.pallas_call(
        _router_kernel,
        out_shape=jax.ShapeDtypeStruct((t, LANES), F32),
        grid=(t // tm,),
        in_specs=[pl.BlockSpec((tm, d), lambda i: (i, 0)), pl.BlockSpec((d, LANES), lambda i: (0, 0))],
        out_specs=pl.BlockSpec((tm, LANES), lambda i: (i, 0)),
        compiler_params=_cparams("parallel"),
        name="moe_route",
    )(x2, w)


def moe_swiglu_ln(x2, router, w1, w3, w2, g, beta, alpha, tm):
    t, d = x2.shape
    n_exp = w1.shape[0]
    routed = moe_route(x2, router)
    idx = routed[:, 0:2].astype(jnp.int32)
    wts = routed[:, 2:4]
    e_flat = idx.T.reshape(-1)
    order = jnp.argsort(e_flat, stable=True)
    sorted_e = e_flat[order]
    counts = jnp.zeros((n_exp,), jnp.int32).at[e_flat].add(1)
    padded = ((counts + tm - 1) // tm) * tm
    pad_start = jnp.cumsum(padded) - padded
    start = jnp.cumsum(counts) - counts
    n_rows = 2 * t + n_exp * tm
    dest = pad_start[sorted_e] + (jnp.arange(2 * t, dtype=jnp.int32) - start[sorted_e])
    tok = order % t
    src = jnp.zeros((n_rows,), jnp.int32).at[dest].set(tok)
    pos = jnp.zeros((2 * t,), jnp.int32).at[order].set(dest)
    n_tiles = n_rows // tm
    tile_start = jnp.arange(n_tiles, dtype=jnp.int32) * tm
    pad_end = jnp.cumsum(padded)
    tile_expert = jnp.minimum(jnp.sum(tile_start[:, None] >= pad_end[None, :], axis=1), n_exp - 1).astype(jnp.int32)
    n_used = (pad_end[-1] // tm).astype(jnp.int32).reshape(1)
    xs = x2.astype(BF16)[src]
    ys = grouped_swiglu(xs, tile_expert, n_used, w1, w3, w2, tm, _pick_tf(w1.shape[2]))
    return residual_ln(x2, ys[pos[:t]], ys[pos[t:]], wts[:, 0:1], wts[:, 1:2], g, beta, alpha)


def dense_swiglu_ln(x2, w1, w3, w2, g, beta, alpha, tm):
    t = x2.shape[0]
    tile_expert = jnp.zeros((t // tm,), jnp.int32)
    n_used = jnp.full((1,), t // tm, jnp.int32)
    ys = grouped_swiglu(x2, tile_expert, n_used, w1[None], w3[None], w2[None], tm, _pick_tf(w1.shape[1], 1408))
    return residual_ln1(x2, ys, g, beta, alpha)


def _ple_kernel(x_ref, p_ref, wg_ref, bg_ref, wp_ref, g_ref, beta_ref, o_ref, *, alpha):
    x = x_ref[...]
    gate = _sigmoid(jnp.dot(x.astype(BF16), wg_ref[...], preferred_element_type=F32) + bg_ref[...])
    e = jnp.dot(p_ref[...].astype(BF16), wp_ref[...], preferred_element_type=F32) * gate
    o_ref[...] = _layer_norm(alpha * x + e, g_ref[...], beta_ref[...])


def ple_ln(x2, p2, gate_w, gate_b, proj, g, beta, alpha, tm=512):
    t, d = x2.shape
    pd = p2.shape[1]
    vec = pl.BlockSpec((1, d), lambda i: (0, 0))
    return pl.pallas_call(
        functools.partial(_ple_kernel, alpha=alpha),
        out_shape=jax.ShapeDtypeStruct((t, d), F32),
        grid=(t // tm,),
        in_specs=[pl.BlockSpec((tm, d), lambda i: (i, 0)), pl.BlockSpec((tm, pd), lambda i: (i, 0)),
                  pl.BlockSpec((d, d), lambda i: (0, 0)), vec,
                  pl.BlockSpec((pd, d), lambda i: (0, 0)), vec, vec],
        out_specs=pl.BlockSpec((tm, d), lambda i: (i, 0)),
        compiler_params=_cparams("parallel"),
        name="ple_ln",
    )(x2, p2, gate_w.astype(BF16), gate_b.reshape(1, -1), proj.astype(BF16), g.reshape(1, -1), beta.reshape(1, -1))


def hybrid_mixer_ln(x2, bsz, seq_len, tables, w_in, w_out, cmp_weights, conv_params, s5_params,
                    g, beta, alpha):
    (q, ksel, kwin, vsel, vwin, kcmp, vcmp, gates, u_conv, zs) = in_proj(x2, _widen_w_in(w_in), tables, seq_len)
    b3 = lambda a: a.reshape(bsz, seq_len, a.shape[1])
    kv_cmp = jnp.stack([b3(kcmp), b3(vcmp)])
    kvc = compress(kv_cmp, _compress_weights(*cmp_weights))
    o_nsa = nsa_attention(b3(q), b3(ksel), b3(vsel), b3(kwin), b3(vwin), kvc, b3(gates))
    o_conv = conformer_conv(b3(u_conv), *conv_params)
    (a_re, a_im, log_dt, b_re, b_im, c_re, c_im, d_skip, glu_w, glu_b) = s5_params
    ops = _s5_operators(a_re, a_im, log_dt, b_re, b_im, c_re, c_im, seq_len // S5_CHUNK)
    y_s5 = s5_scan(b3(zs), ops)
    o_s5 = s5_tail(y_s5.reshape(-1, S5_WIDTH), zs, d_skip, glu_w, glu_b)
    return out_proj_ln(x2, o_nsa.reshape(-1, NSA_WIDTH), o_conv.reshape(-1, CONV_WIDTH), o_s5, w_out,
                       g, beta, alpha)


def kernel(x, p, positions, w_in, w_out, cmp_pe, cmp_w1, cmp_b1, cmp_w2, cmp_b2, conv_w, conv_b, conv_ln_g, conv_ln_b, s5_a_re, s5_a_im, s5_log_dt, s5_b_re, s5_b_im, s5_c_re, s5_c_im, s5_d, s5_glu_w, s5_glu_b, ffn_w1, ffn_w3, ffn_w2, moe_router, moe_w1, moe_w3, moe_w2, ple_gate_w, ple_gate_b, ple_proj, ln_g, ln_b):
    bsz, seq_len, d_model = x.shape
    depth = w_in.shape[0]
    alpha = (2 * depth) ** 0.25
    t = bsz * seq_len
    tables = rope_tables(positions)
    x2 = x.reshape(t, d_model)
    dense_tm = 512
    moe_tm = 512 if t % 1024 else 1024
    for i in range(depth):
        x2 = hybrid_mixer_ln(
            x2, bsz, seq_len, tables, w_in[i], w_out[i],
            (cmp_pe[i], cmp_w1[i], cmp_b1[i], cmp_w2[i], cmp_b2[i]),
            (conv_w[i], conv_b[i], conv_ln_g[i], conv_ln_b[i]),
            (s5_a_re[i], s5_a_im[i], s5_log_dt[i], s5_b_re[i], s5_b_im[i], s5_c_re[i], s5_c_im[i],
             s5_d[i], s5_glu_w[i], s5_glu_b[i]),
            ln_g[i, 0], ln_b[i, 0], alpha)
        j = i // 2
        if i % 2 == 0:
            x2 = dense_swiglu_ln(x2, ffn_w1[j].astype(BF16), ffn_w3[j].astype(BF16), ffn_w2[j].astype(BF16),
                                 ln_g[i, 1], ln_b[i, 1], alpha, dense_tm)
        else:
            x2 = moe_swiglu_ln(x2, moe_router[j], moe_w1[j].astype(BF16), moe_w3[j].astype(BF16),
                               moe_w2[j].astype(BF16), ln_g[i, 1], ln_b[i, 1], alpha, moe_tm)
        x2 = ple_ln(x2, p[i].reshape(t, -1), ple_gate_w[i], ple_gate_b[i], ple_proj[i],
                    ln_g[i, 2], ln_b[i, 2], alpha)
    return x2.reshape(bsz, seq_len, d_model)
```

```python
import functools
import math

import numpy as np
import jax
import jax.numpy as jnp
from jax import lax
from jax.experimental import pallas as pl
from jax.experimental.pallas import tpu as pltpu

F32 = jnp.float32
BF16 = jnp.bfloat16
HIGHEST = lax.Precision.HIGHEST

LANES = 128
VMEM_LIMIT = 56 * 1024 * 1024

HEAD_DIM = 64
HALF_DIM = HEAD_DIM // 2
N_Q_HEADS = 8
N_KV_HEADS = 2
GQA = N_Q_HEADS // N_KV_HEADS
CMP_LEN = 32
CMP_STRIDE = 16
CMP_HIDDEN = 256
SEL_LEN = 64
SEL_TOPK = 16
WINDOW = 512
ROPE_THETA = 10000.0
FORCED_SCORE = 1e9
NEG = -1e30
CONV_WIDTH = 256
CONV_LEN = 31
S5_WIDTH = 256
S5_GROUP_CH = 16
S5_GROUPS = 16
S5_STATE = 64
S5_CHUNK = 8
S5_HALVES = 2
N_EXPERTS = 8
LN_EPS = 1e-5
NSA_WIDTH = N_Q_HEADS * HEAD_DIM

C_Q = 0
C_KSEL = 1024
C_KWIN = 1280
C_VSEL = 1536
C_VWIN = 1792
C_KCMP = 2048
C_VCMP = 2176
C_GATE = 2304
C_CONVA = 2560
C_CONVB = 2816
C_S5 = 3072
C_TOTAL = 3328


def _cparams(*sem):
    return pltpu.CompilerParams(dimension_semantics=sem, vmem_limit_bytes=VMEM_LIMIT)


def _layer_norm(v, g, b):
    mu = jnp.mean(v, axis=-1, keepdims=True)
    d = v - mu
    var = jnp.mean(d * d, axis=-1, keepdims=True)
    return d * lax.rsqrt(var + LN_EPS) * g + b


def _gelu_tanh(x):
    return 0.5 * x * (1.0 + jnp.tanh(math.sqrt(2.0 / math.pi) * (x + 0.044715 * (x * x * x))))


def _sigmoid(x):
    return 1.0 / (1.0 + jnp.exp(-x))


def _rope_table_kernel(pos_ref, freq_ref, tab_ref):
    ang = pos_ref[...] * freq_ref[...]
    c = jnp.cos(ang)
    s = jnp.sin(ang)
    lane = lax.broadcasted_iota(jnp.int32, ang.shape, 1)
    first_half = (lane % HEAD_DIM) < HALF_DIM
    tab_ref[:, 0:LANES] = c
    tab_ref[:, LANES:2 * LANES] = jnp.where(first_half, -s, 0.0)
    tab_ref[:, 2 * LANES:3 * LANES] = jnp.where(first_half, 0.0, s)


def rope_tables(positions, tm=512):
    t = positions.size
    pos = positions.reshape(t, 1).astype(F32)
    inv_freq = ROPE_THETA ** (-jnp.arange(0, HEAD_DIM, 2, dtype=F32) / HEAD_DIM)
    freq = jnp.tile(inv_freq, LANES // HALF_DIM).reshape(1, LANES)
    return pl.pallas_call(
        _rope_table_kernel,
        out_shape=jax.ShapeDtypeStruct((t, 3 * LANES), F32),
        grid=(t // tm,),
        in_specs=[pl.BlockSpec((tm, 1), lambda i: (i, 0)),
                  pl.BlockSpec((1, LANES), lambda i: (0, 0))],
        out_specs=pl.BlockSpec((tm, 3 * LANES), lambda i: (i, 0)),
        compiler_params=_cparams("parallel"),
        name="rope_tables",
    )(pos, freq)


def _in_proj_kernel(x_ref, w_ref, tab_ref, q_ref, ksel_ref, kwin_ref, vsel_ref, vwin_ref,
                    kcmp_ref, vcmp_ref, gate_ref, u_ref, zs_ref, *, seq_len):
    tm = x_ref.shape[0]
    xb = x_ref[...].astype(BF16)
    cos = tab_ref[:, 0:LANES]
    sin_a = tab_ref[:, LANES:2 * LANES]
    sin_b = tab_ref[:, 2 * LANES:3 * LANES]

    def proj(c0, width):
        return jnp.dot(xb, w_ref[:, c0:c0 + width], preferred_element_type=F32)

    def rope(z):
        return (z * cos + pltpu.roll(z, LANES - HALF_DIM, 1) * sin_a
                + pltpu.roll(z, HALF_DIM, 1) * sin_b)

    lane = lax.broadcasted_iota(jnp.int32, (tm, LANES), 1)
    row = lax.broadcasted_iota(jnp.int32, (tm, LANES), 0)
    t_seq = (pl.program_id(0) * tm) % seq_len + row
    blk_onehot = jnp.where(lane == HEAD_DIM + t_seq // SEL_LEN, 1.0, 0.0)
    ones_lane = jnp.where(lane == HEAD_DIM, 1.0, 0.0)

    scale = HEAD_DIM ** -0.5
    for h in range(N_Q_HEADS):
        z = proj(C_Q + h * LANES, LANES)
        q_ref[:, h * LANES:(h + 1) * LANES] = (rope(z) * scale).astype(BF16)
    for h in range(N_KV_HEADS):
        sl = slice(h * LANES, (h + 1) * LANES)
        ksel_ref[:, sl] = (rope(proj(C_KSEL + h * LANES, LANES)) + blk_onehot).astype(BF16)
        kwin_ref[:, sl] = rope(proj(C_KWIN + h * LANES, LANES)).astype(BF16)
        vsel_ref[:, sl] = (proj(C_VSEL + h * LANES, LANES) + ones_lane).astype(BF16)
        vwin_ref[:, sl] = (proj(C_VWIN + h * LANES, LANES) + ones_lane).astype(BF16)
    kcmp_ref[...] = rope(proj(C_KCMP, LANES))
    vcmp_ref[...] = proj(C_VCMP, LANES)
    gate_ref[...] = _sigmoid(proj(C_GATE, 2 * LANES))
    a = proj(C_CONVA, CONV_WIDTH)
    b = proj(C_CONVB, CONV_WIDTH)
    u_ref[...] = a * _sigmoid(b)
    zs_ref[...] = proj(C_S5, S5_WIDTH)


def _widen_w_in(w_in):
    d = w_in.shape[0]
    o_kv = NSA_WIDTH
    o_gate = o_kv + 6 * N_KV_HEADS * HEAD_DIM
    o_conv = o_gate + 3 * N_Q_HEADS
    o_s5 = o_conv + 2 * CONV_WIDTH
    zpad = jnp.zeros((d, HEAD_DIM), w_in.dtype)
    cols = []
    for h in range(N_Q_HEADS):
        cols += [w_in[:, h * HEAD_DIM:(h + 1) * HEAD_DIM], zpad]

    def kv(j, h):
        c0 = o_kv + (j * N_KV_HEADS + h) * HEAD_DIM
        return w_in[:, c0:c0 + HEAD_DIM]

    for j in (2, 4, 3, 5):
        for h in range(N_KV_HEADS):
            cols += [kv(j, h), zpad]
    cols += [kv(0, 0), kv(0, 1), kv(1, 0), kv(1, 1)]
    per_group = 3 * GQA
    for h in range(N_KV_HEADS):
        cols += [w_in[:, o_gate + h * per_group:o_gate + (h + 1) * per_group],
                 jnp.zeros((d, LANES - per_group), w_in.dtype)]
    cols += [w_in[:, o_conv:o_conv + 2 * CONV_WIDTH], w_in[:, o_s5:o_s5 + S5_WIDTH]]
    w = jnp.concatenate(cols, axis=1)
    assert w.shape[1] == C_TOTAL
    return w.astype(BF16)


def in_proj(x2, w_wide, tables, seq_len, tm=512):
    t, d = x2.shape
    row = lambda width: pl.BlockSpec((tm, width), lambda i: (i, 0))
    out_shape = (
        jax.ShapeDtypeStruct((t, N_Q_HEADS * LANES), BF16),
        jax.ShapeDtypeStruct((t, N_KV_HEADS * LANES), BF16),
        jax.ShapeDtypeStruct((t, N_KV_HEADS * LANES), BF16),
        jax.ShapeDtypeStruct((t, N_KV_HEADS * LANES), BF16),
        jax.ShapeDtypeStruct((t, N_KV_HEADS * LANES), BF16),
        jax.ShapeDtypeStruct((t, LANES), F32),
        jax.ShapeDtypeStruct((t, LANES), F32),
        jax.ShapeDtypeStruct((t, 2 * LANES), F32),
        jax.ShapeDtypeStruct((t, CONV_WIDTH), F32),
        jax.ShapeDtypeStruct((t, S5_WIDTH), F32),
    )
    return pl.pallas_call(
        functools.partial(_in_proj_kernel, seq_len=seq_len),
        out_shape=out_shape,
        grid=(t // tm,),
        in_specs=[row(d),
                  pl.BlockSpec((d, C_TOTAL), lambda i: (0, 0)),
                  row(3 * LANES)],
        out_specs=tuple(row(s.shape[1]) for s in out_shape),
        compiler_params=_cparams("parallel"),
        name="in_proj",
    )(x2, w_wide, tables)


def _compress_kernel(x_ref, pe_ref, w1_ref, b1_ref, w2_ref, b2_ref, o_ref, shift_ref):
    nb = x_ref.shape[2]
    x = x_ref[0, 0]
    xa = (x + pe_ref[0, 0:1, :]).astype(BF16)
    xb = (x + pe_ref[0, 1:2, :]).astype(BF16)
    ha = jnp.dot(xa, w1_ref[0, 0], preferred_element_type=F32)
    hb = jnp.dot(xb, w1_ref[0, 1], preferred_element_type=F32)
    shift_ref[0:nb, :] = hb
    shift_ref[nb:nb + 8, :] = jnp.zeros((8, hb.shape[1]), F32)
    h = ha + shift_ref[1:nb + 1, :] + b1_ref[0]
    g = _gelu_tanh(h).astype(BF16)
    for hd in range(N_KV_HEADS):
        gh = g[:, hd * CMP_HIDDEN:(hd + 1) * CMP_HIDDEN]
        o_ref[0, 0, hd] = (jnp.dot(gh, w2_ref[0], preferred_element_type=F32) + b2_ref[0]).astype(BF16)


def _compress_weights(cmp_pe, cmp_w1, cmp_b1, cmp_w2, cmp_b2):
    half = CMP_LEN // 2
    pe = cmp_pe.reshape(2, 2, half, 1, HEAD_DIM)
    pe = jnp.broadcast_to(pe, (2, 2, half, N_KV_HEADS, HEAD_DIM)).reshape(2, 2, half * N_KV_HEADS * HEAD_DIM)
    w1 = cmp_w1.reshape(2, 2, half, HEAD_DIM, CMP_HIDDEN)
    z = jnp.zeros_like(w1)
    w_h0 = jnp.stack([w1, z], axis=3)
    w_h1 = jnp.stack([z, w1], axis=3)
    w1w = jnp.concatenate([w_h0, w_h1], axis=-1)
    w1w = w1w.reshape(2, 2, half * N_KV_HEADS * HEAD_DIM, N_KV_HEADS * CMP_HIDDEN).astype(BF16)
    b1 = jnp.tile(cmp_b1, (1, N_KV_HEADS)).reshape(2, 1, N_KV_HEADS * CMP_HIDDEN)
    w2 = jnp.pad(cmp_w2, ((0, 0), (0, 0), (0, LANES - HEAD_DIM))).astype(BF16)
    b2 = jnp.pad(cmp_b2, ((0, 0), (0, LANES - HEAD_DIM))).reshape(2, 1, LANES)
    return pe, w1w, b1, w2, b2


def compress(kv_cmp, weights):
    pe, w1w, b1, w2, b2 = weights
    _, bsz, seq_len, _ = kv_cmp.shape
    nb = seq_len // CMP_STRIDE
    cw = CMP_STRIDE * LANES
    x = kv_cmp.reshape(2, bsz, nb, cw)
    hid = N_KV_HEADS * CMP_HIDDEN
    return pl.pallas_call(
        _compress_kernel,
        out_shape=jax.ShapeDtypeStruct((2, bsz, N_KV_HEADS, nb, LANES), BF16),
        grid=(2, bsz),
        in_specs=[pl.BlockSpec((1, 1, nb, cw), lambda j, b: (j, b, 0, 0)),
                  pl.BlockSpec((1, 2, cw), lambda j, b: (j, 0, 0)),
                  pl.BlockSpec((1, 2, cw, hid), lambda j, b: (j, 0, 0, 0)),
                  pl.BlockSpec((1, 1, hid), lambda j, b: (j, 0, 0)),
                  pl.BlockSpec((1, CMP_HIDDEN, LANES), lambda j, b: (j, 0, 0)),
                  pl.BlockSpec((1, 1, LANES), lambda j, b: (j, 0, 0))],
        out_specs=pl.BlockSpec((1, 1, N_KV_HEADS, nb, LANES), lambda j, b: (j, b, 0, 0, 0)),
        scratch_shapes=[pltpu.VMEM((nb + 8, hid), F32)],
        compiler_params=_cparams("parallel", "parallel"),
        name="compress",
    )(x, pe, w1w, b1, w2, b2)


SEL_TK = 512
N_SEL_BLOCKS = LANES - HEAD_DIM


def _nsa_kernel(q_ref, ksel_ref, vsel_ref, kwin_ref, vwin_ref, kc_ref, vc_ref, gate_ref, ovt_ref,
                o_ref, m_ref, acc_ref):
    tq = q_ref.shape[1]
    i = pl.program_id(2)
    q0 = i * tq
    qh = [q_ref[0, :, g * LANES:(g + 1) * LANES] for g in range(GQA)]
    t_col = q0 + lax.broadcasted_iota(jnp.int32, (tq, 1), 0)
    t_row = q0 + lax.broadcasted_iota(jnp.int32, (1, tq), 1)
    contract_last = (((1,), (1,)), ((), ()))

    ncb = kc_ref.shape[3]
    kc = kc_ref[0, 0, 0]
    vc = vc_ref[0, 0, 0]
    cmp_end = lax.broadcasted_iota(jnp.int32, (1, ncb), 1) * CMP_STRIDE + (CMP_LEN - 1)
    valid = cmp_end <= t_col
    s_cmp = [lax.dot_general(qh[g], kc, contract_last, preferred_element_type=F32) for g in range(GQA)]
    wn = tq + WINDOW
    ks = pl.multiple_of(jnp.maximum(q0 - WINDOW, 0), tq)
    kw = kwin_ref[0, pl.ds(ks, wn), :]
    vw = vwin_ref[0, pl.ds(ks, wn), :]
    s_win = [lax.dot_general(qh[g], kw, contract_last, preferred_element_type=F32) for g in range(GQA)]
    o_cmp = []
    p_sum = None
    for g in range(GQA):
        s = jnp.where(valid, s_cmp[g], NEG)
        e = jnp.where(valid, jnp.exp(s - jnp.max(s, axis=-1, keepdims=True)), 0.0)
        p = e / jnp.maximum(jnp.sum(e, axis=-1, keepdims=True), 1e-30)
        o_cmp.append(jnp.dot(p.astype(BF16), vc, preferred_element_type=F32))
        p_sum = p if p_sum is None else p_sum + p

    wpos = ks + lax.broadcasted_iota(jnp.int32, (1, wn), 1)
    wmask = (wpos <= t_col) & (wpos > t_col - WINDOW)
    o_win = []
    for g in range(GQA):
        sw = jnp.where(wmask, s_win[g], NEG)
        pw = jnp.exp(sw - jnp.max(sw, axis=-1, keepdims=True))
        accw = jnp.dot(pw.astype(BF16), vw, preferred_element_type=F32)
        o_win.append(accw / accw[:, HEAD_DIM:HEAD_DIM + 1])

    ovt = ovt_ref[...]
    p1 = p_sum.astype(BF16)
    r1 = p_sum - p1.astype(F32)
    p2 = r1.astype(BF16)
    p3 = (r1 - p2.astype(F32)).astype(BF16)
    imp_t = (lax.dot_general(ovt, p1, contract_last, preferred_element_type=F32)
             + lax.dot_general(ovt, p2, contract_last, preferred_element_type=F32)
             + lax.dot_general(ovt, p3, contract_last, preferred_element_type=F32))

    nblk = N_SEL_BLOCKS
    blk = lax.broadcasted_iota(jnp.int32, (nblk, tq), 0)
    qblk = t_row // SEL_LEN
    causal_blk = blk <= qblk
    forced = (blk == 0) | (blk == qblk) | (blk == qblk - 1)
    score_t = jnp.where(forced, FORCED_SCORE, jnp.where(causal_blk, imp_t, NEG))
    groups = [score_t[8 * r:8 * r + 8] for r in range(nblk // 8)]
    sub = lax.broadcasted_iota(jnp.int32, (8, tq), 0)
    counts = [jnp.zeros((8, tq), F32) for _ in groups]
    for b in range(nblk):
        row = score_t[b:b + 1]
        for r, grp in enumerate(groups):
            if 8 * r > b:
                beats = row >= grp
            elif 8 * r + 7 <= b:
                beats = row > grp
            else:
                beats = (row > grp) | ((row == grp) & (sub > b - 8 * r))
            counts[r] = counts[r] + jnp.where(beats, 1.0, 0.0)
    keep_t = (jnp.concatenate(counts, axis=0) < float(SEL_TOPK)) & causal_blk
    bias_t = jnp.where(keep_t, 0.0, NEG)
    bias = jnp.concatenate([jnp.zeros((LANES - nblk, tq), F32), bias_t], axis=0).T.astype(BF16)
    q_sel = [qh[g] + bias for g in range(GQA)]

    tk = SEL_TK
    m_ref[...] = jnp.full(m_ref.shape, NEG, F32)
    acc_ref[...] = jnp.zeros(acc_ref.shape, F32)

    def sel_tile(j, diagonal):
        k0 = pl.multiple_of(j * tk, tk)
        k = ksel_ref[0, pl.ds(k0, tk), :]
        v = vsel_ref[0, pl.ds(k0, tk), :]
        if diagonal:
            causal = (k0 + lax.broadcasted_iota(jnp.int32, (1, tk), 1)) <= t_col
        scores = [lax.dot_general(q_sel[g], k, contract_last, preferred_element_type=F32)
                  for g in range(GQA)]
        for g in range(GQA):
            rs = slice(g * tq, (g + 1) * tq)
            sc = scores[g]
            if diagonal:
                sc = jnp.where(causal, sc, NEG)
            m_old = m_ref[rs, :]
            m_new = jnp.maximum(m_old, jnp.max(sc, axis=-1, keepdims=True))
            p = jnp.exp(sc - jnp.concatenate([m_new] * (tk // LANES), axis=1))
            acc_ref[rs, :] = (jnp.exp(m_old - m_new) * acc_ref[rs, :]
                              + jnp.dot(p.astype(BF16), v, preferred_element_type=F32))
            m_ref[rs, :] = m_new

    def body(j, carry):
        sel_tile(j, False)
        return carry

    n_full = q0 // tk
    lax.fori_loop(0, n_full, body, 0)
    sel_tile(n_full, True)

    gates = gate_ref[0]
    heads = []
    for g in range(GQA):
        acc = acc_ref[g * tq:(g + 1) * tq, :]
        o_sel = acc / acc[:, HEAD_DIM:HEAD_DIM + 1]
        heads.append(gates[:, 3 * g:3 * g + 1] * o_cmp[g] + gates[:, 3 * g + 1:3 * g + 2] * o_sel
                     + gates[:, 3 * g + 2:3 * g + 3] * o_win[g])
    lane = lax.broadcasted_iota(jnp.int32, (tq, LANES), 1)
    low = lane < HEAD_DIM
    pair0 = jnp.where(low, heads[0], pltpu.roll(heads[1], HEAD_DIM, 1))
    pair1 = jnp.where(low, heads[2], pltpu.roll(heads[3], HEAD_DIM, 1))
    o_ref[0] = jnp.concatenate([pair0, pair1], axis=1).astype(BF16)


def _overlap_matrix_t(n_cmp):
    n = np.arange(n_cmp)[None, :]
    s = np.arange(N_SEL_BLOCKS)[:, None]
    c_start = n * CMP_STRIDE
    s_start = s * SEL_LEN
    ov = (c_start < s_start + SEL_LEN) & (c_start + CMP_LEN > s_start)
    return jnp.asarray(ov.astype(np.float32), BF16)


def nsa_attention(q, ksel, vsel, kwin, vwin, kvc, gates, tq=128):
    bsz, seq_len, _ = q.shape
    ncb = kvc.shape[3]
    assert seq_len // SEL_LEN <= N_SEL_BLOCKS and seq_len % SEL_TK == 0 and SEL_TK % tq == 0
    assert seq_len >= tq + WINDOW
    kv_spec = pl.BlockSpec((1, seq_len, LANES), lambda b, h, i: (b, 0, h))
    rows = GQA * tq
    return pl.pallas_call(
        _nsa_kernel,
        out_shape=jax.ShapeDtypeStruct((bsz, seq_len, NSA_WIDTH), BF16),
        grid=(bsz, N_KV_HEADS, seq_len // tq),
        in_specs=[pl.BlockSpec((1, tq, GQA * LANES), lambda b, h, i: (b, i, h)),
                  kv_spec, kv_spec, kv_spec, kv_spec,
                  pl.BlockSpec((1, 1, 1, ncb, LANES), lambda b, h, i: (0, b, h, 0, 0)),
                  pl.BlockSpec((1, 1, 1, ncb, LANES), lambda b, h, i: (1, b, h, 0, 0)),
                  pl.BlockSpec((1, tq, LANES), lambda b, h, i: (b, i, h)),
                  pl.BlockSpec((N_SEL_BLOCKS, ncb), lambda b, h, i: (0, 0))],
        out_specs=pl.BlockSpec((1, tq, GQA * HEAD_DIM), lambda b, h, i: (b, i, h)),
        scratch_shapes=[pltpu.VMEM((rows, LANES), F32), pltpu.VMEM((rows, LANES), F32)],
        compiler_params=_cparams("parallel", "parallel", "arbitrary"),
        name="nsa_attention",
    )(q, ksel, vsel, kwin, vwin, kvc, kvc, gates, _overlap_matrix_t(ncb))


def _conv_kernel(u_ref, w_ref, cb_ref, g_ref, b_ref, o_ref, pad_ref, *, rows):
    seq_len = u_ref.shape[1]
    halo = 32
    pad_ref[0:halo, :] = jnp.zeros((halo, CONV_WIDTH), F32)
    pad_ref[halo:halo + seq_len, :] = u_ref[0]

    first = halo - CONV_LEN + 1

    def body(c, carry):
        r0 = pl.multiple_of(c * rows, rows)
        win = pad_ref[pl.ds(r0, rows + halo), :]
        acc = jnp.zeros((rows, CONV_WIDTH), F32)
        for sub in range(8):
            shifted = win if sub == 0 else pltpu.roll(win, rows + halo - sub, 0)
            for j in range(CONV_LEN):
                if (first + j) % 8 == sub:
                    a0 = first + j - sub
                    acc = acc + w_ref[j:j + 1, :] * shifted[a0:a0 + rows]
        y = _layer_norm(acc + cb_ref[...], g_ref[...], b_ref[...])
        o_ref[0, pl.ds(r0, rows), :] = (y * _sigmoid(y)).astype(BF16)
        return carry

    lax.fori_loop(0, seq_len // rows, body, 0)


def conformer_conv(u, conv_w, conv_b, ln_g, ln_b, rows=64):
    bsz, seq_len, _ = u.shape
    vec = pl.BlockSpec((1, CONV_WIDTH), lambda b: (0, 0))
    return pl.pallas_call(
        functools.partial(_conv_kernel, rows=rows),
        out_shape=jax.ShapeDtypeStruct((bsz, seq_len, CONV_WIDTH), BF16),
        grid=(bsz,),
        in_specs=[pl.BlockSpec((1, seq_len, CONV_WIDTH), lambda b: (b, 0, 0)),
                  pl.BlockSpec((CONV_LEN, CONV_WIDTH), lambda b: (0, 0)), vec, vec, vec],
        out_specs=pl.BlockSpec((1, seq_len, CONV_WIDTH), lambda b: (b, 0, 0)),
        scratch_shapes=[pltpu.VMEM((seq_len + 32, CONV_WIDTH), F32)],
        compiler_params=_cparams("parallel"),
        name="conformer_conv",
    )(u, conv_w, conv_b.reshape(1, -1), ln_g.reshape(1, -1), ln_b.reshape(1, -1))


def _s5_operators(a_re, a_im, log_dt, b_re, b_im, c_re, c_im, n_chunks):
    tc = S5_CHUNK
    dt = jnp.exp(log_dt.astype(F32))[:, None]
    lr = a_re.astype(F32) * dt
    li = a_im.astype(F32) * dt
    mag = jnp.exp(lr)
    ab_re = mag * jnp.cos(li)
    ab_im = mag * jnp.sin(li)
    den = a_re * a_re + a_im * a_im
    coef_re = ((ab_re - 1.0) * a_re + ab_im * a_im) / den
    coef_im = (ab_im * a_re - (ab_re - 1.0) * a_im) / den
    bb_re = coef_re[..., None] * b_re - coef_im[..., None] * b_im
    bb_im = coef_re[..., None] * b_im + coef_im[..., None] * b_re

    def power(tau):
        tau = jnp.asarray(tau, F32)
        m = jnp.exp(lr[..., None] * tau)
        return m * jnp.cos(li[..., None] * tau), m * jnp.sin(li[..., None] * tau)

    lag_re, lag_im = power(jnp.arange(tc + 1))
    ca_re = c_re[..., None] * lag_re[:, None] - c_im[..., None] * lag_im[:, None]
    ca_im = c_re[..., None] * lag_im[:, None] + c_im[..., None] * lag_re[:, None]
    kern = (jnp.einsum('gcnt,gnd->gtcd', ca_re[..., :tc], bb_re, precision=HIGHEST)
            - jnp.einsum('gcnt,gnd->gtcd', ca_im[..., :tc], bb_im, precision=HIGHEST))
    s_idx = np.arange(tc)[:, None]
    i_idx = np.arange(tc)[None, :]
    lag = np.clip(i_idx - s_idx, 0, tc - 1)
    toep = kern[:, lag]
    toep = jnp.where(jnp.asarray(i_idx >= s_idx)[None, :, :, None, None], toep, 0.0)
    rev_re, rev_im = lag_re[..., tc - 1::-1], lag_im[..., tc - 1::-1]
    bop_re = (rev_re[..., None] * bb_re[:, :, None] - rev_im[..., None] * bb_im[:, :, None])
    bop_im = (rev_re[..., None] * bb_im[:, :, None] + rev_im[..., None] * bb_re[:, :, None])
    bop_re = bop_re.transpose(0, 2, 3, 1)
    bop_im = bop_im.transpose(0, 2, 3, 1)
    cop_re = ca_re[..., 1:].transpose(0, 2, 3, 1)
    cop_im = -ca_im[..., 1:].transpose(0, 2, 3, 1)

    gh = S5_GROUPS // S5_HALVES
    eye = jnp.eye(gh, dtype=F32)
    hw = tc * gh * S5_GROUP_CH
    sw = gh * S5_STATE
    split = lambda x: x.reshape((S5_HALVES, gh) + x.shape[1:])
    m_nat = jnp.einsum('hgsiod,gk->hsgdiko', split(toep), eye).reshape(S5_HALVES, hw, hw)
    b_nat = jnp.concatenate(
        [jnp.einsum('hgsdn,gk->hsgdkn', split(bop), eye).reshape(S5_HALVES, hw, sw) for bop in (bop_re, bop_im)],
        axis=2)
    c_nat = jnp.concatenate(
        [jnp.einsum('hgnio,gk->hgniko', split(cop), eye).reshape(S5_HALVES, sw, hw) for cop in (cop_re, cop_im)],
        axis=1)
    levels = max(1, int(math.log2(n_chunks)))
    lv_re, lv_im = power(tc * (2.0 ** jnp.arange(levels)))
    lanes = lambda x: split(x).transpose(0, 3, 1, 2).reshape(S5_HALVES, levels, sw)
    a_lv = jnp.concatenate([lanes(lv_re), lanes(lv_im)], axis=-1)
    return m_nat.astype(BF16), b_nat.astype(BF16), c_nat.astype(BF16), a_lv


def _s5_kernel(u_ref, m_ref, b_ref, c_ref, a_ref, d_ref, gw_ref, gb_ref, o_ref, sre_ref, sim_ref):
    nc = u_ref.shape[1]
    tc = S5_CHUNK
    hl = S5_WIDTH // S5_HALVES
    sw = sre_ref.shape[1]
    y_half = []
    for h in range(S5_HALVES):
        uh = jnp.concatenate(
            [u_ref[0, :, s * S5_WIDTH + h * hl:s * S5_WIDTH + (h + 1) * hl] for s in range(tc)],
            axis=1).astype(BF16)
        v = jnp.dot(uh, b_ref[h], preferred_element_type=F32)
        zeros = jnp.zeros((nc, sw), F32)
        sre_ref[0:nc, :] = zeros
        sim_ref[0:nc, :] = zeros
        sre_ref[nc:2 * nc, :] = v[:, 0:sw]
        sim_ref[nc:2 * nc, :] = v[:, sw:2 * sw]
        for lv in range(a_ref.shape[1]):
            d = 1 << lv
            ar = a_ref[h, lv:lv + 1, 0:sw]
            ai = a_ref[h, lv:lv + 1, sw:2 * sw]
            pr = sre_ref[nc - d:2 * nc - d, :]
            pi = sim_ref[nc - d:2 * nc - d, :]
            cr = sre_ref[nc:2 * nc, :]
            ci = sim_ref[nc:2 * nc, :]
            sre_ref[nc:2 * nc, :] = cr + ar * pr - ai * pi
            sim_ref[nc:2 * nc, :] = ci + ar * pi + ai * pr
        prev_re = sre_ref[nc - 1:2 * nc - 1, :].astype(BF16)
        prev_im = sim_ref[nc - 1:2 * nc - 1, :].astype(BF16)
        y_half.append(jnp.dot(uh, m_ref[h], preferred_element_type=F32)
                      + jnp.dot(prev_re, c_ref[h, 0:sw, :], preferred_element_type=F32)
                      + jnp.dot(prev_im, c_ref[h, sw:2 * sw, :], preferred_element_type=F32))
    for i in range(tc):
        cols = slice(i * S5_WIDTH, (i + 1) * S5_WIDTH)
        y = jnp.concatenate([yh[:, i * hl:(i + 1) * hl] for yh in y_half], axis=1)
        z = _gelu_tanh(y + d_ref[...] * u_ref[0, :, cols])
        gate = jnp.dot(z.astype(BF16), gw_ref[...], preferred_element_type=F32) + gb_ref[...]
        o_ref[0, :, cols] = (z * _sigmoid(gate)).astype(BF16)


def s5_layer(zs, operators, d_skip, glu_w, glu_b):
    m_nat, b_nat, c_nat, a_lv = operators
    bsz, seq_len, _ = zs.shape
    tc = S5_CHUNK
    nc = seq_len // tc
    assert nc & (nc - 1) == 0 and a_lv.shape[1] == int(math.log2(nc))
    cw = tc * S5_WIDTH
    sw = a_lv.shape[2] // 2
    full = lambda a: pl.BlockSpec(a.shape, lambda b: (0,) * a.ndim)
    vec = pl.BlockSpec((1, S5_WIDTH), lambda b: (0, 0))
    glu_wb = glu_w.astype(BF16)
    out = pl.pallas_call(
        _s5_kernel,
        out_shape=jax.ShapeDtypeStruct((bsz, nc, cw), BF16),
        grid=(bsz,),
        in_specs=[pl.BlockSpec((1, nc, cw), lambda b: (b, 0, 0)),
                  full(m_nat), full(b_nat), full(c_nat), full(a_lv), vec, full(glu_wb), vec],
        out_specs=pl.BlockSpec((1, nc, cw), lambda b: (b, 0, 0)),
        scratch_shapes=[pltpu.VMEM((2 * nc, sw), F32), pltpu.VMEM((2 * nc, sw), F32)],
        compiler_params=_cparams("parallel"),
        name="s5_layer",
    )(zs.reshape(bsz, nc, cw), m_nat, b_nat, c_nat, a_lv, d_skip.reshape(1, -1), glu_wb, glu_b.reshape(1, -1))
    return out.reshape(bsz * seq_len, S5_WIDTH)


def _out_proj_kernel(x_ref, a_ref, c_ref, s_ref, w_ref, g_ref, b_ref, o_ref, *, alpha):
    h = jnp.dot(a_ref[...], w_ref[0:NSA_WIDTH, :], preferred_element_type=F32)
    h = h + jnp.dot(c_ref[...], w_ref[NSA_WIDTH:NSA_WIDTH + CONV_WIDTH, :], preferred_element_type=F32)
    h = h + jnp.dot(s_ref[...], w_ref[NSA_WIDTH + CONV_WIDTH:, :], preferred_element_type=F32)
    o_ref[...] = _layer_norm(alpha * x_ref[...] + h, g_ref[...], b_ref[...])


def out_proj_ln(x2, o_nsa, o_conv, o_s5, w_out, g, b, alpha, tm=512):
    t, d = x2.shape
    row = lambda width: pl.BlockSpec((tm, width), lambda i: (i, 0))
    vec = pl.BlockSpec((1, d), lambda i: (0, 0))
    return pl.pallas_call(
        functools.partial(_out_proj_kernel, alpha=alpha),
        out_shape=jax.ShapeDtypeStruct((t, d), F32),
        grid=(t // tm,),
        in_specs=[row(d), row(NSA_WIDTH), row(CONV_WIDTH), row(S5_WIDTH),
                  pl.BlockSpec(w_out.shape, lambda i: (0, 0)), vec, vec],
        out_specs=row(d),
        compiler_params=_cparams("parallel"),
        name="out_proj_ln",
    )(x2, o_nsa, o_conv, o_s5, w_out.astype(BF16), g.reshape(1, -1), b.reshape(1, -1))


def _ffn_kernel(te_ref, nu_ref, x_ref, w1_ref, w3_ref, w2_ref, *rest, alpha):
    ln_refs, (o_ref, acc_ref) = rest[:-2], rest[-2:]
    i = pl.program_id(0)
    f = pl.program_id(1)
    last = pl.num_programs(1) - 1
    used = i < nu_ref[0]

    @pl.when(used)
    def _():
        xb = x_ref[...].astype(BF16)
        h1 = jnp.dot(xb, w1_ref[0], preferred_element_type=F32)
        h3 = jnp.dot(xb, w3_ref[0], preferred_element_type=F32)
        h = (h1 * _sigmoid(h1) * h3).astype(BF16)
        part = jnp.dot(h, w2_ref[0], preferred_element_type=F32)

        @pl.when(f == 0)
        def _():
            acc_ref[...] = part

        @pl.when(f > 0)
        def _():
            acc_ref[...] = acc_ref[...] + part

        @pl.when(f == last)
        def _():
            if ln_refs:
                o_ref[...] = _layer_norm(alpha * x_ref[...] + acc_ref[...], ln_refs[0][...], ln_refs[1][...])
            else:
                o_ref[...] = acc_ref[...].astype(o_ref.dtype)

    @pl.when(jnp.logical_not(used) & (f == last))
    def _():
        o_ref[...] = jnp.zeros(o_ref.shape, o_ref.dtype)


def grouped_swiglu(xs, tile_expert, n_used, w1, w3, w2, tm, tf, out_dtype, ln=None, alpha=1.0):
    p, d = xs.shape
    ff = w1.shape[2]
    assert p % tm == 0 and ff % tf == 0
    vec = pl.BlockSpec((1, d), lambda i, f, te, nu: (0, 0))
    ln_args = () if ln is None else (ln[0].reshape(1, d), ln[1].reshape(1, d))
    grid_spec = pltpu.PrefetchScalarGridSpec(
        num_scalar_prefetch=2,
        grid=(p // tm, ff // tf),
        in_specs=[pl.BlockSpec((tm, d), lambda i, f, te, nu: (i, 0)),
                  pl.BlockSpec((1, d, tf), lambda i, f, te, nu: (te[i], 0, f)),
                  pl.BlockSpec((1, d, tf), lambda i, f, te, nu: (te[i], 0, f)),
                  pl.BlockSpec((1, tf, d), lambda i, f, te, nu: (te[i], f, 0))] + [vec] * len(ln_args),
        out_specs=pl.BlockSpec((tm, d), lambda i, f, te, nu: (i, 0)),
        scratch_shapes=[pltpu.VMEM((tm, d), F32)],
    )
    return pl.pallas_call(
        functools.partial(_ffn_kernel, alpha=alpha),
        out_shape=jax.ShapeDtypeStruct((p, d), out_dtype),
        grid_spec=grid_spec,
        compiler_params=_cparams("arbitrary", "arbitrary"),
        name="grouped_swiglu",
    )(tile_expert, n_used, xs, w1, w3, w2, *ln_args)


def _pick_tf(ff, target=1024):
    best = LANES
    for cand in range(LANES, ff + 1, LANES):
        if ff % cand == 0 and cand <= target:
            best = cand
    return best


def _residual_ln_kernel(x_ref, a_ref, b_ref, wa_ref, wb_ref, g_ref, beta_ref, o_ref, *, alpha):
    f = wa_ref[...] * a_ref[...] + wb_ref[...] * b_ref[...]
    o_ref[...] = _layer_norm(alpha * x_ref[...] + f, g_ref[...], beta_ref[...])


def residual_ln(x2, a, b, wa, wb, g, beta, alpha, tm=512):
    t, d = x2.shape
    row = pl.BlockSpec((tm, d), lambda i: (i, 0))
    col = pl.BlockSpec((tm, 1), lambda i: (i, 0))
    vec = pl.BlockSpec((1, d), lambda i: (0, 0))
    return pl.pallas_call(
        functools.partial(_residual_ln_kernel, alpha=alpha),
        out_shape=jax.ShapeDtypeStruct((t, d), F32),
        grid=(t // tm,),
        in_specs=[row, row, row, col, col, vec, vec],
        out_specs=row,
        compiler_params=_cparams("parallel"),
        name="residual_ln",
    )(x2, a, b, wa, wb, g.reshape(1, -1), beta.reshape(1, -1))


def _router_kernel(x_ref, w_ref, o_ref):
    logits = jnp.dot(x_ref[...], w_ref[...], preferred_element_type=F32, precision=HIGHEST)
    lane = lax.broadcasted_iota(jnp.int32, logits.shape, 1)
    lane_f = lane.astype(F32)
    logits = jnp.where(lane < N_EXPERTS, logits, -jnp.inf)
    v1 = jnp.max(logits, axis=-1, keepdims=True)
    i1 = jnp.min(jnp.where(logits == v1, lane_f, 1e9), axis=-1, keepdims=True)
    rest = jnp.where(lane_f == i1, -jnp.inf, logits)
    v2 = jnp.max(rest, axis=-1, keepdims=True)
    i2 = jnp.min(jnp.where(rest == v2, lane_f, 1e9), axis=-1, keepdims=True)
    e2 = jnp.exp(v2 - v1)
    den = 1.0 + e2
    out = jnp.where(lane == 0, i1, jnp.where(lane == 1, i2, jnp.where(lane == 2, 1.0 / den, e2 / den)))
    o_ref[...] = jnp.where(lane < 4, out, 0.0)


def moe_route(x2, router, tm=512):
    t, d = x2.shape
    w = jnp.pad(router, ((0, 0), (0, LANES - router.shape[1])))
    return pl.pallas_call(
        _router_kernel,
        out_shape=jax.ShapeDtypeStruct((t, LANES), F32),
        grid=(t // tm,),
        in_specs=[pl.BlockSpec((tm, d), lambda i: (i, 0)), pl.BlockSpec((d, LANES), lambda i: (0, 0))],
        out_specs=pl.BlockSpec((tm, LANES), lambda i: (i, 0)),
        compiler_params=_cparams("parallel"),
        name="moe_route",
    )(x2, w)


def moe_swiglu_ln(x2, router, w1, w3, w2, g, beta, alpha, tm):
    t, d = x2.shape
    n_exp = w1.shape[0]
    routed = moe_route(x2, router)
    idx = routed[:, 0:2].astype(jnp.int32)
    wts = routed[:, 2:4]
    e_flat = idx.T.reshape(-1)
    onehot = (e_flat[:, None] == jnp.arange(n_exp, dtype=jnp.int32)[None, :]).astype(jnp.int32)
    running = jnp.cumsum(onehot, axis=0)
    counts = running[-1]
    rank = jnp.sum(onehot * running, axis=1) - 1
    padded = ((counts + tm - 1) // tm) * tm
    pad_end = jnp.cumsum(padded)
    pad_start = pad_end - padded
    start = jnp.cumsum(counts) - counts
    pos = pad_start[e_flat] + rank
    n_rows = 2 * t + n_exp * tm
    n_tiles = n_rows // tm
    tile_start = jnp.arange(n_tiles, dtype=jnp.int32) * tm
    tile_expert = jnp.minimum(jnp.sum(tile_start[:, None] >= pad_end[None, :], axis=1), n_exp - 1).astype(jnp.int32)
    n_used = (pad_end[-1] // tm).astype(jnp.int32).reshape(1)
    order = jnp.argsort(e_flat, stable=True)
    row = jnp.arange(n_rows, dtype=jnp.int32)
    row_expert = jnp.repeat(tile_expert, tm)
    slot = jnp.minimum(start[row_expert] + (row - pad_start[row_expert]), 2 * t - 1)
    src = order[slot] % t
    xs = x2.astype(BF16)[src]
    ys = grouped_swiglu(xs, tile_expert, n_used, w1, w3, w2, tm, _pick_tf(w1.shape[2]), BF16)
    return residual_ln(x2, ys[pos[:t]], ys[pos[t:]], wts[:, 0:1], wts[:, 1:2], g, beta, alpha)


def dense_swiglu_ln(x2, w1, w3, w2, g, beta, alpha, tm):
    t = x2.shape[0]
    tile_expert = jnp.zeros((t // tm,), jnp.int32)
    n_used = jnp.full((1,), t // tm, jnp.int32)
    return grouped_swiglu(x2, tile_expert, n_used, w1[None], w3[None], w2[None], tm, _pick_tf(w1.shape[1], 1408),
                          F32, ln=(g, beta), alpha=alpha)


def _ple_kernel(x_ref, p_ref, wg_ref, bg_ref, wp_ref, g_ref, beta_ref, o_ref, *, alpha):
    x = x_ref[...]
    gate = _sigmoid(jnp.dot(x.astype(BF16), wg_ref[...], preferred_element_type=F32) + bg_ref[...])
    e = jnp.dot(p_ref[...].astype(BF16), wp_ref[...], preferred_element_type=F32) * gate
    o_ref[...] = _layer_norm(alpha * x + e, g_ref[...], beta_ref[...])


def ple_ln(x2, p2, gate_w, gate_b, proj, g, beta, alpha, tm=512):
    t, d = x2.shape
    pd = p2.shape[1]
    vec = pl.BlockSpec((1, d), lambda i: (0, 0))
    return pl.pallas_call(
        functools.partial(_ple_kernel, alpha=alpha),
        out_shape=jax.ShapeDtypeStruct((t, d), F32),
        grid=(t // tm,),
        in_specs=[pl.BlockSpec((tm, d), lambda i: (i, 0)), pl.BlockSpec((tm, pd), lambda i: (i, 0)),
                  pl.BlockSpec((d, d), lambda i: (0, 0)), vec,
                  pl.BlockSpec((pd, d), lambda i: (0, 0)), vec, vec],
        out_specs=pl.BlockSpec((tm, d), lambda i: (i, 0)),
        compiler_params=_cparams("parallel"),
        name="ple_ln",
    )(x2, p2, gate_w.astype(BF16), gate_b.reshape(1, -1), proj.astype(BF16), g.reshape(1, -1), beta.reshape(1, -1))


def hybrid_mixer_ln(x2, bsz, seq_len, tables, w_in, w_out, cmp_weights, conv_params, s5_params,
                    g, beta, alpha):
    (q, ksel, kwin, vsel, vwin, kcmp, vcmp, gates, u_conv, zs) = in_proj(x2, _widen_w_in(w_in), tables, seq_len)
    b3 = lambda a: a.reshape(bsz, seq_len, a.shape[1])
    kv_cmp = jnp.stack([b3(kcmp), b3(vcmp)])
    kvc = compress(kv_cmp, _compress_weights(*cmp_weights))
    o_nsa = nsa_attention(b3(q), b3(ksel), b3(vsel), b3(kwin), b3(vwin), kvc, b3(gates))
    o_conv = conformer_conv(b3(u_conv), *conv_params)
    (a_re, a_im, log_dt, b_re, b_im, c_re, c_im, d_skip, glu_w, glu_b) = s5_params
    ops = _s5_operators(a_re, a_im, log_dt, b_re, b_im, c_re, c_im, seq_len // S5_CHUNK)
    o_s5 = s5_layer(b3(zs), ops, d_skip, glu_w, glu_b)
    return out_proj_ln(x2, o_nsa.reshape(-1, NSA_WIDTH), o_conv.reshape(-1, CONV_WIDTH), o_s5, w_out,
                       g, beta, alpha)


def kernel(x, p, positions, w_in, w_out, cmp_pe, cmp_w1, cmp_b1, cmp_w2, cmp_b2, conv_w, conv_b, conv_ln_g, conv_ln_b, s5_a_re, s5_a_im, s5_log_dt, s5_b_re, s5_b_im, s5_c_re, s5_c_im, s5_d, s5_glu_w, s5_glu_b, ffn_w1, ffn_w3, ffn_w2, moe_router, moe_w1, moe_w3, moe_w2, ple_gate_w, ple_gate_b, ple_proj, ln_g, ln_b):
    bsz, seq_len, d_model = x.shape
    depth = w_in.shape[0]
    alpha = (2 * depth) ** 0.25
    t = bsz * seq_len
    tables = rope_tables(positions)
    x2 = x.reshape(t, d_model)
    dense_tm = 512
    moe_tm = 512 if t % 1024 else 1024
    for i in range(depth):
        x2 = hybrid_mixer_ln(
            x2, bsz, seq_len, tables, w_in[i], w_out[i],
            (cmp_pe[i], cmp_w1[i], cmp_b1[i], cmp_w2[i], cmp_b2[i]),
            (conv_w[i], conv_b[i], conv_ln_g[i], conv_ln_b[i]),
            (s5_a_re[i], s5_a_im[i], s5_log_dt[i], s5_b_re[i], s5_b_im[i], s5_c_re[i], s5_c_im[i],
             s5_d[i], s5_glu_w[i], s5_glu_b[i]),
            ln_g[i, 0], ln_b[i, 0], alpha)
        j = i // 2
        if i % 2 == 0:
            x2 = dense_swiglu_ln(x2, ffn_w1[j].astype(BF16), ffn_w3[j].astype(BF16), ffn_w2[j].astype(BF16),
                                 ln_g[i, 1], ln_b[i, 1], alpha, dense_tm)
        else:
            x2 = moe_swiglu_ln(x2, moe_router[j], moe_w1[j].astype(BF16), moe_w3[j].astype(BF16),
                               moe_w2[j].astype(BF16), ln_g[i, 1], ln_b[i, 1], alpha, moe_tm)
        x2 = ple_ln(x2, p[i].reshape(t, -1), ple_gate_w[i], ple_gate_b[i], ple_proj[i],
                    ln_g[i, 2], ln_b[i, 2], alpha)
    return x2.reshape(bsz, seq_len, d_model)
```

```python
import functools
import math

import numpy as np
import jax
import jax.numpy as jnp
from jax import lax
from jax.experimental import pallas as pl
from jax.experimental.pallas import tpu as pltpu

F32 = jnp.float32
BF16 = jnp.bfloat16
HIGHEST = lax.Precision.HIGHEST

LANES = 128
VMEM_LIMIT = 56 * 1024 * 1024

HEAD_DIM = 64
HALF_DIM = HEAD_DIM // 2
N_Q_HEADS = 8
N_KV_HEADS = 2
GQA = N_Q_HEADS // N_KV_HEADS
CMP_LEN = 32
CMP_STRIDE = 16
CMP_HIDDEN = 256
SEL_LEN = 64
SEL_TOPK = 16
WINDOW = 512
ROPE_THETA = 10000.0
FORCED_SCORE = 1e9
NEG = -1e30
CONV_WIDTH = 256
CONV_LEN = 31
S5_WIDTH = 256
S5_GROUP_CH = 16
S5_GROUPS = 16
S5_STATE = 64
S5_CHUNK = 8
S5_HALVES = 2
N_EXPERTS = 8
LN_EPS = 1e-5
NSA_WIDTH = N_Q_HEADS * HEAD_DIM

C_Q = 0
C_KV = 512
C_GATE = 1280
C_CONVA = 1536
C_CONVB = 1792
C_S5 = 2048
C_TOTAL = 2304
Q_SCALE = HEAD_DIM ** -0.5 * math.log2(math.e)


def _cparams(*sem):
    return pltpu.CompilerParams(dimension_semantics=sem, vmem_limit_bytes=VMEM_LIMIT)


def _layer_norm(v, g, b):
    mu = jnp.mean(v, axis=-1, keepdims=True)
    d = v - mu
    var = jnp.mean(d * d, axis=-1, keepdims=True)
    return d * lax.rsqrt(var + LN_EPS) * g + b


def _gelu_tanh(x):
    return 0.5 * x * (1.0 + jnp.tanh(math.sqrt(2.0 / math.pi) * (x + 0.044715 * (x * x * x))))


def _sigmoid(x):
    return 1.0 / (1.0 + jnp.exp(-x))


def _rope_table_kernel(pos_ref, freq_ref, tab_ref):
    ang = pos_ref[...] * freq_ref[...]
    c = jnp.cos(ang)
    s = jnp.sin(ang)
    lane = lax.broadcasted_iota(jnp.int32, ang.shape, 1)
    first_half = (lane % HEAD_DIM) < HALF_DIM
    tab_ref[:, 0:LANES] = c
    tab_ref[:, LANES:2 * LANES] = jnp.where(first_half, -s, 0.0)
    tab_ref[:, 2 * LANES:3 * LANES] = jnp.where(first_half, 0.0, s)


def rope_tables(positions, tm=512):
    t = positions.size
    pos = positions.reshape(t, 1).astype(F32)
    inv_freq = ROPE_THETA ** (-jnp.arange(0, HEAD_DIM, 2, dtype=F32) / HEAD_DIM)
    freq = jnp.tile(inv_freq, LANES // HALF_DIM).reshape(1, LANES)
    return pl.pallas_call(
        _rope_table_kernel,
        out_shape=jax.ShapeDtypeStruct((t, 3 * LANES), F32),
        grid=(t // tm,),
        in_specs=[pl.BlockSpec((tm, 1), lambda i: (i, 0)),
                  pl.BlockSpec((1, LANES), lambda i: (0, 0))],
        out_specs=pl.BlockSpec((tm, 3 * LANES), lambda i: (i, 0)),
        compiler_params=_cparams("parallel"),
        name="rope_tables",
    )(pos, freq)


def _in_proj_kernel(x_ref, w_ref, tab_ref, q_ref, ksel_ref, kwin_ref, vsel_ref, vwin_ref,
                    kcmp_ref, vcmp_ref, gate_ref, u_ref, zs_ref, *, seq_len):
    tm = x_ref.shape[0]
    xb = x_ref[...].astype(BF16)
    cos = tab_ref[:, 0:LANES]
    sin_a = tab_ref[:, LANES:2 * LANES]
    sin_b = tab_ref[:, 2 * LANES:3 * LANES]

    def proj(c0):
        z = jnp.dot(xb, w_ref[:, c0:c0 + 2 * LANES], preferred_element_type=F32)
        return z[:, 0:LANES], z[:, LANES:2 * LANES]

    def rope(z):
        return (z * cos + pltpu.roll(z, LANES - HALF_DIM, 1) * sin_a
                + pltpu.roll(z, HALF_DIM, 1) * sin_b)

    lane = lax.broadcasted_iota(jnp.int32, (tm, LANES), 1)
    row = lax.broadcasted_iota(jnp.int32, (tm, LANES), 0)
    low = lane < HEAD_DIM
    t_seq = (pl.program_id(0) * tm) % seq_len + row
    blk_onehot = jnp.where(lane == HEAD_DIM + t_seq // SEL_LEN, 1.0, 0.0)
    ones_lane = jnp.where(lane == HEAD_DIM, 1.0, 0.0)

    def spread(z, extra, out_ref, c0):
        out_ref[:, c0:c0 + LANES] = (jnp.where(low, z, 0.0) + extra).astype(BF16)
        out_ref[:, c0 + LANES:c0 + 2 * LANES] = (jnp.where(low, pltpu.roll(z, HEAD_DIM, 1), 0.0) + extra).astype(BF16)

    for pair in range(N_Q_HEADS // 4):
        for n, z in enumerate(proj(C_Q + pair * 2 * LANES)):
            spread(rope(z) * Q_SCALE, 0.0, q_ref, (2 * pair + n) * 2 * LANES)
    k_cmp, v_cmp = proj(C_KV)
    kcmp_ref[...] = rope(k_cmp)
    vcmp_ref[...] = v_cmp
    k_sel, v_sel = proj(C_KV + 2 * LANES)
    spread(rope(k_sel), blk_onehot, ksel_ref, 0)
    spread(v_sel, ones_lane, vsel_ref, 0)
    k_win, v_win = proj(C_KV + 4 * LANES)
    spread(rope(k_win), 0.0, kwin_ref, 0)
    spread(v_win, ones_lane, vwin_ref, 0)
    g0, g1 = proj(C_GATE)
    gate_ref[:, 0:LANES] = _sigmoid(g0)
    gate_ref[:, LANES:2 * LANES] = _sigmoid(g1)
    a0, a1 = proj(C_CONVA)
    b0, b1 = proj(C_CONVB)
    u_ref[:, 0:LANES] = a0 * _sigmoid(b0)
    u_ref[:, LANES:2 * LANES] = a1 * _sigmoid(b1)
    s0, s1 = proj(C_S5)
    zs_ref[:, 0:LANES] = s0
    zs_ref[:, LANES:2 * LANES] = s1


def _widen_w_in(w_in):
    d = w_in.shape[0]
    o_gate = NSA_WIDTH + 6 * N_KV_HEADS * HEAD_DIM
    o_conv = o_gate + 3 * N_Q_HEADS
    per_group = 3 * GQA
    cols = [w_in[:, :o_gate]]
    for h in range(N_KV_HEADS):
        cols += [w_in[:, o_gate + h * per_group:o_gate + (h + 1) * per_group],
                 jnp.zeros((d, LANES - per_group), w_in.dtype)]
    cols += [w_in[:, o_conv:]]
    w = jnp.concatenate(cols, axis=1)
    assert w.shape[1] == C_TOTAL
    return w.astype(BF16)


def in_proj(x2, w_wide, tables, seq_len, tm=512):
    t, d = x2.shape
    row = lambda width: pl.BlockSpec((tm, width), lambda i: (i, 0))
    out_shape = (
        jax.ShapeDtypeStruct((t, N_Q_HEADS * LANES), BF16),
        jax.ShapeDtypeStruct((t, N_KV_HEADS * LANES), BF16),
        jax.ShapeDtypeStruct((t, N_KV_HEADS * LANES), BF16),
        jax.ShapeDtypeStruct((t, N_KV_HEADS * LANES), BF16),
        jax.ShapeDtypeStruct((t, N_KV_HEADS * LANES), BF16),
        jax.ShapeDtypeStruct((t, LANES), F32),
        jax.ShapeDtypeStruct((t, LANES), F32),
        jax.ShapeDtypeStruct((t, 2 * LANES), F32),
        jax.ShapeDtypeStruct((t, CONV_WIDTH), F32),
        jax.ShapeDtypeStruct((t, S5_WIDTH), F32),
    )
    return pl.pallas_call(
        functools.partial(_in_proj_kernel, seq_len=seq_len),
        out_shape=out_shape,
        grid=(t // tm,),
        in_specs=[row(d),
                  pl.BlockSpec((d, C_TOTAL), lambda i: (0, 0)),
                  row(3 * LANES)],
        out_specs=tuple(row(s.shape[1]) for s in out_shape),
        compiler_params=_cparams("parallel"),
        name="in_proj",
    )(x2, w_wide, tables)


def _compress_kernel(x_ref, pe_ref, w1_ref, b1_ref, w2_ref, b2_ref, o_ref, shift_ref):
    nb = x_ref.shape[2]
    x = x_ref[0, 0]
    xa = (x + pe_ref[0, 0:1, :]).astype(BF16)
    xb = (x + pe_ref[0, 1:2, :]).astype(BF16)
    ha = jnp.dot(xa, w1_ref[0, 0], preferred_element_type=F32)
    hb = jnp.dot(xb, w1_ref[0, 1], preferred_element_type=F32)
    shift_ref[0:nb, :] = hb
    shift_ref[nb:nb + 8, :] = jnp.zeros((8, hb.shape[1]), F32)
    h = ha + shift_ref[1:nb + 1, :] + b1_ref[0]
    g = _gelu_tanh(h).astype(BF16)
    for hd in range(N_KV_HEADS):
        gh = g[:, hd * CMP_HIDDEN:(hd + 1) * CMP_HIDDEN]
        o_ref[0, 0, hd] = (jnp.dot(gh, w2_ref[0], preferred_element_type=F32) + b2_ref[0]).astype(BF16)


def _compress_weights(cmp_pe, cmp_w1, cmp_b1, cmp_w2, cmp_b2):
    half = CMP_LEN // 2
    pe = cmp_pe.reshape(2, 2, half, 1, HEAD_DIM)
    pe = jnp.broadcast_to(pe, (2, 2, half, N_KV_HEADS, HEAD_DIM)).reshape(2, 2, half * N_KV_HEADS * HEAD_DIM)
    w1 = cmp_w1.reshape(2, 2, half, HEAD_DIM, CMP_HIDDEN)
    z = jnp.zeros_like(w1)
    w_h0 = jnp.stack([w1, z], axis=3)
    w_h1 = jnp.stack([z, w1], axis=3)
    w1w = jnp.concatenate([w_h0, w_h1], axis=-1)
    w1w = w1w.reshape(2, 2, half * N_KV_HEADS * HEAD_DIM, N_KV_HEADS * CMP_HIDDEN).astype(BF16)
    b1 = jnp.tile(cmp_b1, (1, N_KV_HEADS)).reshape(2, 1, N_KV_HEADS * CMP_HIDDEN)
    w2 = jnp.pad(cmp_w2, ((0, 0), (0, 0), (0, LANES - HEAD_DIM))).astype(BF16)
    b2 = jnp.pad(cmp_b2, ((0, 0), (0, LANES - HEAD_DIM))).reshape(2, 1, LANES)
    return pe, w1w, b1, w2, b2


def compress(kv_cmp, weights):
    pe, w1w, b1, w2, b2 = weights
    _, bsz, seq_len, _ = kv_cmp.shape
    nb = seq_len // CMP_STRIDE
    cw = CMP_STRIDE * LANES
    x = kv_cmp.reshape(2, bsz, nb, cw)
    hid = N_KV_HEADS * CMP_HIDDEN
    return pl.pallas_call(
        _compress_kernel,
        out_shape=jax.ShapeDtypeStruct((2, bsz, N_KV_HEADS, nb, LANES), BF16),
        grid=(2, bsz),
        in_specs=[pl.BlockSpec((1, 1, nb, cw), lambda j, b: (j, b, 0, 0)),
                  pl.BlockSpec((1, 2, cw), lambda j, b: (j, 0, 0)),
                  pl.BlockSpec((1, 2, cw, hid), lambda j, b: (j, 0, 0, 0)),
                  pl.BlockSpec((1, 1, hid), lambda j, b: (j, 0, 0)),
                  pl.BlockSpec((1, CMP_HIDDEN, LANES), lambda j, b: (j, 0, 0)),
                  pl.BlockSpec((1, 1, LANES), lambda j, b: (j, 0, 0))],
        out_specs=pl.BlockSpec((1, 1, N_KV_HEADS, nb, LANES), lambda j, b: (j, b, 0, 0, 0)),
        scratch_shapes=[pltpu.VMEM((nb + 8, hid), F32)],
        compiler_params=_cparams("parallel", "parallel"),
        name="compress",
    )(x, pe, w1w, b1, w2, b2)


SEL_TK = 512
NSA_TQ = 128
NSA_SUBS = 2
N_SEL_BLOCKS = LANES - HEAD_DIM


def _nsa_kernel(q_ref, ksel_ref, vsel_ref, kwin_ref, vwin_ref, kc_ref, vc_ref, gate_ref, ovt_ref,
                o_ref, m_ref, acc_ref):
    tq = NSA_TQ
    subs = range(q_ref.shape[1] // tq)
    units = [(a, g) for a in subs for g in range(GQA)]
    i = pl.program_id(2)
    q0 = i * q_ref.shape[1]
    qh = {(a, g): q_ref[0, a * tq:(a + 1) * tq, g * LANES:(g + 1) * LANES] for a, g in units}
    t_col = [q0 + a * tq + lax.broadcasted_iota(jnp.int32, (tq, 1), 0) for a in subs]
    t_row = [q0 + a * tq + lax.broadcasted_iota(jnp.int32, (1, tq), 1) for a in subs]
    contract_last = (((1,), (1,)), ((), ()))

    ncb = kc_ref.shape[3]
    kc = kc_ref[0, 0, 0]
    vc = vc_ref[0, 0, 0]
    s_cmp = {u: lax.dot_general(qh[u], kc, contract_last, preferred_element_type=F32) for u in units}
    wn = tq + WINDOW
    ks = [pl.multiple_of(jnp.maximum(q0 + a * tq - WINDOW, 0), tq) for a in subs]
    kw = [kwin_ref[0, pl.ds(ks[a], wn), :] for a in subs]
    vw = [vwin_ref[0, pl.ds(ks[a], wn), :] for a in subs]
    s_win = {(a, g): lax.dot_general(qh[a, g], kw[a], contract_last, preferred_element_type=F32)
             for a, g in units}

    cmp_end = lax.broadcasted_iota(jnp.int32, (1, ncb), 1) * CMP_STRIDE + (CMP_LEN - 1)
    o_cmp = {}
    p_sum = [None for _ in subs]
    for a, g in units:
        valid = cmp_end <= t_col[a]
        s = jnp.where(valid, s_cmp[a, g], NEG)
        e = jnp.where(valid, jnp.exp2(s - jnp.max(s, axis=-1, keepdims=True)), 0.0)
        p = e / jnp.maximum(jnp.sum(e, axis=-1, keepdims=True), 1e-30)
        o_cmp[a, g] = jnp.dot(p.astype(BF16), vc, preferred_element_type=F32)
        p_sum[a] = p if p_sum[a] is None else p_sum[a] + p

    o_win = {}
    for a, g in units:
        wpos = ks[a] + lax.broadcasted_iota(jnp.int32, (1, wn), 1)
        wmask = (wpos <= t_col[a]) & (wpos > t_col[a] - WINDOW)
        sw = jnp.where(wmask, s_win[a, g], NEG)
        pw = jnp.exp2(sw - jnp.max(sw, axis=-1, keepdims=True))
        accw = jnp.dot(pw.astype(BF16), vw[a], preferred_element_type=F32)
        o_win[a, g] = accw / accw[:, HEAD_DIM:HEAD_DIM + 1]

    ovt = ovt_ref[...]
    nblk = N_SEL_BLOCKS
    q_sel = {}
    for a in subs:
        p1 = p_sum[a].astype(BF16)
        r1 = p_sum[a] - p1.astype(F32)
        p2 = r1.astype(BF16)
        p3 = (r1 - p2.astype(F32)).astype(BF16)
        imp_t = (lax.dot_general(ovt, p1, contract_last, preferred_element_type=F32)
                 + lax.dot_general(ovt, p2, contract_last, preferred_element_type=F32)
                 + lax.dot_general(ovt, p3, contract_last, preferred_element_type=F32))

        blk = lax.broadcasted_iota(jnp.int32, (nblk, tq), 0)
        qblk = t_row[a] // SEL_LEN
        causal_blk = blk <= qblk
        forced = (blk == 0) | (blk == qblk) | (blk == qblk - 1)
        score_t = jnp.where(forced, FORCED_SCORE, jnp.where(causal_blk, imp_t, NEG))
        groups = [score_t[8 * r:8 * r + 8] for r in range(nblk // 8)]
        sub = lax.broadcasted_iota(jnp.int32, (8, tq), 0)
        later = [jnp.where(sub > s, 1.0, 0.0) for s in range(8)]
        counts = [jnp.zeros((8, tq), F32) for _ in groups]
        for b in range(nblk):
            row = score_t[b:b + 1]
            for r, grp in enumerate(groups):
                if 8 * r > b:
                    counts[r] = counts[r] + jnp.where(row >= grp, 1.0, 0.0)
                elif 8 * r + 7 <= b:
                    counts[r] = counts[r] + jnp.where(row > grp, 1.0, 0.0)
                else:
                    counts[r] = (counts[r] + jnp.where(row > grp, 1.0, 0.0)
                                 + jnp.where(row == grp, later[b - 8 * r], 0.0))
        keep_t = (jnp.concatenate(counts, axis=0) < float(SEL_TOPK)) & causal_blk
        bias_t = jnp.where(keep_t, 0.0, NEG)
        bias = jnp.concatenate([jnp.zeros((LANES - nblk, tq), F32), bias_t], axis=0).T.astype(BF16)
        for g in range(GQA):
            q_sel[a, g] = qh[a, g] + bias

    tk = SEL_TK
    m_ref[...] = jnp.full(m_ref.shape, NEG, F32)
    acc_ref[...] = jnp.zeros(acc_ref.shape, F32)

    def sel_tile(j, diagonal):
        k0 = pl.multiple_of(j * tk, tk)
        k = ksel_ref[0, pl.ds(k0, tk), :]
        v = vsel_ref[0, pl.ds(k0, tk), :]
        scores = {u: lax.dot_general(q_sel[u], k, contract_last, preferred_element_type=F32)
                  for u in units}
        for n, (a, g) in enumerate(units):
            rs = slice(n * tq, (n + 1) * tq)
            sc = scores[a, g]
            if diagonal:
                sc = jnp.where((k0 + lax.broadcasted_iota(jnp.int32, (1, tk), 1)) <= t_col[a], sc, NEG)
            m_old = m_ref[rs, :]
            m_new = jnp.maximum(m_old, jnp.max(sc, axis=-1, keepdims=True))
            p = jnp.exp2(sc - jnp.concatenate([m_new] * (tk // LANES), axis=1))
            acc_ref[rs, :] = (jnp.exp2(m_old - m_new) * acc_ref[rs, :]
                              + jnp.dot(p.astype(BF16), v, preferred_element_type=F32))
            m_ref[rs, :] = m_new

    def body(j, carry):
        sel_tile(j, False)
        return carry

    n_full = q0 // tk
    lax.fori_loop(0, n_full, body, 0)
    sel_tile(n_full, True)

    lane = lax.broadcasted_iota(jnp.int32, (tq, LANES), 1)
    low = lane < HEAD_DIM
    for a in subs:
        gates = gate_ref[0, a * tq:(a + 1) * tq, :]
        heads = []
        for g in range(GQA):
            n = a * GQA + g
            acc = acc_ref[n * tq:(n + 1) * tq, :]
            o_sel = acc / acc[:, HEAD_DIM:HEAD_DIM + 1]
            heads.append(gates[:, 3 * g:3 * g + 1] * o_cmp[a, g] + gates[:, 3 * g + 1:3 * g + 2] * o_sel
                         + gates[:, 3 * g + 2:3 * g + 3] * o_win[a, g])
        pair0 = jnp.where(low, heads[0], pltpu.roll(heads[1], HEAD_DIM, 1))
        pair1 = jnp.where(low, heads[2], pltpu.roll(heads[3], HEAD_DIM, 1))
        o_ref[0, a * tq:(a + 1) * tq, :] = jnp.concatenate([pair0, pair1], axis=1).astype(BF16)


def _overlap_matrix_t(n_cmp):
    n = np.arange(n_cmp)[None, :]
    s = np.arange(N_SEL_BLOCKS)[:, None]
    c_start = n * CMP_STRIDE
    s_start = s * SEL_LEN
    ov = (c_start < s_start + SEL_LEN) & (c_start + CMP_LEN > s_start)
    return jnp.asarray(ov.astype(np.float32), BF16)


def nsa_attention(q, ksel, vsel, kwin, vwin, kvc, gates):
    bsz, seq_len, _ = q.shape
    ncb = kvc.shape[3]
    tq = NSA_SUBS * NSA_TQ
    assert seq_len // SEL_LEN <= N_SEL_BLOCKS and seq_len % SEL_TK == 0 and SEL_TK % tq == 0
    assert seq_len >= NSA_TQ + WINDOW
    kv_spec = pl.BlockSpec((1, seq_len, LANES), lambda b, h, i: (b, 0, h))
    rows = GQA * tq
    return pl.pallas_call(
        _nsa_kernel,
        out_shape=jax.ShapeDtypeStruct((bsz, seq_len, NSA_WIDTH), BF16),
        grid=(bsz, N_KV_HEADS, seq_len // tq),
        in_specs=[pl.BlockSpec((1, tq, GQA * LANES), lambda b, h, i: (b, i, h)),
                  kv_spec, kv_spec, kv_spec, kv_spec,
                  pl.BlockSpec((1, 1, 1, ncb, LANES), lambda b, h, i: (0, b, h, 0, 0)),
                  pl.BlockSpec((1, 1, 1, ncb, LANES), lambda b, h, i: (1, b, h, 0, 0)),
                  pl.BlockSpec((1, tq, LANES), lambda b, h, i: (b, i, h)),
                  pl.BlockSpec((N_SEL_BLOCKS, ncb), lambda b, h, i: (0, 0))],
        out_specs=pl.BlockSpec((1, tq, GQA * HEAD_DIM), lambda b, h, i: (b, i, h)),
        scratch_shapes=[pltpu.VMEM((rows, LANES), F32), pltpu.VMEM((rows, LANES), F32)],
        compiler_params=_cparams("parallel", "parallel", "arbitrary"),
        name="nsa_attention",
    )(q, ksel, vsel, kwin, vwin, kvc, kvc, gates, _overlap_matrix_t(ncb))


def _conv_kernel(u_ref, w_ref, cb_ref, g_ref, b_ref, o_ref, pad_ref, *, rows):
    seq_len = u_ref.shape[1]
    halo = 32
    pad_ref[0:halo, :] = jnp.zeros((halo, CONV_WIDTH), F32)
    pad_ref[halo:halo + seq_len, :] = u_ref[0]

    first = halo - CONV_LEN + 1

    def body(c, carry):
        r0 = pl.multiple_of(c * rows, rows)
        win = pad_ref[pl.ds(r0, rows + halo), :]
        acc = jnp.zeros((rows, CONV_WIDTH), F32)
        for sub in range(8):
            shifted = win if sub == 0 else pltpu.roll(win, rows + halo - sub, 0)
            for j in range(CONV_LEN):
                if (first + j) % 8 == sub:
                    a0 = first + j - sub
                    acc = acc + w_ref[j:j + 1, :] * shifted[a0:a0 + rows]
        y = _layer_norm(acc + cb_ref[...], g_ref[...], b_ref[...])
        o_ref[0, pl.ds(r0, rows), :] = (y * _sigmoid(y)).astype(BF16)
        return carry

    lax.fori_loop(0, seq_len // rows, body, 0)


def conformer_conv(u, conv_w, conv_b, ln_g, ln_b, rows=64):
    bsz, seq_len, _ = u.shape
    vec = pl.BlockSpec((1, CONV_WIDTH), lambda b: (0, 0))
    return pl.pallas_call(
        functools.partial(_conv_kernel, rows=rows),
        out_shape=jax.ShapeDtypeStruct((bsz, seq_len, CONV_WIDTH), BF16),
        grid=(bsz,),
        in_specs=[pl.BlockSpec((1, seq_len, CONV_WIDTH), lambda b: (b, 0, 0)),
                  pl.BlockSpec((CONV_LEN, CONV_WIDTH), lambda b: (0, 0)), vec, vec, vec],
        out_specs=pl.BlockSpec((1, seq_len, CONV_WIDTH), lambda b: (b, 0, 0)),
        scratch_shapes=[pltpu.VMEM((seq_len + 32, CONV_WIDTH), F32)],
        compiler_params=_cparams("parallel"),
        name="conformer_conv",
    )(u, conv_w, conv_b.reshape(1, -1), ln_g.reshape(1, -1), ln_b.reshape(1, -1))


def _s5_operators(a_re, a_im, log_dt, b_re, b_im, c_re, c_im, n_chunks):
    tc = S5_CHUNK
    dt = jnp.exp(log_dt.astype(F32))[:, None]
    lr = a_re.astype(F32) * dt
    li = a_im.astype(F32) * dt
    mag = jnp.exp(lr)
    ab_re = mag * jnp.cos(li)
    ab_im = mag * jnp.sin(li)
    den = a_re * a_re + a_im * a_im
    coef_re = ((ab_re - 1.0) * a_re + ab_im * a_im) / den
    coef_im = (ab_im * a_re - (ab_re - 1.0) * a_im) / den
    bb_re = coef_re[..., None] * b_re - coef_im[..., None] * b_im
    bb_im = coef_re[..., None] * b_im + coef_im[..., None] * b_re

    def power(tau):
        tau = jnp.asarray(tau, F32)
        m = jnp.exp(lr[..., None] * tau)
        return m * jnp.cos(li[..., None] * tau), m * jnp.sin(li[..., None] * tau)

    lag_re, lag_im = power(jnp.arange(tc + 1))
    ca_re = c_re[..., None] * lag_re[:, None] - c_im[..., None] * lag_im[:, None]
    ca_im = c_re[..., None] * lag_im[:, None] + c_im[..., None] * lag_re[:, None]
    kern = (jnp.einsum('gcnt,gnd->gtcd', ca_re[..., :tc], bb_re, precision=HIGHEST)
            - jnp.einsum('gcnt,gnd->gtcd', ca_im[..., :tc], bb_im, precision=HIGHEST))
    s_idx = np.arange(tc)[:, None]
    i_idx = np.arange(tc)[None, :]
    lag = np.clip(i_idx - s_idx, 0, tc - 1)
    toep = kern[:, lag]
    toep = jnp.where(jnp.asarray(i_idx >= s_idx)[None, :, :, None, None], toep, 0.0)
    rev_re, rev_im = lag_re[..., tc - 1::-1], lag_im[..., tc - 1::-1]
    bop_re = (rev_re[..., None] * bb_re[:, :, None] - rev_im[..., None] * bb_im[:, :, None])
    bop_im = (rev_re[..., None] * bb_im[:, :, None] + rev_im[..., None] * bb_re[:, :, None])
    bop_re = bop_re.transpose(0, 2, 3, 1)
    bop_im = bop_im.transpose(0, 2, 3, 1)
    cop_re = ca_re[..., 1:].transpose(0, 2, 3, 1)
    cop_im = -ca_im[..., 1:].transpose(0, 2, 3, 1)

    gh = S5_GROUPS // S5_HALVES
    eye = jnp.eye(gh, dtype=F32)
    hw = tc * gh * S5_GROUP_CH
    sw = gh * S5_STATE
    split = lambda x: x.reshape((S5_HALVES, gh) + x.shape[1:])
    m_nat = jnp.einsum('hgsiod,gk->hsgdiko', split(toep), eye).reshape(S5_HALVES, hw, hw)
    b_nat = jnp.concatenate(
        [jnp.einsum('hgsdn,gk->hsgdkn', split(bop), eye).reshape(S5_HALVES, hw, sw) for bop in (bop_re, bop_im)],
        axis=2)
    c_nat = jnp.concatenate(
        [jnp.einsum('hgnio,gk->hgniko', split(cop), eye).reshape(S5_HALVES, sw, hw) for cop in (cop_re, cop_im)],
        axis=1)
    levels = max(1, int(math.log2(n_chunks)))
    lv_re, lv_im = power(tc * (2.0 ** jnp.arange(levels)))
    lanes = lambda x: split(x).transpose(0, 3, 1, 2).reshape(S5_HALVES, levels, sw)
    a_lv = jnp.concatenate([lanes(lv_re), lanes(lv_im)], axis=-1)
    return m_nat.astype(BF16), b_nat.astype(BF16), c_nat.astype(BF16), a_lv


def _s5_kernel(u_ref, m_ref, b_ref, c_ref, a_ref, d_ref, gw_ref, gb_ref, o_ref, sre_ref, sim_ref):
    nc = u_ref.shape[1]
    tc = S5_CHUNK
    hl = S5_WIDTH // S5_HALVES
    sw = sre_ref.shape[1]
    y_half = []
    for h in range(S5_HALVES):
        uh = jnp.concatenate(
            [u_ref[0, :, s * S5_WIDTH + h * hl:s * S5_WIDTH + (h + 1) * hl] for s in range(tc)],
            axis=1).astype(BF16)
        v = jnp.dot(uh, b_ref[h], preferred_element_type=F32)
        zeros = jnp.zeros((nc, sw), F32)
        sre_ref[0:nc, :] = zeros
        sim_ref[0:nc, :] = zeros
        sre_ref[nc:2 * nc, :] = v[:, 0:sw]
        sim_ref[nc:2 * nc, :] = v[:, sw:2 * sw]
        for lv in range(a_ref.shape[1]):
            d = 1 << lv
            ar = a_ref[h, lv:lv + 1, 0:sw]
            ai = a_ref[h, lv:lv + 1, sw:2 * sw]
            pr = sre_ref[nc - d:2 * nc - d, :]
            pi = sim_ref[nc - d:2 * nc - d, :]
            cr = sre_ref[nc:2 * nc, :]
            ci = sim_ref[nc:2 * nc, :]
            sre_ref[nc:2 * nc, :] = cr + ar * pr - ai * pi
            sim_ref[nc:2 * nc, :] = ci + ar * pi + ai * pr
        prev_re = sre_ref[nc - 1:2 * nc - 1, :].astype(BF16)
        prev_im = sim_ref[nc - 1:2 * nc - 1, :].astype(BF16)
        y_half.append(jnp.dot(uh, m_ref[h], preferred_element_type=F32)
                      + jnp.dot(prev_re, c_ref[h, 0:sw, :], preferred_element_type=F32)
                      + jnp.dot(prev_im, c_ref[h, sw:2 * sw, :], preferred_element_type=F32))
    for i in range(tc):
        cols = slice(i * S5_WIDTH, (i + 1) * S5_WIDTH)
        y = jnp.concatenate([yh[:, i * hl:(i + 1) * hl] for yh in y_half], axis=1)
        z = _gelu_tanh(y + d_ref[...] * u_ref[0, :, cols])
        gate = jnp.dot(z.astype(BF16), gw_ref[...], preferred_element_type=F32) + gb_ref[...]
        o_ref[0, :, cols] = (z * _sigmoid(gate)).astype(BF16)


def s5_layer(zs, operators, d_skip, glu_w, glu_b):
    m_nat, b_nat, c_nat, a_lv = operators
    bsz, seq_len, _ = zs.shape
    tc = S5_CHUNK
    nc = seq_len // tc
    assert nc & (nc - 1) == 0 and a_lv.shape[1] == int(math.log2(nc))
    cw = tc * S5_WIDTH
    sw = a_lv.shape[2] // 2
    full = lambda a: pl.BlockSpec(a.shape, lambda b: (0,) * a.ndim)
    vec = pl.BlockSpec((1, S5_WIDTH), lambda b: (0, 0))
    glu_wb = glu_w.astype(BF16)
    out = pl.pallas_call(
        _s5_kernel,
        out_shape=jax.ShapeDtypeStruct((bsz, nc, cw), BF16),
        grid=(bsz,),
        in_specs=[pl.BlockSpec((1, nc, cw), lambda b: (b, 0, 0)),
                  full(m_nat), full(b_nat), full(c_nat), full(a_lv), vec, full(glu_wb), vec],
        out_specs=pl.BlockSpec((1, nc, cw), lambda b: (b, 0, 0)),
        scratch_shapes=[pltpu.VMEM((2 * nc, sw), F32), pltpu.VMEM((2 * nc, sw), F32)],
        compiler_params=_cparams("parallel"),
        name="s5_layer",
    )(zs.reshape(bsz, nc, cw), m_nat, b_nat, c_nat, a_lv, d_skip.reshape(1, -1), glu_wb, glu_b.reshape(1, -1))
    return out.reshape(bsz * seq_len, S5_WIDTH)


def _out_proj_kernel(x_ref, a_ref, c_ref, s_ref, w_ref, g_ref, b_ref, o_ref, *, alpha):
    h = jnp.dot(a_ref[...], w_ref[0:NSA_WIDTH, :], preferred_element_type=F32)
    h = h + jnp.dot(c_ref[...], w_ref[NSA_WIDTH:NSA_WIDTH + CONV_WIDTH, :], preferred_element_type=F32)
    h = h + jnp.dot(s_ref[...], w_ref[NSA_WIDTH + CONV_WIDTH:, :], preferred_element_type=F32)
    o_ref[...] = _layer_norm(alpha * x_ref[...] + h, g_ref[...], b_ref[...])


def out_proj_ln(x2, o_nsa, o_conv, o_s5, w_out, g, b, alpha, tm=512):
    t, d = x2.shape
    row = lambda width: pl.BlockSpec((tm, width), lambda i: (i, 0))
    vec = pl.BlockSpec((1, d), lambda i: (0, 0))
    return pl.pallas_call(
        functools.partial(_out_proj_kernel, alpha=alpha),
        out_shape=jax.ShapeDtypeStruct((t, d), F32),
        grid=(t // tm,),
        in_specs=[row(d), row(NSA_WIDTH), row(CONV_WIDTH), row(S5_WIDTH),
                  pl.BlockSpec(w_out.shape, lambda i: (0, 0)), vec, vec],
        out_specs=row(d),
        compiler_params=_cparams("parallel"),
        name="out_proj_ln",
    )(x2, o_nsa, o_conv, o_s5, w_out.astype(BF16), g.reshape(1, -1), b.reshape(1, -1))


def _ffn_kernel(te_ref, nu_ref, x_ref, w1_ref, w3_ref, w2_ref, *rest, alpha):
    ln_refs, (o_ref, acc_ref) = rest[:-2], rest[-2:]
    i = pl.program_id(0)
    f = pl.program_id(1)
    last = pl.num_programs(1) - 1
    used = i < nu_ref[0]

    @pl.when(used)
    def _():
        xb = x_ref[...].astype(BF16)
        h1 = jnp.dot(xb, w1_ref[0], preferred_element_type=F32)
        h3 = jnp.dot(xb, w3_ref[0], preferred_element_type=F32)
        h = (h1 * _sigmoid(h1) * h3).astype(BF16)
        part = jnp.dot(h, w2_ref[0], preferred_element_type=F32)

        @pl.when(f == 0)
        def _():
            acc_ref[...] = part

        @pl.when(f > 0)
        def _():
            acc_ref[...] = acc_ref[...] + part

        @pl.when(f == last)
        def _():
            if ln_refs:
                o_ref[...] = _layer_norm(alpha * x_ref[...] + acc_ref[...], ln_refs[0][...], ln_refs[1][...])
            else:
                o_ref[...] = acc_ref[...].astype(o_ref.dtype)

    @pl.when(jnp.logical_not(used) & (f == last))
    def _():
        o_ref[...] = jnp.zeros(o_ref.shape, o_ref.dtype)


def grouped_swiglu(xs, tile_expert, n_used, w1, w3, w2, tm, tf, out_dtype, ln=None, alpha=1.0):
    p, d = xs.shape
    ff = w1.shape[2]
    assert p % tm == 0 and ff % tf == 0
    vec = pl.BlockSpec((1, d), lambda i, f, te, nu: (0, 0))
    ln_args = () if ln is None else (ln[0].reshape(1, d), ln[1].reshape(1, d))
    grid_spec = pltpu.PrefetchScalarGridSpec(
        num_scalar_prefetch=2,
        grid=(p // tm, ff // tf),
        in_specs=[pl.BlockSpec((tm, d), lambda i, f, te, nu: (i, 0)),
                  pl.BlockSpec((1, d, tf), lambda i, f, te, nu: (te[i], 0, f)),
                  pl.BlockSpec((1, d, tf), lambda i, f, te, nu: (te[i], 0, f)),
                  pl.BlockSpec((1, tf, d), lambda i, f, te, nu: (te[i], f, 0))] + [vec] * len(ln_args),
        out_specs=pl.BlockSpec((tm, d), lambda i, f, te, nu: (i, 0)),
        scratch_shapes=[pltpu.VMEM((tm, d), F32)],
    )
    return pl.pallas_call(
        functools.partial(_ffn_kernel, alpha=alpha),
        out_shape=jax.ShapeDtypeStruct((p, d), out_dtype),
        grid_spec=grid_spec,
        compiler_params=_cparams("arbitrary", "arbitrary"),
        name="grouped_swiglu",
    )(tile_expert, n_used, xs, w1, w3, w2, *ln_args)


def _pick_tf(ff, target=1024):
    best = LANES
    for cand in range(LANES, ff + 1, LANES):
        if ff % cand == 0 and cand <= target:
            best = cand
    return best


def _residual_ln_kernel(x_ref, a_ref, b_ref, wa_ref, wb_ref, g_ref, beta_ref, o_ref, *, alpha):
    f = wa_ref[...] * a_ref[...] + wb_ref[...] * b_ref[...]
    o_ref[...] = _layer_norm(alpha * x_ref[...] + f, g_ref[...], beta_ref[...])


def residual_ln(x2, a, b, wa, wb, g, beta, alpha, tm=512):
    t, d = x2.shape
    row = pl.BlockSpec((tm, d), lambda i: (i, 0))
    col = pl.BlockSpec((tm, 1), lambda i: (i, 0))
    vec = pl.BlockSpec((1, d), lambda i: (0, 0))
    return pl.pallas_call(
        functools.partial(_residual_ln_kernel, alpha=alpha),
        out_shape=jax.ShapeDtypeStruct((t, d), F32),
        grid=(t // tm,),
        in_specs=[row, row, row, col, col, vec, vec],
        out_specs=row,
        compiler_params=_cparams("parallel"),
        name="residual_ln",
    )(x2, a, b, wa, wb, g.reshape(1, -1), beta.reshape(1, -1))


def _router_kernel(x_ref, w_ref, o_ref):
    logits = jnp.dot(x_ref[...], w_ref[...], preferred_element_type=F32, precision=HIGHEST)
    lane = lax.broadcasted_iota(jnp.int32, logits.shape, 1)
    lane_f = lane.astype(F32)
    logits = jnp.where(lane < N_EXPERTS, logits, -jnp.inf)
    v1 = jnp.max(logits, axis=-1, keepdims=True)
    i1 = jnp.min(jnp.where(logits == v1, lane_f, 1e9), axis=-1, keepdims=True)
    rest = jnp.where(lane_f == i1, -jnp.inf, logits)
    v2 = jnp.max(rest, axis=-1, keepdims=True)
    i2 = jnp.min(jnp.where(rest == v2, lane_f, 1e9), axis=-1, keepdims=True)
    e2 = jnp.exp(v2 - v1)
    den = 1.0 + e2
    out = jnp.where(lane == 0, i1, jnp.where(lane == 1, i2, jnp.where(lane == 2, 1.0 / den, e2 / den)))
    o_ref[...] = jnp.where(lane < 4, out, 0.0)


def moe_route(x2, router, tm=512):
    t, d = x2.shape
    w = jnp.pad(router, ((0, 0), (0, LANES - router.shape[1])))
    return pl.pallas_call(
        _router_kernel,
        out_shape=jax.ShapeDtypeStruct((t, LANES), F32),
        grid=(t // tm,),
        in_specs=[pl.BlockSpec((tm, d), lambda i: (i, 0)), pl.BlockSpec((d, LANES), lambda i: (0, 0))],
        out_specs=pl.BlockSpec((tm, LANES), lambda i: (i, 0)),
        compiler_params=_cparams("parallel"),
        name="moe_route",
    )(x2, w)


def moe_swiglu_ln(x2, router, w1, w3, w2, g, beta, alpha, tm):
    t, d = x2.shape
    n_exp = w1.shape[0]
    routed = moe_route(x2, router)
    idx = routed[:, 0:2].astype(jnp.int32)
    wts = routed[:, 2:4]
    e_flat = idx.T.reshape(-1)
    onehot = (e_flat[:, None] == jnp.arange(n_exp, dtype=jnp.int32)[None, :]).astype(jnp.int32)
    running = jnp.cumsum(onehot, axis=0)
    counts = running[-1]
    rank = jnp.sum(onehot * running, axis=1) - 1
    padded = ((counts + tm - 1) // tm) * tm
    pad_end = jnp.cumsum(padded)
    pad_start = pad_end - padded
    start = jnp.cumsum(counts) - counts
    pos = pad_start[e_flat] + rank
    n_rows = 2 * t + n_exp * tm
    n_tiles = n_rows // tm
    tile_start = jnp.arange(n_tiles, dtype=jnp.int32) * tm
    tile_expert = jnp.minimum(jnp.sum(tile_start[:, None] >= pad_end[None, :], axis=1), n_exp - 1).astype(jnp.int32)
    n_used = (pad_end[-1] // tm).astype(jnp.int32).reshape(1)
    order = jnp.argsort(e_flat, stable=True)
    row = jnp.arange(n_rows, dtype=jnp.int32)
    row_expert = jnp.repeat(tile_expert, tm)
    slot = jnp.minimum(start[row_expert] + (row - pad_start[row_expert]), 2 * t - 1)
    src = order[slot] % t
    xs = x2.astype(BF16)[src]
    ys = grouped_swiglu(xs, tile_expert, n_used, w1, w3, w2, tm, _pick_tf(w1.shape[2]), BF16)
    return residual_ln(x2, ys[pos[:t]], ys[pos[t:]], wts[:, 0:1], wts[:, 1:2], g, beta, alpha)


def dense_swiglu_ln(x2, w1, w3, w2, g, beta, alpha, tm):
    t = x2.shape[0]
    tile_expert = jnp.zeros((t // tm,), jnp.int32)
    n_used = jnp.full((1,), t // tm, jnp.int32)
    return grouped_swiglu(x2, tile_expert, n_used, w1[None], w3[None], w2[None], tm, _pick_tf(w1.shape[1], 1408),
                          F32, ln=(g, beta), alpha=alpha)


def _ple_kernel(x_ref, p_ref, wg_ref, bg_ref, wp_ref, g_ref, beta_ref, o_ref, *, alpha):
    x = x_ref[...]
    gate = _sigmoid(jnp.dot(x.astype(BF16), wg_ref[...], preferred_element_type=F32) + bg_ref[...])
    e = jnp.dot(p_ref[...].astype(BF16), wp_ref[...], preferred_element_type=F32) * gate
    o_ref[...] = _layer_norm(alpha * x + e, g_ref[...], beta_ref[...])


def ple_ln(x2, p2, gate_w, gate_b, proj, g, beta, alpha, tm=512):
    t, d = x2.shape
    pd = p2.shape[1]
    vec = pl.BlockSpec((1, d), lambda i: (0, 0))
    return pl.pallas_call(
        functools.partial(_ple_kernel, alpha=alpha),
        out_shape=jax.ShapeDtypeStruct((t, d), F32),
        grid=(t // tm,),
        in_specs=[pl.BlockSpec((tm, d), lambda i: (i, 0)), pl.BlockSpec((tm, pd), lambda i: (i, 0)),
                  pl.BlockSpec((d, d), lambda i: (0, 0)), vec,
                  pl.BlockSpec((pd, d), lambda i: (0, 0)), vec, vec],
        out_specs=pl.BlockSpec((tm, d), lambda i: (i, 0)),
        compiler_params=_cparams("parallel"),
        name="ple_ln",
    )(x2, p2, gate_w.astype(BF16), gate_b.reshape(1, -1), proj.astype(BF16), g.reshape(1, -1), beta.reshape(1, -1))


def hybrid_mixer_ln(x2, bsz, seq_len, tables, w_in, w_out, cmp_weights, conv_params, s5_params,
                    g, beta, alpha):
    (q, ksel, kwin, vsel, vwin, kcmp, vcmp, gates, u_conv, zs) = in_proj(x2, _widen_w_in(w_in), tables, seq_len)
    b3 = lambda a: a.reshape(bsz, seq_len, a.shape[1])
    kv_cmp = jnp.stack([b3(kcmp), b3(vcmp)])
    kvc = compress(kv_cmp, _compress_weights(*cmp_weights))
    o_nsa = nsa_attention(b3(q), b3(ksel), b3(vsel), b3(kwin), b3(vwin), kvc, b3(gates))
    o_conv = conformer_conv(b3(u_conv), *conv_params)
    (a_re, a_im, log_dt, b_re, b_im, c_re, c_im, d_skip, glu_w, glu_b) = s5_params
    ops = _s5_operators(a_re, a_im, log_dt, b_re, b_im, c_re, c_im, seq_len // S5_CHUNK)
    o_s5 = s5_layer(b3(zs), ops, d_skip, glu_w, glu_b)
    return out_proj_ln(x2, o_nsa.reshape(-1, NSA_WIDTH), o_conv.reshape(-1, CONV_WIDTH), o_s5, w_out,
                       g, beta, alpha)


def kernel(x, p, positions, w_in, w_out, cmp_pe, cmp_w1, cmp_b1, cmp_w2, cmp_b2, conv_w, conv_b, conv_ln_g, conv_ln_b, s5_a_re, s5_a_im, s5_log_dt, s5_b_re, s5_b_im, s5_c_re, s5_c_im, s5_d, s5_glu_w, s5_glu_b, ffn_w1, ffn_w3, ffn_w2, moe_router, moe_w1, moe_w3, moe_w2, ple_gate_w, ple_gate_b, ple_proj, ln_g, ln_b):
    bsz, seq_len, d_model = x.shape
    depth = w_in.shape[0]
    alpha = (2 * depth) ** 0.25
    t = bsz * seq_len
    tables = rope_tables(positions)
    x2 = x.reshape(t, d_model)
    dense_tm = 512
    moe_tm = 512 if t % 1024 else 1024
    for i in range(depth):
        x2 = hybrid_mixer_ln(
            x2, bsz, seq_len, tables, w_in[i], w_out[i],
            (cmp_pe[i], cmp_w1[i], cmp_b1[i], cmp_w2[i], cmp_b2[i]),
            (conv_w[i], conv_b[i], conv_ln_g[i], conv_ln_b[i]),
            (s5_a_re[i], s5_a_im[i], s5_log_dt[i], s5_b_re[i], s5_b_im[i], s5_c_re[i], s5_c_im[i],
             s5_d[i], s5_glu_w[i], s5_glu_b[i]),
            ln_g[i, 0], ln_b[i, 0], alpha)
        j = i // 2
        if i % 2 == 0:
            x2 = dense_swiglu_ln(x2, ffn_w1[j].astype(BF16), ffn_w3[j].astype(BF16), ffn_w2[j].astype(BF16),
                                 ln_g[i, 1], ln_b[i, 1], alpha, dense_tm)
        else:
            x2 = moe_swiglu_ln(x2, moe_router[j], moe_w1[j].astype(BF16), moe_w3[j].astype(BF16),
                               moe_w2[j].astype(BF16), ln_g[i, 1], ln_b[i, 1], alpha, moe_tm)
        x2 = ple_ln(x2, p[i].reshape(t, -1), ple_gate_w[i], ple_gate_b[i], ple_proj[i],
                    ln_g[i, 2], ln_b[i, 2], alpha)
    return x2.reshape(bsz, seq_len, d_model)
```

```python
import functools
import math

import numpy as np
import jax
import jax.numpy as jnp
from jax import lax
from jax.experimental import pallas as pl
from jax.experimental.pallas import tpu as pltpu

F32 = jnp.float32
BF16 = jnp.bfloat16
HIGHEST = lax.Precision.HIGHEST

LANES = 128
VMEM_LIMIT = 56 * 1024 * 1024

HEAD_DIM = 64
HALF_DIM = HEAD_DIM // 2
N_Q_HEADS = 8
N_KV_HEADS = 2
GQA = N_Q_HEADS // N_KV_HEADS
CMP_LEN = 32
CMP_STRIDE = 16
CMP_HIDDEN = 256
SEL_LEN = 64
SEL_TOPK = 16
WINDOW = 512
ROPE_THETA = 10000.0
FORCED_SCORE = 1e9
NEG = -1e30
CONV_WIDTH = 256
CONV_LEN = 31
S5_WIDTH = 256
S5_GROUP_CH = 16
S5_GROUPS = 16
S5_STATE = 64
S5_CHUNK = 8
S5_HALVES = 2
N_EXPERTS = 8
LN_EPS = 1e-5
NSA_WIDTH = N_Q_HEADS * HEAD_DIM

C_Q = 0
C_KV = 512
C_GATE = 1280
C_CONVA = 1536
C_CONVB = 1792
C_S5 = 2048
C_TOTAL = 2304
Q_SCALE = HEAD_DIM ** -0.5 * math.log2(math.e)


def _cparams(*sem):
    return pltpu.CompilerParams(dimension_semantics=sem, vmem_limit_bytes=VMEM_LIMIT)


def _layer_norm(v, g, b):
    mu = jnp.mean(v, axis=-1, keepdims=True)
    d = v - mu
    var = jnp.mean(d * d, axis=-1, keepdims=True)
    return d * lax.rsqrt(var + LN_EPS) * g + b


def _gelu_tanh(x):
    return 0.5 * x * (1.0 + jnp.tanh(math.sqrt(2.0 / math.pi) * (x + 0.044715 * (x * x * x))))


def _sigmoid(x):
    return 1.0 / (1.0 + jnp.exp(-x))


def _rope_table_kernel(pos_ref, freq_ref, tab_ref):
    ang = pos_ref[...] * freq_ref[...]
    c = jnp.cos(ang)
    s = jnp.sin(ang)
    lane = lax.broadcasted_iota(jnp.int32, ang.shape, 1)
    first_half = (lane % HEAD_DIM) < HALF_DIM
    tab_ref[:, 0:LANES] = c
    tab_ref[:, LANES:2 * LANES] = jnp.where(first_half, -s, 0.0)
    tab_ref[:, 2 * LANES:3 * LANES] = jnp.where(first_half, 0.0, s)


def rope_tables(positions, tm=512):
    t = positions.size
    pos = positions.reshape(t, 1).astype(F32)
    inv_freq = ROPE_THETA ** (-jnp.arange(0, HEAD_DIM, 2, dtype=F32) / HEAD_DIM)
    freq = jnp.tile(inv_freq, LANES // HALF_DIM).reshape(1, LANES)
    return pl.pallas_call(
        _rope_table_kernel,
        out_shape=jax.ShapeDtypeStruct((t, 3 * LANES), F32),
        grid=(t // tm,),
        in_specs=[pl.BlockSpec((tm, 1), lambda i: (i, 0)),
                  pl.BlockSpec((1, LANES), lambda i: (0, 0))],
        out_specs=pl.BlockSpec((tm, 3 * LANES), lambda i: (i, 0)),
        compiler_params=_cparams("parallel"),
        name="rope_tables",
    )(pos, freq)


def _in_proj_kernel(x_ref, w_ref, tab_ref, q_ref, ksel_ref, kwin_ref, vsel_ref, vwin_ref,
                    kvcmp_ref, gate_ref, u_ref, zs_ref, *, seq_len):
    tm = x_ref.shape[0]
    xb = x_ref[...].astype(BF16)
    cos = tab_ref[:, 0:LANES]
    sin_a = tab_ref[:, LANES:2 * LANES]
    sin_b = tab_ref[:, 2 * LANES:3 * LANES]

    def proj(c0):
        z = jnp.dot(xb, w_ref[:, c0:c0 + 2 * LANES], preferred_element_type=F32)
        return z[:, 0:LANES], z[:, LANES:2 * LANES]

    def rope(z):
        return (z * cos + pltpu.roll(z, LANES - HALF_DIM, 1) * sin_a
                + pltpu.roll(z, HALF_DIM, 1) * sin_b)

    lane = lax.broadcasted_iota(jnp.int32, (tm, LANES), 1)
    row = lax.broadcasted_iota(jnp.int32, (tm, LANES), 0)
    low = lane < HEAD_DIM
    t_seq = (pl.program_id(0) * tm) % seq_len + row
    blk_onehot = jnp.where(lane == HEAD_DIM + t_seq // SEL_LEN, 1.0, 0.0)
    ones_lane = jnp.where(lane == HEAD_DIM, 1.0, 0.0)

    def spread(z, extra, out_ref, c0):
        out_ref[:, c0:c0 + LANES] = (jnp.where(low, z, 0.0) + extra).astype(BF16)
        out_ref[:, c0 + LANES:c0 + 2 * LANES] = (jnp.where(low, pltpu.roll(z, HEAD_DIM, 1), 0.0) + extra).astype(BF16)

    for pair in range(N_Q_HEADS // 4):
        for n, z in enumerate(proj(C_Q + pair * 2 * LANES)):
            spread(rope(z) * Q_SCALE, 0.0, q_ref, (2 * pair + n) * 2 * LANES)
    k_cmp, v_cmp = proj(C_KV)
    kvcmp_ref[0] = rope(k_cmp)
    kvcmp_ref[1] = v_cmp
    k_sel, v_sel = proj(C_KV + 2 * LANES)
    spread(rope(k_sel), blk_onehot, ksel_ref, 0)
    spread(v_sel, ones_lane, vsel_ref, 0)
    k_win, v_win = proj(C_KV + 4 * LANES)
    spread(rope(k_win), 0.0, kwin_ref, 0)
    spread(v_win, ones_lane, vwin_ref, 0)
    g0, g1 = proj(C_GATE)
    gate_ref[:, 0:LANES] = _sigmoid(g0)
    gate_ref[:, LANES:2 * LANES] = _sigmoid(g1)
    a0, a1 = proj(C_CONVA)
    b0, b1 = proj(C_CONVB)
    u_ref[:, 0:LANES] = a0 * _sigmoid(b0)
    u_ref[:, LANES:2 * LANES] = a1 * _sigmoid(b1)
    s0, s1 = proj(C_S5)
    zs_ref[:, 0:LANES] = s0
    zs_ref[:, LANES:2 * LANES] = s1


def _widen_w_in(w_in):
    d = w_in.shape[0]
    o_gate = NSA_WIDTH + 6 * N_KV_HEADS * HEAD_DIM
    o_conv = o_gate + 3 * N_Q_HEADS
    per_group = 3 * GQA
    cols = [w_in[:, :o_gate]]
    for h in range(N_KV_HEADS):
        cols += [w_in[:, o_gate + h * per_group:o_gate + (h + 1) * per_group],
                 jnp.zeros((d, LANES - per_group), w_in.dtype)]
    cols += [w_in[:, o_conv:]]
    w = jnp.concatenate(cols, axis=1)
    assert w.shape[1] == C_TOTAL
    return w.astype(BF16)


def in_proj(x2, w_wide, tables, seq_len, tm=512):
    t, d = x2.shape
    row = lambda width: pl.BlockSpec((tm, width), lambda i: (i, 0))
    out_shape = (
        jax.ShapeDtypeStruct((t, N_Q_HEADS * LANES), BF16),
        jax.ShapeDtypeStruct((t, N_KV_HEADS * LANES), BF16),
        jax.ShapeDtypeStruct((t, N_KV_HEADS * LANES), BF16),
        jax.ShapeDtypeStruct((t, N_KV_HEADS * LANES), BF16),
        jax.ShapeDtypeStruct((t, N_KV_HEADS * LANES), BF16),
        jax.ShapeDtypeStruct((2, t, LANES), F32),
        jax.ShapeDtypeStruct((t, 2 * LANES), F32),
        jax.ShapeDtypeStruct((t, CONV_WIDTH), F32),
        jax.ShapeDtypeStruct((t, S5_WIDTH), F32),
    )
    out_spec = lambda s: (row(s.shape[1]) if len(s.shape) == 2
                          else pl.BlockSpec((s.shape[0], tm, s.shape[2]), lambda i: (0, i, 0)))
    return pl.pallas_call(
        functools.partial(_in_proj_kernel, seq_len=seq_len),
        out_shape=out_shape,
        grid=(t // tm,),
        in_specs=[row(d),
                  pl.BlockSpec((d, C_TOTAL), lambda i: (0, 0)),
                  row(3 * LANES)],
        out_specs=tuple(out_spec(s) for s in out_shape),
        compiler_params=_cparams("parallel"),
        name="in_proj",
    )(x2, w_wide, tables)


def _compress_kernel(x_ref, pe_ref, w1_ref, b1_ref, w2_ref, b2_ref, o_ref, shift_ref):
    nb = x_ref.shape[2] // CMP_STRIDE
    x = jnp.concatenate([x_ref[0, 0, pl.ds(s, nb, stride=CMP_STRIDE), :] for s in range(CMP_STRIDE)],
                        axis=1)
    xa = (x + pe_ref[0, 0:1, :]).astype(BF16)
    xb = (x + pe_ref[0, 1:2, :]).astype(BF16)
    ha = jnp.dot(xa, w1_ref[0, 0], preferred_element_type=F32)
    hb = jnp.dot(xb, w1_ref[0, 1], preferred_element_type=F32)
    shift_ref[0:nb, :] = hb
    shift_ref[nb:nb + 8, :] = jnp.zeros((8, hb.shape[1]), F32)
    h = ha + shift_ref[1:nb + 1, :] + b1_ref[0]
    g = _gelu_tanh(h).astype(BF16)
    for hd in range(N_KV_HEADS):
        gh = g[:, hd * CMP_HIDDEN:(hd + 1) * CMP_HIDDEN]
        o_ref[0, 0, hd] = (jnp.dot(gh, w2_ref[0], preferred_element_type=F32) + b2_ref[0]).astype(BF16)


def _compress_weights(cmp_pe, cmp_w1, cmp_b1, cmp_w2, cmp_b2):
    half = CMP_LEN // 2
    pe = cmp_pe.reshape(2, 2, half, 1, HEAD_DIM)
    pe = jnp.broadcast_to(pe, (2, 2, half, N_KV_HEADS, HEAD_DIM)).reshape(2, 2, half * N_KV_HEADS * HEAD_DIM)
    w1 = cmp_w1.reshape(2, 2, half, HEAD_DIM, CMP_HIDDEN)
    z = jnp.zeros_like(w1)
    w_h0 = jnp.stack([w1, z], axis=3)
    w_h1 = jnp.stack([z, w1], axis=3)
    w1w = jnp.concatenate([w_h0, w_h1], axis=-1)
    w1w = w1w.reshape(2, 2, half * N_KV_HEADS * HEAD_DIM, N_KV_HEADS * CMP_HIDDEN).astype(BF16)
    b1 = jnp.tile(cmp_b1, (1, N_KV_HEADS)).reshape(2, 1, N_KV_HEADS * CMP_HIDDEN)
    w2 = jnp.pad(cmp_w2, ((0, 0), (0, 0), (0, LANES - HEAD_DIM))).astype(BF16)
    b2 = jnp.pad(cmp_b2, ((0, 0), (0, LANES - HEAD_DIM))).reshape(2, 1, LANES)
    return pe, w1w, b1, w2, b2


def compress(kv_cmp, weights):
    pe, w1w, b1, w2, b2 = weights
    _, bsz, seq_len, _ = kv_cmp.shape
    nb = seq_len // CMP_STRIDE
    cw = CMP_STRIDE * LANES
    hid = N_KV_HEADS * CMP_HIDDEN
    return pl.pallas_call(
        _compress_kernel,
        out_shape=jax.ShapeDtypeStruct((2, bsz, N_KV_HEADS, nb, LANES), BF16),
        grid=(2, bsz),
        in_specs=[pl.BlockSpec((1, 1, seq_len, LANES), lambda j, b: (j, b, 0, 0)),
                  pl.BlockSpec((1, 2, cw), lambda j, b: (j, 0, 0)),
                  pl.BlockSpec((1, 2, cw, hid), lambda j, b: (j, 0, 0, 0)),
                  pl.BlockSpec((1, 1, hid), lambda j, b: (j, 0, 0)),
                  pl.BlockSpec((1, CMP_HIDDEN, LANES), lambda j, b: (j, 0, 0)),
                  pl.BlockSpec((1, 1, LANES), lambda j, b: (j, 0, 0))],
        out_specs=pl.BlockSpec((1, 1, N_KV_HEADS, nb, LANES), lambda j, b: (j, b, 0, 0, 0)),
        scratch_shapes=[pltpu.VMEM((nb + 8, hid), F32)],
        compiler_params=_cparams("parallel", "parallel"),
        name="compress",
    )(kv_cmp, pe, w1w, b1, w2, b2)


SEL_TK = 512
NSA_TQ = 128
NSA_SUBS = 2
N_SEL_BLOCKS = LANES - HEAD_DIM


def _nsa_kernel(q_ref, ksel_ref, vsel_ref, kwin_ref, vwin_ref, kc_ref, vc_ref, gate_ref, ovt_ref,
                o_ref, m_ref, acc_ref):
    tq = NSA_TQ
    subs = range(q_ref.shape[1] // tq)
    units = [(a, g) for a in subs for g in range(GQA)]
    i = pl.program_id(2)
    q0 = i * q_ref.shape[1]
    qh = {(a, g): q_ref[0, a * tq:(a + 1) * tq, g * LANES:(g + 1) * LANES] for a, g in units}
    t_col = [q0 + a * tq + lax.broadcasted_iota(jnp.int32, (tq, 1), 0) for a in subs]
    t_row = [q0 + a * tq + lax.broadcasted_iota(jnp.int32, (1, tq), 1) for a in subs]
    contract_last = (((1,), (1,)), ((), ()))

    ncb = kc_ref.shape[3]
    kc = kc_ref[0, 0, 0]
    vc = vc_ref[0, 0, 0]
    s_cmp = {u: lax.dot_general(qh[u], kc, contract_last, preferred_element_type=F32) for u in units}
    wn = tq + WINDOW
    ks = [pl.multiple_of(jnp.maximum(q0 + a * tq - WINDOW, 0), tq) for a in subs]
    kw = [kwin_ref[0, pl.ds(ks[a], wn), :] for a in subs]
    vw = [vwin_ref[0, pl.ds(ks[a], wn), :] for a in subs]
    s_win = {(a, g): lax.dot_general(qh[a, g], kw[a], contract_last, preferred_element_type=F32)
             for a, g in units}

    cmp_end = lax.broadcasted_iota(jnp.int32, (1, ncb), 1) * CMP_STRIDE + (CMP_LEN - 1)
    o_cmp = {}
    p_sum = [None for _ in subs]
    for a, g in units:
        valid = cmp_end <= t_col[a]
        s = jnp.where(valid, s_cmp[a, g], NEG)
        e = jnp.where(valid, jnp.exp2(s - jnp.max(s, axis=-1, keepdims=True)), 0.0)
        p = e / jnp.maximum(jnp.sum(e, axis=-1, keepdims=True), 1e-30)
        o_cmp[a, g] = jnp.dot(p.astype(BF16), vc, preferred_element_type=F32)
        p_sum[a] = p if p_sum[a] is None else p_sum[a] + p

    o_win = {}
    for a, g in units:
        wpos = ks[a] + lax.broadcasted_iota(jnp.int32, (1, wn), 1)
        wmask = (wpos <= t_col[a]) & (wpos > t_col[a] - WINDOW)
        sw = jnp.where(wmask, s_win[a, g], NEG)
        pw = jnp.exp2(sw - jnp.max(sw, axis=-1, keepdims=True))
        accw = jnp.dot(pw.astype(BF16), vw[a], preferred_element_type=F32)
        o_win[a, g] = accw / accw[:, HEAD_DIM:HEAD_DIM + 1]

    ovt = ovt_ref[...]
    nblk = N_SEL_BLOCKS
    q_sel = {}
    for a in subs:
        p1 = p_sum[a].astype(BF16)
        r1 = p_sum[a] - p1.astype(F32)
        p2 = r1.astype(BF16)
        p3 = (r1 - p2.astype(F32)).astype(BF16)
        imp_t = (lax.dot_general(ovt, p1, contract_last, preferred_element_type=F32)
                 + lax.dot_general(ovt, p2, contract_last, preferred_element_type=F32)
                 + lax.dot_general(ovt, p3, contract_last, preferred_element_type=F32))

        blk = lax.broadcasted_iota(jnp.int32, (nblk, tq), 0)
        qblk = t_row[a] // SEL_LEN
        causal_blk = blk <= qblk
        forced = (blk == 0) | (blk == qblk) | (blk == qblk - 1)
        score_t = jnp.where(forced, FORCED_SCORE, jnp.where(causal_blk, imp_t, NEG))
        groups = [score_t[8 * r:8 * r + 8] for r in range(nblk // 8)]
        sub = lax.broadcasted_iota(jnp.int32, (8, tq), 0)
        later = [jnp.where(sub > s, 1.0, 0.0) for s in range(8)]
        counts = [jnp.zeros((8, tq), F32) for _ in groups]
        for b in range(nblk):
            row = score_t[b:b + 1]
            for r, grp in enumerate(groups):
                if 8 * r > b:
                    counts[r] = counts[r] + jnp.where(row >= grp, 1.0, 0.0)
                elif 8 * r + 7 <= b:
                    counts[r] = counts[r] + jnp.where(row > grp, 1.0, 0.0)
                else:
                    counts[r] = (counts[r] + jnp.where(row > grp, 1.0, 0.0)
                                 + jnp.where(row == grp, later[b - 8 * r], 0.0))
        keep_t = (jnp.concatenate(counts, axis=0) < float(SEL_TOPK)) & causal_blk
        bias_t = jnp.where(keep_t, 0.0, NEG)
        bias = jnp.concatenate([jnp.zeros((LANES - nblk, tq), F32), bias_t], axis=0).T.astype(BF16)
        for g in range(GQA):
            q_sel[a, g] = qh[a, g] + bias

    tk = SEL_TK
    m_ref[...] = jnp.full(m_ref.shape, NEG, F32)
    acc_ref[...] = jnp.zeros(acc_ref.shape, F32)

    def sel_tile(j, carry):
        k0 = pl.multiple_of(j * tk, tk)
        k = ksel_ref[0, pl.ds(k0, tk), :]
        v = vsel_ref[0, pl.ds(k0, tk), :]
        scores = {u: lax.dot_general(q_sel[u], k, contract_last, preferred_element_type=F32)
                  for u in units}
        kpos = k0 + lax.broadcasted_iota(jnp.int32, (1, tk), 1)
        for n, (a, g) in enumerate(units):
            rs = slice(n * tq, (n + 1) * tq)
            sc = jnp.where(kpos <= t_col[a], scores[a, g], NEG)
            m_old = m_ref[rs, :]
            m_new = jnp.maximum(m_old, jnp.max(sc, axis=-1, keepdims=True))
            p = jnp.exp2(sc - jnp.concatenate([m_new] * (tk // LANES), axis=1))
            acc_ref[rs, :] = (jnp.exp2(m_old - m_new) * acc_ref[rs, :]
                              + jnp.dot(p.astype(BF16), v, preferred_element_type=F32))
            m_ref[rs, :] = m_new
        return carry

    lax.fori_loop(0, q0 // tk + 1, sel_tile, 0)

    lane = lax.broadcasted_iota(jnp.int32, (tq, LANES), 1)
    low = lane < HEAD_DIM
    for a in subs:
        gates = gate_ref[0, a * tq:(a + 1) * tq, :]
        heads = []
        for g in range(GQA):
            n = a * GQA + g
            acc = acc_ref[n * tq:(n + 1) * tq, :]
            o_sel = acc / acc[:, HEAD_DIM:HEAD_DIM + 1]
            heads.append(gates[:, 3 * g:3 * g + 1] * o_cmp[a, g] + gates[:, 3 * g + 1:3 * g + 2] * o_sel
                         + gates[:, 3 * g + 2:3 * g + 3] * o_win[a, g])
        pair0 = jnp.where(low, heads[0], pltpu.roll(heads[1], HEAD_DIM, 1))
        pair1 = jnp.where(low, heads[2], pltpu.roll(heads[3], HEAD_DIM, 1))
        o_ref[0, a * tq:(a + 1) * tq, :] = jnp.concatenate([pair0, pair1], axis=1).astype(BF16)


def _overlap_matrix_t(n_cmp):
    n = np.arange(n_cmp)[None, :]
    s = np.arange(N_SEL_BLOCKS)[:, None]
    c_start = n * CMP_STRIDE
    s_start = s * SEL_LEN
    ov = (c_start < s_start + SEL_LEN) & (c_start + CMP_LEN > s_start)
    return jnp.asarray(ov.astype(np.float32), BF16)


def nsa_attention(q, ksel, vsel, kwin, vwin, kvc, gates):
    bsz, seq_len, _ = q.shape
    ncb = kvc.shape[3]
    tq = NSA_SUBS * NSA_TQ
    assert seq_len // SEL_LEN <= N_SEL_BLOCKS and seq_len % SEL_TK == 0 and SEL_TK % tq == 0
    assert seq_len >= NSA_TQ + WINDOW
    kv_spec = pl.BlockSpec((1, seq_len, LANES), lambda b, h, i: (b, 0, h))
    rows = GQA * tq
    return pl.pallas_call(
        _nsa_kernel,
        out_shape=jax.ShapeDtypeStruct((bsz, seq_len, NSA_WIDTH), BF16),
        grid=(bsz, N_KV_HEADS, seq_len // tq),
        in_specs=[pl.BlockSpec((1, tq, GQA * LANES), lambda b, h, i: (b, i, h)),
                  kv_spec, kv_spec, kv_spec, kv_spec,
                  pl.BlockSpec((1, 1, 1, ncb, LANES), lambda b, h, i: (0, b, h, 0, 0)),
                  pl.BlockSpec((1, 1, 1, ncb, LANES), lambda b, h, i: (1, b, h, 0, 0)),
                  pl.BlockSpec((1, tq, LANES), lambda b, h, i: (b, i, h)),
                  pl.BlockSpec((N_SEL_BLOCKS, ncb), lambda b, h, i: (0, 0))],
        out_specs=pl.BlockSpec((1, tq, GQA * HEAD_DIM), lambda b, h, i: (b, i, h)),
        scratch_shapes=[pltpu.VMEM((rows, LANES), F32), pltpu.VMEM((rows, LANES), F32)],
        compiler_params=_cparams("parallel", "parallel", "arbitrary"),
        name="nsa_attention",
    )(q, ksel, vsel, kwin, vwin, kvc, kvc, gates, _overlap_matrix_t(ncb))


def _conv_kernel(u_ref, w_ref, cb_ref, g_ref, b_ref, o_ref, pad_ref, *, rows):
    seq_len = u_ref.shape[1]
    halo = 32
    pad_ref[0:halo, :] = jnp.zeros((halo, CONV_WIDTH), F32)
    pad_ref[halo:halo + seq_len, :] = u_ref[0]

    first = halo - CONV_LEN + 1

    def body(c, carry):
        r0 = pl.multiple_of(c * rows, rows)
        win = pad_ref[pl.ds(r0, rows + halo), :]
        acc = jnp.zeros((rows, CONV_WIDTH), F32)
        for sub in range(8):
            shifted = win if sub == 0 else pltpu.roll(win, rows + halo - sub, 0)
            for j in range(CONV_LEN):
                if (first + j) % 8 == sub:
                    a0 = first + j - sub
                    acc = acc + w_ref[j:j + 1, :] * shifted[a0:a0 + rows]
        y = _layer_norm(acc + cb_ref[...], g_ref[...], b_ref[...])
        o_ref[0, pl.ds(r0, rows), :] = (y * _sigmoid(y)).astype(BF16)
        return carry

    lax.fori_loop(0, seq_len // rows, body, 0)


def conformer_conv(u, conv_w, conv_b, ln_g, ln_b, rows=64):
    bsz, seq_len, _ = u.shape
    vec = pl.BlockSpec((1, CONV_WIDTH), lambda b: (0, 0))
    return pl.pallas_call(
        functools.partial(_conv_kernel, rows=rows),
        out_shape=jax.ShapeDtypeStruct((bsz, seq_len, CONV_WIDTH), BF16),
        grid=(bsz,),
        in_specs=[pl.BlockSpec((1, seq_len, CONV_WIDTH), lambda b: (b, 0, 0)),
                  pl.BlockSpec((CONV_LEN, CONV_WIDTH), lambda b: (0, 0)), vec, vec, vec],
        out_specs=pl.BlockSpec((1, seq_len, CONV_WIDTH), lambda b: (b, 0, 0)),
        scratch_shapes=[pltpu.VMEM((seq_len + 32, CONV_WIDTH), F32)],
        compiler_params=_cparams("parallel"),
        name="conformer_conv",
    )(u, conv_w, conv_b.reshape(1, -1), ln_g.reshape(1, -1), ln_b.reshape(1, -1))


def _s5_operators(a_re, a_im, log_dt, b_re, b_im, c_re, c_im, n_chunks):
    tc = S5_CHUNK
    dt = jnp.exp(log_dt.astype(F32))[:, None]
    lr = a_re.astype(F32) * dt
    li = a_im.astype(F32) * dt
    mag = jnp.exp(lr)
    ab_re = mag * jnp.cos(li)
    ab_im = mag * jnp.sin(li)
    den = a_re * a_re + a_im * a_im
    coef_re = ((ab_re - 1.0) * a_re + ab_im * a_im) / den
    coef_im = (ab_im * a_re - (ab_re - 1.0) * a_im) / den
    bb_re = coef_re[..., None] * b_re - coef_im[..., None] * b_im
    bb_im = coef_re[..., None] * b_im + coef_im[..., None] * b_re

    def power(tau):
        tau = jnp.asarray(tau, F32)
        m = jnp.exp(lr[..., None] * tau)
        return m * jnp.cos(li[..., None] * tau), m * jnp.sin(li[..., None] * tau)

    lag_re, lag_im = power(jnp.arange(tc + 1))
    ca_re = c_re[..., None] * lag_re[:, None] - c_im[..., None] * lag_im[:, None]
    ca_im = c_re[..., None] * lag_im[:, None] + c_im[..., None] * lag_re[:, None]
    kern = (jnp.einsum('gcnt,gnd->gtcd', ca_re[..., :tc], bb_re, precision=HIGHEST)
            - jnp.einsum('gcnt,gnd->gtcd', ca_im[..., :tc], bb_im, precision=HIGHEST))
    s_idx = np.arange(tc)[:, None]
    i_idx = np.arange(tc)[None, :]
    lag = np.clip(i_idx - s_idx, 0, tc - 1)
    toep = kern[:, lag]
    toep = jnp.where(jnp.asarray(i_idx >= s_idx)[None, :, :, None, None], toep, 0.0)
    rev_re, rev_im = lag_re[..., tc - 1::-1], lag_im[..., tc - 1::-1]
    bop_re = (rev_re[..., None] * bb_re[:, :, None] - rev_im[..., None] * bb_im[:, :, None])
    bop_im = (rev_re[..., None] * bb_im[:, :, None] + rev_im[..., None] * bb_re[:, :, None])
    bop_re = bop_re.transpose(0, 2, 3, 1)
    bop_im = bop_im.transpose(0, 2, 3, 1)
    cop_re = ca_re[..., 1:].transpose(0, 2, 3, 1)
    cop_im = -ca_im[..., 1:].transpose(0, 2, 3, 1)

    gh = S5_GROUPS // S5_HALVES
    eye = jnp.eye(gh, dtype=F32)
    hw = tc * gh * S5_GROUP_CH
    sw = gh * S5_STATE
    split = lambda x: x.reshape((S5_HALVES, gh) + x.shape[1:])
    m_nat = jnp.einsum('hgsiod,gk->hsgdiko', split(toep), eye).reshape(S5_HALVES, hw, hw)
    b_nat = jnp.concatenate(
        [jnp.einsum('hgsdn,gk->hsgdkn', split(bop), eye).reshape(S5_HALVES, hw, sw) for bop in (bop_re, bop_im)],
        axis=2)
    c_nat = jnp.concatenate(
        [jnp.einsum('hgnio,gk->hgniko', split(cop), eye).reshape(S5_HALVES, sw, hw) for cop in (cop_re, cop_im)],
        axis=1)
    levels = max(1, int(math.log2(n_chunks)))
    lv_re, lv_im = power(tc * (2.0 ** jnp.arange(levels)))
    lanes = lambda x: split(x).transpose(0, 3, 1, 2).reshape(S5_HALVES, levels, sw)
    a_lv = jnp.concatenate([lanes(lv_re), lanes(lv_im)], axis=-1)
    return m_nat.astype(BF16), b_nat.astype(BF16), c_nat.astype(BF16), a_lv


def _s5_kernel(u0_ref, u1_ref, m_ref, b_ref, c_ref, a_ref, d_ref, gw_ref, gb_ref, o0_ref, o1_ref,
               sre_ref, sim_ref):
    tc = S5_CHUNK
    nc = u0_ref.shape[1] // tc
    hl = S5_WIDTH // S5_HALVES
    sw = sre_ref.shape[1]
    u_step = [[u_ref[0, pl.ds(s, nc, stride=tc), :] for s in range(tc)] for u_ref in (u0_ref, u1_ref)]
    y_half = []
    for h in range(S5_HALVES):
        uh = jnp.concatenate(u_step[h], axis=1).astype(BF16)
        v = jnp.dot(uh, b_ref[h], preferred_element_type=F32)
        zeros = jnp.zeros((nc, sw), F32)
        sre_ref[0:nc, :] = zeros
        sim_ref[0:nc, :] = zeros
        sre_ref[nc:2 * nc, :] = v[:, 0:sw]
        sim_ref[nc:2 * nc, :] = v[:, sw:2 * sw]
        for lv in range(a_ref.shape[1]):
            d = 1 << lv
            ar = a_ref[h, lv:lv + 1, 0:sw]
            ai = a_ref[h, lv:lv + 1, sw:2 * sw]
            pr = sre_ref[nc - d:2 * nc - d, :]
            pi = sim_ref[nc - d:2 * nc - d, :]
            cr = sre_ref[nc:2 * nc, :]
            ci = sim_ref[nc:2 * nc, :]
            sre_ref[nc:2 * nc, :] = cr + ar * pr - ai * pi
            sim_ref[nc:2 * nc, :] = ci + ar * pi + ai * pr
        prev_re = sre_ref[nc - 1:2 * nc - 1, :].astype(BF16)
        prev_im = sim_ref[nc - 1:2 * nc - 1, :].astype(BF16)
        y_half.append(jnp.dot(uh, m_ref[h], preferred_element_type=F32)
                      + jnp.dot(prev_re, c_ref[h, 0:sw, :], preferred_element_type=F32)
                      + jnp.dot(prev_im, c_ref[h, sw:2 * sw, :], preferred_element_type=F32))
    for i in range(tc):
        y = jnp.concatenate([yh[:, i * hl:(i + 1) * hl] for yh in y_half], axis=1)
        z = _gelu_tanh(y + d_ref[...] * jnp.concatenate([u_step[0][i], u_step[1][i]], axis=1))
        gate = jnp.dot(z.astype(BF16), gw_ref[...], preferred_element_type=F32) + gb_ref[...]
        out = z * _sigmoid(gate)
        o0_ref[0, pl.ds(i, nc, stride=tc), :] = out[:, 0:hl]
        o1_ref[0, pl.ds(i, nc, stride=tc), :] = out[:, hl:2 * hl]


def s5_layer(zs, operators, d_skip, glu_w, glu_b):
    m_nat, b_nat, c_nat, a_lv = operators
    bsz, seq_len, _ = zs.shape
    tc = S5_CHUNK
    nc = seq_len // tc
    assert nc & (nc - 1) == 0 and a_lv.shape[1] == int(math.log2(nc))
    hl = S5_WIDTH // S5_HALVES
    sw = a_lv.shape[2] // 2
    full = lambda a: pl.BlockSpec(a.shape, lambda b: (0,) * a.ndim)
    vec = pl.BlockSpec((1, S5_WIDTH), lambda b: (0, 0))
    half = lambda h: pl.BlockSpec((1, seq_len, hl), lambda b: (b, 0, h))
    glu_wb = glu_w.astype(BF16)
    out = pl.pallas_call(
        _s5_kernel,
        out_shape=(jax.ShapeDtypeStruct((bsz, seq_len, hl), F32),) * S5_HALVES,
        grid=(bsz,),
        in_specs=[half(0), half(1), full(m_nat), full(b_nat), full(c_nat), full(a_lv), vec, full(glu_wb), vec],
        out_specs=(half(0),) * S5_HALVES,
        scratch_shapes=[pltpu.VMEM((2 * nc, sw), F32), pltpu.VMEM((2 * nc, sw), F32)],
        compiler_params=_cparams("parallel"),
        name="s5_layer",
    )(zs, zs, m_nat, b_nat, c_nat, a_lv, d_skip.reshape(1, -1), glu_wb, glu_b.reshape(1, -1))
    return tuple(o.reshape(bsz * seq_len, hl) for o in out)


def _out_proj_kernel(x_ref, a_ref, c_ref, s0_ref, s1_ref, w_ref, g_ref, b_ref, o_ref, *, alpha):
    o_s5 = jnp.concatenate([s0_ref[...], s1_ref[...]], axis=1).astype(BF16)
    h = jnp.dot(a_ref[...], w_ref[0:NSA_WIDTH, :], preferred_element_type=F32)
    h = h + jnp.dot(c_ref[...], w_ref[NSA_WIDTH:NSA_WIDTH + CONV_WIDTH, :], preferred_element_type=F32)
    h = h + jnp.dot(o_s5, w_ref[NSA_WIDTH + CONV_WIDTH:, :], preferred_element_type=F32)
    o_ref[...] = _layer_norm(alpha * x_ref[...] + h, g_ref[...], b_ref[...])


def out_proj_ln(x2, o_nsa, o_conv, o_s5, w_out, g, b, alpha, tm=512):
    t, d = x2.shape
    row = lambda width: pl.BlockSpec((tm, width), lambda i: (i, 0))
    vec = pl.BlockSpec((1, d), lambda i: (0, 0))
    return pl.pallas_call(
        functools.partial(_out_proj_kernel, alpha=alpha),
        out_shape=jax.ShapeDtypeStruct((t, d), F32),
        grid=(t // tm,),
        in_specs=[row(d), row(NSA_WIDTH), row(CONV_WIDTH), row(S5_WIDTH // S5_HALVES), row(S5_WIDTH // S5_HALVES),
                  pl.BlockSpec(w_out.shape, lambda i: (0, 0)), vec, vec],
        out_specs=row(d),
        compiler_params=_cparams("parallel"),
        name="out_proj_ln",
    )(x2, o_nsa, o_conv, *o_s5, w_out.astype(BF16), g.reshape(1, -1), b.reshape(1, -1))


def _ffn_kernel(te_ref, nu_ref, x_ref, w1_ref, w3_ref, w2_ref, *rest, alpha):
    ln_refs, (o_ref, acc_ref) = rest[:-2], rest[-2:]
    i = pl.program_id(0)
    f = pl.program_id(1)
    last = pl.num_programs(1) - 1
    used = i < nu_ref[0]

    @pl.when(used)
    def _():
        xb = x_ref[...].astype(BF16)
        h1 = jnp.dot(xb, w1_ref[0], preferred_element_type=F32)
        h3 = jnp.dot(xb, w3_ref[0], preferred_element_type=F32)
        h = (h1 * _sigmoid(h1) * h3).astype(BF16)
        part = jnp.dot(h, w2_ref[0], preferred_element_type=F32)

        @pl.when(f == 0)
        def _():
            acc_ref[...] = part

        @pl.when(f > 0)
        def _():
            acc_ref[...] = acc_ref[...] + part

        @pl.when(f == last)
        def _():
            if ln_refs:
                o_ref[...] = _layer_norm(alpha * x_ref[...] + acc_ref[...], ln_refs[0][...], ln_refs[1][...])
            else:
                o_ref[...] = acc_ref[...].astype(o_ref.dtype)

    @pl.when(jnp.logical_not(used) & (f == last))
    def _():
        o_ref[...] = jnp.zeros(o_ref.shape, o_ref.dtype)


def grouped_swiglu(xs, tile_expert, n_used, w1, w3, w2, tm, tf, out_dtype, ln=None, alpha=1.0):
    p, d = xs.shape
    ff = w1.shape[2]
    assert p % tm == 0 and ff % tf == 0
    vec = pl.BlockSpec((1, d), lambda i, f, te, nu: (0, 0))
    ln_args = () if ln is None else (ln[0].reshape(1, d), ln[1].reshape(1, d))
    grid_spec = pltpu.PrefetchScalarGridSpec(
        num_scalar_prefetch=2,
        grid=(p // tm, ff // tf),
        in_specs=[pl.BlockSpec((tm, d), lambda i, f, te, nu: (i, 0)),
                  pl.BlockSpec((1, d, tf), lambda i, f, te, nu: (te[i], 0, f)),
                  pl.BlockSpec((1, d, tf), lambda i, f, te, nu: (te[i], 0, f)),
                  pl.BlockSpec((1, tf, d), lambda i, f, te, nu: (te[i], f, 0))] + [vec] * len(ln_args),
        out_specs=pl.BlockSpec((tm, d), lambda i, f, te, nu: (i, 0)),
        scratch_shapes=[pltpu.VMEM((tm, d), F32)],
    )
    return pl.pallas_call(
        functools.partial(_ffn_kernel, alpha=alpha),
        out_shape=jax.ShapeDtypeStruct((p, d), out_dtype),
        grid_spec=grid_spec,
        compiler_params=_cparams("arbitrary", "arbitrary"),
        name="grouped_swiglu",
    )(tile_expert, n_used, xs, w1, w3, w2, *ln_args)


def _pick_tf(ff, target=1024):
    best = LANES
    for cand in range(LANES, ff + 1, LANES):
        if ff % cand == 0 and cand <= target:
            best = cand
    return best


def _residual_ln_kernel(x_ref, a_ref, b_ref, wa_ref, wb_ref, g_ref, beta_ref, o_ref, *, alpha):
    f = wa_ref[...] * a_ref[...] + wb_ref[...] * b_ref[...]
    o_ref[...] = _layer_norm(alpha * x_ref[...] + f, g_ref[...], beta_ref[...])


def residual_ln(x2, a, b, wa, wb, g, beta, alpha, tm=512):
    t, d = x2.shape
    row = pl.BlockSpec((tm, d), lambda i: (i, 0))
    col = pl.BlockSpec((tm, 1), lambda i: (i, 0))
    vec = pl.BlockSpec((1, d), lambda i: (0, 0))
    return pl.pallas_call(
        functools.partial(_residual_ln_kernel, alpha=alpha),
        out_shape=jax.ShapeDtypeStruct((t, d), F32),
        grid=(t // tm,),
        in_specs=[row, row, row, col, col, vec, vec],
        out_specs=row,
        compiler_params=_cparams("parallel"),
        name="residual_ln",
    )(x2, a, b, wa, wb, g.reshape(1, -1), beta.reshape(1, -1))


def _router_kernel(x_ref, w_ref, o_ref):
    logits = jnp.dot(x_ref[...], w_ref[...], preferred_element_type=F32, precision=HIGHEST)
    lane = lax.broadcasted_iota(jnp.int32, logits.shape, 1)
    lane_f = lane.astype(F32)
    logits = jnp.where(lane < N_EXPERTS, logits, -jnp.inf)
    v1 = jnp.max(logits, axis=-1, keepdims=True)
    i1 = jnp.min(jnp.where(logits == v1, lane_f, 1e9), axis=-1, keepdims=True)
    rest = jnp.where(lane_f == i1, -jnp.inf, logits)
    v2 = jnp.max(rest, axis=-1, keepdims=True)
    i2 = jnp.min(jnp.where(rest == v2, lane_f, 1e9), axis=-1, keepdims=True)
    e2 = jnp.exp(v2 - v1)
    den = 1.0 + e2
    out = jnp.where(lane == 0, i1, jnp.where(lane == 1, i2, jnp.where(lane == 2, 1.0 / den, e2 / den)))
    o_ref[...] = jnp.where(lane < 4, out, 0.0)


def moe_route(x2, router, tm=512):
    t, d = x2.shape
    w = jnp.pad(router, ((0, 0), (0, LANES - router.shape[1])))
    return pl.pallas_call(
        _router_kernel,
        out_shape=jax.ShapeDtypeStruct((t, LANES), F32),
        grid=(t // tm,),
        in_specs=[pl.BlockSpec((tm, d), lambda i: (i, 0)), pl.BlockSpec((d, LANES), lambda i: (0, 0))],
        out_specs=pl.BlockSpec((tm, LANES), lambda i: (i, 0)),
        compiler_params=_cparams("parallel"),
        name="moe_route",
    )(x2, w)


def moe_swiglu_ln(x2, router, w1, w3, w2, g, beta, alpha, tm):
    t, d = x2.shape
    n_exp = w1.shape[0]
    routed = moe_route(x2, router)
    idx = routed[:, 0:2].astype(jnp.int32)
    wts = routed[:, 2:4]
    e_flat = idx.T.reshape(-1)
    onehot = (e_flat[:, None] == jnp.arange(n_exp, dtype=jnp.int32)[None, :]).astype(jnp.int32)
    running = jnp.cumsum(onehot, axis=0)
    counts = running[-1]
    rank = jnp.sum(onehot * running, axis=1) - 1
    padded = ((counts + tm - 1) // tm) * tm
    pad_end = jnp.cumsum(padded)
    pad_start = pad_end - padded
    start = jnp.cumsum(counts) - counts
    pos = pad_start[e_flat] + rank
    n_rows = 2 * t + n_exp * tm
    n_tiles = n_rows // tm
    tile_start = jnp.arange(n_tiles, dtype=jnp.int32) * tm
    tile_expert = jnp.minimum(jnp.sum(tile_start[:, None] >= pad_end[None, :], axis=1), n_exp - 1).astype(jnp.int32)
    n_used = (pad_end[-1] // tm).astype(jnp.int32).reshape(1)
    order = jnp.argsort(e_flat, stable=True)
    row = jnp.arange(n_rows, dtype=jnp.int32)
    row_expert = jnp.repeat(tile_expert, tm)
    slot = jnp.minimum(start[row_expert] + (row - pad_start[row_expert]), 2 * t - 1)
    src = order[slot] % t
    xs = x2.astype(BF16)[src]
    ys = grouped_swiglu(xs, tile_expert, n_used, w1, w3, w2, tm, _pick_tf(w1.shape[2]), BF16)
    return residual_ln(x2, ys[pos[:t]], ys[pos[t:]], wts[:, 0:1], wts[:, 1:2], g, beta, alpha)


def dense_swiglu_ln(x2, w1, w3, w2, g, beta, alpha, tm):
    t = x2.shape[0]
    tile_expert = jnp.zeros((t // tm,), jnp.int32)
    n_used = jnp.full((1,), t // tm, jnp.int32)
    return grouped_swiglu(x2, tile_expert, n_used, w1[None], w3[None], w2[None], tm, _pick_tf(w1.shape[1], 1408),
                          F32, ln=(g, beta), alpha=alpha)


def _ple_kernel(x_ref, p_ref, wg_ref, bg_ref, wp_ref, g_ref, beta_ref, o_ref, *, alpha):
    x = x_ref[...]
    gate = _sigmoid(jnp.dot(x.astype(BF16), wg_ref[...], preferred_element_type=F32) + bg_ref[...])
    e = jnp.dot(p_ref[...].astype(BF16), wp_ref[...], preferred_element_type=F32) * gate
    o_ref[...] = _layer_norm(alpha * x + e, g_ref[...], beta_ref[...])


def ple_ln(x2, p_all, layer, gate_w, gate_b, proj, g, beta, alpha, tm=512):
    t, d = x2.shape
    pd = p_all.shape[1]
    first = layer * (t // tm)
    vec = pl.BlockSpec((1, d), lambda i: (0, 0))
    return pl.pallas_call(
        functools.partial(_ple_kernel, alpha=alpha),
        out_shape=jax.ShapeDtypeStruct((t, d), F32),
        grid=(t // tm,),
        in_specs=[pl.BlockSpec((tm, d), lambda i: (i, 0)), pl.BlockSpec((tm, pd), lambda i: (first + i, 0)),
                  pl.BlockSpec((d, d), lambda i: (0, 0)), vec,
                  pl.BlockSpec((pd, d), lambda i: (0, 0)), vec, vec],
        out_specs=pl.BlockSpec((tm, d), lambda i: (i, 0)),
        compiler_params=_cparams("parallel"),
        name="ple_ln",
    )(x2, p_all, gate_w.astype(BF16), gate_b.reshape(1, -1), proj.astype(BF16), g.reshape(1, -1), beta.reshape(1, -1))


def hybrid_mixer_ln(x2, bsz, seq_len, tables, w_in, w_out, cmp_weights, conv_params, s5_params,
                    g, beta, alpha):
    (q, ksel, kwin, vsel, vwin, kv_cmp, gates, u_conv, zs) = in_proj(x2, _widen_w_in(w_in), tables, seq_len)
    b3 = lambda a: a.reshape(bsz, seq_len, a.shape[1])
    kvc = compress(kv_cmp.reshape(2, bsz, seq_len, LANES), _compress_weights(*cmp_weights))
    o_nsa = nsa_attention(b3(q), b3(ksel), b3(vsel), b3(kwin), b3(vwin), kvc, b3(gates))
    o_conv = conformer_conv(b3(u_conv), *conv_params)
    (a_re, a_im, log_dt, b_re, b_im, c_re, c_im, d_skip, glu_w, glu_b) = s5_params
    ops = _s5_operators(a_re, a_im, log_dt, b_re, b_im, c_re, c_im, seq_len // S5_CHUNK)
    o_s5 = s5_layer(b3(zs), ops, d_skip, glu_w, glu_b)
    return out_proj_ln(x2, o_nsa.reshape(-1, NSA_WIDTH), o_conv.reshape(-1, CONV_WIDTH), o_s5, w_out,
                       g, beta, alpha)


def kernel(x, p, positions, w_in, w_out, cmp_pe, cmp_w1, cmp_b1, cmp_w2, cmp_b2, conv_w, conv_b, conv_ln_g, conv_ln_b, s5_a_re, s5_a_im, s5_log_dt, s5_b_re, s5_b_im, s5_c_re, s5_c_im, s5_d, s5_glu_w, s5_glu_b, ffn_w1, ffn_w3, ffn_w2, moe_router, moe_w1, moe_w3, moe_w2, ple_gate_w, ple_gate_b, ple_proj, ln_g, ln_b):
    bsz, seq_len, d_model = x.shape
    depth = w_in.shape[0]
    alpha = (2 * depth) ** 0.25
    t = bsz * seq_len
    tables = rope_tables(positions)
    x2 = x.reshape(t, d_model)
    dense_tm = 512
    moe_tm = 512 if t % 1024 else 1024
    for i in range(depth):
        x2 = hybrid_mixer_ln(
            x2, bsz, seq_len, tables, w_in[i], w_out[i],
            (cmp_pe[i], cmp_w1[i], cmp_b1[i], cmp_w2[i], cmp_b2[i]),
            (conv_w[i], conv_b[i], conv_ln_g[i], conv_ln_b[i]),
            (s5_a_re[i], s5_a_im[i], s5_log_dt[i], s5_b_re[i], s5_b_im[i], s5_c_re[i], s5_c_im[i],
             s5_d[i], s5_glu_w[i], s5_glu_b[i]),
            ln_g[i, 0], ln_b[i, 0], alpha)
        j = i // 2
        if i % 2 == 0:
            x2 = dense_swiglu_ln(x2, ffn_w1[j].astype(BF16), ffn_w3[j].astype(BF16), ffn_w2[j].astype(BF16),
                                 ln_g[i, 1], ln_b[i, 1], alpha, dense_tm)
        else:
            x2 = moe_swiglu_ln(x2, moe_router[j], moe_w1[j].astype(BF16), moe_w3[j].astype(BF16),
                               moe_w2[j].astype(BF16), ln_g[i, 1], ln_b[i, 1], alpha, moe_tm)
        x2 = ple_ln(x2, p.reshape(depth * t, -1), i, ple_gate_w[i], ple_gate_b[i], ple_proj[i],
                    ln_g[i, 2], ln_b[i, 2], alpha)
    return x2.reshape(bsz, seq_len, d_model)
```

```python
import functools
import math

import numpy as np
import jax
import jax.numpy as jnp
from jax import lax
from jax.experimental import pallas as pl
from jax.experimental.pallas import tpu as pltpu

F32 = jnp.float32
BF16 = jnp.bfloat16
HIGHEST = lax.Precision.HIGHEST

LANES = 128
VMEM_LIMIT = 56 * 1024 * 1024

HEAD_DIM = 64
HALF_DIM = HEAD_DIM // 2
N_Q_HEADS = 8
N_KV_HEADS = 2
GQA = N_Q_HEADS // N_KV_HEADS
CMP_LEN = 32
CMP_STRIDE = 16
CMP_HIDDEN = 256
SEL_LEN = 64
SEL_TOPK = 16
WINDOW = 512
ROPE_THETA = 10000.0
FORCED_SCORE = 1e9
NEG = -1e30
CONV_WIDTH = 256
CONV_LEN = 31
S5_WIDTH = 256
S5_GROUP_CH = 16
S5_GROUPS = 16
S5_STATE = 64
S5_CHUNK = 8
S5_HALVES = 2
N_EXPERTS = 8
LN_EPS = 1e-5
NSA_WIDTH = N_Q_HEADS * HEAD_DIM

C_Q = 0
C_KV = 512
C_GATE = 1280
C_CONVA = 1536
C_CONVB = 1792
C_S5 = 2048
C_TOTAL = 2304
Q_SCALE = HEAD_DIM ** -0.5 * math.log2(math.e)


def _cparams(*sem):
    return pltpu.CompilerParams(dimension_semantics=sem, vmem_limit_bytes=VMEM_LIMIT)


def _layer_norm(v, g, b):
    mu = jnp.mean(v, axis=-1, keepdims=True)
    d = v - mu
    var = jnp.mean(d * d, axis=-1, keepdims=True)
    return d * lax.rsqrt(var + LN_EPS) * g + b


def _gelu_tanh(x):
    return 0.5 * x * (1.0 + jnp.tanh(math.sqrt(2.0 / math.pi) * (x + 0.044715 * (x * x * x))))


def _sigmoid(x):
    return 1.0 / (1.0 + jnp.exp(-x))


def _rope_table_kernel(pos_ref, freq_ref, tab_ref):
    ang = pos_ref[...] * freq_ref[...]
    c = jnp.cos(ang)
    s = jnp.sin(ang)
    lane = lax.broadcasted_iota(jnp.int32, ang.shape, 1)
    first_half = (lane % HEAD_DIM) < HALF_DIM
    tab_ref[:, 0:LANES] = c
    tab_ref[:, LANES:2 * LANES] = jnp.where(first_half, -s, 0.0)
    tab_ref[:, 2 * LANES:3 * LANES] = jnp.where(first_half, 0.0, s)


def rope_tables(positions, tm=512):
    t = positions.size
    pos = positions.reshape(t, 1).astype(F32)
    inv_freq = ROPE_THETA ** (-jnp.arange(0, HEAD_DIM, 2, dtype=F32) / HEAD_DIM)
    freq = jnp.tile(inv_freq, LANES // HALF_DIM).reshape(1, LANES)
    return pl.pallas_call(
        _rope_table_kernel,
        out_shape=jax.ShapeDtypeStruct((t, 3 * LANES), F32),
        grid=(t // tm,),
        in_specs=[pl.BlockSpec((tm, 1), lambda i: (i, 0)),
                  pl.BlockSpec((1, LANES), lambda i: (0, 0))],
        out_specs=pl.BlockSpec((tm, 3 * LANES), lambda i: (i, 0)),
        compiler_params=_cparams("parallel"),
        name="rope_tables",
    )(pos, freq)


def _in_proj_kernel(x_ref, w_ref, tab_ref, q_ref, ksel_ref, kwin_ref, vsel_ref, vwin_ref,
                    kvcmp_ref, gate_ref, u_ref, zs_ref, *, seq_len):
    tm = x_ref.shape[0]
    xb = x_ref[...].astype(BF16)
    cos = tab_ref[:, 0:LANES]
    sin_a = tab_ref[:, LANES:2 * LANES]
    sin_b = tab_ref[:, 2 * LANES:3 * LANES]

    def proj(c0):
        z = jnp.dot(xb, w_ref[:, c0:c0 + 2 * LANES], preferred_element_type=F32)
        return z[:, 0:LANES], z[:, LANES:2 * LANES]

    def rope(z):
        return (z * cos + pltpu.roll(z, LANES - HALF_DIM, 1) * sin_a
                + pltpu.roll(z, HALF_DIM, 1) * sin_b)

    lane = lax.broadcasted_iota(jnp.int32, (tm, LANES), 1)
    row = lax.broadcasted_iota(jnp.int32, (tm, LANES), 0)
    low = lane < HEAD_DIM
    t_seq = (pl.program_id(0) * tm) % seq_len + row
    blk_onehot = jnp.where(lane == HEAD_DIM + t_seq // SEL_LEN, 1.0, 0.0)
    ones_lane = jnp.where(lane == HEAD_DIM, 1.0, 0.0)

    def spread(z, extra, out_ref, c0):
        out_ref[:, c0:c0 + LANES] = (jnp.where(low, z, 0.0) + extra).astype(BF16)
        out_ref[:, c0 + LANES:c0 + 2 * LANES] = (jnp.where(low, pltpu.roll(z, HEAD_DIM, 1), 0.0) + extra).astype(BF16)

    for pair in range(N_Q_HEADS // 4):
        for n, z in enumerate(proj(C_Q + pair * 2 * LANES)):
            spread(rope(z) * Q_SCALE, 0.0, q_ref, (2 * pair + n) * 2 * LANES)
    k_cmp, v_cmp = proj(C_KV)
    kvcmp_ref[0] = rope(k_cmp)
    kvcmp_ref[1] = v_cmp
    k_sel, v_sel = proj(C_KV + 2 * LANES)
    spread(rope(k_sel), blk_onehot, ksel_ref, 0)
    spread(v_sel, ones_lane, vsel_ref, 0)
    k_win, v_win = proj(C_KV + 4 * LANES)
    spread(rope(k_win), 0.0, kwin_ref, 0)
    spread(v_win, ones_lane, vwin_ref, 0)
    g0, g1 = proj(C_GATE)
    gate_ref[:, 0:LANES] = _sigmoid(g0)
    gate_ref[:, LANES:2 * LANES] = _sigmoid(g1)
    a0, a1 = proj(C_CONVA)
    b0, b1 = proj(C_CONVB)
    u_ref[:, 0:LANES] = a0 * _sigmoid(b0)
    u_ref[:, LANES:2 * LANES] = a1 * _sigmoid(b1)
    s0, s1 = proj(C_S5)
    zs_ref[:, 0:LANES] = s0
    zs_ref[:, LANES:2 * LANES] = s1


def _widen_w_in(w_in):
    d = w_in.shape[0]
    o_gate = NSA_WIDTH + 6 * N_KV_HEADS * HEAD_DIM
    o_conv = o_gate + 3 * N_Q_HEADS
    per_group = 3 * GQA
    cols = [w_in[:, :o_gate]]
    for h in range(N_KV_HEADS):
        cols += [w_in[:, o_gate + h * per_group:o_gate + (h + 1) * per_group],
                 jnp.zeros((d, LANES - per_group), w_in.dtype)]
    cols += [w_in[:, o_conv:]]
    w = jnp.concatenate(cols, axis=1)
    assert w.shape[1] == C_TOTAL
    return w.astype(BF16)


def in_proj(x2, w_wide, tables, seq_len, tm=512):
    t, d = x2.shape
    row = lambda width: pl.BlockSpec((tm, width), lambda i: (i, 0))
    out_shape = (
        jax.ShapeDtypeStruct((t, N_Q_HEADS * LANES), BF16),
        jax.ShapeDtypeStruct((t, N_KV_HEADS * LANES), BF16),
        jax.ShapeDtypeStruct((t, N_KV_HEADS * LANES), BF16),
        jax.ShapeDtypeStruct((t, N_KV_HEADS * LANES), BF16),
        jax.ShapeDtypeStruct((t, N_KV_HEADS * LANES), BF16),
        jax.ShapeDtypeStruct((2, t, LANES), F32),
        jax.ShapeDtypeStruct((t, 2 * LANES), F32),
        jax.ShapeDtypeStruct((t, CONV_WIDTH), F32),
        jax.ShapeDtypeStruct((t, S5_WIDTH), F32),
    )
    out_spec = lambda s: (row(s.shape[1]) if len(s.shape) == 2
                          else pl.BlockSpec((s.shape[0], tm, s.shape[2]), lambda i: (0, i, 0)))
    return pl.pallas_call(
        functools.partial(_in_proj_kernel, seq_len=seq_len),
        out_shape=out_shape,
        grid=(t // tm,),
        in_specs=[row(d),
                  pl.BlockSpec((d, C_TOTAL), lambda i: (0, 0)),
                  row(3 * LANES)],
        out_specs=tuple(out_spec(s) for s in out_shape),
        compiler_params=_cparams("parallel"),
        name="in_proj",
    )(x2, w_wide, tables)


def _compress_kernel(x_ref, pe_ref, w1_ref, b1_ref, w2_ref, b2_ref, o_ref, shift_ref):
    nb = x_ref.shape[2] // CMP_STRIDE
    x = jnp.concatenate([x_ref[0, 0, pl.ds(s, nb, stride=CMP_STRIDE), :] for s in range(CMP_STRIDE)],
                        axis=1)
    xa = (x + pe_ref[0, 0:1, :]).astype(BF16)
    xb = (x + pe_ref[0, 1:2, :]).astype(BF16)
    ha = jnp.dot(xa, w1_ref[0, 0], preferred_element_type=F32)
    hb = jnp.dot(xb, w1_ref[0, 1], preferred_element_type=F32)
    shift_ref[0:nb, :] = hb
    shift_ref[nb:nb + 8, :] = jnp.zeros((8, hb.shape[1]), F32)
    h = ha + shift_ref[1:nb + 1, :] + b1_ref[0]
    g = _gelu_tanh(h).astype(BF16)
    for hd in range(N_KV_HEADS):
        gh = g[:, hd * CMP_HIDDEN:(hd + 1) * CMP_HIDDEN]
        o_ref[0, 0, hd] = (jnp.dot(gh, w2_ref[0], preferred_element_type=F32) + b2_ref[0]).astype(BF16)


def _compress_weights(cmp_pe, cmp_w1, cmp_b1, cmp_w2, cmp_b2):
    half = CMP_LEN // 2
    pe = cmp_pe.reshape(2, 2, half, 1, HEAD_DIM)
    pe = jnp.broadcast_to(pe, (2, 2, half, N_KV_HEADS, HEAD_DIM)).reshape(2, 2, half * N_KV_HEADS * HEAD_DIM)
    w1 = cmp_w1.reshape(2, 2, half, HEAD_DIM, CMP_HIDDEN)
    z = jnp.zeros_like(w1)
    w_h0 = jnp.stack([w1, z], axis=3)
    w_h1 = jnp.stack([z, w1], axis=3)
    w1w = jnp.concatenate([w_h0, w_h1], axis=-1)
    w1w = w1w.reshape(2, 2, half * N_KV_HEADS * HEAD_DIM, N_KV_HEADS * CMP_HIDDEN).astype(BF16)
    b1 = jnp.tile(cmp_b1, (1, N_KV_HEADS)).reshape(2, 1, N_KV_HEADS * CMP_HIDDEN)
    w2 = jnp.pad(cmp_w2, ((0, 0), (0, 0), (0, LANES - HEAD_DIM))).astype(BF16)
    b2 = jnp.pad(cmp_b2, ((0, 0), (0, LANES - HEAD_DIM))).reshape(2, 1, LANES)
    return pe, w1w, b1, w2, b2


def compress(kv_cmp, weights):
    pe, w1w, b1, w2, b2 = weights
    _, bsz, seq_len, _ = kv_cmp.shape
    nb = seq_len // CMP_STRIDE
    cw = CMP_STRIDE * LANES
    hid = N_KV_HEADS * CMP_HIDDEN
    return pl.pallas_call(
        _compress_kernel,
        out_shape=jax.ShapeDtypeStruct((2, bsz, N_KV_HEADS, nb, LANES), BF16),
        grid=(2, bsz),
        in_specs=[pl.BlockSpec((1, 1, seq_len, LANES), lambda j, b: (j, b, 0, 0)),
                  pl.BlockSpec((1, 2, cw), lambda j, b: (j, 0, 0)),
                  pl.BlockSpec((1, 2, cw, hid), lambda j, b: (j, 0, 0, 0)),
                  pl.BlockSpec((1, 1, hid), lambda j, b: (j, 0, 0)),
                  pl.BlockSpec((1, CMP_HIDDEN, LANES), lambda j, b: (j, 0, 0)),
                  pl.BlockSpec((1, 1, LANES), lambda j, b: (j, 0, 0))],
        out_specs=pl.BlockSpec((1, 1, N_KV_HEADS, nb, LANES), lambda j, b: (j, b, 0, 0, 0)),
        scratch_shapes=[pltpu.VMEM((nb + 8, hid), F32)],
        compiler_params=_cparams("parallel", "parallel"),
        name="compress",
    )(kv_cmp, pe, w1w, b1, w2, b2)


SEL_TK = 512
NSA_TQ = 128
NSA_SUBS = 2
N_SEL_BLOCKS = LANES - HEAD_DIM


def _nsa_kernel(q_ref, ksel_ref, vsel_ref, kwin_ref, vwin_ref, kc_ref, vc_ref, gate_ref, ovt_ref,
                o_ref, m_ref, acc_ref):
    tq = NSA_TQ
    subs = range(q_ref.shape[1] // tq)
    units = [(a, g) for a in subs for g in range(GQA)]
    i = pl.program_id(2)
    q0 = i * q_ref.shape[1]
    qh = {(a, g): q_ref[0, a * tq:(a + 1) * tq, g * LANES:(g + 1) * LANES] for a, g in units}
    t_col = [q0 + a * tq + lax.broadcasted_iota(jnp.int32, (tq, 1), 0) for a in subs]
    t_row = [q0 + a * tq + lax.broadcasted_iota(jnp.int32, (1, tq), 1) for a in subs]
    contract_last = (((1,), (1,)), ((), ()))

    ncb = kc_ref.shape[3]
    kc = kc_ref[0, 0, 0]
    vc = vc_ref[0, 0, 0]
    s_cmp = {u: lax.dot_general(qh[u], kc, contract_last, preferred_element_type=F32) for u in units}
    wn = tq + WINDOW
    ks = [pl.multiple_of(jnp.maximum(q0 + a * tq - WINDOW, 0), tq) for a in subs]
    kw = [kwin_ref[0, pl.ds(ks[a], wn), :] for a in subs]
    vw = [vwin_ref[0, pl.ds(ks[a], wn), :] for a in subs]
    s_win = {(a, g): lax.dot_general(qh[a, g], kw[a], contract_last, preferred_element_type=F32)
             for a, g in units}

    cmp_end = lax.broadcasted_iota(jnp.int32, (1, ncb), 1) * CMP_STRIDE + (CMP_LEN - 1)
    o_cmp = {}
    p_sum = [None for _ in subs]
    for a, g in units:
        valid = cmp_end <= t_col[a]
        s = jnp.where(valid, s_cmp[a, g], NEG)
        e = jnp.where(valid, jnp.exp2(s - jnp.max(s, axis=-1, keepdims=True)), 0.0)
        p = e / jnp.maximum(jnp.sum(e, axis=-1, keepdims=True), 1e-30)
        o_cmp[a, g] = jnp.dot(p.astype(BF16), vc, preferred_element_type=F32)
        p_sum[a] = p if p_sum[a] is None else p_sum[a] + p

    o_win = {}
    for a, g in units:
        wpos = ks[a] + lax.broadcasted_iota(jnp.int32, (1, wn), 1)
        wmask = (wpos <= t_col[a]) & (wpos > t_col[a] - WINDOW)
        sw = jnp.where(wmask, s_win[a, g], NEG)
        pw = jnp.exp2(sw - jnp.max(sw, axis=-1, keepdims=True))
        accw = jnp.dot(pw.astype(BF16), vw[a], preferred_element_type=F32)
        o_win[a, g] = accw / accw[:, HEAD_DIM:HEAD_DIM + 1]

    ovt = ovt_ref[...]
    nblk = N_SEL_BLOCKS
    q_sel = {}
    for a in subs:
        p1 = p_sum[a].astype(BF16)
        r1 = p_sum[a] - p1.astype(F32)
        p2 = r1.astype(BF16)
        p3 = (r1 - p2.astype(F32)).astype(BF16)
        imp_t = (lax.dot_general(ovt, p1, contract_last, preferred_element_type=F32)
                 + lax.dot_general(ovt, p2, contract_last, preferred_element_type=F32)
                 + lax.dot_general(ovt, p3, contract_last, preferred_element_type=F32))

        blk = lax.broadcasted_iota(jnp.int32, (nblk, tq), 0)
        qblk = t_row[a] // SEL_LEN
        causal_blk = blk <= qblk
        forced = (blk == 0) | (blk == qblk) | (blk == qblk - 1)
        score_t = jnp.where(forced, FORCED_SCORE, jnp.where(causal_blk, imp_t, NEG))
        groups = [score_t[8 * r:8 * r + 8] for r in range(nblk // 8)]
        sub = lax.broadcasted_iota(jnp.int32, (8, tq), 0)
        later = [jnp.where(sub > s, 1.0, 0.0) for s in range(8)]
        counts = [jnp.zeros((8, tq), F32) for _ in groups]
        for b in range(nblk):
            row = score_t[b:b + 1]
            for r, grp in enumerate(groups):
                if 8 * r > b:
                    counts[r] = counts[r] + jnp.where(row >= grp, 1.0, 0.0)
                elif 8 * r + 7 <= b:
                    counts[r] = counts[r] + jnp.where(row > grp, 1.0, 0.0)
                else:
                    counts[r] = (counts[r] + jnp.where(row > grp, 1.0, 0.0)
                                 + jnp.where(row == grp, later[b - 8 * r], 0.0))
        keep_t = (jnp.concatenate(counts, axis=0) < float(SEL_TOPK)) & causal_blk
        bias_t = jnp.where(keep_t, 0.0, NEG)
        bias = jnp.concatenate([jnp.zeros((LANES - nblk, tq), F32), bias_t], axis=0).T.astype(BF16)
        for g in range(GQA):
            q_sel[a, g] = qh[a, g] + bias

    tk = SEL_TK
    m_ref[...] = jnp.full(m_ref.shape, NEG, F32)
    acc_ref[...] = jnp.zeros(acc_ref.shape, F32)

    def sel_tile(j, carry):
        k0 = pl.multiple_of(j * tk, tk)
        k = ksel_ref[0, pl.ds(k0, tk), :]
        v = vsel_ref[0, pl.ds(k0, tk), :]
        scores = {u: lax.dot_general(q_sel[u], k, contract_last, preferred_element_type=F32)
                  for u in units}
        kpos = k0 + lax.broadcasted_iota(jnp.int32, (1, tk), 1)
        for n, (a, g) in enumerate(units):
            rs = slice(n * tq, (n + 1) * tq)
            sc = jnp.where(kpos <= t_col[a], scores[a, g], NEG)
            m_old = m_ref[rs, :]
            m_new = jnp.maximum(m_old, jnp.max(sc, axis=-1, keepdims=True))
            p = jnp.exp2(sc - jnp.concatenate([m_new] * (tk // LANES), axis=1))
            acc_ref[rs, :] = (jnp.exp2(m_old - m_new) * acc_ref[rs, :]
                              + jnp.dot(p.astype(BF16), v, preferred_element_type=F32))
            m_ref[rs, :] = m_new
        return carry

    lax.fori_loop(0, q0 // tk + 1, sel_tile, 0)

    lane = lax.broadcasted_iota(jnp.int32, (tq, LANES), 1)
    low = lane < HEAD_DIM
    for a in subs:
        gates = gate_ref[0, a * tq:(a + 1) * tq, :]
        heads = []
        for g in range(GQA):
            n = a * GQA + g
            acc = acc_ref[n * tq:(n + 1) * tq, :]
            o_sel = acc / acc[:, HEAD_DIM:HEAD_DIM + 1]
            heads.append(gates[:, 3 * g:3 * g + 1] * o_cmp[a, g] + gates[:, 3 * g + 1:3 * g + 2] * o_sel
                         + gates[:, 3 * g + 2:3 * g + 3] * o_win[a, g])
        pair0 = jnp.where(low, heads[0], pltpu.roll(heads[1], HEAD_DIM, 1))
        pair1 = jnp.where(low, heads[2], pltpu.roll(heads[3], HEAD_DIM, 1))
        o_ref[0, a * tq:(a + 1) * tq, :] = jnp.concatenate([pair0, pair1], axis=1).astype(BF16)


def _overlap_matrix_t(n_cmp):
    n = np.arange(n_cmp)[None, :]
    s = np.arange(N_SEL_BLOCKS)[:, None]
    c_start = n * CMP_STRIDE
    s_start = s * SEL_LEN
    ov = (c_start < s_start + SEL_LEN) & (c_start + CMP_LEN > s_start)
    return jnp.asarray(ov.astype(np.float32), BF16)


def nsa_attention(q, ksel, vsel, kwin, vwin, kvc, gates):
    bsz, seq_len, _ = q.shape
    ncb = kvc.shape[3]
    tq = NSA_SUBS * NSA_TQ
    assert seq_len // SEL_LEN <= N_SEL_BLOCKS and seq_len % SEL_TK == 0 and SEL_TK % tq == 0
    assert seq_len >= NSA_TQ + WINDOW
    kv_spec = pl.BlockSpec((1, seq_len, LANES), lambda b, h, i: (b, 0, h))
    rows = GQA * tq
    return pl.pallas_call(
        _nsa_kernel,
        out_shape=jax.ShapeDtypeStruct((bsz, seq_len, NSA_WIDTH), BF16),
        grid=(bsz, N_KV_HEADS, seq_len // tq),
        in_specs=[pl.BlockSpec((1, tq, GQA * LANES), lambda b, h, i: (b, i, h)),
                  kv_spec, kv_spec, kv_spec, kv_spec,
                  pl.BlockSpec((1, 1, 1, ncb, LANES), lambda b, h, i: (0, b, h, 0, 0)),
                  pl.BlockSpec((1, 1, 1, ncb, LANES), lambda b, h, i: (1, b, h, 0, 0)),
                  pl.BlockSpec((1, tq, LANES), lambda b, h, i: (b, i, h)),
                  pl.BlockSpec((N_SEL_BLOCKS, ncb), lambda b, h, i: (0, 0))],
        out_specs=pl.BlockSpec((1, tq, GQA * HEAD_DIM), lambda b, h, i: (b, i, h)),
        scratch_shapes=[pltpu.VMEM((rows, LANES), F32), pltpu.VMEM((rows, LANES), F32)],
        compiler_params=_cparams("parallel", "parallel", "arbitrary"),
        name="nsa_attention",
    )(q, ksel, vsel, kwin, vwin, kvc, kvc, gates, _overlap_matrix_t(ncb))


def _conv_kernel(u_ref, w_ref, cb_ref, g_ref, b_ref, o_ref, pad_ref, *, rows):
    seq_len = u_ref.shape[1]
    halo = 32
    pad_ref[0:halo, :] = jnp.zeros((halo, CONV_WIDTH), F32)
    pad_ref[halo:halo + seq_len, :] = u_ref[0]

    first = halo - CONV_LEN + 1

    def body(c, carry):
        r0 = pl.multiple_of(c * rows, rows)
        win = pad_ref[pl.ds(r0, rows + halo), :]
        acc = jnp.zeros((rows, CONV_WIDTH), F32)
        for sub in range(8):
            shifted = win if sub == 0 else pltpu.roll(win, rows + halo - sub, 0)
            for j in range(CONV_LEN):
                if (first + j) % 8 == sub:
                    a0 = first + j - sub
                    acc = acc + w_ref[j:j + 1, :] * shifted[a0:a0 + rows]
        y = _layer_norm(acc + cb_ref[...], g_ref[...], b_ref[...])
        o_ref[0, pl.ds(r0, rows), :] = (y * _sigmoid(y)).astype(BF16)
        return carry

    lax.fori_loop(0, seq_len // rows, body, 0)


def conformer_conv(u, conv_w, conv_b, ln_g, ln_b, rows=64):
    bsz, seq_len, _ = u.shape
    vec = pl.BlockSpec((1, CONV_WIDTH), lambda b: (0, 0))
    return pl.pallas_call(
        functools.partial(_conv_kernel, rows=rows),
        out_shape=jax.ShapeDtypeStruct((bsz, seq_len, CONV_WIDTH), BF16),
        grid=(bsz,),
        in_specs=[pl.BlockSpec((1, seq_len, CONV_WIDTH), lambda b: (b, 0, 0)),
                  pl.BlockSpec((CONV_LEN, CONV_WIDTH), lambda b: (0, 0)), vec, vec, vec],
        out_specs=pl.BlockSpec((1, seq_len, CONV_WIDTH), lambda b: (b, 0, 0)),
        scratch_shapes=[pltpu.VMEM((seq_len + 32, CONV_WIDTH), F32)],
        compiler_params=_cparams("parallel"),
        name="conformer_conv",
    )(u, conv_w, conv_b.reshape(1, -1), ln_g.reshape(1, -1), ln_b.reshape(1, -1))


def _s5_operators(a_re, a_im, log_dt, b_re, b_im, c_re, c_im, n_chunks):
    tc = S5_CHUNK
    dt = jnp.exp(log_dt.astype(F32))[:, None]
    lr = a_re.astype(F32) * dt
    li = a_im.astype(F32) * dt
    mag = jnp.exp(lr)
    ab_re = mag * jnp.cos(li)
    ab_im = mag * jnp.sin(li)
    den = a_re * a_re + a_im * a_im
    coef_re = ((ab_re - 1.0) * a_re + ab_im * a_im) / den
    coef_im = (ab_im * a_re - (ab_re - 1.0) * a_im) / den
    bb_re = coef_re[..., None] * b_re - coef_im[..., None] * b_im
    bb_im = coef_re[..., None] * b_im + coef_im[..., None] * b_re

    def power(tau):
        tau = jnp.asarray(tau, F32)
        m = jnp.exp(lr[..., None] * tau)
        return m * jnp.cos(li[..., None] * tau), m * jnp.sin(li[..., None] * tau)

    lag_re, lag_im = power(jnp.arange(tc + 1))
    ca_re = c_re[..., None] * lag_re[:, None] - c_im[..., None] * lag_im[:, None]
    ca_im = c_re[..., None] * lag_im[:, None] + c_im[..., None] * lag_re[:, None]
    kern = (jnp.einsum('gcnt,gnd->gtcd', ca_re[..., :tc], bb_re, precision=HIGHEST)
            - jnp.einsum('gcnt,gnd->gtcd', ca_im[..., :tc], bb_im, precision=HIGHEST))
    s_idx = np.arange(tc)[:, None]
    i_idx = np.arange(tc)[None, :]
    lag = np.clip(i_idx - s_idx, 0, tc - 1)
    toep = kern[:, lag]
    toep = jnp.where(jnp.asarray(i_idx >= s_idx)[None, :, :, None, None], toep, 0.0)
    rev_re, rev_im = lag_re[..., tc - 1::-1], lag_im[..., tc - 1::-1]
    bop_re = (rev_re[..., None] * bb_re[:, :, None] - rev_im[..., None] * bb_im[:, :, None])
    bop_im = (rev_re[..., None] * bb_im[:, :, None] + rev_im[..., None] * bb_re[:, :, None])
    bop_re = bop_re.transpose(0, 2, 3, 1)
    bop_im = bop_im.transpose(0, 2, 3, 1)
    cop_re = ca_re[..., 1:].transpose(0, 2, 3, 1)
    cop_im = -ca_im[..., 1:].transpose(0, 2, 3, 1)

    gh = S5_GROUPS // S5_HALVES
    eye = jnp.eye(gh, dtype=F32)
    hw = tc * gh * S5_GROUP_CH
    sw = gh * S5_STATE
    split = lambda x: x.reshape((S5_HALVES, gh) + x.shape[1:])
    m_nat = jnp.einsum('hgsiod,gk->hsgdiko', split(toep), eye).reshape(S5_HALVES, hw, hw)
    b_nat = jnp.concatenate(
        [jnp.einsum('hgsdn,gk->hsgdkn', split(bop), eye).reshape(S5_HALVES, hw, sw) for bop in (bop_re, bop_im)],
        axis=2)
    c_nat = jnp.concatenate(
        [jnp.einsum('hgnio,gk->hgniko', split(cop), eye).reshape(S5_HALVES, sw, hw) for cop in (cop_re, cop_im)],
        axis=1)
    levels = max(1, int(math.log2(n_chunks)))
    lv_re, lv_im = power(tc * (2.0 ** jnp.arange(levels)))
    lanes = lambda x: split(x).transpose(0, 3, 1, 2).reshape(S5_HALVES, levels, sw)
    a_lv = jnp.concatenate([lanes(lv_re), lanes(lv_im)], axis=-1)
    return m_nat.astype(BF16), b_nat.astype(BF16), c_nat.astype(BF16), a_lv


def _s5_kernel(u0_ref, u1_ref, m_ref, b_ref, c_ref, a_ref, d_ref, gw_ref, gb_ref, o0_ref, o1_ref,
               sre_ref, sim_ref):
    tc = S5_CHUNK
    nc = u0_ref.shape[1] // tc
    hl = S5_WIDTH // S5_HALVES
    sw = sre_ref.shape[1]
    u_step = [[u_ref[0, pl.ds(s, nc, stride=tc), :] for s in range(tc)] for u_ref in (u0_ref, u1_ref)]
    y_half = []
    for h in range(S5_HALVES):
        uh = jnp.concatenate(u_step[h], axis=1).astype(BF16)
        v = jnp.dot(uh, b_ref[h], preferred_element_type=F32)
        zeros = jnp.zeros((nc, sw), F32)
        sre_ref[0:nc, :] = zeros
        sim_ref[0:nc, :] = zeros
        sre_ref[nc:2 * nc, :] = v[:, 0:sw]
        sim_ref[nc:2 * nc, :] = v[:, sw:2 * sw]
        for lv in range(a_ref.shape[1]):
            d = 1 << lv
            ar = a_ref[h, lv:lv + 1, 0:sw]
            ai = a_ref[h, lv:lv + 1, sw:2 * sw]
            pr = sre_ref[nc - d:2 * nc - d, :]
            pi = sim_ref[nc - d:2 * nc - d, :]
            cr = sre_ref[nc:2 * nc, :]
            ci = sim_ref[nc:2 * nc, :]
            sre_ref[nc:2 * nc, :] = cr + ar * pr - ai * pi
            sim_ref[nc:2 * nc, :] = ci + ar * pi + ai * pr
        prev_re = sre_ref[nc - 1:2 * nc - 1, :].astype(BF16)
        prev_im = sim_ref[nc - 1:2 * nc - 1, :].astype(BF16)
        y_half.append(jnp.dot(uh, m_ref[h], preferred_element_type=F32)
                      + jnp.dot(prev_re, c_ref[h, 0:sw, :], preferred_element_type=F32)
                      + jnp.dot(prev_im, c_ref[h, sw:2 * sw, :], preferred_element_type=F32))
    for i in range(tc):
        y = jnp.concatenate([yh[:, i * hl:(i + 1) * hl] for yh in y_half], axis=1)
        z = _gelu_tanh(y + d_ref[...] * jnp.concatenate([u_step[0][i], u_step[1][i]], axis=1))
        gate = jnp.dot(z.astype(BF16), gw_ref[...], preferred_element_type=F32) + gb_ref[...]
        out = z * _sigmoid(gate)
        o0_ref[0, pl.ds(i, nc, stride=tc), :] = out[:, 0:hl]
        o1_ref[0, pl.ds(i, nc, stride=tc), :] = out[:, hl:2 * hl]


def s5_layer(zs, operators, d_skip, glu_w, glu_b):
    m_nat, b_nat, c_nat, a_lv = operators
    bsz, seq_len, _ = zs.shape
    tc = S5_CHUNK
    nc = seq_len // tc
    assert nc & (nc - 1) == 0 and a_lv.shape[1] == int(math.log2(nc))
    hl = S5_WIDTH // S5_HALVES
    sw = a_lv.shape[2] // 2
    full = lambda a: pl.BlockSpec(a.shape, lambda b: (0,) * a.ndim)
    vec = pl.BlockSpec((1, S5_WIDTH), lambda b: (0, 0))
    half = lambda h: pl.BlockSpec((1, seq_len, hl), lambda b: (b, 0, h))
    glu_wb = glu_w.astype(BF16)
    out = pl.pallas_call(
        _s5_kernel,
        out_shape=(jax.ShapeDtypeStruct((bsz, seq_len, hl), F32),) * S5_HALVES,
        grid=(bsz,),
        in_specs=[half(0), half(1), full(m_nat), full(b_nat), full(c_nat), full(a_lv), vec, full(glu_wb), vec],
        out_specs=(half(0),) * S5_HALVES,
        scratch_shapes=[pltpu.VMEM((2 * nc, sw), F32), pltpu.VMEM((2 * nc, sw), F32)],
        compiler_params=_cparams("parallel"),
        name="s5_layer",
    )(zs, zs, m_nat, b_nat, c_nat, a_lv, d_skip.reshape(1, -1), glu_wb, glu_b.reshape(1, -1))
    return tuple(o.reshape(bsz * seq_len, hl) for o in out)


def _route_top2(logits):
    lane = lax.broadcasted_iota(jnp.int32, logits.shape, 1)
    lane_f = lane.astype(F32)
    logits = jnp.where(lane < N_EXPERTS, logits, -jnp.inf)
    v1 = jnp.max(logits, axis=-1, keepdims=True)
    i1 = jnp.min(jnp.where(logits == v1, lane_f, 1e9), axis=-1, keepdims=True)
    rest = jnp.where(lane_f == i1, -jnp.inf, logits)
    v2 = jnp.max(rest, axis=-1, keepdims=True)
    i2 = jnp.min(jnp.where(rest == v2, lane_f, 1e9), axis=-1, keepdims=True)
    e2 = jnp.exp(v2 - v1)
    den = 1.0 + e2
    out = jnp.where(lane == 0, i1, jnp.where(lane == 1, i2, jnp.where(lane == 2, 1.0 / den, e2 / den)))
    return jnp.where(lane < 4, out, 0.0)


def _out_proj_kernel(x_ref, a_ref, c_ref, s0_ref, s1_ref, w_ref, g_ref, b_ref, *rest, alpha):
    o_ref = rest[0] if len(rest) == 1 else rest[1]
    o_s5 = jnp.concatenate([s0_ref[...], s1_ref[...]], axis=1).astype(BF16)
    h = jnp.dot(a_ref[...], w_ref[0:NSA_WIDTH, :], preferred_element_type=F32)
    h = h + jnp.dot(c_ref[...], w_ref[NSA_WIDTH:NSA_WIDTH + CONV_WIDTH, :], preferred_element_type=F32)
    h = h + jnp.dot(o_s5, w_ref[NSA_WIDTH + CONV_WIDTH:, :], preferred_element_type=F32)
    y = _layer_norm(alpha * x_ref[...] + h, g_ref[...], b_ref[...])
    o_ref[...] = y
    if len(rest) > 1:
        router_ref, _, routed_ref, ybf_ref = rest
        ybf_ref[...] = y.astype(BF16)
        routed_ref[...] = _route_top2(jnp.dot(y, router_ref[...], preferred_element_type=F32, precision=HIGHEST))


def out_proj_ln(x2, o_nsa, o_conv, o_s5, w_out, g, b, alpha, router=None, tm=512):
    t, d = x2.shape
    row = lambda width: pl.BlockSpec((tm, width), lambda i: (i, 0))
    vec = pl.BlockSpec((1, d), lambda i: (0, 0))
    in_specs = [row(d), row(NSA_WIDTH), row(CONV_WIDTH), row(S5_WIDTH // S5_HALVES), row(S5_WIDTH // S5_HALVES),
                pl.BlockSpec(w_out.shape, lambda i: (0, 0)), vec, vec]
    args = [x2, o_nsa, o_conv, *o_s5, w_out.astype(BF16), g.reshape(1, -1), b.reshape(1, -1)]
    out_shape = jax.ShapeDtypeStruct((t, d), F32)
    out_specs = row(d)
    if router is not None:
        in_specs.append(pl.BlockSpec((d, LANES), lambda i: (0, 0)))
        args.append(jnp.pad(router, ((0, 0), (0, LANES - router.shape[1]))))
        out_shape = (out_shape, jax.ShapeDtypeStruct((t, LANES), F32), jax.ShapeDtypeStruct((t, d), BF16))
        out_specs = (out_specs, row(LANES), row(d))
    return pl.pallas_call(
        functools.partial(_out_proj_kernel, alpha=alpha),
        out_shape=out_shape,
        grid=(t // tm,),
        in_specs=in_specs,
        out_specs=out_specs,
        compiler_params=_cparams("parallel"),
        name="out_proj_ln",
    )(*args)


def _ffn_kernel(te_ref, nu_ref, x_ref, w1_ref, w3_ref, w2_ref, *rest, alpha):
    ln_refs, (o_ref, acc_ref) = rest[:-2], rest[-2:]
    i = pl.program_id(0)
    f = pl.program_id(1)
    last = pl.num_programs(1) - 1
    used = i < nu_ref[0]

    @pl.when(used)
    def _():
        xb = x_ref[...].astype(BF16)
        h1 = jnp.dot(xb, w1_ref[0], preferred_element_type=F32)
        h3 = jnp.dot(xb, w3_ref[0], preferred_element_type=F32)
        h = (h1 * _sigmoid(h1) * h3).astype(BF16)
        part = jnp.dot(h, w2_ref[0], preferred_element_type=F32)

        @pl.when(f == 0)
        def _():
            acc_ref[...] = part

        @pl.when(f > 0)
        def _():
            acc_ref[...] = acc_ref[...] + part

        @pl.when(f == last)
        def _():
            if ln_refs:
                o_ref[...] = _layer_norm(alpha * x_ref[...] + acc_ref[...], ln_refs[0][...], ln_refs[1][...])
            else:
                o_ref[...] = acc_ref[...].astype(o_ref.dtype)

    @pl.when(jnp.logical_not(used) & (f == last))
    def _():
        o_ref[...] = jnp.zeros(o_ref.shape, o_ref.dtype)


def grouped_swiglu(xs, tile_expert, n_used, w1, w3, w2, tm, tf, out_dtype, ln=None, alpha=1.0):
    p, d = xs.shape
    ff = w1.shape[2]
    assert p % tm == 0 and ff % tf == 0
    vec = pl.BlockSpec((1, d), lambda i, f, te, nu: (0, 0))
    ln_args = () if ln is None else (ln[0].reshape(1, d), ln[1].reshape(1, d))
    grid_spec = pltpu.PrefetchScalarGridSpec(
        num_scalar_prefetch=2,
        grid=(p // tm, ff // tf),
        in_specs=[pl.BlockSpec((tm, d), lambda i, f, te, nu: (i, 0)),
                  pl.BlockSpec((1, d, tf), lambda i, f, te, nu: (te[i], 0, f)),
                  pl.BlockSpec((1, d, tf), lambda i, f, te, nu: (te[i], 0, f)),
                  pl.BlockSpec((1, tf, d), lambda i, f, te, nu: (te[i], f, 0))] + [vec] * len(ln_args),
        out_specs=pl.BlockSpec((tm, d), lambda i, f, te, nu: (i, 0)),
        scratch_shapes=[pltpu.VMEM((tm, d), F32)],
    )
    return pl.pallas_call(
        functools.partial(_ffn_kernel, alpha=alpha),
        out_shape=jax.ShapeDtypeStruct((p, d), out_dtype),
        grid_spec=grid_spec,
        compiler_params=_cparams("arbitrary", "arbitrary"),
        name="grouped_swiglu",
    )(tile_expert, n_used, xs, w1, w3, w2, *ln_args)


def _pick_tf(ff, target=1024):
    best = LANES
    for cand in range(LANES, ff + 1, LANES):
        if ff % cand == 0 and cand <= target:
            best = cand
    return best


def _residual_ln_kernel(x_ref, a_ref, b_ref, wa_ref, wb_ref, g_ref, beta_ref, o_ref, *, alpha):
    f = wa_ref[...] * a_ref[...] + wb_ref[...] * b_ref[...]
    o_ref[...] = _layer_norm(alpha * x_ref[...] + f, g_ref[...], beta_ref[...])


def residual_ln(x2, a, b, wa, wb, g, beta, alpha, tm=512):
    t, d = x2.shape
    row = pl.BlockSpec((tm, d), lambda i: (i, 0))
    col = pl.BlockSpec((tm, 1), lambda i: (i, 0))
    vec = pl.BlockSpec((1, d), lambda i: (0, 0))
    return pl.pallas_call(
        functools.partial(_residual_ln_kernel, alpha=alpha),
        out_shape=jax.ShapeDtypeStruct((t, d), F32),
        grid=(t // tm,),
        in_specs=[row, row, row, col, col, vec, vec],
        out_specs=row,
        compiler_params=_cparams("parallel"),
        name="residual_ln",
    )(x2, a, b, wa, wb, g.reshape(1, -1), beta.reshape(1, -1))


def moe_swiglu_ln(x2, x2_bf16, routed, w1, w3, w2, g, beta, alpha, tm):
    t, d = x2.shape
    n_exp = w1.shape[0]
    idx = routed[:, 0:2].astype(jnp.int32)
    wts = routed[:, 2:4]
    e_flat = idx.T.reshape(-1)
    onehot = (e_flat[:, None] == jnp.arange(n_exp, dtype=jnp.int32)[None, :]).astype(jnp.int32)
    running = jnp.cumsum(onehot, axis=0)
    counts = running[-1]
    rank = jnp.sum(onehot * running, axis=1) - 1
    padded = ((counts + tm - 1) // tm) * tm
    pad_end = jnp.cumsum(padded)
    pad_start = pad_end - padded
    start = jnp.cumsum(counts) - counts
    pos = pad_start[e_flat] + rank
    n_rows = 2 * t + n_exp * tm
    n_tiles = n_rows // tm
    tile_start = jnp.arange(n_tiles, dtype=jnp.int32) * tm
    tile_expert = jnp.minimum(jnp.sum(tile_start[:, None] >= pad_end[None, :], axis=1), n_exp - 1).astype(jnp.int32)
    n_used = (pad_end[-1] // tm).astype(jnp.int32).reshape(1)
    order = jnp.argsort(e_flat, stable=True)
    row = jnp.arange(n_rows, dtype=jnp.int32)
    row_expert = jnp.repeat(tile_expert, tm)
    slot = jnp.minimum(start[row_expert] + (row - pad_start[row_expert]), 2 * t - 1)
    src = order[slot] % t
    xs = x2_bf16[src]
    ys = grouped_swiglu(xs, tile_expert, n_used, w1, w3, w2, tm, _pick_tf(w1.shape[2]), BF16)
    return residual_ln(x2, ys[pos[:t]], ys[pos[t:]], wts[:, 0:1], wts[:, 1:2], g, beta, alpha)


def dense_swiglu_ln(x2, w1, w3, w2, g, beta, alpha, tm):
    t = x2.shape[0]
    tile_expert = jnp.zeros((t // tm,), jnp.int32)
    n_used = jnp.full((1,), t // tm, jnp.int32)
    return grouped_swiglu(x2, tile_expert, n_used, w1[None], w3[None], w2[None], tm, _pick_tf(w1.shape[1], 1408),
                          F32, ln=(g, beta), alpha=alpha)


def _ple_kernel(x_ref, p_ref, wg_ref, bg_ref, wp_ref, g_ref, beta_ref, o_ref, *, alpha):
    x = x_ref[...]
    gate = _sigmoid(jnp.dot(x.astype(BF16), wg_ref[...], preferred_element_type=F32) + bg_ref[...])
    e = jnp.dot(p_ref[...].astype(BF16), wp_ref[...], preferred_element_type=F32) * gate
    o_ref[...] = _layer_norm(alpha * x + e, g_ref[...], beta_ref[...])


def ple_ln(x2, p_all, layer, gate_w, gate_b, proj, g, beta, alpha, tm=512):
    t, d = x2.shape
    pd = p_all.shape[1]
    first = layer * (t // tm)
    vec = pl.BlockSpec((1, d), lambda i: (0, 0))
    return pl.pallas_call(
        functools.partial(_ple_kernel, alpha=alpha),
        out_shape=jax.ShapeDtypeStruct((t, d), F32),
        grid=(t // tm,),
        in_specs=[pl.BlockSpec((tm, d), lambda i: (i, 0)), pl.BlockSpec((tm, pd), lambda i: (first + i, 0)),
                  pl.BlockSpec((d, d), lambda i: (0, 0)), vec,
                  pl.BlockSpec((pd, d), lambda i: (0, 0)), vec, vec],
        out_specs=pl.BlockSpec((tm, d), lambda i: (i, 0)),
        compiler_params=_cparams("parallel"),
        name="ple_ln",
    )(x2, p_all, gate_w.astype(BF16), gate_b.reshape(1, -1), proj.astype(BF16), g.reshape(1, -1), beta.reshape(1, -1))


def hybrid_mixer_ln(x2, bsz, seq_len, tables, w_in, w_out, cmp_weights, conv_params, s5_params,
                    g, beta, alpha, router=None):
    (q, ksel, kwin, vsel, vwin, kv_cmp, gates, u_conv, zs) = in_proj(x2, _widen_w_in(w_in), tables, seq_len)
    b3 = lambda a: a.reshape(bsz, seq_len, a.shape[1])
    kvc = compress(kv_cmp.reshape(2, bsz, seq_len, LANES), _compress_weights(*cmp_weights))
    o_nsa = nsa_attention(b3(q), b3(ksel), b3(vsel), b3(kwin), b3(vwin), kvc, b3(gates))
    o_conv = conformer_conv(b3(u_conv), *conv_params)
    (a_re, a_im, log_dt, b_re, b_im, c_re, c_im, d_skip, glu_w, glu_b) = s5_params
    ops = _s5_operators(a_re, a_im, log_dt, b_re, b_im, c_re, c_im, seq_len // S5_CHUNK)
    o_s5 = s5_layer(b3(zs), ops, d_skip, glu_w, glu_b)
    return out_proj_ln(x2, o_nsa.reshape(-1, NSA_WIDTH), o_conv.reshape(-1, CONV_WIDTH), o_s5, w_out,
                       g, beta, alpha, router)


def kernel(x, p, positions, w_in, w_out, cmp_pe, cmp_w1, cmp_b1, cmp_w2, cmp_b2, conv_w, conv_b, conv_ln_g, conv_ln_b, s5_a_re, s5_a_im, s5_log_dt, s5_b_re, s5_b_im, s5_c_re, s5_c_im, s5_d, s5_glu_w, s5_glu_b, ffn_w1, ffn_w3, ffn_w2, moe_router, moe_w1, moe_w3, moe_w2, ple_gate_w, ple_gate_b, ple_proj, ln_g, ln_b):
    bsz, seq_len, d_model = x.shape
    depth = w_in.shape[0]
    alpha = (2 * depth) ** 0.25
    t = bsz * seq_len
    tables = rope_tables(positions)
    x2 = x.reshape(t, d_model)
    dense_tm = 512
    moe_tm = 512 if t % 1024 else 1024
    for i in range(depth):
        j = i // 2
        routed_layer = i % 2 == 1
        mixed = hybrid_mixer_ln(
            x2, bsz, seq_len, tables, w_in[i], w_out[i],
            (cmp_pe[i], cmp_w1[i], cmp_b1[i], cmp_w2[i], cmp_b2[i]),
            (conv_w[i], conv_b[i], conv_ln_g[i], conv_ln_b[i]),
            (s5_a_re[i], s5_a_im[i], s5_log_dt[i], s5_b_re[i], s5_b_im[i], s5_c_re[i], s5_c_im[i],
             s5_d[i], s5_glu_w[i], s5_glu_b[i]),
            ln_g[i, 0], ln_b[i, 0], alpha, moe_router[j] if routed_layer else None)
        if routed_layer:
            x2, routed, x2_bf16 = mixed
            x2 = moe_swiglu_ln(x2, x2_bf16, routed, moe_w1[j].astype(BF16), moe_w3[j].astype(BF16),
                               moe_w2[j].astype(BF16), ln_g[i, 1], ln_b[i, 1], alpha, moe_tm)
        else:
            x2 = dense_swiglu_ln(mixed, ffn_w1[j].astype(BF16), ffn_w3[j].astype(BF16), ffn_w2[j].astype(BF16),
                                 ln_g[i, 1], ln_b[i, 1], alpha, dense_tm)
        x2 = ple_ln(x2, p.reshape(depth * t, -1), i, ple_gate_w[i], ple_gate_b[i], ple_proj[i],
                    ln_g[i, 2], ln_b[i, 2], alpha)
    return x2.reshape(bsz, seq_len, d_model)
```

```python
import functools
import math

import numpy as np
import jax
import jax.numpy as jnp
from jax import lax
from jax.experimental import pallas as pl
from jax.experimental.pallas import tpu as pltpu

F32 = jnp.float32
BF16 = jnp.bfloat16
HIGHEST = lax.Precision.HIGHEST

LANES = 128
VMEM_LIMIT = 56 * 1024 * 1024

HEAD_DIM = 64
HALF_DIM = HEAD_DIM // 2
N_Q_HEADS = 8
N_KV_HEADS = 2
GQA = N_Q_HEADS // N_KV_HEADS
CMP_LEN = 32
CMP_STRIDE = 16
CMP_HIDDEN = 256
SEL_LEN = 64
SEL_TOPK = 16
WINDOW = 512
ROPE_THETA = 10000.0
FORCED_SCORE = 1e9
NEG = -1e30
CONV_WIDTH = 256
CONV_LEN = 31
S5_WIDTH = 256
S5_GROUP_CH = 16
S5_GROUPS = 16
S5_STATE = 64
S5_CHUNK = 8
S5_HALVES = 2
N_EXPERTS = 8
LN_EPS = 1e-5
NSA_WIDTH = N_Q_HEADS * HEAD_DIM

C_Q = 0
C_KV = 512
C_GATE = 1280
C_CONVA = 1536
C_CONVB = 1792
C_S5 = 2048
C_TOTAL = 2304
Q_SCALE = HEAD_DIM ** -0.5 * math.log2(math.e)


def _cparams(*sem):
    return pltpu.CompilerParams(dimension_semantics=sem, vmem_limit_bytes=VMEM_LIMIT)


def _layer_norm(v, g, b):
    mu = jnp.mean(v, axis=-1, keepdims=True)
    d = v - mu
    var = jnp.mean(d * d, axis=-1, keepdims=True)
    return d * lax.rsqrt(var + LN_EPS) * g + b


def _gelu_tanh(x):
    return 0.5 * x * (1.0 + jnp.tanh(math.sqrt(2.0 / math.pi) * (x + 0.044715 * (x * x * x))))


def _sigmoid(x):
    return 1.0 / (1.0 + jnp.exp(-x))


def _top_bits(x):
    return pltpu.bitcast(pltpu.bitcast(x, jnp.uint32) & jnp.uint32(0xFFFF0000), F32)


def _rope_table_kernel(pos_ref, freq_ref, tab_ref):
    ang = pos_ref[...] * freq_ref[...]
    c = jnp.cos(ang)
    s = jnp.sin(ang)
    lane = lax.broadcasted_iota(jnp.int32, ang.shape, 1)
    first_half = (lane % HEAD_DIM) < HALF_DIM
    tab_ref[:, 0:LANES] = c
    tab_ref[:, LANES:2 * LANES] = jnp.where(first_half, -s, 0.0)
    tab_ref[:, 2 * LANES:3 * LANES] = jnp.where(first_half, 0.0, s)


def rope_tables(positions, tm=512):
    t = positions.size
    pos = positions.reshape(t, 1).astype(F32)
    inv_freq = ROPE_THETA ** (-jnp.arange(0, HEAD_DIM, 2, dtype=F32) / HEAD_DIM)
    freq = jnp.tile(inv_freq, LANES // HALF_DIM).reshape(1, LANES)
    return pl.pallas_call(
        _rope_table_kernel,
        out_shape=jax.ShapeDtypeStruct((t, 3 * LANES), F32),
        grid=(t // tm,),
        in_specs=[pl.BlockSpec((tm, 1), lambda i: (i, 0)),
                  pl.BlockSpec((1, LANES), lambda i: (0, 0))],
        out_specs=pl.BlockSpec((tm, 3 * LANES), lambda i: (i, 0)),
        compiler_params=_cparams("parallel"),
        name="rope_tables",
    )(pos, freq)


def _in_proj_kernel(x_ref, w_ref, tab_ref, q_ref, ksel_ref, kwin_ref, vsel_ref, vwin_ref,
                    kvcmp_ref, gate_ref, u_ref, zs_ref, *, seq_len):
    tm = x_ref.shape[0]
    xb = x_ref[...].astype(BF16)
    cos = tab_ref[:, 0:LANES]
    sin_a = tab_ref[:, LANES:2 * LANES]
    sin_b = tab_ref[:, 2 * LANES:3 * LANES]

    def proj(c0):
        z = jnp.dot(xb, w_ref[:, c0:c0 + 2 * LANES], preferred_element_type=F32)
        return z[:, 0:LANES], z[:, LANES:2 * LANES]

    def rope(z):
        return (z * cos + pltpu.roll(z, LANES - HALF_DIM, 1) * sin_a
                + pltpu.roll(z, HALF_DIM, 1) * sin_b)

    lane = lax.broadcasted_iota(jnp.int32, (tm, LANES), 1)
    row = lax.broadcasted_iota(jnp.int32, (tm, LANES), 0)
    low = lane < HEAD_DIM
    t_seq = (pl.program_id(0) * tm) % seq_len + row
    blk_onehot = jnp.where(lane == HEAD_DIM + t_seq // SEL_LEN, 1.0, 0.0)
    ones_lane = jnp.where(lane == HEAD_DIM, 1.0, 0.0)

    def spread(z, extra, out_ref, c0):
        out_ref[:, c0:c0 + LANES] = (jnp.where(low, z, 0.0) + extra).astype(BF16)
        out_ref[:, c0 + LANES:c0 + 2 * LANES] = (jnp.where(low, pltpu.roll(z, HEAD_DIM, 1), 0.0) + extra).astype(BF16)

    for pair in range(N_Q_HEADS // 4):
        for n, z in enumerate(proj(C_Q + pair * 2 * LANES)):
            spread(rope(z) * Q_SCALE, 0.0, q_ref, (2 * pair + n) * 2 * LANES)
    k_cmp, v_cmp = proj(C_KV)
    kvcmp_ref[0] = rope(k_cmp)
    kvcmp_ref[1] = v_cmp
    k_sel, v_sel = proj(C_KV + 2 * LANES)
    spread(rope(k_sel), blk_onehot, ksel_ref, 0)
    spread(v_sel, ones_lane, vsel_ref, 0)
    k_win, v_win = proj(C_KV + 4 * LANES)
    spread(rope(k_win), 0.0, kwin_ref, 0)
    spread(v_win, ones_lane, vwin_ref, 0)
    g0, g1 = proj(C_GATE)
    gate_ref[:, 0:LANES] = _sigmoid(g0)
    gate_ref[:, LANES:2 * LANES] = _sigmoid(g1)
    a0, a1 = proj(C_CONVA)
    b0, b1 = proj(C_CONVB)
    u_ref[:, 0:LANES] = a0 * _sigmoid(b0)
    u_ref[:, LANES:2 * LANES] = a1 * _sigmoid(b1)
    s0, s1 = proj(C_S5)
    zs_ref[:, 0:LANES] = s0
    zs_ref[:, LANES:2 * LANES] = s1


def _widen_w_in(w_in):
    d = w_in.shape[0]
    o_gate = NSA_WIDTH + 6 * N_KV_HEADS * HEAD_DIM
    o_conv = o_gate + 3 * N_Q_HEADS
    per_group = 3 * GQA
    cols = [w_in[:, :o_gate]]
    for h in range(N_KV_HEADS):
        cols += [w_in[:, o_gate + h * per_group:o_gate + (h + 1) * per_group],
                 jnp.zeros((d, LANES - per_group), w_in.dtype)]
    cols += [w_in[:, o_conv:]]
    w = jnp.concatenate(cols, axis=1)
    assert w.shape[1] == C_TOTAL
    return w.astype(BF16)


def in_proj(x2, w_wide, tables, seq_len, tm=512):
    t, d = x2.shape
    row = lambda width: pl.BlockSpec((tm, width), lambda i: (i, 0))
    out_shape = (
        jax.ShapeDtypeStruct((t, N_Q_HEADS * LANES), BF16),
        jax.ShapeDtypeStruct((t, N_KV_HEADS * LANES), BF16),
        jax.ShapeDtypeStruct((t, N_KV_HEADS * LANES), BF16),
        jax.ShapeDtypeStruct((t, N_KV_HEADS * LANES), BF16),
        jax.ShapeDtypeStruct((t, N_KV_HEADS * LANES), BF16),
        jax.ShapeDtypeStruct((2, t, LANES), F32),
        jax.ShapeDtypeStruct((t, 2 * LANES), F32),
        jax.ShapeDtypeStruct((t, CONV_WIDTH), F32),
        jax.ShapeDtypeStruct((t, S5_WIDTH), F32),
    )
    out_spec = lambda s: (row(s.shape[1]) if len(s.shape) == 2
                          else pl.BlockSpec((s.shape[0], tm, s.shape[2]), lambda i: (0, i, 0)))
    return pl.pallas_call(
        functools.partial(_in_proj_kernel, seq_len=seq_len),
        out_shape=out_shape,
        grid=(t // tm,),
        in_specs=[row(d),
                  pl.BlockSpec((d, C_TOTAL), lambda i: (0, 0)),
                  row(3 * LANES)],
        out_specs=tuple(out_spec(s) for s in out_shape),
        compiler_params=_cparams("parallel"),
        name="in_proj",
    )(x2, w_wide, tables)


def _compress_kernel(x_ref, pe_ref, w1_ref, b1_ref, w2_ref, b2_ref, o_ref, shift_ref):
    nb = x_ref.shape[2] // CMP_STRIDE
    x = jnp.concatenate([x_ref[0, 0, pl.ds(s, nb, stride=CMP_STRIDE), :] for s in range(CMP_STRIDE)],
                        axis=1)
    xa = (x + pe_ref[0, 0:1, :]).astype(BF16)
    xb = (x + pe_ref[0, 1:2, :]).astype(BF16)
    ha = jnp.dot(xa, w1_ref[0, 0], preferred_element_type=F32)
    hb = jnp.dot(xb, w1_ref[0, 1], preferred_element_type=F32)
    shift_ref[0:nb, :] = hb
    shift_ref[nb:nb + 8, :] = jnp.zeros((8, hb.shape[1]), F32)
    h = ha + shift_ref[1:nb + 1, :] + b1_ref[0]
    g = _gelu_tanh(h).astype(BF16)
    for hd in range(N_KV_HEADS):
        gh = g[:, hd * CMP_HIDDEN:(hd + 1) * CMP_HIDDEN]
        o_ref[0, 0, hd] = (jnp.dot(gh, w2_ref[0], preferred_element_type=F32) + b2_ref[0]).astype(BF16)


def _compress_weights(cmp_pe, cmp_w1, cmp_b1, cmp_w2, cmp_b2):
    half = CMP_LEN // 2
    pe = cmp_pe.reshape(2, 2, half, 1, HEAD_DIM)
    pe = jnp.broadcast_to(pe, (2, 2, half, N_KV_HEADS, HEAD_DIM)).reshape(2, 2, half * N_KV_HEADS * HEAD_DIM)
    w1 = cmp_w1.reshape(2, 2, half, HEAD_DIM, CMP_HIDDEN)
    z = jnp.zeros_like(w1)
    w_h0 = jnp.stack([w1, z], axis=3)
    w_h1 = jnp.stack([z, w1], axis=3)
    w1w = jnp.concatenate([w_h0, w_h1], axis=-1)
    w1w = w1w.reshape(2, 2, half * N_KV_HEADS * HEAD_DIM, N_KV_HEADS * CMP_HIDDEN).astype(BF16)
    b1 = jnp.tile(cmp_b1, (1, N_KV_HEADS)).reshape(2, 1, N_KV_HEADS * CMP_HIDDEN)
    w2 = jnp.pad(cmp_w2, ((0, 0), (0, 0), (0, LANES - HEAD_DIM))).astype(BF16)
    b2 = jnp.pad(cmp_b2, ((0, 0), (0, LANES - HEAD_DIM))).reshape(2, 1, LANES)
    return pe, w1w, b1, w2, b2


def compress(kv_cmp, weights):
    pe, w1w, b1, w2, b2 = weights
    _, bsz, seq_len, _ = kv_cmp.shape
    nb = seq_len // CMP_STRIDE
    cw = CMP_STRIDE * LANES
    hid = N_KV_HEADS * CMP_HIDDEN
    return pl.pallas_call(
        _compress_kernel,
        out_shape=jax.ShapeDtypeStruct((2, bsz, N_KV_HEADS, nb, LANES), BF16),
        grid=(2, bsz),
        in_specs=[pl.BlockSpec((1, 1, seq_len, LANES), lambda j, b: (j, b, 0, 0)),
                  pl.BlockSpec((1, 2, cw), lambda j, b: (j, 0, 0)),
                  pl.BlockSpec((1, 2, cw, hid), lambda j, b: (j, 0, 0, 0)),
                  pl.BlockSpec((1, 1, hid), lambda j, b: (j, 0, 0)),
                  pl.BlockSpec((1, CMP_HIDDEN, LANES), lambda j, b: (j, 0, 0)),
                  pl.BlockSpec((1, 1, LANES), lambda j, b: (j, 0, 0))],
        out_specs=pl.BlockSpec((1, 1, N_KV_HEADS, nb, LANES), lambda j, b: (j, b, 0, 0, 0)),
        scratch_shapes=[pltpu.VMEM((nb + 8, hid), F32)],
        compiler_params=_cparams("parallel", "parallel"),
        name="compress",
    )(kv_cmp, pe, w1w, b1, w2, b2)


SEL_TK = 512
NSA_TQ = 128
NSA_SUBS = 2
N_SEL_BLOCKS = LANES - HEAD_DIM


def _nsa_kernel(q_ref, ksel_ref, vsel_ref, kwin_ref, vwin_ref, kc_ref, vc_ref, gate_ref, ovt_ref,
                o_ref, m_ref, acc_ref):
    tq = NSA_TQ
    subs = range(q_ref.shape[1] // tq)
    units = [(a, g) for a in subs for g in range(GQA)]
    i = pl.program_id(2)
    q0 = i * q_ref.shape[1]
    qh = {(a, g): q_ref[0, a * tq:(a + 1) * tq, g * LANES:(g + 1) * LANES] for a, g in units}
    t_col = [q0 + a * tq + lax.broadcasted_iota(jnp.int32, (tq, 1), 0) for a in subs]
    t_row = [q0 + a * tq + lax.broadcasted_iota(jnp.int32, (1, tq), 1) for a in subs]
    contract_last = (((1,), (1,)), ((), ()))

    ncb = kc_ref.shape[3]
    kc = kc_ref[0, 0, 0]
    vc = vc_ref[0, 0, 0]
    s_cmp = {u: lax.dot_general(qh[u], kc, contract_last, preferred_element_type=F32) for u in units}
    wn = tq + WINDOW
    ks = [pl.multiple_of(jnp.maximum(q0 + a * tq - WINDOW, 0), tq) for a in subs]
    kw = [kwin_ref[0, pl.ds(ks[a], wn), :] for a in subs]
    vw = [vwin_ref[0, pl.ds(ks[a], wn), :] for a in subs]
    s_win = {(a, g): lax.dot_general(qh[a, g], kw[a], contract_last, preferred_element_type=F32)
             for a, g in units}

    cmp_end = lax.broadcasted_iota(jnp.int32, (1, ncb), 1) * CMP_STRIDE + (CMP_LEN - 1)
    o_cmp = {}
    p_sum = [None for _ in subs]
    for a, g in units:
        valid = cmp_end <= t_col[a]
        s = jnp.where(valid, s_cmp[a, g], NEG)
        e = jnp.where(valid, jnp.exp2(s - jnp.max(s, axis=-1, keepdims=True)), 0.0)
        p = e / jnp.maximum(jnp.sum(e, axis=-1, keepdims=True), 1e-30)
        o_cmp[a, g] = jnp.dot(p.astype(BF16), vc, preferred_element_type=F32)
        p_sum[a] = p if p_sum[a] is None else p_sum[a] + p

    o_win = {}
    for a, g in units:
        wpos = ks[a] + lax.broadcasted_iota(jnp.int32, (1, wn), 1)
        wmask = (wpos <= t_col[a]) & (wpos > t_col[a] - WINDOW)
        sw = jnp.where(wmask, s_win[a, g], NEG)
        pw = jnp.exp2(sw - jnp.max(sw, axis=-1, keepdims=True))
        accw = jnp.dot(pw.astype(BF16), vw[a], preferred_element_type=F32)
        o_win[a, g] = accw / accw[:, HEAD_DIM:HEAD_DIM + 1]

    ovt = ovt_ref[...]
    nblk = N_SEL_BLOCKS
    q_sel = {}
    for a in subs:
        t1 = _top_bits(p_sum[a])
        r1 = p_sum[a] - t1
        t2 = _top_bits(r1)
        p1, p2, p3 = t1.astype(BF16), t2.astype(BF16), (r1 - t2).astype(BF16)
        imp_t = (lax.dot_general(ovt, p1, contract_last, preferred_element_type=F32)
                 + lax.dot_general(ovt, p2, contract_last, preferred_element_type=F32)
                 + lax.dot_general(ovt, p3, contract_last, preferred_element_type=F32))

        blk = lax.broadcasted_iota(jnp.int32, (nblk, tq), 0)
        qblk = t_row[a] // SEL_LEN
        causal_blk = blk <= qblk
        forced = (blk == 0) | (blk == qblk) | (blk == qblk - 1)
        score_t = jnp.where(forced, FORCED_SCORE, jnp.where(causal_blk, imp_t, NEG))
        groups = [score_t[8 * r:8 * r + 8] for r in range(nblk // 8)]
        sub = lax.broadcasted_iota(jnp.int32, (8, tq), 0)
        later = [jnp.where(sub > s, 1.0, 0.0) for s in range(8)]
        counts = [jnp.zeros((8, tq), F32) for _ in groups]
        for b in range(nblk):
            row = score_t[b:b + 1]
            for r, grp in enumerate(groups):
                if 8 * r > b:
                    counts[r] = counts[r] + jnp.where(row >= grp, 1.0, 0.0)
                elif 8 * r + 7 <= b:
                    counts[r] = counts[r] + jnp.where(row > grp, 1.0, 0.0)
                else:
                    counts[r] = (counts[r] + jnp.where(row > grp, 1.0, 0.0)
                                 + jnp.where(row == grp, later[b - 8 * r], 0.0))
        keep_t = (jnp.concatenate(counts, axis=0) < float(SEL_TOPK)) & causal_blk
        bias_t = jnp.where(keep_t, 0.0, NEG)
        bias = jnp.concatenate([jnp.zeros((LANES - nblk, tq), F32), bias_t], axis=0).T.astype(BF16)
        for g in range(GQA):
            q_sel[a, g] = qh[a, g] + bias

    tk = SEL_TK
    m_ref[...] = jnp.full(m_ref.shape, NEG, F32)
    acc_ref[...] = jnp.zeros(acc_ref.shape, F32)

    def sel_tile(j, carry):
        k0 = pl.multiple_of(j * tk, tk)
        k = ksel_ref[0, pl.ds(k0, tk), :]
        v = vsel_ref[0, pl.ds(k0, tk), :]
        scores = {u: lax.dot_general(q_sel[u], k, contract_last, preferred_element_type=F32)
                  for u in units}
        kpos = k0 + lax.broadcasted_iota(jnp.int32, (1, tk), 1)
        for n, (a, g) in enumerate(units):
            rs = slice(n * tq, (n + 1) * tq)
            sc = jnp.where(kpos <= t_col[a], scores[a, g], NEG)
            m_old = m_ref[rs, :]
            m_new = jnp.maximum(m_old, jnp.max(sc, axis=-1, keepdims=True))
            p = jnp.exp2(sc - jnp.concatenate([m_new] * (tk // LANES), axis=1))
            acc_ref[rs, :] = (jnp.exp2(m_old - m_new) * acc_ref[rs, :]
                              + jnp.dot(p.astype(BF16), v, preferred_element_type=F32))
            m_ref[rs, :] = m_new
        return carry

    lax.fori_loop(0, q0 // tk + 1, sel_tile, 0)

    lane = lax.broadcasted_iota(jnp.int32, (tq, LANES), 1)
    low = lane < HEAD_DIM
    for a in subs:
        gates = gate_ref[0, a * tq:(a + 1) * tq, :]
        heads = []
        for g in range(GQA):
            n = a * GQA + g
            acc = acc_ref[n * tq:(n + 1) * tq, :]
            o_sel = acc / acc[:, HEAD_DIM:HEAD_DIM + 1]
            heads.append(gates[:, 3 * g:3 * g + 1] * o_cmp[a, g] + gates[:, 3 * g + 1:3 * g + 2] * o_sel
                         + gates[:, 3 * g + 2:3 * g + 3] * o_win[a, g])
        pair0 = jnp.where(low, heads[0], pltpu.roll(heads[1], HEAD_DIM, 1))
        pair1 = jnp.where(low, heads[2], pltpu.roll(heads[3], HEAD_DIM, 1))
        o_ref[0, a * tq:(a + 1) * tq, :] = jnp.concatenate([pair0, pair1], axis=1).astype(BF16)


def _overlap_matrix_t(n_cmp):
    n = np.arange(n_cmp)[None, :]
    s = np.arange(N_SEL_BLOCKS)[:, None]
    c_start = n * CMP_STRIDE
    s_start = s * SEL_LEN
    ov = (c_start < s_start + SEL_LEN) & (c_start + CMP_LEN > s_start)
    return jnp.asarray(ov.astype(np.float32), BF16)


def nsa_attention(q, ksel, vsel, kwin, vwin, kvc, gates):
    bsz, seq_len, _ = q.shape
    ncb = kvc.shape[3]
    tq = NSA_SUBS * NSA_TQ
    assert seq_len // SEL_LEN <= N_SEL_BLOCKS and seq_len % SEL_TK == 0 and SEL_TK % tq == 0
    assert seq_len >= NSA_TQ + WINDOW
    kv_spec = pl.BlockSpec((1, seq_len, LANES), lambda b, h, i: (b, 0, h))
    rows = GQA * tq
    return pl.pallas_call(
        _nsa_kernel,
        out_shape=jax.ShapeDtypeStruct((bsz, seq_len, NSA_WIDTH), BF16),
        grid=(bsz, N_KV_HEADS, seq_len // tq),
        in_specs=[pl.BlockSpec((1, tq, GQA * LANES), lambda b, h, i: (b, i, h)),
                  kv_spec, kv_spec, kv_spec, kv_spec,
                  pl.BlockSpec((1, 1, 1, ncb, LANES), lambda b, h, i: (0, b, h, 0, 0)),
                  pl.BlockSpec((1, 1, 1, ncb, LANES), lambda b, h, i: (1, b, h, 0, 0)),
                  pl.BlockSpec((1, tq, LANES), lambda b, h, i: (b, i, h)),
                  pl.BlockSpec((N_SEL_BLOCKS, ncb), lambda b, h, i: (0, 0))],
        out_specs=pl.BlockSpec((1, tq, GQA * HEAD_DIM), lambda b, h, i: (b, i, h)),
        scratch_shapes=[pltpu.VMEM((rows, LANES), F32), pltpu.VMEM((rows, LANES), F32)],
        compiler_params=_cparams("parallel", "parallel", "arbitrary"),
        name="nsa_attention",
    )(q, ksel, vsel, kwin, vwin, kvc, kvc, gates, _overlap_matrix_t(ncb))


def _conv_kernel(u_ref, w_ref, cb_ref, g_ref, b_ref, o_ref, pad_ref, *, rows):
    seq_len = u_ref.shape[1]
    halo = 32
    pad_ref[0:halo, :] = jnp.zeros((halo, CONV_WIDTH), F32)
    pad_ref[halo:halo + seq_len, :] = u_ref[0]

    first = halo - CONV_LEN + 1

    def body(c, carry):
        r0 = pl.multiple_of(c * rows, rows)
        win = pad_ref[pl.ds(r0, rows + halo), :]
        acc = jnp.zeros((rows, CONV_WIDTH), F32)
        for sub in range(8):
            shifted = win if sub == 0 else pltpu.roll(win, rows + halo - sub, 0)
            for j in range(CONV_LEN):
                if (first + j) % 8 == sub:
                    a0 = first + j - sub
                    acc = acc + w_ref[j:j + 1, :] * shifted[a0:a0 + rows]
        y = _layer_norm(acc + cb_ref[...], g_ref[...], b_ref[...])
        o_ref[0, pl.ds(r0, rows), :] = (y * _sigmoid(y)).astype(BF16)
        return carry

    lax.fori_loop(0, seq_len // rows, body, 0)


def conformer_conv(u, conv_w, conv_b, ln_g, ln_b, rows=256):
    bsz, seq_len, _ = u.shape
    vec = pl.BlockSpec((1, CONV_WIDTH), lambda b: (0, 0))
    return pl.pallas_call(
        functools.partial(_conv_kernel, rows=rows),
        out_shape=jax.ShapeDtypeStruct((bsz, seq_len, CONV_WIDTH), BF16),
        grid=(bsz,),
        in_specs=[pl.BlockSpec((1, seq_len, CONV_WIDTH), lambda b: (b, 0, 0)),
                  pl.BlockSpec((CONV_LEN, CONV_WIDTH), lambda b: (0, 0)), vec, vec, vec],
        out_specs=pl.BlockSpec((1, seq_len, CONV_WIDTH), lambda b: (b, 0, 0)),
        scratch_shapes=[pltpu.VMEM((seq_len + 32, CONV_WIDTH), F32)],
        compiler_params=_cparams("parallel"),
        name="conformer_conv",
    )(u, conv_w, conv_b.reshape(1, -1), ln_g.reshape(1, -1), ln_b.reshape(1, -1))


def _s5_operators(a_re, a_im, log_dt, b_re, b_im, c_re, c_im, n_chunks):
    tc = S5_CHUNK
    dt = jnp.exp(log_dt.astype(F32))[:, None]
    lr = a_re.astype(F32) * dt
    li = a_im.astype(F32) * dt
    mag = jnp.exp(lr)
    ab_re = mag * jnp.cos(li)
    ab_im = mag * jnp.sin(li)
    den = a_re * a_re + a_im * a_im
    coef_re = ((ab_re - 1.0) * a_re + ab_im * a_im) / den
    coef_im = (ab_im * a_re - (ab_re - 1.0) * a_im) / den
    bb_re = coef_re[..., None] * b_re - coef_im[..., None] * b_im
    bb_im = coef_re[..., None] * b_im + coef_im[..., None] * b_re

    def power(tau):
        tau = jnp.asarray(tau, F32)
        m = jnp.exp(lr[..., None] * tau)
        return m * jnp.cos(li[..., None] * tau), m * jnp.sin(li[..., None] * tau)

    lag_re, lag_im = power(jnp.arange(tc + 1))
    ca_re = c_re[..., None] * lag_re[:, None] - c_im[..., None] * lag_im[:, None]
    ca_im = c_re[..., None] * lag_im[:, None] + c_im[..., None] * lag_re[:, None]
    kern = (jnp.einsum('gcnt,gnd->gtcd', ca_re[..., :tc], bb_re, precision=HIGHEST)
            - jnp.einsum('gcnt,gnd->gtcd', ca_im[..., :tc], bb_im, precision=HIGHEST))
    s_idx = np.arange(tc)[:, None]
    i_idx = np.arange(tc)[None, :]
    lag = np.clip(i_idx - s_idx, 0, tc - 1)
    toep = kern[:, lag]
    toep = jnp.where(jnp.asarray(i_idx >= s_idx)[None, :, :, None, None], toep, 0.0)
    rev_re, rev_im = lag_re[..., tc - 1::-1], lag_im[..., tc - 1::-1]
    bop_re = (rev_re[..., None] * bb_re[:, :, None] - rev_im[..., None] * bb_im[:, :, None])
    bop_im = (rev_re[..., None] * bb_im[:, :, None] + rev_im[..., None] * bb_re[:, :, None])
    bop_re = bop_re.transpose(0, 2, 3, 1)
    bop_im = bop_im.transpose(0, 2, 3, 1)
    cop_re = ca_re[..., 1:].transpose(0, 2, 3, 1)
    cop_im = -ca_im[..., 1:].transpose(0, 2, 3, 1)

    gh = S5_GROUPS // S5_HALVES
    eye = jnp.eye(gh, dtype=F32)
    hw = tc * gh * S5_GROUP_CH
    sw = gh * S5_STATE
    split = lambda x: x.reshape((S5_HALVES, gh) + x.shape[1:])
    m_nat = jnp.einsum('hgsiod,gk->hsgdiko', split(toep), eye).reshape(S5_HALVES, hw, hw)
    b_nat = jnp.concatenate(
        [jnp.einsum('hgsdn,gk->hsgdkn', split(bop), eye).reshape(S5_HALVES, hw, sw) for bop in (bop_re, bop_im)],
        axis=2)
    c_nat = jnp.concatenate(
        [jnp.einsum('hgnio,gk->hgniko', split(cop), eye).reshape(S5_HALVES, sw, hw) for cop in (cop_re, cop_im)],
        axis=1)
    levels = max(1, int(math.log2(n_chunks)))
    lv_re, lv_im = power(tc * (2.0 ** jnp.arange(levels)))
    lanes = lambda x: split(x).transpose(0, 3, 1, 2).reshape(S5_HALVES, levels, sw)
    a_lv = jnp.concatenate([lanes(lv_re), lanes(lv_im)], axis=-1)
    return m_nat.astype(BF16), b_nat.astype(BF16), c_nat.astype(BF16), a_lv


def _s5_kernel(u0_ref, u1_ref, m_ref, b_ref, c_ref, a_ref, d_ref, gw_ref, gb_ref, o0_ref, o1_ref,
               sre_ref, sim_ref):
    tc = S5_CHUNK
    nc = u0_ref.shape[1] // tc
    hl = S5_WIDTH // S5_HALVES
    sw = sre_ref.shape[1]
    u_step = [[u_ref[0, pl.ds(s, nc, stride=tc), :] for s in range(tc)] for u_ref in (u0_ref, u1_ref)]
    y_half = []
    for h in range(S5_HALVES):
        uh = jnp.concatenate(u_step[h], axis=1).astype(BF16)
        v = jnp.dot(uh, b_ref[h], preferred_element_type=F32)
        zeros = jnp.zeros((nc, sw), F32)
        sre_ref[0:nc, :] = zeros
        sim_ref[0:nc, :] = zeros
        sre_ref[nc:2 * nc, :] = v[:, 0:sw]
        sim_ref[nc:2 * nc, :] = v[:, sw:2 * sw]
        for lv in range(a_ref.shape[1]):
            d = 1 << lv
            ar = a_ref[h, lv:lv + 1, 0:sw]
            ai = a_ref[h, lv:lv + 1, sw:2 * sw]
            pr = sre_ref[nc - d:2 * nc - d, :]
            pi = sim_ref[nc - d:2 * nc - d, :]
            cr = sre_ref[nc:2 * nc, :]
            ci = sim_ref[nc:2 * nc, :]
            sre_ref[nc:2 * nc, :] = cr + ar * pr - ai * pi
            sim_ref[nc:2 * nc, :] = ci + ar * pi + ai * pr
        prev_re = sre_ref[nc - 1:2 * nc - 1, :].astype(BF16)
        prev_im = sim_ref[nc - 1:2 * nc - 1, :].astype(BF16)
        y_half.append(jnp.dot(uh, m_ref[h], preferred_element_type=F32)
                      + jnp.dot(prev_re, c_ref[h, 0:sw, :], preferred_element_type=F32)
                      + jnp.dot(prev_im, c_ref[h, sw:2 * sw, :], preferred_element_type=F32))
    for i in range(tc):
        y = jnp.concatenate([yh[:, i * hl:(i + 1) * hl] for yh in y_half], axis=1)
        z = _gelu_tanh(y + d_ref[...] * jnp.concatenate([u_step[0][i], u_step[1][i]], axis=1))
        gate = jnp.dot(z.astype(BF16), gw_ref[...], preferred_element_type=F32) + gb_ref[...]
        out = z * _sigmoid(gate)
        o0_ref[0, pl.ds(i, nc, stride=tc), :] = out[:, 0:hl]
        o1_ref[0, pl.ds(i, nc, stride=tc), :] = out[:, hl:2 * hl]


def s5_layer(zs, operators, d_skip, glu_w, glu_b):
    m_nat, b_nat, c_nat, a_lv = operators
    bsz, seq_len, _ = zs.shape
    tc = S5_CHUNK
    nc = seq_len // tc
    assert nc & (nc - 1) == 0 and a_lv.shape[1] == int(math.log2(nc))
    hl = S5_WIDTH // S5_HALVES
    sw = a_lv.shape[2] // 2
    full = lambda a: pl.BlockSpec(a.shape, lambda b: (0,) * a.ndim)
    vec = pl.BlockSpec((1, S5_WIDTH), lambda b: (0, 0))
    half = lambda h: pl.BlockSpec((1, seq_len, hl), lambda b: (b, 0, h))
    glu_wb = glu_w.astype(BF16)
    out = pl.pallas_call(
        _s5_kernel,
        out_shape=(jax.ShapeDtypeStruct((bsz, seq_len, hl), F32),) * S5_HALVES,
        grid=(bsz,),
        in_specs=[half(0), half(1), full(m_nat), full(b_nat), full(c_nat), full(a_lv), vec, full(glu_wb), vec],
        out_specs=(half(0),) * S5_HALVES,
        scratch_shapes=[pltpu.VMEM((2 * nc, sw), F32), pltpu.VMEM((2 * nc, sw), F32)],
        compiler_params=_cparams("parallel"),
        name="s5_layer",
    )(zs, zs, m_nat, b_nat, c_nat, a_lv, d_skip.reshape(1, -1), glu_wb, glu_b.reshape(1, -1))
    return tuple(o.reshape(bsz * seq_len, hl) for o in out)


def _route_top2(logits):
    lane = lax.broadcasted_iota(jnp.int32, logits.shape, 1)
    lane_f = lane.astype(F32)
    logits = jnp.where(lane < N_EXPERTS, logits, -jnp.inf)
    v1 = jnp.max(logits, axis=-1, keepdims=True)
    i1 = jnp.min(jnp.where(logits == v1, lane_f, 1e9), axis=-1, keepdims=True)
    rest = jnp.where(lane_f == i1, -jnp.inf, logits)
    v2 = jnp.max(rest, axis=-1, keepdims=True)
    i2 = jnp.min(jnp.where(rest == v2, lane_f, 1e9), axis=-1, keepdims=True)
    e2 = jnp.exp(v2 - v1)
    den = 1.0 + e2
    out = jnp.where(lane == 0, i1, jnp.where(lane == 1, i2, jnp.where(lane == 2, 1.0 / den, e2 / den)))
    return jnp.where(lane < 4, out, 0.0)


def _out_proj_kernel(x_ref, a_ref, c_ref, s0_ref, s1_ref, w_ref, g_ref, b_ref, *rest, alpha):
    o_ref = rest[0] if len(rest) == 1 else rest[1]
    o_s5 = jnp.concatenate([s0_ref[...], s1_ref[...]], axis=1).astype(BF16)
    h = jnp.dot(a_ref[...], w_ref[0:NSA_WIDTH, :], preferred_element_type=F32)
    h = h + jnp.dot(c_ref[...], w_ref[NSA_WIDTH:NSA_WIDTH + CONV_WIDTH, :], preferred_element_type=F32)
    h = h + jnp.dot(o_s5, w_ref[NSA_WIDTH + CONV_WIDTH:, :], preferred_element_type=F32)
    y = _layer_norm(alpha * x_ref[...] + h, g_ref[...], b_ref[...])
    o_ref[...] = y
    if len(rest) > 1:
        router_ref, _, routed_ref, ybf_ref = rest
        ybf_ref[...] = y.astype(BF16)
        y_top = _top_bits(y)
        y_hi = y_top.astype(BF16)
        y_lo = (y - y_top).astype(BF16)
        w_hi, w_lo = router_ref[0], router_ref[1]
        logits = (jnp.dot(y_hi, w_hi, preferred_element_type=F32) + jnp.dot(y_hi, w_lo, preferred_element_type=F32)
                  + jnp.dot(y_lo, w_hi, preferred_element_type=F32))
        routed_ref[...] = _route_top2(logits)


def out_proj_ln(x2, o_nsa, o_conv, o_s5, w_out, g, b, alpha, router=None, tm=512):
    t, d = x2.shape
    row = lambda width: pl.BlockSpec((tm, width), lambda i: (i, 0))
    vec = pl.BlockSpec((1, d), lambda i: (0, 0))
    in_specs = [row(d), row(NSA_WIDTH), row(CONV_WIDTH), row(S5_WIDTH // S5_HALVES), row(S5_WIDTH // S5_HALVES),
                pl.BlockSpec(w_out.shape, lambda i: (0, 0)), vec, vec]
    args = [x2, o_nsa, o_conv, *o_s5, w_out.astype(BF16), g.reshape(1, -1), b.reshape(1, -1)]
    out_shape = jax.ShapeDtypeStruct((t, d), F32)
    out_specs = row(d)
    if router is not None:
        w = jnp.pad(router, ((0, 0), (0, LANES - router.shape[1])))
        w_top = lax.bitcast_convert_type(
            lax.bitcast_convert_type(w, jnp.uint32) & jnp.uint32(0xFFFF0000), F32)
        w_hi = w_top.astype(BF16)
        w_lo = (w - w_top).astype(BF16)
        in_specs.append(pl.BlockSpec((2, d, LANES), lambda i: (0, 0, 0)))
        args.append(jnp.stack([w_hi, w_lo]))
        out_shape = (out_shape, jax.ShapeDtypeStruct((t, LANES), F32), jax.ShapeDtypeStruct((t, d), BF16))
        out_specs = (out_specs, row(LANES), row(d))
    return pl.pallas_call(
        functools.partial(_out_proj_kernel, alpha=alpha),
        out_shape=out_shape,
        grid=(t // tm,),
        in_specs=in_specs,
        out_specs=out_specs,
        compiler_params=_cparams("parallel"),
        name="out_proj_ln",
    )(*args)


def _ffn_kernel(te_ref, nu_ref, x_ref, w1_ref, w3_ref, w2_ref, *rest, alpha):
    ln_refs, (o_ref, acc_ref) = rest[:-2], rest[-2:]
    i = pl.program_id(0)
    f = pl.program_id(1)
    last = pl.num_programs(1) - 1
    used = i < nu_ref[0]

    @pl.when(used)
    def _():
        xb = x_ref[...].astype(BF16)
        h1 = jnp.dot(xb, w1_ref[0], preferred_element_type=F32)
        h3 = jnp.dot(xb, w3_ref[0], preferred_element_type=F32)
        h = (h1 * _sigmoid(h1) * h3).astype(BF16)
        part = jnp.dot(h, w2_ref[0], preferred_element_type=F32)

        @pl.when(f == 0)
        def _():
            acc_ref[...] = part

        @pl.when(f > 0)
        def _():
            acc_ref[...] = acc_ref[...] + part

        @pl.when(f == last)
        def _():
            if ln_refs:
                o_ref[...] = _layer_norm(alpha * x_ref[...] + acc_ref[...], ln_refs[0][...], ln_refs[1][...])
            else:
                o_ref[...] = acc_ref[...].astype(o_ref.dtype)

    @pl.when(jnp.logical_not(used) & (f == last))
    def _():
        o_ref[...] = jnp.zeros(o_ref.shape, o_ref.dtype)


def grouped_swiglu(xs, tile_expert, n_used, w1, w3, w2, tm, tf, out_dtype, ln=None, alpha=1.0):
    p, d = xs.shape
    ff = w1.shape[2]
    assert p % tm == 0 and ff % tf == 0
    vec = pl.BlockSpec((1, d), lambda i, f, te, nu: (0, 0))
    ln_args = () if ln is None else (ln[0].reshape(1, d), ln[1].reshape(1, d))
    grid_spec = pltpu.PrefetchScalarGridSpec(
        num_scalar_prefetch=2,
        grid=(p // tm, ff // tf),
        in_specs=[pl.BlockSpec((tm, d), lambda i, f, te, nu: (i, 0)),
                  pl.BlockSpec((1, d, tf), lambda i, f, te, nu: (te[i], 0, f)),
                  pl.BlockSpec((1, d, tf), lambda i, f, te, nu: (te[i], 0, f)),
                  pl.BlockSpec((1, tf, d), lambda i, f, te, nu: (te[i], f, 0))] + [vec] * len(ln_args),
        out_specs=pl.BlockSpec((tm, d), lambda i, f, te, nu: (i, 0)),
        scratch_shapes=[pltpu.VMEM((tm, d), F32)],
    )
    return pl.pallas_call(
        functools.partial(_ffn_kernel, alpha=alpha),
        out_shape=jax.ShapeDtypeStruct((p, d), out_dtype),
        grid_spec=grid_spec,
        compiler_params=_cparams("arbitrary", "arbitrary"),
        name="grouped_swiglu",
    )(tile_expert, n_used, xs, w1, w3, w2, *ln_args)


def _pick_tf(ff, target=1024):
    best = LANES
    for cand in range(LANES, ff + 1, LANES):
        if ff % cand == 0 and cand <= target:
            best = cand
    return best


def _residual_ln_kernel(x_ref, a_ref, b_ref, wa_ref, wb_ref, g_ref, beta_ref, o_ref, *, alpha):
    f = wa_ref[...] * a_ref[...] + wb_ref[...] * b_ref[...]
    o_ref[...] = _layer_norm(alpha * x_ref[...] + f, g_ref[...], beta_ref[...])


def residual_ln(x2, a, b, wa, wb, g, beta, alpha, tm=512):
    t, d = x2.shape
    row = pl.BlockSpec((tm, d), lambda i: (i, 0))
    col = pl.BlockSpec((tm, 1), lambda i: (i, 0))
    vec = pl.BlockSpec((1, d), lambda i: (0, 0))
    return pl.pallas_call(
        functools.partial(_residual_ln_kernel, alpha=alpha),
        out_shape=jax.ShapeDtypeStruct((t, d), F32),
        grid=(t // tm,),
        in_specs=[row, row, row, col, col, vec, vec],
        out_specs=row,
        compiler_params=_cparams("parallel"),
        name="residual_ln",
    )(x2, a, b, wa, wb, g.reshape(1, -1), beta.reshape(1, -1))


def moe_swiglu_ln(x2, x2_bf16, routed, w1, w3, w2, first_expert, g, beta, alpha, tm):
    t, d = x2.shape
    n_exp = N_EXPERTS
    idx = routed[:, 0:2].astype(jnp.int32)
    wts = routed[:, 2:4]
    e_flat = idx.T.reshape(-1)
    onehot = (e_flat[:, None] == jnp.arange(n_exp, dtype=jnp.int32)[None, :]).astype(jnp.int32)
    running = jnp.cumsum(onehot, axis=0)
    counts = running[-1]
    rank = jnp.sum(onehot * running, axis=1) - 1
    padded = ((counts + tm - 1) // tm) * tm
    pad_end = jnp.cumsum(padded)
    pad_start = pad_end - padded
    start = jnp.cumsum(counts) - counts
    pos = pad_start[e_flat] + rank
    n_rows = 2 * t + n_exp * tm
    n_tiles = n_rows // tm
    tile_start = jnp.arange(n_tiles, dtype=jnp.int32) * tm
    tile_expert = jnp.minimum(jnp.sum(tile_start[:, None] >= pad_end[None, :], axis=1), n_exp - 1).astype(jnp.int32)
    n_used = (pad_end[-1] // tm).astype(jnp.int32).reshape(1)
    order = jnp.argsort(e_flat, stable=True)
    row = jnp.arange(n_rows, dtype=jnp.int32)
    row_expert = jnp.repeat(tile_expert, tm)
    slot = jnp.minimum(start[row_expert] + (row - pad_start[row_expert]), 2 * t - 1)
    src = order[slot] % t
    xs = x2_bf16[src]
    ys = grouped_swiglu(xs, tile_expert + first_expert, n_used, w1, w3, w2, tm, _pick_tf(w1.shape[2]), BF16)
    return residual_ln(x2, ys[pos[:t]], ys[pos[t:]], wts[:, 0:1], wts[:, 1:2], g, beta, alpha)


def dense_swiglu_ln(x2, w1, w3, w2, layer, g, beta, alpha, tm):
    t = x2.shape[0]
    tile_expert = jnp.full((t // tm,), layer, jnp.int32)
    n_used = jnp.full((1,), t // tm, jnp.int32)
    return grouped_swiglu(x2, tile_expert, n_used, w1, w3, w2, tm, _pick_tf(w1.shape[2], 1408),
                          F32, ln=(g, beta), alpha=alpha)


def _ple_kernel(x_ref, p_ref, wg_ref, bg_ref, wp_ref, g_ref, beta_ref, o_ref, *, alpha):
    x = x_ref[...]
    gate = _sigmoid(jnp.dot(x.astype(BF16), wg_ref[...], preferred_element_type=F32) + bg_ref[...])
    e = jnp.dot(p_ref[...].astype(BF16), wp_ref[...], preferred_element_type=F32) * gate
    o_ref[...] = _layer_norm(alpha * x + e, g_ref[...], beta_ref[...])


def ple_ln(x2, p_all, layer, gate_w, gate_b, proj, g, beta, alpha, tm=512):
    t, d = x2.shape
    pd = p_all.shape[1]
    first = layer * (t // tm)
    vec = pl.BlockSpec((1, d), lambda i: (0, 0))
    return pl.pallas_call(
        functools.partial(_ple_kernel, alpha=alpha),
        out_shape=jax.ShapeDtypeStruct((t, d), F32),
        grid=(t // tm,),
        in_specs=[pl.BlockSpec((tm, d), lambda i: (i, 0)), pl.BlockSpec((tm, pd), lambda i: (first + i, 0)),
                  pl.BlockSpec((d, d), lambda i: (0, 0)), vec,
                  pl.BlockSpec((pd, d), lambda i: (0, 0)), vec, vec],
        out_specs=pl.BlockSpec((tm, d), lambda i: (i, 0)),
        compiler_params=_cparams("parallel"),
        name="ple_ln",
    )(x2, p_all, gate_w.astype(BF16), gate_b.reshape(1, -1), proj.astype(BF16), g.reshape(1, -1), beta.reshape(1, -1))


def hybrid_mixer_ln(x2, bsz, seq_len, tables, w_in, w_out, cmp_weights, conv_params, s5_params,
                    g, beta, alpha, router=None):
    (q, ksel, kwin, vsel, vwin, kv_cmp, gates, u_conv, zs) = in_proj(x2, _widen_w_in(w_in), tables, seq_len)
    b3 = lambda a: a.reshape(bsz, seq_len, a.shape[1])
    kvc = compress(kv_cmp.reshape(2, bsz, seq_len, LANES), _compress_weights(*cmp_weights))
    o_nsa = nsa_attention(b3(q), b3(ksel), b3(vsel), b3(kwin), b3(vwin), kvc, b3(gates))
    o_conv = conformer_conv(b3(u_conv), *conv_params)
    (a_re, a_im, log_dt, b_re, b_im, c_re, c_im, d_skip, glu_w, glu_b) = s5_params
    ops = _s5_operators(a_re, a_im, log_dt, b_re, b_im, c_re, c_im, seq_len // S5_CHUNK)
    o_s5 = s5_layer(b3(zs), ops, d_skip, glu_w, glu_b)
    return out_proj_ln(x2, o_nsa.reshape(-1, NSA_WIDTH), o_conv.reshape(-1, CONV_WIDTH), o_s5, w_out,
                       g, beta, alpha, router)


def kernel(x, p, positions, w_in, w_out, cmp_pe, cmp_w1, cmp_b1, cmp_w2, cmp_b2, conv_w, conv_b, conv_ln_g, conv_ln_b, s5_a_re, s5_a_im, s5_log_dt, s5_b_re, s5_b_im, s5_c_re, s5_c_im, s5_d, s5_glu_w, s5_glu_b, ffn_w1, ffn_w3, ffn_w2, moe_router, moe_w1, moe_w3, moe_w2, ple_gate_w, ple_gate_b, ple_proj, ln_g, ln_b):
    bsz, seq_len, d_model = x.shape
    depth = w_in.shape[0]
    alpha = (2 * depth) ** 0.25
    t = bsz * seq_len
    tables = rope_tables(positions)
    x2 = x.reshape(t, d_model)
    dense_tm = 512
    moe_tm = 512 if t % 1024 else 1024
    stack = lambda w: w.astype(BF16).reshape((-1,) + w.shape[-2:])
    ffn_w = (stack(ffn_w1), stack(ffn_w3), stack(ffn_w2))
    moe_w = (stack(moe_w1), stack(moe_w3), stack(moe_w2))
    for i in range(depth):
        j = i // 2
        routed_layer = i % 2 == 1
        mixed = hybrid_mixer_ln(
            x2, bsz, seq_len, tables, w_in[i], w_out[i],
            (cmp_pe[i], cmp_w1[i], cmp_b1[i], cmp_w2[i], cmp_b2[i]),
            (conv_w[i], conv_b[i], conv_ln_g[i], conv_ln_b[i]),
            (s5_a_re[i], s5_a_im[i], s5_log_dt[i], s5_b_re[i], s5_b_im[i], s5_c_re[i], s5_c_im[i],
             s5_d[i], s5_glu_w[i], s5_glu_b[i]),
            ln_g[i, 0], ln_b[i, 0], alpha, moe_router[j] if routed_layer else None)
        if routed_layer:
            x2, routed, x2_bf16 = mixed
            x2 = moe_swiglu_ln(x2, x2_bf16, routed, *moe_w, j * N_EXPERTS, ln_g[i, 1], ln_b[i, 1], alpha, moe_tm)
        else:
            x2 = dense_swiglu_ln(mixed, *ffn_w, j, ln_g[i, 1], ln_b[i, 1], alpha, dense_tm)
        x2 = ple_ln(x2, p.reshape(depth * t, -1), i, ple_gate_w[i], ple_gate_b[i], ple_proj[i],
                    ln_g[i, 2], ln_b[i, 2], alpha)
    return x2.reshape(bsz, seq_len, d_model)
```

```python
import functools
import math

import numpy as np
import jax
import jax.numpy as jnp
from jax import lax
from jax.experimental import pallas as pl
from jax.experimental.pallas import tpu as pltpu

F32 = jnp.float32
BF16 = jnp.bfloat16
HIGHEST = lax.Precision.HIGHEST

LANES = 128
VMEM_LIMIT = 56 * 1024 * 1024

HEAD_DIM = 64
HALF_DIM = HEAD_DIM // 2
N_Q_HEADS = 8
N_KV_HEADS = 2
GQA = N_Q_HEADS // N_KV_HEADS
CMP_LEN = 32
CMP_STRIDE = 16
CMP_HIDDEN = 256
SEL_LEN = 64
SEL_TOPK = 16
WINDOW = 512
ROPE_THETA = 10000.0
FORCED_SCORE = 1e9
NEG = -1e30
CONV_WIDTH = 256
CONV_LEN = 31
S5_WIDTH = 256
S5_GROUP_CH = 16
S5_GROUPS = 16
S5_STATE = 64
S5_CHUNK = 8
S5_HALVES = 2
N_EXPERTS = 8
LN_EPS = 1e-5
NSA_WIDTH = N_Q_HEADS * HEAD_DIM

C_Q = 0
C_KV = 512
C_GATE = 1280
C_CONVA = 1536
C_CONVB = 1792
C_S5 = 2048
C_TOTAL = 2304
Q_SCALE = HEAD_DIM ** -0.5 * math.log2(math.e)


def _cparams(*sem):
    return pltpu.CompilerParams(dimension_semantics=sem, vmem_limit_bytes=VMEM_LIMIT)


def _layer_norm(v, g, b):
    mu = jnp.mean(v, axis=-1, keepdims=True)
    d = v - mu
    var = jnp.mean(d * d, axis=-1, keepdims=True)
    return d * lax.rsqrt(var + LN_EPS) * g + b


def _gelu_tanh(x):
    return 0.5 * x * (1.0 + jnp.tanh(math.sqrt(2.0 / math.pi) * (x + 0.044715 * (x * x * x))))


def _sigmoid(x):
    return 1.0 / (1.0 + jnp.exp(-x))


def _top_bits(x):
    return pltpu.bitcast(pltpu.bitcast(x, jnp.uint32) & jnp.uint32(0xFFFF0000), F32)


def _rope_table_kernel(pos_ref, freq_ref, tab_ref):
    ang = pos_ref[...] * freq_ref[...]
    c = jnp.cos(ang)
    s = jnp.sin(ang)
    lane = lax.broadcasted_iota(jnp.int32, ang.shape, 1)
    first_half = (lane % HEAD_DIM) < HALF_DIM
    tab_ref[:, 0:LANES] = c
    tab_ref[:, LANES:2 * LANES] = jnp.where(first_half, -s, 0.0)
    tab_ref[:, 2 * LANES:3 * LANES] = jnp.where(first_half, 0.0, s)


def rope_tables(positions, tm=512):
    t = positions.size
    pos = positions.reshape(t, 1).astype(F32)
    inv_freq = ROPE_THETA ** (-jnp.arange(0, HEAD_DIM, 2, dtype=F32) / HEAD_DIM)
    freq = jnp.tile(inv_freq, LANES // HALF_DIM).reshape(1, LANES)
    return pl.pallas_call(
        _rope_table_kernel,
        out_shape=jax.ShapeDtypeStruct((t, 3 * LANES), F32),
        grid=(t // tm,),
        in_specs=[pl.BlockSpec((tm, 1), lambda i: (i, 0)),
                  pl.BlockSpec((1, LANES), lambda i: (0, 0))],
        out_specs=pl.BlockSpec((tm, 3 * LANES), lambda i: (i, 0)),
        compiler_params=_cparams("parallel"),
        name="rope_tables",
    )(pos, freq)


def _in_proj_kernel(x_ref, w_ref, tab_ref, q_ref, ksel_ref, kwin_ref, vsel_ref, vwin_ref,
                    kvcmp_ref, gate_ref, u_ref, zs_ref, *, seq_len):
    tm = x_ref.shape[0]
    xb = x_ref[...].astype(BF16)
    cos = tab_ref[:, 0:LANES]
    sin_a = tab_ref[:, LANES:2 * LANES]
    sin_b = tab_ref[:, 2 * LANES:3 * LANES]

    def proj(c0):
        z = jnp.dot(xb, w_ref[:, c0:c0 + 2 * LANES], preferred_element_type=F32)
        return z[:, 0:LANES], z[:, LANES:2 * LANES]

    def rope(z):
        return (z * cos + pltpu.roll(z, LANES - HALF_DIM, 1) * sin_a
                + pltpu.roll(z, HALF_DIM, 1) * sin_b)

    lane = lax.broadcasted_iota(jnp.int32, (tm, LANES), 1)
    row = lax.broadcasted_iota(jnp.int32, (tm, LANES), 0)
    low = lane < HEAD_DIM
    t_seq = (pl.program_id(0) * tm) % seq_len + row
    blk_onehot = jnp.where(lane == HEAD_DIM + t_seq // SEL_LEN, 1.0, 0.0)
    ones_lane = jnp.where(lane == HEAD_DIM, 1.0, 0.0)

    def spread(z, extra, out_ref, c0):
        out_ref[:, c0:c0 + LANES] = (jnp.where(low, z, 0.0) + extra).astype(BF16)
        out_ref[:, c0 + LANES:c0 + 2 * LANES] = (jnp.where(low, pltpu.roll(z, HEAD_DIM, 1), 0.0) + extra).astype(BF16)

    for pair in range(N_Q_HEADS // 4):
        for n, z in enumerate(proj(C_Q + pair * 2 * LANES)):
            spread(rope(z) * Q_SCALE, 0.0, q_ref, (2 * pair + n) * 2 * LANES)
    k_cmp, v_cmp = proj(C_KV)
    kvcmp_ref[0] = rope(k_cmp)
    kvcmp_ref[1] = v_cmp
    k_sel, v_sel = proj(C_KV + 2 * LANES)
    spread(rope(k_sel), blk_onehot, ksel_ref, 0)
    spread(v_sel, ones_lane, vsel_ref, 0)
    k_win, v_win = proj(C_KV + 4 * LANES)
    spread(rope(k_win), 0.0, kwin_ref, 0)
    spread(v_win, ones_lane, vwin_ref, 0)
    g0, g1 = proj(C_GATE)
    gate_ref[:, 0:LANES] = _sigmoid(g0)
    gate_ref[:, LANES:2 * LANES] = _sigmoid(g1)
    a0, a1 = proj(C_CONVA)
    b0, b1 = proj(C_CONVB)
    u_ref[:, 0:LANES] = a0 * _sigmoid(b0)
    u_ref[:, LANES:2 * LANES] = a1 * _sigmoid(b1)
    s0, s1 = proj(C_S5)
    zs_ref[:, 0:LANES] = s0
    zs_ref[:, LANES:2 * LANES] = s1


def _widen_w_in(w_in):
    d = w_in.shape[0]
    o_gate = NSA_WIDTH + 6 * N_KV_HEADS * HEAD_DIM
    o_conv = o_gate + 3 * N_Q_HEADS
    per_group = 3 * GQA
    cols = [w_in[:, :o_gate]]
    for h in range(N_KV_HEADS):
        cols += [w_in[:, o_gate + h * per_group:o_gate + (h + 1) * per_group],
                 jnp.zeros((d, LANES - per_group), w_in.dtype)]
    cols += [w_in[:, o_conv:]]
    w = jnp.concatenate(cols, axis=1)
    assert w.shape[1] == C_TOTAL
    return w.astype(BF16)


def in_proj(x2, w_wide, tables, seq_len, tm=1024):
    t, d = x2.shape
    row = lambda width: pl.BlockSpec((tm, width), lambda i: (i, 0))
    out_shape = (
        jax.ShapeDtypeStruct((t, N_Q_HEADS * LANES), BF16),
        jax.ShapeDtypeStruct((t, N_KV_HEADS * LANES), BF16),
        jax.ShapeDtypeStruct((t, N_KV_HEADS * LANES), BF16),
        jax.ShapeDtypeStruct((t, N_KV_HEADS * LANES), BF16),
        jax.ShapeDtypeStruct((t, N_KV_HEADS * LANES), BF16),
        jax.ShapeDtypeStruct((2, t, LANES), F32),
        jax.ShapeDtypeStruct((t, 2 * LANES), F32),
        jax.ShapeDtypeStruct((t, CONV_WIDTH), F32),
        jax.ShapeDtypeStruct((t, S5_WIDTH), F32),
    )
    out_spec = lambda s: (row(s.shape[1]) if len(s.shape) == 2
                          else pl.BlockSpec((s.shape[0], tm, s.shape[2]), lambda i: (0, i, 0)))
    return pl.pallas_call(
        functools.partial(_in_proj_kernel, seq_len=seq_len),
        out_shape=out_shape,
        grid=(t // tm,),
        in_specs=[row(d),
                  pl.BlockSpec((d, C_TOTAL), lambda i: (0, 0)),
                  row(3 * LANES)],
        out_specs=tuple(out_spec(s) for s in out_shape),
        compiler_params=_cparams("parallel"),
        name="in_proj",
    )(x2, w_wide, tables)


def _compress_kernel(x_ref, pe_ref, w1_ref, b1_ref, w2_ref, b2_ref, o_ref, shift_ref):
    nb = x_ref.shape[2] // CMP_STRIDE
    x = jnp.concatenate([x_ref[0, 0, pl.ds(s, nb, stride=CMP_STRIDE), :] for s in range(CMP_STRIDE)],
                        axis=1)
    xa = (x + pe_ref[0, 0:1, :]).astype(BF16)
    xb = (x + pe_ref[0, 1:2, :]).astype(BF16)
    ha = jnp.dot(xa, w1_ref[0, 0], preferred_element_type=F32)
    hb = jnp.dot(xb, w1_ref[0, 1], preferred_element_type=F32)
    shift_ref[0:nb, :] = hb
    shift_ref[nb:nb + 8, :] = jnp.zeros((8, hb.shape[1]), F32)
    h = ha + shift_ref[1:nb + 1, :] + b1_ref[0]
    g = _gelu_tanh(h).astype(BF16)
    for hd in range(N_KV_HEADS):
        gh = g[:, hd * CMP_HIDDEN:(hd + 1) * CMP_HIDDEN]
        o_ref[0, 0, hd] = (jnp.dot(gh, w2_ref[0], preferred_element_type=F32) + b2_ref[0]).astype(BF16)


def _compress_weights(cmp_pe, cmp_w1, cmp_b1, cmp_w2, cmp_b2):
    half = CMP_LEN // 2
    pe = cmp_pe.reshape(2, 2, half, 1, HEAD_DIM)
    pe = jnp.broadcast_to(pe, (2, 2, half, N_KV_HEADS, HEAD_DIM)).reshape(2, 2, half * N_KV_HEADS * HEAD_DIM)
    w1 = cmp_w1.reshape(2, 2, half, HEAD_DIM, CMP_HIDDEN)
    z = jnp.zeros_like(w1)
    w_h0 = jnp.stack([w1, z], axis=3)
    w_h1 = jnp.stack([z, w1], axis=3)
    w1w = jnp.concatenate([w_h0, w_h1], axis=-1)
    w1w = w1w.reshape(2, 2, half * N_KV_HEADS * HEAD_DIM, N_KV_HEADS * CMP_HIDDEN).astype(BF16)
    b1 = jnp.tile(cmp_b1, (1, N_KV_HEADS)).reshape(2, 1, N_KV_HEADS * CMP_HIDDEN)
    w2 = jnp.pad(cmp_w2, ((0, 0), (0, 0), (0, LANES - HEAD_DIM))).astype(BF16)
    b2 = jnp.pad(cmp_b2, ((0, 0), (0, LANES - HEAD_DIM))).reshape(2, 1, LANES)
    return pe, w1w, b1, w2, b2


def compress(kv_cmp, weights):
    pe, w1w, b1, w2, b2 = weights
    _, bsz, seq_len, _ = kv_cmp.shape
    nb = seq_len // CMP_STRIDE
    cw = CMP_STRIDE * LANES
    hid = N_KV_HEADS * CMP_HIDDEN
    return pl.pallas_call(
        _compress_kernel,
        out_shape=jax.ShapeDtypeStruct((2, bsz, N_KV_HEADS, nb, LANES), BF16),
        grid=(2, bsz),
        in_specs=[pl.BlockSpec((1, 1, seq_len, LANES), lambda j, b: (j, b, 0, 0)),
                  pl.BlockSpec((1, 2, cw), lambda j, b: (j, 0, 0)),
                  pl.BlockSpec((1, 2, cw, hid), lambda j, b: (j, 0, 0, 0)),
                  pl.BlockSpec((1, 1, hid), lambda j, b: (j, 0, 0)),
                  pl.BlockSpec((1, CMP_HIDDEN, LANES), lambda j, b: (j, 0, 0)),
                  pl.BlockSpec((1, 1, LANES), lambda j, b: (j, 0, 0))],
        out_specs=pl.BlockSpec((1, 1, N_KV_HEADS, nb, LANES), lambda j, b: (j, b, 0, 0, 0)),
        scratch_shapes=[pltpu.VMEM((nb + 8, hid), F32)],
        compiler_params=_cparams("parallel", "parallel"),
        name="compress",
    )(kv_cmp, pe, w1w, b1, w2, b2)


SEL_TK = 512
NSA_TQ = 128
NSA_SUBS = 4
N_SEL_BLOCKS = LANES - HEAD_DIM


def _nsa_kernel(q_ref, ksel_ref, vsel_ref, kwin_ref, vwin_ref, kc_ref, vc_ref, gate_ref, ovt_ref,
                o_ref, m_ref, acc_ref):
    tq = NSA_TQ
    subs = range(q_ref.shape[1] // tq)
    units = [(a, g) for a in subs for g in range(GQA)]
    i = pl.program_id(2)
    q0 = i * q_ref.shape[1]
    qh = {(a, g): q_ref[0, a * tq:(a + 1) * tq, g * LANES:(g + 1) * LANES] for a, g in units}
    t_col = [q0 + a * tq + lax.broadcasted_iota(jnp.int32, (tq, 1), 0) for a in subs]
    t_row = [q0 + a * tq + lax.broadcasted_iota(jnp.int32, (1, tq), 1) for a in subs]
    contract_last = (((1,), (1,)), ((), ()))

    ncb = kc_ref.shape[3]
    kc = kc_ref[0, 0, 0]
    vc = vc_ref[0, 0, 0]
    s_cmp = {u: lax.dot_general(qh[u], kc, contract_last, preferred_element_type=F32) for u in units}
    wn = tq + WINDOW
    ks = [pl.multiple_of(jnp.maximum(q0 + a * tq - WINDOW, 0), tq) for a in subs]
    kw = [kwin_ref[0, pl.ds(ks[a], wn), :] for a in subs]
    vw = [vwin_ref[0, pl.ds(ks[a], wn), :] for a in subs]
    s_win = {(a, g): lax.dot_general(qh[a, g], kw[a], contract_last, preferred_element_type=F32)
             for a, g in units}

    cmp_end = lax.broadcasted_iota(jnp.int32, (1, ncb), 1) * CMP_STRIDE + (CMP_LEN - 1)
    o_cmp = {}
    p_sum = [None for _ in subs]
    for a, g in units:
        valid = cmp_end <= t_col[a]
        s = jnp.where(valid, s_cmp[a, g], NEG)
        e = jnp.where(valid, jnp.exp2(s - jnp.max(s, axis=-1, keepdims=True)), 0.0)
        p = e / jnp.maximum(jnp.sum(e, axis=-1, keepdims=True), 1e-30)
        o_cmp[a, g] = jnp.dot(p.astype(BF16), vc, preferred_element_type=F32)
        p_sum[a] = p if p_sum[a] is None else p_sum[a] + p

    o_win = {}
    for a, g in units:
        wpos = ks[a] + lax.broadcasted_iota(jnp.int32, (1, wn), 1)
        wmask = (wpos <= t_col[a]) & (wpos > t_col[a] - WINDOW)
        sw = jnp.where(wmask, s_win[a, g], NEG)
        pw = jnp.exp2(sw - jnp.max(sw, axis=-1, keepdims=True))
        accw = jnp.dot(pw.astype(BF16), vw[a], preferred_element_type=F32)
        o_win[a, g] = accw / accw[:, HEAD_DIM:HEAD_DIM + 1]

    ovt = ovt_ref[...]
    nblk = N_SEL_BLOCKS
    q_sel = {}
    for a in subs:
        t1 = _top_bits(p_sum[a])
        r1 = p_sum[a] - t1
        t2 = _top_bits(r1)
        p1, p2, p3 = t1.astype(BF16), t2.astype(BF16), (r1 - t2).astype(BF16)
        imp_t = (lax.dot_general(ovt, p1, contract_last, preferred_element_type=F32)
                 + lax.dot_general(ovt, p2, contract_last, preferred_element_type=F32)
                 + lax.dot_general(ovt, p3, contract_last, preferred_element_type=F32))

        blk = lax.broadcasted_iota(jnp.int32, (nblk, tq), 0)
        qblk = t_row[a] // SEL_LEN
        causal_blk = blk <= qblk
        forced = (blk == 0) | (blk == qblk) | (blk == qblk - 1)
        score_t = jnp.where(forced, FORCED_SCORE, jnp.where(causal_blk, imp_t, NEG))
        groups = [score_t[8 * r:8 * r + 8] for r in range(nblk // 8)]
        sub = lax.broadcasted_iota(jnp.int32, (8, tq), 0)
        later = [jnp.where(sub > s, 1.0, 0.0) for s in range(8)]
        counts = [jnp.zeros((8, tq), F32) for _ in groups]
        for b in range(nblk):
            row = score_t[b:b + 1]
            for r, grp in enumerate(groups):
                if 8 * r > b:
                    counts[r] = counts[r] + jnp.where(row >= grp, 1.0, 0.0)
                elif 8 * r + 7 <= b:
                    counts[r] = counts[r] + jnp.where(row > grp, 1.0, 0.0)
                else:
                    counts[r] = (counts[r] + jnp.where(row > grp, 1.0, 0.0)
                                 + jnp.where(row == grp, later[b - 8 * r], 0.0))
        keep_t = (jnp.concatenate(counts, axis=0) < float(SEL_TOPK)) & causal_blk
        bias_t = jnp.where(keep_t, 0.0, NEG)
        bias = jnp.concatenate([jnp.zeros((LANES - nblk, tq), F32), bias_t], axis=0).T.astype(BF16)
        for g in range(GQA):
            q_sel[a, g] = qh[a, g] + bias

    tk = SEL_TK
    m_ref[...] = jnp.full(m_ref.shape, NEG, F32)
    acc_ref[...] = jnp.zeros(acc_ref.shape, F32)

    def sel_tile(j, carry):
        k0 = pl.multiple_of(j * tk, tk)
        k = ksel_ref[0, pl.ds(k0, tk), :]
        v = vsel_ref[0, pl.ds(k0, tk), :]
        scores = {u: lax.dot_general(q_sel[u], k, contract_last, preferred_element_type=F32)
                  for u in units}
        kpos = k0 + lax.broadcasted_iota(jnp.int32, (1, tk), 1)
        for n, (a, g) in enumerate(units):
            rs = slice(n * tq, (n + 1) * tq)
            sc = jnp.where(kpos <= t_col[a], scores[a, g], NEG)
            m_old = m_ref[rs, :]
            m_new = jnp.maximum(m_old, jnp.max(sc, axis=-1, keepdims=True))
            p = jnp.exp2(sc - jnp.concatenate([m_new] * (tk // LANES), axis=1))
            acc_ref[rs, :] = (jnp.exp2(m_old - m_new) * acc_ref[rs, :]
                              + jnp.dot(p.astype(BF16), v, preferred_element_type=F32))
            m_ref[rs, :] = m_new
        return carry

    lax.fori_loop(0, q0 // tk + 1, sel_tile, 0)

    lane = lax.broadcasted_iota(jnp.int32, (tq, LANES), 1)
    low = lane < HEAD_DIM
    for a in subs:
        gates = gate_ref[0, a * tq:(a + 1) * tq, :]
        heads = []
        for g in range(GQA):
            n = a * GQA + g
            acc = acc_ref[n * tq:(n + 1) * tq, :]
            o_sel = acc / acc[:, HEAD_DIM:HEAD_DIM + 1]
            heads.append(gates[:, 3 * g:3 * g + 1] * o_cmp[a, g] + gates[:, 3 * g + 1:3 * g + 2] * o_sel
                         + gates[:, 3 * g + 2:3 * g + 3] * o_win[a, g])
        pair0 = jnp.where(low, heads[0], pltpu.roll(heads[1], HEAD_DIM, 1))
        pair1 = jnp.where(low, heads[2], pltpu.roll(heads[3], HEAD_DIM, 1))
        o_ref[0, a * tq:(a + 1) * tq, :] = jnp.concatenate([pair0, pair1], axis=1).astype(BF16)


def _overlap_matrix_t(n_cmp):
    n = np.arange(n_cmp)[None, :]
    s = np.arange(N_SEL_BLOCKS)[:, None]
    c_start = n * CMP_STRIDE
    s_start = s * SEL_LEN
    ov = (c_start < s_start + SEL_LEN) & (c_start + CMP_LEN > s_start)
    return jnp.asarray(ov.astype(np.float32), BF16)


def nsa_attention(q, ksel, vsel, kwin, vwin, kvc, gates):
    bsz, seq_len, _ = q.shape
    ncb = kvc.shape[3]
    tq = NSA_SUBS * NSA_TQ
    assert seq_len // SEL_LEN <= N_SEL_BLOCKS and seq_len % SEL_TK == 0 and SEL_TK % tq == 0
    assert seq_len >= NSA_TQ + WINDOW
    kv_spec = pl.BlockSpec((1, seq_len, LANES), lambda b, h, i: (b, 0, h))
    rows = GQA * tq
    return pl.pallas_call(
        _nsa_kernel,
        out_shape=jax.ShapeDtypeStruct((bsz, seq_len, NSA_WIDTH), BF16),
        grid=(bsz, N_KV_HEADS, seq_len // tq),
        in_specs=[pl.BlockSpec((1, tq, GQA * LANES), lambda b, h, i: (b, i, h)),
                  kv_spec, kv_spec, kv_spec, kv_spec,
                  pl.BlockSpec((1, 1, 1, ncb, LANES), lambda b, h, i: (0, b, h, 0, 0)),
                  pl.BlockSpec((1, 1, 1, ncb, LANES), lambda b, h, i: (1, b, h, 0, 0)),
                  pl.BlockSpec((1, tq, LANES), lambda b, h, i: (b, i, h)),
                  pl.BlockSpec((N_SEL_BLOCKS, ncb), lambda b, h, i: (0, 0))],
        out_specs=pl.BlockSpec((1, tq, GQA * HEAD_DIM), lambda b, h, i: (b, i, h)),
        scratch_shapes=[pltpu.VMEM((rows, LANES), F32), pltpu.VMEM((rows, LANES), F32)],
        compiler_params=_cparams("parallel", "parallel", "arbitrary"),
        name="nsa_attention",
    )(q, ksel, vsel, kwin, vwin, kvc, kvc, gates, _overlap_matrix_t(ncb))


def _conv_kernel(u_ref, w_ref, cb_ref, g_ref, b_ref, o_ref, pad_ref, *, rows):
    seq_len = u_ref.shape[1]
    halo = 32
    pad_ref[0:halo, :] = jnp.zeros((halo, CONV_WIDTH), F32)
    pad_ref[halo:halo + seq_len, :] = u_ref[0]

    first = halo - CONV_LEN + 1

    def body(c, carry):
        r0 = pl.multiple_of(c * rows, rows)
        win = pad_ref[pl.ds(r0, rows + halo), :]
        acc = jnp.zeros((rows, CONV_WIDTH), F32)
        for sub in range(8):
            shifted = win if sub == 0 else pltpu.roll(win, rows + halo - sub, 0)
            for j in range(CONV_LEN):
                if (first + j) % 8 == sub:
                    a0 = first + j - sub
                    acc = acc + w_ref[j:j + 1, :] * shifted[a0:a0 + rows]
        y = _layer_norm(acc + cb_ref[...], g_ref[...], b_ref[...])
        o_ref[0, pl.ds(r0, rows), :] = (y * _sigmoid(y)).astype(BF16)
        return carry

    lax.fori_loop(0, seq_len // rows, body, 0)


def conformer_conv(u, conv_w, conv_b, ln_g, ln_b, rows=256):
    bsz, seq_len, _ = u.shape
    vec = pl.BlockSpec((1, CONV_WIDTH), lambda b: (0, 0))
    return pl.pallas_call(
        functools.partial(_conv_kernel, rows=rows),
        out_shape=jax.ShapeDtypeStruct((bsz, seq_len, CONV_WIDTH), BF16),
        grid=(bsz,),
        in_specs=[pl.BlockSpec((1, seq_len, CONV_WIDTH), lambda b: (b, 0, 0)),
                  pl.BlockSpec((CONV_LEN, CONV_WIDTH), lambda b: (0, 0)), vec, vec, vec],
        out_specs=pl.BlockSpec((1, seq_len, CONV_WIDTH), lambda b: (b, 0, 0)),
        scratch_shapes=[pltpu.VMEM((seq_len + 32, CONV_WIDTH), F32)],
        compiler_params=_cparams("parallel"),
        name="conformer_conv",
    )(u, conv_w, conv_b.reshape(1, -1), ln_g.reshape(1, -1), ln_b.reshape(1, -1))


def _s5_operators(a_re, a_im, log_dt, b_re, b_im, c_re, c_im, n_chunks):
    tc = S5_CHUNK
    dt = jnp.exp(log_dt.astype(F32))[:, None]
    lr = a_re.astype(F32) * dt
    li = a_im.astype(F32) * dt
    mag = jnp.exp(lr)
    ab_re = mag * jnp.cos(li)
    ab_im = mag * jnp.sin(li)
    den = a_re * a_re + a_im * a_im
    coef_re = ((ab_re - 1.0) * a_re + ab_im * a_im) / den
    coef_im = (ab_im * a_re - (ab_re - 1.0) * a_im) / den
    bb_re = coef_re[..., None] * b_re - coef_im[..., None] * b_im
    bb_im = coef_re[..., None] * b_im + coef_im[..., None] * b_re

    def power(tau):
        tau = jnp.asarray(tau, F32)
        m = jnp.exp(lr[..., None] * tau)
        return m * jnp.cos(li[..., None] * tau), m * jnp.sin(li[..., None] * tau)

    lag_re, lag_im = power(jnp.arange(tc + 1))
    ca_re = c_re[..., None] * lag_re[:, None] - c_im[..., None] * lag_im[:, None]
    ca_im = c_re[..., None] * lag_im[:, None] + c_im[..., None] * lag_re[:, None]
    kern = (jnp.einsum('gcnt,gnd->gtcd', ca_re[..., :tc], bb_re, precision=HIGHEST)
            - jnp.einsum('gcnt,gnd->gtcd', ca_im[..., :tc], bb_im, precision=HIGHEST))
    s_idx = np.arange(tc)[:, None]
    i_idx = np.arange(tc)[None, :]
    lag = np.clip(i_idx - s_idx, 0, tc - 1)
    toep = kern[:, lag]
    toep = jnp.where(jnp.asarray(i_idx >= s_idx)[None, :, :, None, None], toep, 0.0)
    rev_re, rev_im = lag_re[..., tc - 1::-1], lag_im[..., tc - 1::-1]
    bop_re = (rev_re[..., None] * bb_re[:, :, None] - rev_im[..., None] * bb_im[:, :, None])
    bop_im = (rev_re[..., None] * bb_im[:, :, None] + rev_im[..., None] * bb_re[:, :, None])
    bop_re = bop_re.transpose(0, 2, 3, 1)
    bop_im = bop_im.transpose(0, 2, 3, 1)
    cop_re = ca_re[..., 1:].transpose(0, 2, 3, 1)
    cop_im = -ca_im[..., 1:].transpose(0, 2, 3, 1)

    gh = S5_GROUPS // S5_HALVES
    eye = jnp.eye(gh, dtype=F32)
    hw = tc * gh * S5_GROUP_CH
    sw = gh * S5_STATE
    split = lambda x: x.reshape((S5_HALVES, gh) + x.shape[1:])
    m_nat = jnp.einsum('hgsiod,gk->hsgdiko', split(toep), eye).reshape(S5_HALVES, hw, hw)
    b_nat = jnp.concatenate(
        [jnp.einsum('hgsdn,gk->hsgdkn', split(bop), eye).reshape(S5_HALVES, hw, sw) for bop in (bop_re, bop_im)],
        axis=2)
    c_nat = jnp.concatenate(
        [jnp.einsum('hgnio,gk->hgniko', split(cop), eye).reshape(S5_HALVES, sw, hw) for cop in (cop_re, cop_im)],
        axis=1)
    levels = max(1, int(math.log2(n_chunks)))
    lv_re, lv_im = power(tc * (2.0 ** jnp.arange(levels)))
    lanes = lambda x: split(x).transpose(0, 3, 1, 2).reshape(S5_HALVES, levels, sw)
    a_lv = jnp.concatenate([lanes(lv_re), lanes(lv_im)], axis=-1)
    return m_nat.astype(BF16), b_nat.astype(BF16), c_nat.astype(BF16), a_lv


def _s5_kernel(u0_ref, u1_ref, m_ref, b_ref, c_ref, a_ref, d_ref, gw_ref, gb_ref, o0_ref, o1_ref,
               sre_ref, sim_ref):
    tc = S5_CHUNK
    nc = u0_ref.shape[1] // tc
    hl = S5_WIDTH // S5_HALVES
    sw = sre_ref.shape[1]
    u_step = [[u_ref[0, pl.ds(s, nc, stride=tc), :] for s in range(tc)] for u_ref in (u0_ref, u1_ref)]
    y_half = []
    for h in range(S5_HALVES):
        uh = jnp.concatenate(u_step[h], axis=1).astype(BF16)
        v = jnp.dot(uh, b_ref[h], preferred_element_type=F32)
        zeros = jnp.zeros((nc, sw), F32)
        sre_ref[0:nc, :] = zeros
        sim_ref[0:nc, :] = zeros
        sre_ref[nc:2 * nc, :] = v[:, 0:sw]
        sim_ref[nc:2 * nc, :] = v[:, sw:2 * sw]
        for lv in range(a_ref.shape[1]):
            d = 1 << lv
            ar = a_ref[h, lv:lv + 1, 0:sw]
            ai = a_ref[h, lv:lv + 1, sw:2 * sw]
            pr = sre_ref[nc - d:2 * nc - d, :]
            pi = sim_ref[nc - d:2 * nc - d, :]
            cr = sre_ref[nc:2 * nc, :]
            ci = sim_ref[nc:2 * nc, :]
            sre_ref[nc:2 * nc, :] = cr + ar * pr - ai * pi
            sim_ref[nc:2 * nc, :] = ci + ar * pi + ai * pr
        prev_re = sre_ref[nc - 1:2 * nc - 1, :].astype(BF16)
        prev_im = sim_ref[nc - 1:2 * nc - 1, :].astype(BF16)
        y_half.append(jnp.dot(uh, m_ref[h], preferred_element_type=F32)
                      + jnp.dot(prev_re, c_ref[h, 0:sw, :], preferred_element_type=F32)
                      + jnp.dot(prev_im, c_ref[h, sw:2 * sw, :], preferred_element_type=F32))
    for i in range(tc):
        y = jnp.concatenate([yh[:, i * hl:(i + 1) * hl] for yh in y_half], axis=1)
        z = _gelu_tanh(y + d_ref[...] * jnp.concatenate([u_step[0][i], u_step[1][i]], axis=1))
        gate = jnp.dot(z.astype(BF16), gw_ref[...], preferred_element_type=F32) + gb_ref[...]
        out = z * _sigmoid(gate)
        o0_ref[0, pl.ds(i, nc, stride=tc), :] = out[:, 0:hl]
        o1_ref[0, pl.ds(i, nc, stride=tc), :] = out[:, hl:2 * hl]


def s5_layer(zs, operators, d_skip, glu_w, glu_b):
    m_nat, b_nat, c_nat, a_lv = operators
    bsz, seq_len, _ = zs.shape
    tc = S5_CHUNK
    nc = seq_len // tc
    assert nc & (nc - 1) == 0 and a_lv.shape[1] == int(math.log2(nc))
    hl = S5_WIDTH // S5_HALVES
    sw = a_lv.shape[2] // 2
    full = lambda a: pl.BlockSpec(a.shape, lambda b: (0,) * a.ndim)
    vec = pl.BlockSpec((1, S5_WIDTH), lambda b: (0, 0))
    half = lambda h: pl.BlockSpec((1, seq_len, hl), lambda b: (b, 0, h))
    glu_wb = glu_w.astype(BF16)
    out = pl.pallas_call(
        _s5_kernel,
        out_shape=(jax.ShapeDtypeStruct((bsz, seq_len, hl), F32),) * S5_HALVES,
        grid=(bsz,),
        in_specs=[half(0), half(1), full(m_nat), full(b_nat), full(c_nat), full(a_lv), vec, full(glu_wb), vec],
        out_specs=(half(0),) * S5_HALVES,
        scratch_shapes=[pltpu.VMEM((2 * nc, sw), F32), pltpu.VMEM((2 * nc, sw), F32)],
        compiler_params=_cparams("parallel"),
        name="s5_layer",
    )(zs, zs, m_nat, b_nat, c_nat, a_lv, d_skip.reshape(1, -1), glu_wb, glu_b.reshape(1, -1))
    return tuple(o.reshape(bsz * seq_len, hl) for o in out)


def _route_top2(logits):
    lane = lax.broadcasted_iota(jnp.int32, logits.shape, 1)
    lane_f = lane.astype(F32)
    logits = jnp.where(lane < N_EXPERTS, logits, -jnp.inf)
    v1 = jnp.max(logits, axis=-1, keepdims=True)
    i1 = jnp.min(jnp.where(logits == v1, lane_f, 1e9), axis=-1, keepdims=True)
    rest = jnp.where(lane_f == i1, -jnp.inf, logits)
    v2 = jnp.max(rest, axis=-1, keepdims=True)
    i2 = jnp.min(jnp.where(rest == v2, lane_f, 1e9), axis=-1, keepdims=True)
    e2 = jnp.exp(v2 - v1)
    den = 1.0 + e2
    out = jnp.where(lane == 0, i1, jnp.where(lane == 1, i2, jnp.where(lane == 2, 1.0 / den, e2 / den)))
    return jnp.where(lane < 4, out, 0.0)


def _out_proj_kernel(x_ref, a_ref, c_ref, s0_ref, s1_ref, w_ref, g_ref, b_ref, *rest, alpha):
    o_ref = rest[0] if len(rest) == 1 else rest[1]
    o_s5 = jnp.concatenate([s0_ref[...], s1_ref[...]], axis=1).astype(BF16)
    h = jnp.dot(a_ref[...], w_ref[0:NSA_WIDTH, :], preferred_element_type=F32)
    h = h + jnp.dot(c_ref[...], w_ref[NSA_WIDTH:NSA_WIDTH + CONV_WIDTH, :], preferred_element_type=F32)
    h = h + jnp.dot(o_s5, w_ref[NSA_WIDTH + CONV_WIDTH:, :], preferred_element_type=F32)
    y = _layer_norm(alpha * x_ref[...] + h, g_ref[...], b_ref[...])
    o_ref[...] = y
    if len(rest) > 1:
        router_ref, _, routed_ref, ybf_ref = rest
        ybf_ref[...] = y.astype(BF16)
        y_top = _top_bits(y)
        y_hi = y_top.astype(BF16)
        y_lo = (y - y_top).astype(BF16)
        rows = y.shape[0]
        prod = jnp.dot(jnp.concatenate([y_hi, y_lo], axis=0), router_ref[...],
                       preferred_element_type=F32)
        logits = (prod[0:rows, 0:LANES] + prod[0:rows, LANES:2 * LANES]
                  + prod[rows:2 * rows, 0:LANES] + prod[rows:2 * rows, LANES:2 * LANES])
        routed_ref[...] = _route_top2(logits)


def out_proj_ln(x2, o_nsa, o_conv, o_s5, w_out, g, b, alpha, router=None, tm=512):
    t, d = x2.shape
    row = lambda width: pl.BlockSpec((tm, width), lambda i: (i, 0))
    vec = pl.BlockSpec((1, d), lambda i: (0, 0))
    in_specs = [row(d), row(NSA_WIDTH), row(CONV_WIDTH), row(S5_WIDTH // S5_HALVES), row(S5_WIDTH // S5_HALVES),
                pl.BlockSpec(w_out.shape, lambda i: (0, 0)), vec, vec]
    args = [x2, o_nsa, o_conv, *o_s5, w_out.astype(BF16), g.reshape(1, -1), b.reshape(1, -1)]
    out_shape = jax.ShapeDtypeStruct((t, d), F32)
    out_specs = row(d)
    if router is not None:
        w = jnp.pad(router, ((0, 0), (0, LANES - router.shape[1])))
        w_top = lax.bitcast_convert_type(
            lax.bitcast_convert_type(w, jnp.uint32) & jnp.uint32(0xFFFF0000), F32)
        w_hi = w_top.astype(BF16)
        w_lo = (w - w_top).astype(BF16)
        in_specs.append(pl.BlockSpec((d, 2 * LANES), lambda i: (0, 0)))
        args.append(jnp.concatenate([w_hi, w_lo], axis=1))
        out_shape = (out_shape, jax.ShapeDtypeStruct((t, LANES), F32), jax.ShapeDtypeStruct((t, d), BF16))
        out_specs = (out_specs, row(LANES), row(d))
    return pl.pallas_call(
        functools.partial(_out_proj_kernel, alpha=alpha),
        out_shape=out_shape,
        grid=(t // tm,),
        in_specs=in_specs,
        out_specs=out_specs,
        compiler_params=_cparams("parallel"),
        name="out_proj_ln",
    )(*args)


def _ffn_kernel(te_ref, nu_ref, x_ref, w1_ref, w3_ref, w2_ref, *rest, alpha):
    ln_refs, (o_ref, acc_ref) = rest[:-2], rest[-2:]
    i = pl.program_id(0)
    f = pl.program_id(1)
    last = pl.num_programs(1) - 1
    used = i < nu_ref[0]

    @pl.when(used)
    def _():
        xb = x_ref[...].astype(BF16)
        h1 = jnp.dot(xb, w1_ref[0], preferred_element_type=F32)
        h3 = jnp.dot(xb, w3_ref[0], preferred_element_type=F32)
        h = (h1 * _sigmoid(h1) * h3).astype(BF16)
        part = jnp.dot(h, w2_ref[0], preferred_element_type=F32)

        @pl.when(f == 0)
        def _():
            acc_ref[...] = part

        @pl.when(f > 0)
        def _():
            acc_ref[...] = acc_ref[...] + part

        @pl.when(f == last)
        def _():
            if ln_refs:
                o_ref[...] = _layer_norm(alpha * x_ref[...] + acc_ref[...], ln_refs[0][...], ln_refs[1][...])
            else:
                o_ref[...] = acc_ref[...].astype(o_ref.dtype)

    @pl.when(jnp.logical_not(used) & (f == last))
    def _():
        o_ref[...] = jnp.zeros(o_ref.shape, o_ref.dtype)


def grouped_swiglu(xs, tile_expert, n_used, w1, w3, w2, tm, tf, out_dtype, ln=None, alpha=1.0):
    p, d = xs.shape
    ff = w1.shape[2]
    assert p % tm == 0 and ff % tf == 0
    vec = pl.BlockSpec((1, d), lambda i, f, te, nu: (0, 0))
    ln_args = () if ln is None else (ln[0].reshape(1, d), ln[1].reshape(1, d))
    grid_spec = pltpu.PrefetchScalarGridSpec(
        num_scalar_prefetch=2,
        grid=(p // tm, ff // tf),
        in_specs=[pl.BlockSpec((tm, d), lambda i, f, te, nu: (i, 0)),
                  pl.BlockSpec((1, d, tf), lambda i, f, te, nu: (te[i], 0, f)),
                  pl.BlockSpec((1, d, tf), lambda i, f, te, nu: (te[i], 0, f)),
                  pl.BlockSpec((1, tf, d), lambda i, f, te, nu: (te[i], f, 0))] + [vec] * len(ln_args),
        out_specs=pl.BlockSpec((tm, d), lambda i, f, te, nu: (i, 0)),
        scratch_shapes=[pltpu.VMEM((tm, d), F32)],
    )
    return pl.pallas_call(
        functools.partial(_ffn_kernel, alpha=alpha),
        out_shape=jax.ShapeDtypeStruct((p, d), out_dtype),
        grid_spec=grid_spec,
        compiler_params=_cparams("arbitrary", "arbitrary"),
        name="grouped_swiglu",
    )(tile_expert, n_used, xs, w1, w3, w2, *ln_args)


def _pick_tf(ff, target=1024):
    best = LANES
    for cand in range(LANES, ff + 1, LANES):
        if ff % cand == 0 and cand <= target:
            best = cand
    return best


def _residual_ln_kernel(x_ref, a_ref, b_ref, wa_ref, wb_ref, g_ref, beta_ref, o_ref, *, alpha):
    f = wa_ref[...] * a_ref[...] + wb_ref[...] * b_ref[...]
    o_ref[...] = _layer_norm(alpha * x_ref[...] + f, g_ref[...], beta_ref[...])


def residual_ln(x2, a, b, wa, wb, g, beta, alpha, tm=1024):
    t, d = x2.shape
    row = pl.BlockSpec((tm, d), lambda i: (i, 0))
    col = pl.BlockSpec((tm, 1), lambda i: (i, 0))
    vec = pl.BlockSpec((1, d), lambda i: (0, 0))
    return pl.pallas_call(
        functools.partial(_residual_ln_kernel, alpha=alpha),
        out_shape=jax.ShapeDtypeStruct((t, d), F32),
        grid=(t // tm,),
        in_specs=[row, row, row, col, col, vec, vec],
        out_specs=row,
        compiler_params=_cparams("parallel"),
        name="residual_ln",
    )(x2, a, b, wa, wb, g.reshape(1, -1), beta.reshape(1, -1))


def moe_swiglu_ln(x2, x2_bf16, routed, w1, w3, w2, first_expert, g, beta, alpha, tm):
    t, d = x2.shape
    n_exp = N_EXPERTS
    idx = routed[:, 0:2].astype(jnp.int32)
    wts = routed[:, 2:4]
    e_flat = idx.T.reshape(-1)
    onehot = (e_flat[:, None] == jnp.arange(n_exp, dtype=jnp.int32)[None, :]).astype(jnp.int32)
    running = jnp.cumsum(onehot, axis=0)
    counts = running[-1]
    rank = jnp.sum(onehot * running, axis=1) - 1
    padded = ((counts + tm - 1) // tm) * tm
    pad_end = jnp.cumsum(padded)
    pad_start = pad_end - padded
    start = jnp.cumsum(counts) - counts
    pos = pad_start[e_flat] + rank
    n_rows = 2 * t + n_exp * tm
    n_tiles = n_rows // tm
    tile_start = jnp.arange(n_tiles, dtype=jnp.int32) * tm
    tile_expert = jnp.minimum(jnp.sum(tile_start[:, None] >= pad_end[None, :], axis=1), n_exp - 1).astype(jnp.int32)
    n_used = (pad_end[-1] // tm).astype(jnp.int32).reshape(1)
    order = jnp.argsort(e_flat, stable=True)
    row = jnp.arange(n_rows, dtype=jnp.int32)
    row_expert = jnp.repeat(tile_expert, tm)
    slot = jnp.minimum(start[row_expert] + (row - pad_start[row_expert]), 2 * t - 1)
    src = order[slot] % t
    xs = x2_bf16[src]
    ys = grouped_swiglu(xs, tile_expert + first_expert, n_used, w1, w3, w2, tm, _pick_tf(w1.shape[2]), BF16)
    return residual_ln(x2, ys[pos[:t]], ys[pos[t:]], wts[:, 0:1], wts[:, 1:2], g, beta, alpha)


def dense_swiglu_ln(x2, w1, w3, w2, layer, g, beta, alpha, tm):
    t = x2.shape[0]
    tile_expert = jnp.full((t // tm,), layer, jnp.int32)
    n_used = jnp.full((1,), t // tm, jnp.int32)
    return grouped_swiglu(x2, tile_expert, n_used, w1, w3, w2, tm, _pick_tf(w1.shape[2], 1408),
                          F32, ln=(g, beta), alpha=alpha)


def _ple_kernel(x_ref, p_ref, wg_ref, bg_ref, wp_ref, g_ref, beta_ref, o_ref, *, alpha):
    x = x_ref[...]
    gate = _sigmoid(jnp.dot(x.astype(BF16), wg_ref[...], preferred_element_type=F32) + bg_ref[...])
    e = jnp.dot(p_ref[...].astype(BF16), wp_ref[...], preferred_element_type=F32) * gate
    o_ref[...] = _layer_norm(alpha * x + e, g_ref[...], beta_ref[...])


def ple_ln(x2, p_all, layer, gate_w, gate_b, proj, g, beta, alpha, tm=1024):
    t, d = x2.shape
    pd = p_all.shape[1]
    first = layer * (t // tm)
    vec = pl.BlockSpec((1, d), lambda i: (0, 0))
    return pl.pallas_call(
        functools.partial(_ple_kernel, alpha=alpha),
        out_shape=jax.ShapeDtypeStruct((t, d), F32),
        grid=(t // tm,),
        in_specs=[pl.BlockSpec((tm, d), lambda i: (i, 0)), pl.BlockSpec((tm, pd), lambda i: (first + i, 0)),
                  pl.BlockSpec((d, d), lambda i: (0, 0)), vec,
                  pl.BlockSpec((pd, d), lambda i: (0, 0)), vec, vec],
        out_specs=pl.BlockSpec((tm, d), lambda i: (i, 0)),
        compiler_params=_cparams("parallel"),
        name="ple_ln",
    )(x2, p_all, gate_w.astype(BF16), gate_b.reshape(1, -1), proj.astype(BF16), g.reshape(1, -1), beta.reshape(1, -1))


def hybrid_mixer_ln(x2, bsz, seq_len, tables, w_in, w_out, cmp_weights, conv_params, s5_params,
                    g, beta, alpha, router=None):
    (q, ksel, kwin, vsel, vwin, kv_cmp, gates, u_conv, zs) = in_proj(x2, _widen_w_in(w_in), tables, seq_len)
    b3 = lambda a: a.reshape(bsz, seq_len, a.shape[1])
    kvc = compress(kv_cmp.reshape(2, bsz, seq_len, LANES), _compress_weights(*cmp_weights))
    o_nsa = nsa_attention(b3(q), b3(ksel), b3(vsel), b3(kwin), b3(vwin), kvc, b3(gates))
    o_conv = conformer_conv(b3(u_conv), *conv_params)
    (a_re, a_im, log_dt, b_re, b_im, c_re, c_im, d_skip, glu_w, glu_b) = s5_params
    ops = _s5_operators(a_re, a_im, log_dt, b_re, b_im, c_re, c_im, seq_len // S5_CHUNK)
    o_s5 = s5_layer(b3(zs), ops, d_skip, glu_w, glu_b)
    return out_proj_ln(x2, o_nsa.reshape(-1, NSA_WIDTH), o_conv.reshape(-1, CONV_WIDTH), o_s5, w_out,
                       g, beta, alpha, router)


def kernel(x, p, positions, w_in, w_out, cmp_pe, cmp_w1, cmp_b1, cmp_w2, cmp_b2, conv_w, conv_b, conv_ln_g, conv_ln_b, s5_a_re, s5_a_im, s5_log_dt, s5_b_re, s5_b_im, s5_c_re, s5_c_im, s5_d, s5_glu_w, s5_glu_b, ffn_w1, ffn_w3, ffn_w2, moe_router, moe_w1, moe_w3, moe_w2, ple_gate_w, ple_gate_b, ple_proj, ln_g, ln_b):
    bsz, seq_len, d_model = x.shape
    depth = w_in.shape[0]
    alpha = (2 * depth) ** 0.25
    t = bsz * seq_len
    tables = rope_tables(positions)
    x2 = x.reshape(t, d_model)
    dense_tm = 512
    moe_tm = 512 if t % 1024 else 1024
    stack = lambda w: w.astype(BF16).reshape((-1,) + w.shape[-2:])
    ffn_w = (stack(ffn_w1), stack(ffn_w3), stack(ffn_w2))
    moe_w = (stack(moe_w1), stack(moe_w3), stack(moe_w2))
    for i in range(depth):
        j = i // 2
        routed_layer = i % 2 == 1
        mixed = hybrid_mixer_ln(
            x2, bsz, seq_len, tables, w_in[i], w_out[i],
            (cmp_pe[i], cmp_w1[i], cmp_b1[i], cmp_w2[i], cmp_b2[i]),
            (conv_w[i], conv_b[i], conv_ln_g[i], conv_ln_b[i]),
            (s5_a_re[i], s5_a_im[i], s5_log_dt[i], s5_b_re[i], s5_b_im[i], s5_c_re[i], s5_c_im[i],
             s5_d[i], s5_glu_w[i], s5_glu_b[i]),
            ln_g[i, 0], ln_b[i, 0], alpha, moe_router[j] if routed_layer else None)
        if routed_layer:
            x2, routed, x2_bf16 = mixed
            x2 = moe_swiglu_ln(x2, x2_bf16, routed, *moe_w, j * N_EXPERTS, ln_g[i, 1], ln_b[i, 1], alpha, moe_tm)
        else:
            x2 = dense_swiglu_ln(mixed, *ffn_w, j, ln_g[i, 1], ln_b[i, 1], alpha, dense_tm)
        x2 = ple_ln(x2, p.reshape(depth * t, -1), i, ple_gate_w[i], ple_gate_b[i], ple_proj[i],
                    ln_g[i, 2], ln_b[i, 2], alpha)
    return x2.reshape(bsz, seq_len, d_model)
```

```python
import functools
import math

import numpy as np
import jax
import jax.numpy as jnp
from jax import lax
from jax.experimental import pallas as pl
from jax.experimental.pallas import tpu as pltpu

F32 = jnp.float32
BF16 = jnp.bfloat16
HIGHEST = lax.Precision.HIGHEST

LANES = 128
VMEM_LIMIT = 56 * 1024 * 1024

ROW_TILE = 1024
ROW_TILE_SMALL = 512
CONV_ROWS = 256
CONV_HALO = 32

HEAD_DIM = 64
HALF_DIM = HEAD_DIM // 2
N_Q_HEADS = 8
N_KV_HEADS = 2
GQA = N_Q_HEADS // N_KV_HEADS
CMP_LEN = 32
CMP_STRIDE = 16
CMP_HIDDEN = 256
SEL_LEN = 64
SEL_TOPK = 16
WINDOW = 512
ROPE_THETA = 10000.0
FORCED_SCORE = 1e9
NEG = -1e30
CONV_WIDTH = 256
CONV_LEN = 31
S5_WIDTH = 256
S5_GROUP_CH = 16
S5_GROUPS = 16
S5_STATE = 64
S5_CHUNK = 8
S5_HALVES = 2
N_EXPERTS = 8
LN_EPS = 1e-5
NSA_WIDTH = N_Q_HEADS * HEAD_DIM

C_Q = 0
C_KV = C_Q + NSA_WIDTH
C_GATE = C_KV + 6 * N_KV_HEADS * HEAD_DIM
C_CONVA = C_GATE + N_KV_HEADS * LANES
C_CONVB = C_CONVA + CONV_WIDTH
C_S5 = C_CONVB + CONV_WIDTH
C_TOTAL = C_S5 + S5_WIDTH
Q_SCALE = HEAD_DIM ** -0.5 * math.log2(math.e)


def _cparams(*sem):
    return pltpu.CompilerParams(dimension_semantics=sem, vmem_limit_bytes=VMEM_LIMIT)


def _layer_norm(v, g, b):
    mu = jnp.mean(v, axis=-1, keepdims=True)
    d = v - mu
    var = jnp.mean(d * d, axis=-1, keepdims=True)
    return d * lax.rsqrt(var + LN_EPS) * g + b


def _gelu_tanh(x):
    return 0.5 * x * (1.0 + jnp.tanh(math.sqrt(2.0 / math.pi) * (x + 0.044715 * (x * x * x))))


def _sigmoid(x):
    return 1.0 / (1.0 + jnp.exp(-x))


def _top_bits(x):
    return pltpu.bitcast(pltpu.bitcast(x, jnp.uint32) & jnp.uint32(0xFFFF0000), F32)


def _rope_table_kernel(pos_ref, freq_ref, tab_ref):
    ang = pos_ref[...] * freq_ref[...]
    c = jnp.cos(ang)
    s = jnp.sin(ang)
    lane = lax.broadcasted_iota(jnp.int32, ang.shape, 1)
    first_half = (lane % HEAD_DIM) < HALF_DIM
    tab_ref[:, 0:LANES] = c
    tab_ref[:, LANES:2 * LANES] = jnp.where(first_half, -s, 0.0)
    tab_ref[:, 2 * LANES:3 * LANES] = jnp.where(first_half, 0.0, s)


def rope_tables(positions, tm=ROW_TILE_SMALL):
    t = positions.size
    pos = positions.reshape(t, 1).astype(F32)
    inv_freq = ROPE_THETA ** (-jnp.arange(0, HEAD_DIM, 2, dtype=F32) / HEAD_DIM)
    freq = jnp.tile(inv_freq, LANES // HALF_DIM).reshape(1, LANES)
    return pl.pallas_call(
        _rope_table_kernel,
        out_shape=jax.ShapeDtypeStruct((t, 3 * LANES), F32),
        grid=(t // tm,),
        in_specs=[pl.BlockSpec((tm, 1), lambda i: (i, 0)),
                  pl.BlockSpec((1, LANES), lambda i: (0, 0))],
        out_specs=pl.BlockSpec((tm, 3 * LANES), lambda i: (i, 0)),
        compiler_params=_cparams("parallel"),
        name="rope_tables",
    )(pos, freq)


def _in_proj_kernel(x_ref, w_ref, tab_ref, q_ref, ksel_ref, kwin_ref, vsel_ref, vwin_ref,
                    kvcmp_ref, gate_ref, u_ref, zs_ref, *, seq_len):
    tm = x_ref.shape[0]
    xb = x_ref[...].astype(BF16)
    cos = tab_ref[:, 0:LANES]
    sin_a = tab_ref[:, LANES:2 * LANES]
    sin_b = tab_ref[:, 2 * LANES:3 * LANES]

    def proj(c0):
        z = jnp.dot(xb, w_ref[:, c0:c0 + 2 * LANES], preferred_element_type=F32)
        return z[:, 0:LANES], z[:, LANES:2 * LANES]

    def rope(z):
        return (z * cos + pltpu.roll(z, LANES - HALF_DIM, 1) * sin_a
                + pltpu.roll(z, HALF_DIM, 1) * sin_b)

    lane = lax.broadcasted_iota(jnp.int32, (tm, LANES), 1)
    row = lax.broadcasted_iota(jnp.int32, (tm, LANES), 0)
    low = lane < HEAD_DIM
    t_seq = (pl.program_id(0) * tm) % seq_len + row
    blk_onehot = jnp.where(lane == HEAD_DIM + t_seq // SEL_LEN, 1.0, 0.0)
    ones_lane = jnp.where(lane == HEAD_DIM, 1.0, 0.0)

    def spread(z, extra, out_ref, c0):
        out_ref[:, c0:c0 + LANES] = (jnp.where(low, z, 0.0) + extra).astype(BF16)
        out_ref[:, c0 + LANES:c0 + 2 * LANES] = (jnp.where(low, pltpu.roll(z, HEAD_DIM, 1), 0.0) + extra).astype(BF16)

    for pair in range(N_Q_HEADS // 4):
        for n, z in enumerate(proj(C_Q + pair * 2 * LANES)):
            spread(rope(z) * Q_SCALE, 0.0, q_ref, (2 * pair + n) * 2 * LANES)
    k_cmp, v_cmp = proj(C_KV)
    kvcmp_ref[0] = rope(k_cmp)
    kvcmp_ref[1] = v_cmp
    k_sel, v_sel = proj(C_KV + 2 * LANES)
    spread(rope(k_sel), blk_onehot, ksel_ref, 0)
    spread(v_sel, ones_lane, vsel_ref, 0)
    k_win, v_win = proj(C_KV + 4 * LANES)
    spread(rope(k_win), 0.0, kwin_ref, 0)
    spread(v_win, ones_lane, vwin_ref, 0)
    g0, g1 = proj(C_GATE)
    gate_ref[:, 0:LANES] = _sigmoid(g0)
    gate_ref[:, LANES:2 * LANES] = _sigmoid(g1)
    a0, a1 = proj(C_CONVA)
    b0, b1 = proj(C_CONVB)
    u_ref[:, 0:LANES] = a0 * _sigmoid(b0)
    u_ref[:, LANES:2 * LANES] = a1 * _sigmoid(b1)
    s0, s1 = proj(C_S5)
    zs_ref[:, 0:LANES] = s0
    zs_ref[:, LANES:2 * LANES] = s1


def _widen_w_in(w_in):
    d = w_in.shape[0]
    o_gate = NSA_WIDTH + 6 * N_KV_HEADS * HEAD_DIM
    o_conv = o_gate + 3 * N_Q_HEADS
    per_group = 3 * GQA
    cols = [w_in[:, :o_gate]]
    for h in range(N_KV_HEADS):
        cols += [w_in[:, o_gate + h * per_group:o_gate + (h + 1) * per_group],
                 jnp.zeros((d, LANES - per_group), w_in.dtype)]
    cols += [w_in[:, o_conv:]]
    w = jnp.concatenate(cols, axis=1)
    assert w.shape[1] == C_TOTAL
    return w.astype(BF16)


def in_proj(x2, w_wide, tables, seq_len, tm=ROW_TILE):
    t, d = x2.shape
    row = lambda width: pl.BlockSpec((tm, width), lambda i: (i, 0))
    out_shape = (
        jax.ShapeDtypeStruct((t, N_Q_HEADS * LANES), BF16),
        jax.ShapeDtypeStruct((t, N_KV_HEADS * LANES), BF16),
        jax.ShapeDtypeStruct((t, N_KV_HEADS * LANES), BF16),
        jax.ShapeDtypeStruct((t, N_KV_HEADS * LANES), BF16),
        jax.ShapeDtypeStruct((t, N_KV_HEADS * LANES), BF16),
        jax.ShapeDtypeStruct((2, t, LANES), F32),
        jax.ShapeDtypeStruct((t, 2 * LANES), F32),
        jax.ShapeDtypeStruct((t, CONV_WIDTH), F32),
        jax.ShapeDtypeStruct((t, S5_WIDTH), F32),
    )
    out_spec = lambda s: (row(s.shape[1]) if len(s.shape) == 2
                          else pl.BlockSpec((s.shape[0], tm, s.shape[2]), lambda i: (0, i, 0)))
    return pl.pallas_call(
        functools.partial(_in_proj_kernel, seq_len=seq_len),
        out_shape=out_shape,
        grid=(t // tm,),
        in_specs=[row(d),
                  pl.BlockSpec((d, C_TOTAL), lambda i: (0, 0)),
                  row(3 * LANES)],
        out_specs=tuple(out_spec(s) for s in out_shape),
        compiler_params=_cparams("parallel"),
        name="in_proj",
    )(x2, w_wide, tables)


def _compress_kernel(x_ref, pe_ref, w1_ref, b1_ref, w2_ref, b2_ref, o_ref, shift_ref):
    nb = x_ref.shape[2] // CMP_STRIDE
    x = jnp.concatenate([x_ref[0, 0, pl.ds(s, nb, stride=CMP_STRIDE), :] for s in range(CMP_STRIDE)],
                        axis=1)
    xa = (x + pe_ref[0, 0:1, :]).astype(BF16)
    xb = (x + pe_ref[0, 1:2, :]).astype(BF16)
    ha = jnp.dot(xa, w1_ref[0, 0], preferred_element_type=F32)
    hb = jnp.dot(xb, w1_ref[0, 1], preferred_element_type=F32)
    shift_ref[0:nb, :] = hb
    shift_ref[nb:nb + 8, :] = jnp.zeros((8, hb.shape[1]), F32)
    h = ha + shift_ref[1:nb + 1, :] + b1_ref[0]
    g = _gelu_tanh(h).astype(BF16)
    for hd in range(N_KV_HEADS):
        gh = g[:, hd * CMP_HIDDEN:(hd + 1) * CMP_HIDDEN]
        o_ref[0, 0, hd] = (jnp.dot(gh, w2_ref[0], preferred_element_type=F32) + b2_ref[0]).astype(BF16)


def _compress_weights(cmp_pe, cmp_w1, cmp_b1, cmp_w2, cmp_b2):
    half = CMP_LEN // 2
    pe = cmp_pe.reshape(2, 2, half, 1, HEAD_DIM)
    pe = jnp.broadcast_to(pe, (2, 2, half, N_KV_HEADS, HEAD_DIM)).reshape(2, 2, half * N_KV_HEADS * HEAD_DIM)
    w1 = cmp_w1.reshape(2, 2, half, HEAD_DIM, CMP_HIDDEN)
    z = jnp.zeros_like(w1)
    w_h0 = jnp.stack([w1, z], axis=3)
    w_h1 = jnp.stack([z, w1], axis=3)
    w1w = jnp.concatenate([w_h0, w_h1], axis=-1)
    w1w = w1w.reshape(2, 2, half * N_KV_HEADS * HEAD_DIM, N_KV_HEADS * CMP_HIDDEN).astype(BF16)
    b1 = jnp.tile(cmp_b1, (1, N_KV_HEADS)).reshape(2, 1, N_KV_HEADS * CMP_HIDDEN)
    w2 = jnp.pad(cmp_w2, ((0, 0), (0, 0), (0, LANES - HEAD_DIM))).astype(BF16)
    b2 = jnp.pad(cmp_b2, ((0, 0), (0, LANES - HEAD_DIM))).reshape(2, 1, LANES)
    return pe, w1w, b1, w2, b2


def compress(kv_cmp, weights):
    pe, w1w, b1, w2, b2 = weights
    _, bsz, seq_len, _ = kv_cmp.shape
    nb = seq_len // CMP_STRIDE
    cw = CMP_STRIDE * LANES
    hid = N_KV_HEADS * CMP_HIDDEN
    return pl.pallas_call(
        _compress_kernel,
        out_shape=jax.ShapeDtypeStruct((2, bsz, N_KV_HEADS, nb, LANES), BF16),
        grid=(2, bsz),
        in_specs=[pl.BlockSpec((1, 1, seq_len, LANES), lambda j, b: (j, b, 0, 0)),
                  pl.BlockSpec((1, 2, cw), lambda j, b: (j, 0, 0)),
                  pl.BlockSpec((1, 2, cw, hid), lambda j, b: (j, 0, 0, 0)),
                  pl.BlockSpec((1, 1, hid), lambda j, b: (j, 0, 0)),
                  pl.BlockSpec((1, CMP_HIDDEN, LANES), lambda j, b: (j, 0, 0)),
                  pl.BlockSpec((1, 1, LANES), lambda j, b: (j, 0, 0))],
        out_specs=pl.BlockSpec((1, 1, N_KV_HEADS, nb, LANES), lambda j, b: (j, b, 0, 0, 0)),
        scratch_shapes=[pltpu.VMEM((nb + 8, hid), F32)],
        compiler_params=_cparams("parallel", "parallel"),
        name="compress",
    )(kv_cmp, pe, w1w, b1, w2, b2)


SEL_TK = 512
NSA_TQ = 128
NSA_SUBS = 4
N_SEL_BLOCKS = LANES - HEAD_DIM


def _nsa_kernel(q_ref, ksel_ref, vsel_ref, kwin_ref, vwin_ref, kc_ref, vc_ref, gate_ref, ovt_ref,
                o_ref, m_ref, acc_ref):
    tq = NSA_TQ
    subs = range(q_ref.shape[1] // tq)
    units = [(a, g) for a in subs for g in range(GQA)]
    i = pl.program_id(2)
    q0 = i * q_ref.shape[1]
    qh = {(a, g): q_ref[0, a * tq:(a + 1) * tq, g * LANES:(g + 1) * LANES] for a, g in units}
    t_col = [q0 + a * tq + lax.broadcasted_iota(jnp.int32, (tq, 1), 0) for a in subs]
    t_row = [q0 + a * tq + lax.broadcasted_iota(jnp.int32, (1, tq), 1) for a in subs]
    contract_last = (((1,), (1,)), ((), ()))

    ncb = kc_ref.shape[3]
    kc = kc_ref[0, 0, 0]
    vc = vc_ref[0, 0, 0]
    s_cmp = {u: lax.dot_general(qh[u], kc, contract_last, preferred_element_type=F32) for u in units}
    wn = tq + WINDOW
    ks = [pl.multiple_of(jnp.maximum(q0 + a * tq - WINDOW, 0), tq) for a in subs]
    kw = [kwin_ref[0, pl.ds(ks[a], wn), :] for a in subs]
    vw = [vwin_ref[0, pl.ds(ks[a], wn), :] for a in subs]
    s_win = {(a, g): lax.dot_general(qh[a, g], kw[a], contract_last, preferred_element_type=F32)
             for a, g in units}

    cmp_end = lax.broadcasted_iota(jnp.int32, (1, ncb), 1) * CMP_STRIDE + (CMP_LEN - 1)
    o_cmp = {}
    p_sum = [None for _ in subs]
    for a, g in units:
        valid = cmp_end <= t_col[a]
        s = jnp.where(valid, s_cmp[a, g], NEG)
        e = jnp.where(valid, jnp.exp2(s - jnp.max(s, axis=-1, keepdims=True)), 0.0)
        p = e / jnp.maximum(jnp.sum(e, axis=-1, keepdims=True), 1e-30)
        o_cmp[a, g] = jnp.dot(p.astype(BF16), vc, preferred_element_type=F32)
        p_sum[a] = p if p_sum[a] is None else p_sum[a] + p

    o_win = {}
    inside = []
    for a in subs:
        back = t_col[a] - (ks[a] + lax.broadcasted_iota(jnp.int32, (1, wn), 1))
        inside.append(pltpu.bitcast(back, jnp.uint32) < jnp.uint32(WINDOW))
    for a, g in units:
        sw = jnp.where(inside[a], s_win[a, g], NEG)
        pw = jnp.exp2(sw - jnp.max(sw, axis=-1, keepdims=True))
        accw = jnp.dot(pw.astype(BF16), vw[a], preferred_element_type=F32)
        o_win[a, g] = accw / accw[:, HEAD_DIM:HEAD_DIM + 1]

    ovt = ovt_ref[...]
    nblk = N_SEL_BLOCKS
    q_sel = {}
    for a in subs:
        t1 = _top_bits(p_sum[a])
        r1 = p_sum[a] - t1
        t2 = _top_bits(r1)
        p1, p2, p3 = t1.astype(BF16), t2.astype(BF16), (r1 - t2).astype(BF16)
        imp_t = (lax.dot_general(ovt, p1, contract_last, preferred_element_type=F32)
                 + lax.dot_general(ovt, p2, contract_last, preferred_element_type=F32)
                 + lax.dot_general(ovt, p3, contract_last, preferred_element_type=F32))

        blk = lax.broadcasted_iota(jnp.int32, (nblk, tq), 0)
        qblk = t_row[a] // SEL_LEN
        causal_blk = blk <= qblk
        forced = (blk == 0) | (blk == qblk) | (blk == qblk - 1)
        score_t = jnp.where(forced, FORCED_SCORE, jnp.where(causal_blk, imp_t, NEG))
        groups = [score_t[8 * r:8 * r + 8] for r in range(nblk // 8)]
        sub = lax.broadcasted_iota(jnp.int32, (8, tq), 0)
        later = [jnp.where(sub > s, 1.0, 0.0) for s in range(8)]
        counts = [jnp.zeros((8, tq), F32) for _ in groups]
        for b in range(nblk):
            row = score_t[b:b + 1]
            for r, grp in enumerate(groups):
                if 8 * r > b:
                    counts[r] = counts[r] + jnp.where(row >= grp, 1.0, 0.0)
                elif 8 * r + 7 <= b:
                    counts[r] = counts[r] + jnp.where(row > grp, 1.0, 0.0)
                else:
                    counts[r] = (counts[r] + jnp.where(row > grp, 1.0, 0.0)
                                 + jnp.where(row == grp, later[b - 8 * r], 0.0))
        keep_t = (jnp.concatenate(counts, axis=0) < float(SEL_TOPK)) & causal_blk
        bias_t = jnp.where(keep_t, 0.0, NEG)
        bias = jnp.concatenate([jnp.zeros((LANES - nblk, tq), F32), bias_t], axis=0).T.astype(BF16)
        for g in range(GQA):
            q_sel[a, g] = qh[a, g] + bias

    tk = SEL_TK
    m_ref[...] = jnp.full(m_ref.shape, NEG, F32)
    acc_ref[...] = jnp.zeros(acc_ref.shape, F32)

    def sel_tile(j, carry):
        k0 = pl.multiple_of(j * tk, tk)
        k = ksel_ref[0, pl.ds(k0, tk), :]
        v = vsel_ref[0, pl.ds(k0, tk), :]
        scores = {u: lax.dot_general(q_sel[u], k, contract_last, preferred_element_type=F32)
                  for u in units}
        kpos = k0 + lax.broadcasted_iota(jnp.int32, (1, tk), 1)
        for n, (a, g) in enumerate(units):
            rs = slice(n * tq, (n + 1) * tq)
            sc = jnp.where(kpos <= t_col[a], scores[a, g], NEG)
            m_old = m_ref[rs, :]
            m_new = jnp.maximum(m_old, jnp.max(sc, axis=-1, keepdims=True))
            p = jnp.exp2(sc - jnp.concatenate([m_new] * (tk // LANES), axis=1))
            acc_ref[rs, :] = (jnp.exp2(m_old - m_new) * acc_ref[rs, :]
                              + jnp.dot(p.astype(BF16), v, preferred_element_type=F32))
            m_ref[rs, :] = m_new
        return carry

    lax.fori_loop(0, q0 // tk + 1, sel_tile, 0)

    lane = lax.broadcasted_iota(jnp.int32, (tq, LANES), 1)
    low = lane < HEAD_DIM
    for a in subs:
        gates = gate_ref[0, a * tq:(a + 1) * tq, :]
        heads = []
        for g in range(GQA):
            n = a * GQA + g
            acc = acc_ref[n * tq:(n + 1) * tq, :]
            o_sel = acc / acc[:, HEAD_DIM:HEAD_DIM + 1]
            heads.append(gates[:, 3 * g:3 * g + 1] * o_cmp[a, g] + gates[:, 3 * g + 1:3 * g + 2] * o_sel
                         + gates[:, 3 * g + 2:3 * g + 3] * o_win[a, g])
        pair0 = jnp.where(low, heads[0], pltpu.roll(heads[1], HEAD_DIM, 1))
        pair1 = jnp.where(low, heads[2], pltpu.roll(heads[3], HEAD_DIM, 1))
        o_ref[0, a * tq:(a + 1) * tq, :] = jnp.concatenate([pair0, pair1], axis=1).astype(BF16)


def _overlap_matrix_t(n_cmp):
    n = np.arange(n_cmp)[None, :]
    s = np.arange(N_SEL_BLOCKS)[:, None]
    c_start = n * CMP_STRIDE
    s_start = s * SEL_LEN
    ov = (c_start < s_start + SEL_LEN) & (c_start + CMP_LEN > s_start)
    return jnp.asarray(ov.astype(np.float32), BF16)


def nsa_attention(q, ksel, vsel, kwin, vwin, kvc, gates):
    bsz, seq_len, _ = q.shape
    ncb = kvc.shape[3]
    tq = NSA_SUBS * NSA_TQ
    assert seq_len // SEL_LEN <= N_SEL_BLOCKS and seq_len % SEL_TK == 0 and SEL_TK % tq == 0
    assert seq_len >= NSA_TQ + WINDOW
    kv_spec = pl.BlockSpec((1, seq_len, LANES), lambda b, h, i: (b, 0, h))
    rows = GQA * tq
    return pl.pallas_call(
        _nsa_kernel,
        out_shape=jax.ShapeDtypeStruct((bsz, seq_len, NSA_WIDTH), BF16),
        grid=(bsz, N_KV_HEADS, seq_len // tq),
        in_specs=[pl.BlockSpec((1, tq, GQA * LANES), lambda b, h, i: (b, i, h)),
                  kv_spec, kv_spec, kv_spec, kv_spec,
                  pl.BlockSpec((1, 1, 1, ncb, LANES), lambda b, h, i: (0, b, h, 0, 0)),
                  pl.BlockSpec((1, 1, 1, ncb, LANES), lambda b, h, i: (1, b, h, 0, 0)),
                  pl.BlockSpec((1, tq, LANES), lambda b, h, i: (b, i, h)),
                  pl.BlockSpec((N_SEL_BLOCKS, ncb), lambda b, h, i: (0, 0))],
        out_specs=pl.BlockSpec((1, tq, GQA * HEAD_DIM), lambda b, h, i: (b, i, h)),
        scratch_shapes=[pltpu.VMEM((rows, LANES), F32), pltpu.VMEM((rows, LANES), F32)],
        compiler_params=_cparams("parallel", "parallel", "arbitrary"),
        name="nsa_attention",
    )(q, ksel, vsel, kwin, vwin, kvc, kvc, gates, _overlap_matrix_t(ncb))


def _conv_kernel(u_ref, w_ref, cb_ref, g_ref, b_ref, o_ref, pad_ref, *, rows):
    seq_len = u_ref.shape[1]
    halo = CONV_HALO
    pad_ref[0:halo, :] = jnp.zeros((halo, CONV_WIDTH), F32)
    pad_ref[halo:halo + seq_len, :] = u_ref[0]

    first = halo - CONV_LEN + 1

    def body(c, carry):
        r0 = pl.multiple_of(c * rows, rows)
        win = pad_ref[pl.ds(r0, rows + halo), :]
        acc = jnp.zeros((rows, CONV_WIDTH), F32)
        for sub in range(8):
            shifted = win if sub == 0 else pltpu.roll(win, rows + halo - sub, 0)
            for j in range(CONV_LEN):
                if (first + j) % 8 == sub:
                    a0 = first + j - sub
                    acc = acc + w_ref[j:j + 1, :] * shifted[a0:a0 + rows]
        y = _layer_norm(acc + cb_ref[...], g_ref[...], b_ref[...])
        o_ref[0, pl.ds(r0, rows), :] = (y * _sigmoid(y)).astype(BF16)
        return carry

    lax.fori_loop(0, seq_len // rows, body, 0)


def conformer_conv(u, conv_w, conv_b, ln_g, ln_b, rows=CONV_ROWS):
    bsz, seq_len, _ = u.shape
    vec = pl.BlockSpec((1, CONV_WIDTH), lambda b: (0, 0))
    return pl.pallas_call(
        functools.partial(_conv_kernel, rows=rows),
        out_shape=jax.ShapeDtypeStruct((bsz, seq_len, CONV_WIDTH), BF16),
        grid=(bsz,),
        in_specs=[pl.BlockSpec((1, seq_len, CONV_WIDTH), lambda b: (b, 0, 0)),
                  pl.BlockSpec((CONV_LEN, CONV_WIDTH), lambda b: (0, 0)), vec, vec, vec],
        out_specs=pl.BlockSpec((1, seq_len, CONV_WIDTH), lambda b: (b, 0, 0)),
        scratch_shapes=[pltpu.VMEM((seq_len + CONV_HALO, CONV_WIDTH), F32)],
        compiler_params=_cparams("parallel"),
        name="conformer_conv",
    )(u, conv_w, conv_b.reshape(1, -1), ln_g.reshape(1, -1), ln_b.reshape(1, -1))


def _s5_operators(a_re, a_im, log_dt, b_re, b_im, c_re, c_im, n_chunks):
    tc = S5_CHUNK
    dt = jnp.exp(log_dt.astype(F32))[:, None]
    lr = a_re.astype(F32) * dt
    li = a_im.astype(F32) * dt
    mag = jnp.exp(lr)
    ab_re = mag * jnp.cos(li)
    ab_im = mag * jnp.sin(li)
    den = a_re * a_re + a_im * a_im
    coef_re = ((ab_re - 1.0) * a_re + ab_im * a_im) / den
    coef_im = (ab_im * a_re - (ab_re - 1.0) * a_im) / den
    bb_re = coef_re[..., None] * b_re - coef_im[..., None] * b_im
    bb_im = coef_re[..., None] * b_im + coef_im[..., None] * b_re

    def power(tau):
        tau = jnp.asarray(tau, F32)
        m = jnp.exp(lr[..., None] * tau)
        return m * jnp.cos(li[..., None] * tau), m * jnp.sin(li[..., None] * tau)

    lag_re, lag_im = power(jnp.arange(tc + 1))
    ca_re = c_re[..., None] * lag_re[:, None] - c_im[..., None] * lag_im[:, None]
    ca_im = c_re[..., None] * lag_im[:, None] + c_im[..., None] * lag_re[:, None]
    kern = (jnp.einsum('gcnt,gnd->gtcd', ca_re[..., :tc], bb_re, precision=HIGHEST)
            - jnp.einsum('gcnt,gnd->gtcd', ca_im[..., :tc], bb_im, precision=HIGHEST))
    s_idx = np.arange(tc)[:, None]
    i_idx = np.arange(tc)[None, :]
    lag = np.clip(i_idx - s_idx, 0, tc - 1)
    toep = kern[:, lag]
    toep = jnp.where(jnp.asarray(i_idx >= s_idx)[None, :, :, None, None], toep, 0.0)
    rev_re, rev_im = lag_re[..., tc - 1::-1], lag_im[..., tc - 1::-1]
    bop_re = (rev_re[..., None] * bb_re[:, :, None] - rev_im[..., None] * bb_im[:, :, None])
    bop_im = (rev_re[..., None] * bb_im[:, :, None] + rev_im[..., None] * bb_re[:, :, None])
    bop_re = bop_re.transpose(0, 2, 3, 1)
    bop_im = bop_im.transpose(0, 2, 3, 1)
    cop_re = ca_re[..., 1:].transpose(0, 2, 3, 1)
    cop_im = -ca_im[..., 1:].transpose(0, 2, 3, 1)

    gh = S5_GROUPS // S5_HALVES
    eye = jnp.eye(gh, dtype=F32)
    hw = tc * gh * S5_GROUP_CH
    sw = gh * S5_STATE
    split = lambda x: x.reshape((S5_HALVES, gh) + x.shape[1:])
    m_nat = jnp.einsum('hgsiod,gk->hsgdiko', split(toep), eye).reshape(S5_HALVES, hw, hw)
    b_nat = jnp.concatenate(
        [jnp.einsum('hgsdn,gk->hsgdkn', split(bop), eye).reshape(S5_HALVES, hw, sw) for bop in (bop_re, bop_im)],
        axis=2)
    c_nat = jnp.concatenate(
        [jnp.einsum('hgnio,gk->hgniko', split(cop), eye).reshape(S5_HALVES, sw, hw) for cop in (cop_re, cop_im)],
        axis=1)
    levels = max(1, int(math.log2(n_chunks)))
    lv_re, lv_im = power(tc * (2.0 ** jnp.arange(levels)))
    lanes = lambda x: split(x).transpose(0, 3, 1, 2).reshape(S5_HALVES, levels, sw)
    a_lv = jnp.concatenate([lanes(lv_re), lanes(lv_im)], axis=-1)
    return m_nat.astype(BF16), b_nat.astype(BF16), c_nat.astype(BF16), a_lv


def _s5_kernel(u0_ref, u1_ref, m_ref, b_ref, c_ref, a_ref, d_ref, gw_ref, gb_ref, o0_ref, o1_ref,
               sre_ref, sim_ref):
    tc = S5_CHUNK
    nc = u0_ref.shape[1] // tc
    hl = S5_WIDTH // S5_HALVES
    sw = sre_ref.shape[1]
    u_step = [[u_ref[0, pl.ds(s, nc, stride=tc), :] for s in range(tc)] for u_ref in (u0_ref, u1_ref)]
    y_half = []
    for h in range(S5_HALVES):
        uh = jnp.concatenate(u_step[h], axis=1).astype(BF16)
        v = jnp.dot(uh, b_ref[h], preferred_element_type=F32)
        zeros = jnp.zeros((nc, sw), F32)
        sre_ref[0:nc, :] = zeros
        sim_ref[0:nc, :] = zeros
        sre_ref[nc:2 * nc, :] = v[:, 0:sw]
        sim_ref[nc:2 * nc, :] = v[:, sw:2 * sw]
        for lv in range(a_ref.shape[1]):
            d = 1 << lv
            ar = a_ref[h, lv:lv + 1, 0:sw]
            ai = a_ref[h, lv:lv + 1, sw:2 * sw]
            pr = sre_ref[nc - d:2 * nc - d, :]
            pi = sim_ref[nc - d:2 * nc - d, :]
            cr = sre_ref[nc:2 * nc, :]
            ci = sim_ref[nc:2 * nc, :]
            sre_ref[nc:2 * nc, :] = cr + ar * pr - ai * pi
            sim_ref[nc:2 * nc, :] = ci + ar * pi + ai * pr
        prev_re = sre_ref[nc - 1:2 * nc - 1, :].astype(BF16)
        prev_im = sim_ref[nc - 1:2 * nc - 1, :].astype(BF16)
        y_half.append(jnp.dot(uh, m_ref[h], preferred_element_type=F32)
                      + jnp.dot(prev_re, c_ref[h, 0:sw, :], preferred_element_type=F32)
                      + jnp.dot(prev_im, c_ref[h, sw:2 * sw, :], preferred_element_type=F32))
    for i in range(tc):
        y = jnp.concatenate([yh[:, i * hl:(i + 1) * hl] for yh in y_half], axis=1)
        z = _gelu_tanh(y + d_ref[...] * jnp.concatenate([u_step[0][i], u_step[1][i]], axis=1))
        gate = jnp.dot(z.astype(BF16), gw_ref[...], preferred_element_type=F32) + gb_ref[...]
        out = z * _sigmoid(gate)
        o0_ref[0, pl.ds(i, nc, stride=tc), :] = out[:, 0:hl]
        o1_ref[0, pl.ds(i, nc, stride=tc), :] = out[:, hl:2 * hl]


def s5_layer(zs, operators, d_skip, glu_w, glu_b):
    m_nat, b_nat, c_nat, a_lv = operators
    bsz, seq_len, _ = zs.shape
    tc = S5_CHUNK
    nc = seq_len // tc
    assert nc & (nc - 1) == 0 and a_lv.shape[1] == int(math.log2(nc))
    hl = S5_WIDTH // S5_HALVES
    sw = a_lv.shape[2] // 2
    full = lambda a: pl.BlockSpec(a.shape, lambda b: (0,) * a.ndim)
    vec = pl.BlockSpec((1, S5_WIDTH), lambda b: (0, 0))
    half = lambda h: pl.BlockSpec((1, seq_len, hl), lambda b: (b, 0, h))
    glu_wb = glu_w.astype(BF16)
    out = pl.pallas_call(
        _s5_kernel,
        out_shape=(jax.ShapeDtypeStruct((bsz, seq_len, hl), F32),) * S5_HALVES,
        grid=(bsz,),
        in_specs=[half(0), half(1), full(m_nat), full(b_nat), full(c_nat), full(a_lv), vec, full(glu_wb), vec],
        out_specs=(half(0),) * S5_HALVES,
        scratch_shapes=[pltpu.VMEM((2 * nc, sw), F32), pltpu.VMEM((2 * nc, sw), F32)],
        compiler_params=_cparams("parallel"),
        name="s5_layer",
    )(zs, zs, m_nat, b_nat, c_nat, a_lv, d_skip.reshape(1, -1), glu_wb, glu_b.reshape(1, -1))
    return tuple(o.reshape(bsz * seq_len, hl) for o in out)


def _route_top2(logits):
    lane = lax.broadcasted_iota(jnp.int32, logits.shape, 1)
    lane_f = lane.astype(F32)
    logits = jnp.where(lane < N_EXPERTS, logits, -jnp.inf)
    v1 = jnp.max(logits, axis=-1, keepdims=True)
    i1 = jnp.min(jnp.where(logits == v1, lane_f, 1e9), axis=-1, keepdims=True)
    rest = jnp.where(lane_f == i1, -jnp.inf, logits)
    v2 = jnp.max(rest, axis=-1, keepdims=True)
    i2 = jnp.min(jnp.where(rest == v2, lane_f, 1e9), axis=-1, keepdims=True)
    e2 = jnp.exp(v2 - v1)
    den = 1.0 + e2
    out = jnp.where(lane == 0, i1, jnp.where(lane == 1, i2, jnp.where(lane == 2, 1.0 / den, e2 / den)))
    return jnp.where(lane < 4, out, 0.0)


def _out_proj_kernel(x_ref, a_ref, c_ref, s0_ref, s1_ref, w_ref, g_ref, b_ref, *rest, alpha):
    o_ref = rest[0] if len(rest) == 1 else rest[1]
    o_s5 = jnp.concatenate([s0_ref[...], s1_ref[...]], axis=1).astype(BF16)
    h = jnp.dot(a_ref[...], w_ref[0:NSA_WIDTH, :], preferred_element_type=F32)
    h = h + jnp.dot(c_ref[...], w_ref[NSA_WIDTH:NSA_WIDTH + CONV_WIDTH, :], preferred_element_type=F32)
    h = h + jnp.dot(o_s5, w_ref[NSA_WIDTH + CONV_WIDTH:, :], preferred_element_type=F32)
    y = _layer_norm(alpha * x_ref[...] + h, g_ref[...], b_ref[...])
    o_ref[...] = y
    if len(rest) > 1:
        router_ref, _, routed_ref, ybf_ref = rest
        ybf_ref[...] = y.astype(BF16)
        y_top = _top_bits(y)
        y_hi = y_top.astype(BF16)
        y_lo = (y - y_top).astype(BF16)
        rows = y.shape[0]
        prod = jnp.dot(jnp.concatenate([y_hi, y_lo], axis=0), router_ref[...],
                       preferred_element_type=F32)
        logits = (prod[0:rows, 0:LANES] + prod[0:rows, LANES:2 * LANES]
                  + prod[rows:2 * rows, 0:LANES] + prod[rows:2 * rows, LANES:2 * LANES])
        routed_ref[...] = _route_top2(logits)


def out_proj_ln(x2, o_nsa, o_conv, o_s5, w_out, g, b, alpha, router=None, tm=ROW_TILE_SMALL):
    t, d = x2.shape
    row = lambda width: pl.BlockSpec((tm, width), lambda i: (i, 0))
    vec = pl.BlockSpec((1, d), lambda i: (0, 0))
    in_specs = [row(d), row(NSA_WIDTH), row(CONV_WIDTH), row(S5_WIDTH // S5_HALVES), row(S5_WIDTH // S5_HALVES),
                pl.BlockSpec(w_out.shape, lambda i: (0, 0)), vec, vec]
    args = [x2, o_nsa, o_conv, *o_s5, w_out.astype(BF16), g.reshape(1, -1), b.reshape(1, -1)]
    out_shape = jax.ShapeDtypeStruct((t, d), F32)
    out_specs = row(d)
    if router is not None:
        w = jnp.pad(router, ((0, 0), (0, LANES - router.shape[1])))
        w_top = lax.bitcast_convert_type(
            lax.bitcast_convert_type(w, jnp.uint32) & jnp.uint32(0xFFFF0000), F32)
        w_hi = w_top.astype(BF16)
        w_lo = (w - w_top).astype(BF16)
        in_specs.append(pl.BlockSpec((d, 2 * LANES), lambda i: (0, 0)))
        args.append(jnp.concatenate([w_hi, w_lo], axis=1))
        out_shape = (out_shape, jax.ShapeDtypeStruct((t, LANES), F32), jax.ShapeDtypeStruct((t, d), BF16))
        out_specs = (out_specs, row(LANES), row(d))
    return pl.pallas_call(
        functools.partial(_out_proj_kernel, alpha=alpha),
        out_shape=out_shape,
        grid=(t // tm,),
        in_specs=in_specs,
        out_specs=out_specs,
        compiler_params=_cparams("parallel"),
        name="out_proj_ln",
    )(*args)


def _ffn_kernel(te_ref, nu_ref, x_ref, w1_ref, w3_ref, w2_ref, *rest, alpha):
    ln_refs, (o_ref, acc_ref) = rest[:-2], rest[-2:]
    i = pl.program_id(0)
    f = pl.program_id(1)
    last = pl.num_programs(1) - 1
    used = i < nu_ref[0]

    @pl.when(used)
    def _():
        xb = x_ref[...].astype(BF16)
        h1 = jnp.dot(xb, w1_ref[0], preferred_element_type=F32)
        h3 = jnp.dot(xb, w3_ref[0], preferred_element_type=F32)
        h = (h1 * _sigmoid(h1) * h3).astype(BF16)
        part = jnp.dot(h, w2_ref[0], preferred_element_type=F32)

        @pl.when(f == 0)
        def _():
            acc_ref[...] = part

        @pl.when(f > 0)
        def _():
            acc_ref[...] = acc_ref[...] + part

        @pl.when(f == last)
        def _():
            if ln_refs:
                o_ref[...] = _layer_norm(alpha * x_ref[...] + acc_ref[...], ln_refs[0][...], ln_refs[1][...])
            else:
                o_ref[...] = acc_ref[...].astype(o_ref.dtype)

    @pl.when(jnp.logical_not(used) & (f == last))
    def _():
        o_ref[...] = jnp.zeros(o_ref.shape, o_ref.dtype)


def grouped_swiglu(xs, tile_expert, n_used, w1, w3, w2, tm, tf, out_dtype, ln=None, alpha=1.0):
    p, d = xs.shape
    ff = w1.shape[2]
    assert p % tm == 0 and ff % tf == 0
    vec = pl.BlockSpec((1, d), lambda i, f, te, nu: (0, 0))
    ln_args = () if ln is None else (ln[0].reshape(1, d), ln[1].reshape(1, d))
    grid_spec = pltpu.PrefetchScalarGridSpec(
        num_scalar_prefetch=2,
        grid=(p // tm, ff // tf),
        in_specs=[pl.BlockSpec((tm, d), lambda i, f, te, nu: (i, 0)),
                  pl.BlockSpec((1, d, tf), lambda i, f, te, nu: (te[i], 0, f)),
                  pl.BlockSpec((1, d, tf), lambda i, f, te, nu: (te[i], 0, f)),
                  pl.BlockSpec((1, tf, d), lambda i, f, te, nu: (te[i], f, 0))] + [vec] * len(ln_args),
        out_specs=pl.BlockSpec((tm, d), lambda i, f, te, nu: (i, 0)),
        scratch_shapes=[pltpu.VMEM((tm, d), F32)],
    )
    return pl.pallas_call(
        functools.partial(_ffn_kernel, alpha=alpha),
        out_shape=jax.ShapeDtypeStruct((p, d), out_dtype),
        grid_spec=grid_spec,
        compiler_params=_cparams("arbitrary", "arbitrary"),
        name="grouped_swiglu",
    )(tile_expert, n_used, xs, w1, w3, w2, *ln_args)


def _pick_tf(ff, target=1024):
    best = LANES
    for cand in range(LANES, ff + 1, LANES):
        if ff % cand == 0 and cand <= target:
            best = cand
    return best


def _residual_ln_kernel(x_ref, a_ref, b_ref, wa_ref, wb_ref, g_ref, beta_ref, o_ref, *, alpha):
    f = wa_ref[...] * a_ref[...] + wb_ref[...] * b_ref[...]
    o_ref[...] = _layer_norm(alpha * x_ref[...] + f, g_ref[...], beta_ref[...])


def residual_ln(x2, a, b, wa, wb, g, beta, alpha, tm=ROW_TILE):
    t, d = x2.shape
    row = pl.BlockSpec((tm, d), lambda i: (i, 0))
    col = pl.BlockSpec((tm, 1), lambda i: (i, 0))
    vec = pl.BlockSpec((1, d), lambda i: (0, 0))
    return pl.pallas_call(
        functools.partial(_residual_ln_kernel, alpha=alpha),
        out_shape=jax.ShapeDtypeStruct((t, d), F32),
        grid=(t // tm,),
        in_specs=[row, row, row, col, col, vec, vec],
        out_specs=row,
        compiler_params=_cparams("parallel"),
        name="residual_ln",
    )(x2, a, b, wa, wb, g.reshape(1, -1), beta.reshape(1, -1))


def moe_swiglu_ln(x2, x2_bf16, routed, w1, w3, w2, first_expert, g, beta, alpha, tm):
    t, d = x2.shape
    n_exp = N_EXPERTS
    idx = routed[:, 0:2].astype(jnp.int32)
    wts = routed[:, 2:4]
    e_flat = idx.T.reshape(-1)
    onehot = (e_flat[:, None] == jnp.arange(n_exp, dtype=jnp.int32)[None, :]).astype(jnp.int32)
    running = jnp.cumsum(onehot, axis=0)
    counts = running[-1]
    rank = jnp.sum(onehot * running, axis=1) - 1
    padded = ((counts + tm - 1) // tm) * tm
    pad_end = jnp.cumsum(padded)
    pad_start = pad_end - padded
    start = jnp.cumsum(counts) - counts
    pos = pad_start[e_flat] + rank
    n_rows = 2 * t + n_exp * tm
    n_tiles = n_rows // tm
    tile_start = jnp.arange(n_tiles, dtype=jnp.int32) * tm
    tile_expert = jnp.minimum(jnp.sum(tile_start[:, None] >= pad_end[None, :], axis=1), n_exp - 1).astype(jnp.int32)
    n_used = (pad_end[-1] // tm).astype(jnp.int32).reshape(1)
    order = jnp.argsort(e_flat, stable=True)
    row = jnp.arange(n_rows, dtype=jnp.int32)
    row_expert = jnp.repeat(tile_expert, tm)
    slot = jnp.minimum(start[row_expert] + (row - pad_start[row_expert]), 2 * t - 1)
    src = order[slot] % t
    xs = x2_bf16[src]
    ys = grouped_swiglu(xs, tile_expert + first_expert, n_used, w1, w3, w2, tm, _pick_tf(w1.shape[2]), BF16)
    return residual_ln(x2, ys[pos[:t]], ys[pos[t:]], wts[:, 0:1], wts[:, 1:2], g, beta, alpha)


def dense_swiglu_ln(x2, w1, w3, w2, layer, g, beta, alpha, tm):
    t = x2.shape[0]
    tile_expert = jnp.full((t // tm,), layer, jnp.int32)
    n_used = jnp.full((1,), t // tm, jnp.int32)
    return grouped_swiglu(x2, tile_expert, n_used, w1, w3, w2, tm, _pick_tf(w1.shape[2], 1408),
                          F32, ln=(g, beta), alpha=alpha)


def _ple_kernel(x_ref, p_ref, wg_ref, bg_ref, wp_ref, g_ref, beta_ref, o_ref, *, alpha):
    x = x_ref[...]
    gate = _sigmoid(jnp.dot(x.astype(BF16), wg_ref[...], preferred_element_type=F32) + bg_ref[...])
    e = jnp.dot(p_ref[...].astype(BF16), wp_ref[...], preferred_element_type=F32) * gate
    o_ref[...] = _layer_norm(alpha * x + e, g_ref[...], beta_ref[...])


def ple_ln(x2, p_all, layer, gate_w, gate_b, proj, g, beta, alpha, tm=ROW_TILE):
    t, d = x2.shape
    pd = p_all.shape[1]
    first = layer * (t // tm)
    vec = pl.BlockSpec((1, d), lambda i: (0, 0))
    return pl.pallas_call(
        functools.partial(_ple_kernel, alpha=alpha),
        out_shape=jax.ShapeDtypeStruct((t, d), F32),
        grid=(t // tm,),
        in_specs=[pl.BlockSpec((tm, d), lambda i: (i, 0)), pl.BlockSpec((tm, pd), lambda i: (first + i, 0)),
                  pl.BlockSpec((d, d), lambda i: (0, 0)), vec,
                  pl.BlockSpec((pd, d), lambda i: (0, 0)), vec, vec],
        out_specs=pl.BlockSpec((tm, d), lambda i: (i, 0)),
        compiler_params=_cparams("parallel"),
        name="ple_ln",
    )(x2, p_all, gate_w.astype(BF16), gate_b.reshape(1, -1), proj.astype(BF16), g.reshape(1, -1), beta.reshape(1, -1))


def hybrid_mixer_ln(x2, bsz, seq_len, tables, w_in, w_out, cmp_weights, conv_params, s5_params,
                    g, beta, alpha, router=None):
    (q, ksel, kwin, vsel, vwin, kv_cmp, gates, u_conv, zs) = in_proj(x2, _widen_w_in(w_in), tables, seq_len)
    b3 = lambda a: a.reshape(bsz, seq_len, a.shape[1])
    kvc = compress(kv_cmp.reshape(2, bsz, seq_len, LANES), _compress_weights(*cmp_weights))
    o_nsa = nsa_attention(b3(q), b3(ksel), b3(vsel), b3(kwin), b3(vwin), kvc, b3(gates))
    o_conv = conformer_conv(b3(u_conv), *conv_params)
    (a_re, a_im, log_dt, b_re, b_im, c_re, c_im, d_skip, glu_w, glu_b) = s5_params
    ops = _s5_operators(a_re, a_im, log_dt, b_re, b_im, c_re, c_im, seq_len // S5_CHUNK)
    o_s5 = s5_layer(b3(zs), ops, d_skip, glu_w, glu_b)
    return out_proj_ln(x2, o_nsa.reshape(-1, NSA_WIDTH), o_conv.reshape(-1, CONV_WIDTH), o_s5, w_out,
                       g, beta, alpha, router)


def kernel(x, p, positions, w_in, w_out, cmp_pe, cmp_w1, cmp_b1, cmp_w2, cmp_b2, conv_w, conv_b, conv_ln_g, conv_ln_b, s5_a_re, s5_a_im, s5_log_dt, s5_b_re, s5_b_im, s5_c_re, s5_c_im, s5_d, s5_glu_w, s5_glu_b, ffn_w1, ffn_w3, ffn_w2, moe_router, moe_w1, moe_w3, moe_w2, ple_gate_w, ple_gate_b, ple_proj, ln_g, ln_b):
    bsz, seq_len, d_model = x.shape
    depth = w_in.shape[0]
    alpha = (2 * depth) ** 0.25
    t = bsz * seq_len
    tables = rope_tables(positions)
    x2 = x.reshape(t, d_model)
    dense_tm = ROW_TILE_SMALL
    moe_tm = ROW_TILE if t % ROW_TILE == 0 else ROW_TILE_SMALL
    stack = lambda w: w.astype(BF16).reshape((-1,) + w.shape[-2:])
    ffn_w = (stack(ffn_w1), stack(ffn_w3), stack(ffn_w2))
    moe_w = (stack(moe_w1), stack(moe_w3), stack(moe_w2))
    for i in range(depth):
        j = i // 2
        routed_layer = i % 2 == 1
        mixed = hybrid_mixer_ln(
            x2, bsz, seq_len, tables, w_in[i], w_out[i],
            (cmp_pe[i], cmp_w1[i], cmp_b1[i], cmp_w2[i], cmp_b2[i]),
            (conv_w[i], conv_b[i], conv_ln_g[i], conv_ln_b[i]),
            (s5_a_re[i], s5_a_im[i], s5_log_dt[i], s5_b_re[i], s5_b_im[i], s5_c_re[i], s5_c_im[i],
             s5_d[i], s5_glu_w[i], s5_glu_b[i]),
            ln_g[i, 0], ln_b[i, 0], alpha, moe_router[j] if routed_layer else None)
        if routed_layer:
            x2, routed, x2_bf16 = mixed
            x2 = moe_swiglu_ln(x2, x2_bf16, routed, *moe_w, j * N_EXPERTS, ln_g[i, 1], ln_b[i, 1], alpha, moe_tm)
        else:
            x2 = dense_swiglu_ln(mixed, *ffn_w, j, ln_g[i, 1], ln_b[i, 1], alpha, dense_tm)
        x2 = ple_ln(x2, p.reshape(depth * t, -1), i, ple_gate_w[i], ple_gate_b[i], ple_proj[i],
                    ln_g[i, 2], ln_b[i, 2], alpha)
    return x2.reshape(bsz, seq_len, d_model)
```

```python
import functools
import math

import numpy as np
import jax
import jax.numpy as jnp
from jax import lax
from jax.experimental import pallas as pl
from jax.experimental.pallas import tpu as pltpu

F32 = jnp.float32
BF16 = jnp.bfloat16
HIGHEST = lax.Precision.HIGHEST

LANES = 128
VMEM_LIMIT = 56 * 1024 * 1024

ROW_TILE = 1024
ROW_TILE_SMALL = 512
CONV_ROWS = 256
CONV_HALO = 32

HEAD_DIM = 64
HALF_DIM = HEAD_DIM // 2
N_Q_HEADS = 8
N_KV_HEADS = 2
GQA = N_Q_HEADS // N_KV_HEADS
CMP_LEN = 32
CMP_STRIDE = 16
CMP_HIDDEN = 256
SEL_LEN = 64
SEL_TOPK = 16
WINDOW = 512
ROPE_THETA = 10000.0
FORCED_SCORE = 1e9
NEG = -1e30
CONV_WIDTH = 256
CONV_LEN = 31
S5_WIDTH = 256
S5_GROUP_CH = 16
S5_GROUPS = 16
S5_STATE = 64
S5_CHUNK = 8
S5_HALVES = 2
N_EXPERTS = 8
LN_EPS = 1e-5
NSA_WIDTH = N_Q_HEADS * HEAD_DIM

C_Q = 0
C_KV = C_Q + NSA_WIDTH
C_GATE = C_KV + 6 * N_KV_HEADS * HEAD_DIM
C_CONVA = C_GATE + N_KV_HEADS * LANES
C_CONVB = C_CONVA + CONV_WIDTH
C_S5 = C_CONVB + CONV_WIDTH
C_TOTAL = C_S5 + S5_WIDTH
Q_SCALE = HEAD_DIM ** -0.5 * math.log2(math.e)


def _cparams(*sem):
    return pltpu.CompilerParams(dimension_semantics=sem, vmem_limit_bytes=VMEM_LIMIT)


def _layer_norm(v, g, b):
    mu = jnp.mean(v, axis=-1, keepdims=True)
    d = v - mu
    var = jnp.mean(d * d, axis=-1, keepdims=True)
    return d * lax.rsqrt(var + LN_EPS) * g + b


def _gelu_tanh(x):
    return 0.5 * x * (1.0 + jnp.tanh(math.sqrt(2.0 / math.pi) * (x + 0.044715 * (x * x * x))))


def _sigmoid(x):
    return 1.0 / (1.0 + jnp.exp(-x))


def _top_bits(x):
    return pltpu.bitcast(pltpu.bitcast(x, jnp.uint32) & jnp.uint32(0xFFFF0000), F32)


def _rope_table_kernel(pos_ref, freq_ref, tab_ref):
    ang = pos_ref[...] * freq_ref[...]
    c = jnp.cos(ang)
    s = jnp.sin(ang)
    lane = lax.broadcasted_iota(jnp.int32, ang.shape, 1)
    first_half = (lane % HEAD_DIM) < HALF_DIM
    tab_ref[:, 0:LANES] = c
    tab_ref[:, LANES:2 * LANES] = jnp.where(first_half, -s, 0.0)
    tab_ref[:, 2 * LANES:3 * LANES] = jnp.where(first_half, 0.0, s)


def rope_tables(positions, tm=ROW_TILE_SMALL):
    t = positions.size
    pos = positions.reshape(t, 1).astype(F32)
    inv_freq = ROPE_THETA ** (-jnp.arange(0, HEAD_DIM, 2, dtype=F32) / HEAD_DIM)
    freq = jnp.tile(inv_freq, LANES // HALF_DIM).reshape(1, LANES)
    return pl.pallas_call(
        _rope_table_kernel,
        out_shape=jax.ShapeDtypeStruct((t, 3 * LANES), F32),
        grid=(t // tm,),
        in_specs=[pl.BlockSpec((tm, 1), lambda i: (i, 0)),
                  pl.BlockSpec((1, LANES), lambda i: (0, 0))],
        out_specs=pl.BlockSpec((tm, 3 * LANES), lambda i: (i, 0)),
        compiler_params=_cparams("parallel"),
        name="rope_tables",
    )(pos, freq)


def _in_proj_kernel(x_ref, w_ref, tab_ref, q_ref, ksel_ref, kwin_ref, vsel_ref, vwin_ref,
                    kvcmp_ref, gate_ref, u_ref, zs_ref, *, seq_len):
    tm = x_ref.shape[0]
    xb = x_ref[...].astype(BF16)
    cos = tab_ref[:, 0:LANES]
    sin_a = tab_ref[:, LANES:2 * LANES]
    sin_b = tab_ref[:, 2 * LANES:3 * LANES]

    def proj(c0):
        z = jnp.dot(xb, w_ref[:, c0:c0 + 2 * LANES], preferred_element_type=F32)
        return z[:, 0:LANES], z[:, LANES:2 * LANES]

    def rope(z):
        return (z * cos + pltpu.roll(z, LANES - HALF_DIM, 1) * sin_a
                + pltpu.roll(z, HALF_DIM, 1) * sin_b)

    lane = lax.broadcasted_iota(jnp.int32, (tm, LANES), 1)
    row = lax.broadcasted_iota(jnp.int32, (tm, LANES), 0)
    low = lane < HEAD_DIM
    t_seq = (pl.program_id(0) * tm) % seq_len + row
    blk_onehot = jnp.where(lane == HEAD_DIM + t_seq // SEL_LEN, 1.0, 0.0)
    ones_lane = jnp.where(lane == HEAD_DIM, 1.0, 0.0)

    def spread(z, extra, out_ref, c0):
        out_ref[:, c0:c0 + LANES] = (jnp.where(low, z, 0.0) + extra).astype(BF16)
        out_ref[:, c0 + LANES:c0 + 2 * LANES] = (jnp.where(low, pltpu.roll(z, HEAD_DIM, 1), 0.0) + extra).astype(BF16)

    for pair in range(N_Q_HEADS // 4):
        for n, z in enumerate(proj(C_Q + pair * 2 * LANES)):
            spread(rope(z) * Q_SCALE, 0.0, q_ref, (2 * pair + n) * 2 * LANES)
    k_cmp, v_cmp = proj(C_KV)
    kvcmp_ref[0] = rope(k_cmp)
    kvcmp_ref[1] = v_cmp
    k_sel, v_sel = proj(C_KV + 2 * LANES)
    spread(rope(k_sel), blk_onehot, ksel_ref, 0)
    spread(v_sel, ones_lane, vsel_ref, 0)
    k_win, v_win = proj(C_KV + 4 * LANES)
    spread(rope(k_win), 0.0, kwin_ref, 0)
    spread(v_win, ones_lane, vwin_ref, 0)
    g0, g1 = proj(C_GATE)
    gate_ref[:, 0:LANES] = _sigmoid(g0)
    gate_ref[:, LANES:2 * LANES] = _sigmoid(g1)
    a0, a1 = proj(C_CONVA)
    b0, b1 = proj(C_CONVB)
    u_ref[:, 0:LANES] = a0 * _sigmoid(b0)
    u_ref[:, LANES:2 * LANES] = a1 * _sigmoid(b1)
    s0, s1 = proj(C_S5)
    zs_ref[:, 0:LANES] = s0
    zs_ref[:, LANES:2 * LANES] = s1


def _widen_w_in(w_in):
    d = w_in.shape[0]
    o_gate = NSA_WIDTH + 6 * N_KV_HEADS * HEAD_DIM
    o_conv = o_gate + 3 * N_Q_HEADS
    per_group = 3 * GQA
    cols = [w_in[:, :o_gate]]
    for h in range(N_KV_HEADS):
        cols += [w_in[:, o_gate + h * per_group:o_gate + (h + 1) * per_group],
                 jnp.zeros((d, LANES - per_group), w_in.dtype)]
    cols += [w_in[:, o_conv:]]
    w = jnp.concatenate(cols, axis=1)
    assert w.shape[1] == C_TOTAL
    return w.astype(BF16)


def in_proj(x2, w_wide, tables, seq_len, tm=ROW_TILE):
    t, d = x2.shape
    row = lambda width: pl.BlockSpec((tm, width), lambda i: (i, 0))
    out_shape = (
        jax.ShapeDtypeStruct((t, N_Q_HEADS * LANES), BF16),
        jax.ShapeDtypeStruct((t, N_KV_HEADS * LANES), BF16),
        jax.ShapeDtypeStruct((t, N_KV_HEADS * LANES), BF16),
        jax.ShapeDtypeStruct((t, N_KV_HEADS * LANES), BF16),
        jax.ShapeDtypeStruct((t, N_KV_HEADS * LANES), BF16),
        jax.ShapeDtypeStruct((2, t, LANES), F32),
        jax.ShapeDtypeStruct((t, 2 * LANES), F32),
        jax.ShapeDtypeStruct((t, CONV_WIDTH), F32),
        jax.ShapeDtypeStruct((t, S5_WIDTH), F32),
    )
    out_spec = lambda s: (row(s.shape[1]) if len(s.shape) == 2
                          else pl.BlockSpec((s.shape[0], tm, s.shape[2]), lambda i: (0, i, 0)))
    return pl.pallas_call(
        functools.partial(_in_proj_kernel, seq_len=seq_len),
        out_shape=out_shape,
        grid=(t // tm,),
        in_specs=[row(d),
                  pl.BlockSpec((d, C_TOTAL), lambda i: (0, 0)),
                  row(3 * LANES)],
        out_specs=tuple(out_spec(s) for s in out_shape),
        compiler_params=_cparams("parallel"),
        name="in_proj",
    )(x2, w_wide, tables)


def _compress_kernel(x_ref, pe_ref, w1_ref, b1_ref, w2_ref, b2_ref, o_ref, shift_ref):
    nb = x_ref.shape[2] // CMP_STRIDE
    x = jnp.concatenate([x_ref[0, 0, pl.ds(s, nb, stride=CMP_STRIDE), :] for s in range(CMP_STRIDE)],
                        axis=1)
    xa = (x + pe_ref[0, 0:1, :]).astype(BF16)
    xb = (x + pe_ref[0, 1:2, :]).astype(BF16)
    ha = jnp.dot(xa, w1_ref[0, 0], preferred_element_type=F32)
    hb = jnp.dot(xb, w1_ref[0, 1], preferred_element_type=F32)
    shift_ref[0:nb, :] = hb
    shift_ref[nb:nb + 8, :] = jnp.zeros((8, hb.shape[1]), F32)
    h = ha + shift_ref[1:nb + 1, :] + b1_ref[0]
    g = _gelu_tanh(h).astype(BF16)
    for hd in range(N_KV_HEADS):
        gh = g[:, hd * CMP_HIDDEN:(hd + 1) * CMP_HIDDEN]
        o_ref[0, 0, hd] = (jnp.dot(gh, w2_ref[0], preferred_element_type=F32) + b2_ref[0]).astype(BF16)


def _compress_weights(cmp_pe, cmp_w1, cmp_b1, cmp_w2, cmp_b2):
    half = CMP_LEN // 2
    pe = cmp_pe.reshape(2, 2, half, 1, HEAD_DIM)
    pe = jnp.broadcast_to(pe, (2, 2, half, N_KV_HEADS, HEAD_DIM)).reshape(2, 2, half * N_KV_HEADS * HEAD_DIM)
    w1 = cmp_w1.reshape(2, 2, half, HEAD_DIM, CMP_HIDDEN)
    z = jnp.zeros_like(w1)
    w_h0 = jnp.stack([w1, z], axis=3)
    w_h1 = jnp.stack([z, w1], axis=3)
    w1w = jnp.concatenate([w_h0, w_h1], axis=-1)
    w1w = w1w.reshape(2, 2, half * N_KV_HEADS * HEAD_DIM, N_KV_HEADS * CMP_HIDDEN).astype(BF16)
    b1 = jnp.tile(cmp_b1, (1, N_KV_HEADS)).reshape(2, 1, N_KV_HEADS * CMP_HIDDEN)
    w2 = jnp.pad(cmp_w2, ((0, 0), (0, 0), (0, LANES - HEAD_DIM))).astype(BF16)
    b2 = jnp.pad(cmp_b2, ((0, 0), (0, LANES - HEAD_DIM))).reshape(2, 1, LANES)
    return pe, w1w, b1, w2, b2


def compress(kv_cmp, weights):
    pe, w1w, b1, w2, b2 = weights
    _, bsz, seq_len, _ = kv_cmp.shape
    nb = seq_len // CMP_STRIDE
    cw = CMP_STRIDE * LANES
    hid = N_KV_HEADS * CMP_HIDDEN
    return pl.pallas_call(
        _compress_kernel,
        out_shape=jax.ShapeDtypeStruct((2, bsz, N_KV_HEADS, nb, LANES), BF16),
        grid=(2, bsz),
        in_specs=[pl.BlockSpec((1, 1, seq_len, LANES), lambda j, b: (j, b, 0, 0)),
                  pl.BlockSpec((1, 2, cw), lambda j, b: (j, 0, 0)),
                  pl.BlockSpec((1, 2, cw, hid), lambda j, b: (j, 0, 0, 0)),
                  pl.BlockSpec((1, 1, hid), lambda j, b: (j, 0, 0)),
                  pl.BlockSpec((1, CMP_HIDDEN, LANES), lambda j, b: (j, 0, 0)),
                  pl.BlockSpec((1, 1, LANES), lambda j, b: (j, 0, 0))],
        out_specs=pl.BlockSpec((1, 1, N_KV_HEADS, nb, LANES), lambda j, b: (j, b, 0, 0, 0)),
        scratch_shapes=[pltpu.VMEM((nb + 8, hid), F32)],
        compiler_params=_cparams("parallel", "parallel"),
        name="compress",
    )(kv_cmp, pe, w1w, b1, w2, b2)


SEL_TK = 512
NSA_TQ = 128
NSA_SUBS = 4
N_SEL_BLOCKS = LANES - HEAD_DIM


def _nsa_kernel(q_ref, ksel_ref, vsel_ref, kwin_ref, vwin_ref, kc_ref, vc_ref, gate_ref, ovt_ref,
                o_ref, m_ref, acc_ref):
    tq = NSA_TQ
    subs = range(q_ref.shape[1] // tq)
    units = [(a, g) for a in subs for g in range(GQA)]
    i = pl.program_id(2)
    q0 = i * q_ref.shape[1]
    qh = {(a, g): q_ref[0, a * tq:(a + 1) * tq, g * LANES:(g + 1) * LANES] for a, g in units}
    t_col = [q0 + a * tq + lax.broadcasted_iota(jnp.int32, (tq, 1), 0) for a in subs]
    t_row = [q0 + a * tq + lax.broadcasted_iota(jnp.int32, (1, tq), 1) for a in subs]
    contract_last = (((1,), (1,)), ((), ()))

    ncb = kc_ref.shape[3]
    kc = kc_ref[0, 0, 0]
    vc = vc_ref[0, 0, 0]
    s_cmp = {u: lax.dot_general(qh[u], kc, contract_last, preferred_element_type=F32) for u in units}
    wn = tq + WINDOW
    ks = [pl.multiple_of(jnp.maximum(q0 + a * tq - WINDOW, 0), tq) for a in subs]
    kw = [kwin_ref[0, pl.ds(ks[a], wn), :] for a in subs]
    vw = [vwin_ref[0, pl.ds(ks[a], wn), :] for a in subs]
    s_win = {(a, g): lax.dot_general(qh[a, g], kw[a], contract_last, preferred_element_type=F32)
             for a, g in units}

    cmp_end = lax.broadcasted_iota(jnp.int32, (1, ncb), 1) * CMP_STRIDE + (CMP_LEN - 1)
    o_cmp = {}
    p_sum = [None for _ in subs]
    for a, g in units:
        valid = cmp_end <= t_col[a]
        s = jnp.where(valid, s_cmp[a, g], NEG)
        e = jnp.where(valid, jnp.exp2(s - jnp.max(s, axis=-1, keepdims=True)), 0.0)
        p = e / jnp.maximum(jnp.sum(e, axis=-1, keepdims=True), 1e-30)
        o_cmp[a, g] = jnp.dot(p.astype(BF16), vc, preferred_element_type=F32)
        p_sum[a] = p if p_sum[a] is None else p_sum[a] + p

    o_win = {}
    inside = []
    for a in subs:
        back = t_col[a] - (ks[a] + lax.broadcasted_iota(jnp.int32, (1, wn), 1))
        inside.append(pltpu.bitcast(back, jnp.uint32) < jnp.uint32(WINDOW))
    for a, g in units:
        sw = jnp.where(inside[a], s_win[a, g], NEG)
        pw = jnp.exp2(sw - jnp.max(sw, axis=-1, keepdims=True))
        accw = jnp.dot(pw.astype(BF16), vw[a], preferred_element_type=F32)
        o_win[a, g] = accw / accw[:, HEAD_DIM:HEAD_DIM + 1]

    ovt = ovt_ref[...]
    nblk = N_SEL_BLOCKS
    q_sel = {}
    for a in subs:
        t1 = _top_bits(p_sum[a])
        r1 = p_sum[a] - t1
        t2 = _top_bits(r1)
        p1, p2, p3 = t1.astype(BF16), t2.astype(BF16), (r1 - t2).astype(BF16)
        imp_t = (lax.dot_general(ovt, p1, contract_last, preferred_element_type=F32)
                 + lax.dot_general(ovt, p2, contract_last, preferred_element_type=F32)
                 + lax.dot_general(ovt, p3, contract_last, preferred_element_type=F32))

        blk = lax.broadcasted_iota(jnp.int32, (nblk, tq), 0)
        qblk = t_row[a] // SEL_LEN
        causal_blk = blk <= qblk
        forced = (blk == 0) | (blk == qblk) | (blk == qblk - 1)
        score_t = jnp.where(forced, FORCED_SCORE, jnp.where(causal_blk, imp_t, NEG))
        groups = [score_t[8 * r:8 * r + 8] for r in range(nblk // 8)]
        sub = lax.broadcasted_iota(jnp.int32, (8, tq), 0)
        later = [jnp.where(sub > s, 1.0, 0.0) for s in range(8)]
        counts = [jnp.zeros((8, tq), F32) for _ in groups]
        for b in range(nblk):
            row = score_t[b:b + 1]
            for r, grp in enumerate(groups):
                if 8 * r > b:
                    counts[r] = counts[r] + jnp.where(row >= grp, 1.0, 0.0)
                elif 8 * r + 7 <= b:
                    counts[r] = counts[r] + jnp.where(row > grp, 1.0, 0.0)
                else:
                    counts[r] = (counts[r] + jnp.where(row > grp, 1.0, 0.0)
                                 + jnp.where(row == grp, later[b - 8 * r], 0.0))
        keep_t = (jnp.concatenate(counts, axis=0) < float(SEL_TOPK)) & causal_blk
        bias_t = jnp.where(keep_t, 0.0, NEG)
        bias = jnp.concatenate([jnp.zeros((LANES - nblk, tq), F32), bias_t], axis=0).T.astype(BF16)
        for g in range(GQA):
            q_sel[a, g] = qh[a, g] + bias

    tk = SEL_TK
    m_ref[...] = jnp.full(m_ref.shape, NEG, F32)
    acc_ref[...] = jnp.zeros(acc_ref.shape, F32)

    def sel_tile(j, carry):
        k0 = pl.multiple_of(j * tk, tk)
        k = ksel_ref[0, pl.ds(k0, tk), :]
        v = vsel_ref[0, pl.ds(k0, tk), :]
        scores = {u: lax.dot_general(q_sel[u], k, contract_last, preferred_element_type=F32)
                  for u in units}
        kpos = k0 + lax.broadcasted_iota(jnp.int32, (1, tk), 1)
        for n, (a, g) in enumerate(units):
            rs = slice(n * tq, (n + 1) * tq)
            sc = jnp.where(kpos <= t_col[a], scores[a, g], NEG)
            m_old = m_ref[rs, :]
            m_new = jnp.maximum(m_old, jnp.max(sc, axis=-1, keepdims=True))
            p = jnp.exp2(sc - jnp.concatenate([m_new] * (tk // LANES), axis=1))
            acc_ref[rs, :] = (jnp.exp2(m_old - m_new) * acc_ref[rs, :]
                              + jnp.dot(p.astype(BF16), v, preferred_element_type=F32))
            m_ref[rs, :] = m_new
        return carry

    lax.fori_loop(0, q0 // tk + 1, sel_tile, 0)

    lane = lax.broadcasted_iota(jnp.int32, (tq, LANES), 1)
    low = lane < HEAD_DIM
    for a in subs:
        gates = gate_ref[0, a * tq:(a + 1) * tq, :]
        heads = []
        for g in range(GQA):
            n = a * GQA + g
            acc = acc_ref[n * tq:(n + 1) * tq, :]
            o_sel = acc / acc[:, HEAD_DIM:HEAD_DIM + 1]
            heads.append(gates[:, 3 * g:3 * g + 1] * o_cmp[a, g] + gates[:, 3 * g + 1:3 * g + 2] * o_sel
                         + gates[:, 3 * g + 2:3 * g + 3] * o_win[a, g])
        pair0 = jnp.where(low, heads[0], pltpu.roll(heads[1], HEAD_DIM, 1))
        pair1 = jnp.where(low, heads[2], pltpu.roll(heads[3], HEAD_DIM, 1))
        o_ref[0, a * tq:(a + 1) * tq, :] = jnp.concatenate([pair0, pair1], axis=1).astype(BF16)


def _overlap_matrix_t(n_cmp):
    n = np.arange(n_cmp)[None, :]
    s = np.arange(N_SEL_BLOCKS)[:, None]
    c_start = n * CMP_STRIDE
    s_start = s * SEL_LEN
    ov = (c_start < s_start + SEL_LEN) & (c_start + CMP_LEN > s_start)
    return jnp.asarray(ov.astype(np.float32), BF16)


def nsa_attention(q, ksel, vsel, kwin, vwin, kvc, gates):
    bsz, seq_len, _ = q.shape
    ncb = kvc.shape[3]
    tq = NSA_SUBS * NSA_TQ
    assert seq_len // SEL_LEN <= N_SEL_BLOCKS and seq_len % SEL_TK == 0 and SEL_TK % tq == 0
    assert seq_len >= NSA_TQ + WINDOW
    kv_spec = pl.BlockSpec((1, seq_len, LANES), lambda b, h, i: (b, 0, h))
    rows = GQA * tq
    return pl.pallas_call(
        _nsa_kernel,
        out_shape=jax.ShapeDtypeStruct((bsz, seq_len, NSA_WIDTH), BF16),
        grid=(bsz, N_KV_HEADS, seq_len // tq),
        in_specs=[pl.BlockSpec((1, tq, GQA * LANES), lambda b, h, i: (b, i, h)),
                  kv_spec, kv_spec, kv_spec, kv_spec,
                  pl.BlockSpec((1, 1, 1, ncb, LANES), lambda b, h, i: (0, b, h, 0, 0)),
                  pl.BlockSpec((1, 1, 1, ncb, LANES), lambda b, h, i: (1, b, h, 0, 0)),
                  pl.BlockSpec((1, tq, LANES), lambda b, h, i: (b, i, h)),
                  pl.BlockSpec((N_SEL_BLOCKS, ncb), lambda b, h, i: (0, 0))],
        out_specs=pl.BlockSpec((1, tq, GQA * HEAD_DIM), lambda b, h, i: (b, i, h)),
        scratch_shapes=[pltpu.VMEM((rows, LANES), F32), pltpu.VMEM((rows, LANES), F32)],
        compiler_params=_cparams("parallel", "parallel", "arbitrary"),
        name="nsa_attention",
    )(q, ksel, vsel, kwin, vwin, kvc, kvc, gates, _overlap_matrix_t(ncb))


def _conv_kernel(u_ref, w_ref, cb_ref, g_ref, b_ref, o_ref, pad_ref, *, rows):
    seq_len = u_ref.shape[1]
    halo = CONV_HALO
    pad_ref[0:halo, :] = jnp.zeros((halo, CONV_WIDTH), F32)
    pad_ref[halo:halo + seq_len, :] = u_ref[0]

    first = halo - CONV_LEN + 1

    def body(c, carry):
        r0 = pl.multiple_of(c * rows, rows)
        win = pad_ref[pl.ds(r0, rows + halo), :]
        acc = jnp.zeros((rows, CONV_WIDTH), F32)
        for sub in range(8):
            shifted = win if sub == 0 else pltpu.roll(win, rows + halo - sub, 0)
            for j in range(CONV_LEN):
                if (first + j) % 8 == sub:
                    a0 = first + j - sub
                    acc = acc + w_ref[j:j + 1, :] * shifted[a0:a0 + rows]
        y = _layer_norm(acc + cb_ref[...], g_ref[...], b_ref[...])
        o_ref[0, pl.ds(r0, rows), :] = (y * _sigmoid(y)).astype(BF16)
        return carry

    lax.fori_loop(0, seq_len // rows, body, 0)


def conformer_conv(u, conv_w, conv_b, ln_g, ln_b, rows=CONV_ROWS):
    bsz, seq_len, _ = u.shape
    vec = pl.BlockSpec((1, CONV_WIDTH), lambda b: (0, 0))
    return pl.pallas_call(
        functools.partial(_conv_kernel, rows=rows),
        out_shape=jax.ShapeDtypeStruct((bsz, seq_len, CONV_WIDTH), BF16),
        grid=(bsz,),
        in_specs=[pl.BlockSpec((1, seq_len, CONV_WIDTH), lambda b: (b, 0, 0)),
                  pl.BlockSpec((CONV_LEN, CONV_WIDTH), lambda b: (0, 0)), vec, vec, vec],
        out_specs=pl.BlockSpec((1, seq_len, CONV_WIDTH), lambda b: (b, 0, 0)),
        scratch_shapes=[pltpu.VMEM((seq_len + CONV_HALO, CONV_WIDTH), F32)],
        compiler_params=_cparams("parallel"),
        name="conformer_conv",
    )(u, conv_w, conv_b.reshape(1, -1), ln_g.reshape(1, -1), ln_b.reshape(1, -1))


def _s5_operators(a_re, a_im, log_dt, b_re, b_im, c_re, c_im, n_chunks):
    tc = S5_CHUNK
    dt = jnp.exp(log_dt.astype(F32))[:, None]
    lr = a_re.astype(F32) * dt
    li = a_im.astype(F32) * dt
    mag = jnp.exp(lr)
    ab_re = mag * jnp.cos(li)
    ab_im = mag * jnp.sin(li)
    den = a_re * a_re + a_im * a_im
    coef_re = ((ab_re - 1.0) * a_re + ab_im * a_im) / den
    coef_im = (ab_im * a_re - (ab_re - 1.0) * a_im) / den
    bb_re = coef_re[..., None] * b_re - coef_im[..., None] * b_im
    bb_im = coef_re[..., None] * b_im + coef_im[..., None] * b_re

    def power(tau):
        tau = jnp.asarray(tau, F32)
        m = jnp.exp(lr[..., None] * tau)
        return m * jnp.cos(li[..., None] * tau), m * jnp.sin(li[..., None] * tau)

    lag_re, lag_im = power(jnp.arange(tc + 1))
    ca_re = c_re[..., None] * lag_re[:, None] - c_im[..., None] * lag_im[:, None]
    ca_im = c_re[..., None] * lag_im[:, None] + c_im[..., None] * lag_re[:, None]
    kern = (jnp.einsum('gcnt,gnd->gtcd', ca_re[..., :tc], bb_re, precision=HIGHEST)
            - jnp.einsum('gcnt,gnd->gtcd', ca_im[..., :tc], bb_im, precision=HIGHEST))
    s_idx = np.arange(tc)[:, None]
    i_idx = np.arange(tc)[None, :]
    lag = np.clip(i_idx - s_idx, 0, tc - 1)
    toep = kern[:, lag]
    toep = jnp.where(jnp.asarray(i_idx >= s_idx)[None, :, :, None, None], toep, 0.0)
    rev_re, rev_im = lag_re[..., tc - 1::-1], lag_im[..., tc - 1::-1]
    bop_re = (rev_re[..., None] * bb_re[:, :, None] - rev_im[..., None] * bb_im[:, :, None])
    bop_im = (rev_re[..., None] * bb_im[:, :, None] + rev_im[..., None] * bb_re[:, :, None])
    bop_re = bop_re.transpose(0, 2, 3, 1)
    bop_im = bop_im.transpose(0, 2, 3, 1)
    cop_re = ca_re[..., 1:].transpose(0, 2, 3, 1)
    cop_im = -ca_im[..., 1:].transpose(0, 2, 3, 1)

    gh = S5_GROUPS // S5_HALVES
    eye = jnp.eye(gh, dtype=F32)
    hw = tc * gh * S5_GROUP_CH
    sw = gh * S5_STATE
    split = lambda x: x.reshape((S5_HALVES, gh) + x.shape[1:])
    m_nat = jnp.einsum('hgsiod,gk->hsgdiko', split(toep), eye).reshape(S5_HALVES, hw, hw)
    b_nat = jnp.concatenate(
        [jnp.einsum('hgsdn,gk->hsgdkn', split(bop), eye).reshape(S5_HALVES, hw, sw) for bop in (bop_re, bop_im)],
        axis=2)
    c_nat = jnp.concatenate(
        [jnp.einsum('hgnio,gk->hgniko', split(cop), eye).reshape(S5_HALVES, sw, hw) for cop in (cop_re, cop_im)],
        axis=1)
    levels = max(1, int(math.log2(n_chunks)))
    lv_re, lv_im = power(tc * (2.0 ** jnp.arange(levels)))
    lanes = lambda x: split(x).transpose(0, 3, 1, 2).reshape(S5_HALVES, levels, sw)
    a_lv = jnp.concatenate([lanes(lv_re), lanes(lv_im)], axis=-1)
    return m_nat.astype(BF16), b_nat.astype(BF16), c_nat.astype(BF16), a_lv


def _s5_kernel(u0_ref, u1_ref, m_ref, b_ref, c_ref, a_ref, d_ref, gw_ref, gb_ref, o0_ref, o1_ref,
               sre_ref, sim_ref):
    tc = S5_CHUNK
    nc = u0_ref.shape[1] // tc
    hl = S5_WIDTH // S5_HALVES
    sw = sre_ref.shape[1]
    u_step = [[u_ref[0, pl.ds(s, nc, stride=tc), :] for s in range(tc)] for u_ref in (u0_ref, u1_ref)]
    y_half = []
    for h in range(S5_HALVES):
        uh = jnp.concatenate(u_step[h], axis=1).astype(BF16)
        v = jnp.dot(uh, b_ref[h], preferred_element_type=F32)
        zeros = jnp.zeros((nc, sw), F32)
        sre_ref[0:nc, :] = zeros
        sim_ref[0:nc, :] = zeros
        sre_ref[nc:2 * nc, :] = v[:, 0:sw]
        sim_ref[nc:2 * nc, :] = v[:, sw:2 * sw]
        for lv in range(a_ref.shape[1]):
            d = 1 << lv
            ar = a_ref[h, lv:lv + 1, 0:sw]
            ai = a_ref[h, lv:lv + 1, sw:2 * sw]
            pr = sre_ref[nc - d:2 * nc - d, :]
            pi = sim_ref[nc - d:2 * nc - d, :]
            cr = sre_ref[nc:2 * nc, :]
            ci = sim_ref[nc:2 * nc, :]
            sre_ref[nc:2 * nc, :] = cr + ar * pr - ai * pi
            sim_ref[nc:2 * nc, :] = ci + ar * pi + ai * pr
        prev_re = sre_ref[nc - 1:2 * nc - 1, :].astype(BF16)
        prev_im = sim_ref[nc - 1:2 * nc - 1, :].astype(BF16)
        y_half.append(jnp.dot(uh, m_ref[h], preferred_element_type=F32)
                      + jnp.dot(prev_re, c_ref[h, 0:sw, :], preferred_element_type=F32)
                      + jnp.dot(prev_im, c_ref[h, sw:2 * sw, :], preferred_element_type=F32))
    for i in range(tc):
        y = jnp.concatenate([yh[:, i * hl:(i + 1) * hl] for yh in y_half], axis=1)
        z = _gelu_tanh(y + d_ref[...] * jnp.concatenate([u_step[0][i], u_step[1][i]], axis=1))
        gate = jnp.dot(z.astype(BF16), gw_ref[...], preferred_element_type=F32) + gb_ref[...]
        out = z * _sigmoid(gate)
        o0_ref[0, pl.ds(i, nc, stride=tc), :] = out[:, 0:hl]
        o1_ref[0, pl.ds(i, nc, stride=tc), :] = out[:, hl:2 * hl]


def s5_layer(zs, operators, d_skip, glu_w, glu_b):
    m_nat, b_nat, c_nat, a_lv = operators
    bsz, seq_len, _ = zs.shape
    tc = S5_CHUNK
    nc = seq_len // tc
    assert nc & (nc - 1) == 0 and a_lv.shape[1] == int(math.log2(nc))
    hl = S5_WIDTH // S5_HALVES
    sw = a_lv.shape[2] // 2
    full = lambda a: pl.BlockSpec(a.shape, lambda b: (0,) * a.ndim)
    vec = pl.BlockSpec((1, S5_WIDTH), lambda b: (0, 0))
    half = lambda h: pl.BlockSpec((1, seq_len, hl), lambda b: (b, 0, h))
    glu_wb = glu_w.astype(BF16)
    out = pl.pallas_call(
        _s5_kernel,
        out_shape=(jax.ShapeDtypeStruct((bsz, seq_len, hl), F32),) * S5_HALVES,
        grid=(bsz,),
        in_specs=[half(0), half(1), full(m_nat), full(b_nat), full(c_nat), full(a_lv), vec, full(glu_wb), vec],
        out_specs=(half(0),) * S5_HALVES,
        scratch_shapes=[pltpu.VMEM((2 * nc, sw), F32), pltpu.VMEM((2 * nc, sw), F32)],
        compiler_params=_cparams("parallel"),
        name="s5_layer",
    )(zs, zs, m_nat, b_nat, c_nat, a_lv, d_skip.reshape(1, -1), glu_wb, glu_b.reshape(1, -1))
    return tuple(o.reshape(bsz * seq_len, hl) for o in out)


def _route_top2(logits):
    lane = lax.broadcasted_iota(jnp.int32, logits.shape, 1)
    lane_f = lane.astype(F32)
    logits = jnp.where(lane < N_EXPERTS, logits, -jnp.inf)
    v1 = jnp.max(logits, axis=-1, keepdims=True)
    i1 = jnp.min(jnp.where(logits == v1, lane_f, 1e9), axis=-1, keepdims=True)
    rest = jnp.where(lane_f == i1, -jnp.inf, logits)
    v2 = jnp.max(rest, axis=-1, keepdims=True)
    i2 = jnp.min(jnp.where(rest == v2, lane_f, 1e9), axis=-1, keepdims=True)
    e2 = jnp.exp(v2 - v1)
    den = 1.0 + e2
    out = jnp.where(lane == 0, i1, jnp.where(lane == 1, i2, jnp.where(lane == 2, 1.0 / den, e2 / den)))
    return jnp.where(lane < 4, out, 0.0)


def _out_proj_kernel(x_ref, a_ref, c_ref, s0_ref, s1_ref, w_ref, g_ref, b_ref, *rest, alpha):
    o_ref = rest[0] if len(rest) == 1 else rest[1]
    o_s5 = jnp.concatenate([s0_ref[...], s1_ref[...]], axis=1).astype(BF16)
    h = jnp.dot(a_ref[...], w_ref[0:NSA_WIDTH, :], preferred_element_type=F32)
    h = h + jnp.dot(c_ref[...], w_ref[NSA_WIDTH:NSA_WIDTH + CONV_WIDTH, :], preferred_element_type=F32)
    h = h + jnp.dot(o_s5, w_ref[NSA_WIDTH + CONV_WIDTH:, :], preferred_element_type=F32)
    y = _layer_norm(alpha * x_ref[...] + h, g_ref[...], b_ref[...])
    o_ref[...] = y
    if len(rest) > 1:
        router_ref, _, routed_ref, ybf_ref = rest
        ybf_ref[...] = y.astype(BF16)
        y_top = _top_bits(y)
        y_hi = y_top.astype(BF16)
        y_lo = (y - y_top).astype(BF16)
        rows = y.shape[0]
        prod = jnp.dot(jnp.concatenate([y_hi, y_lo], axis=0), router_ref[...],
                       preferred_element_type=F32)
        logits = (prod[0:rows, 0:LANES] + prod[0:rows, LANES:2 * LANES]
                  + prod[rows:2 * rows, 0:LANES] + prod[rows:2 * rows, LANES:2 * LANES])
        routed_ref[...] = _route_top2(logits)


def out_proj_ln(x2, o_nsa, o_conv, o_s5, w_out, g, b, alpha, router=None, tm=ROW_TILE_SMALL):
    t, d = x2.shape
    row = lambda width: pl.BlockSpec((tm, width), lambda i: (i, 0))
    vec = pl.BlockSpec((1, d), lambda i: (0, 0))
    in_specs = [row(d), row(NSA_WIDTH), row(CONV_WIDTH), row(S5_WIDTH // S5_HALVES), row(S5_WIDTH // S5_HALVES),
                pl.BlockSpec(w_out.shape, lambda i: (0, 0)), vec, vec]
    args = [x2, o_nsa, o_conv, *o_s5, w_out.astype(BF16), g.reshape(1, -1), b.reshape(1, -1)]
    out_shape = jax.ShapeDtypeStruct((t, d), F32)
    out_specs = row(d)
    if router is not None:
        w = jnp.pad(router, ((0, 0), (0, LANES - router.shape[1])))
        w_top = lax.bitcast_convert_type(
            lax.bitcast_convert_type(w, jnp.uint32) & jnp.uint32(0xFFFF0000), F32)
        w_hi = w_top.astype(BF16)
        w_lo = (w - w_top).astype(BF16)
        in_specs.append(pl.BlockSpec((d, 2 * LANES), lambda i: (0, 0)))
        args.append(jnp.concatenate([w_hi, w_lo], axis=1))
        out_shape = (out_shape, jax.ShapeDtypeStruct((t, LANES), F32), jax.ShapeDtypeStruct((t, d), BF16))
        out_specs = (out_specs, row(LANES), row(d))
    return pl.pallas_call(
        functools.partial(_out_proj_kernel, alpha=alpha),
        out_shape=out_shape,
        grid=(t // tm,),
        in_specs=in_specs,
        out_specs=out_specs,
        compiler_params=_cparams("parallel"),
        name="out_proj_ln",
    )(*args)


def _ffn_kernel(te_ref, nu_ref, x_ref, w1_ref, w3_ref, w2_ref, *rest, alpha):
    ln_refs, (o_ref, acc_ref) = rest[:-2], rest[-2:]
    i = pl.program_id(0)
    f = pl.program_id(1)
    last = pl.num_programs(1) - 1
    used = i < nu_ref[0]

    @pl.when(used)
    def _():
        xb = x_ref[...].astype(BF16)
        h1 = jnp.dot(xb, w1_ref[0].astype(BF16), preferred_element_type=F32)
        h3 = jnp.dot(xb, w3_ref[0].astype(BF16), preferred_element_type=F32)
        h = (h1 * _sigmoid(h1) * h3).astype(BF16)
        part = jnp.dot(h, w2_ref[0].astype(BF16), preferred_element_type=F32)

        @pl.when(f == 0)
        def _():
            acc_ref[...] = part

        @pl.when(f > 0)
        def _():
            acc_ref[...] = acc_ref[...] + part

        @pl.when(f == last)
        def _():
            if ln_refs:
                o_ref[...] = _layer_norm(alpha * x_ref[...] + acc_ref[...], ln_refs[0][...], ln_refs[1][...])
            else:
                o_ref[...] = acc_ref[...].astype(o_ref.dtype)

    @pl.when(jnp.logical_not(used) & (f == last))
    def _():
        o_ref[...] = jnp.zeros(o_ref.shape, o_ref.dtype)


def grouped_swiglu(xs, tile_expert, n_used, w1, w3, w2, tm, tf, out_dtype, ln=None, alpha=1.0):
    p, d = xs.shape
    ff = w1.shape[2]
    assert p % tm == 0 and ff % tf == 0
    vec = pl.BlockSpec((1, d), lambda i, f, te, nu: (0, 0))
    ln_args = () if ln is None else (ln[0].reshape(1, d), ln[1].reshape(1, d))
    grid_spec = pltpu.PrefetchScalarGridSpec(
        num_scalar_prefetch=2,
        grid=(p // tm, ff // tf),
        in_specs=[pl.BlockSpec((tm, d), lambda i, f, te, nu: (i, 0)),
                  pl.BlockSpec((1, d, tf), lambda i, f, te, nu: (te[i], 0, f)),
                  pl.BlockSpec((1, d, tf), lambda i, f, te, nu: (te[i], 0, f)),
                  pl.BlockSpec((1, tf, d), lambda i, f, te, nu: (te[i], f, 0))] + [vec] * len(ln_args),
        out_specs=pl.BlockSpec((tm, d), lambda i, f, te, nu: (i, 0)),
        scratch_shapes=[pltpu.VMEM((tm, d), F32)],
    )
    return pl.pallas_call(
        functools.partial(_ffn_kernel, alpha=alpha),
        out_shape=jax.ShapeDtypeStruct((p, d), out_dtype),
        grid_spec=grid_spec,
        compiler_params=_cparams("arbitrary", "arbitrary"),
        name="grouped_swiglu",
    )(tile_expert, n_used, xs, w1, w3, w2, *ln_args)


def _pick_tf(ff, target=1024):
    best = LANES
    for cand in range(LANES, ff + 1, LANES):
        if ff % cand == 0 and cand <= target:
            best = cand
    return best


def _residual_ln_kernel(x_ref, a_ref, b_ref, wa_ref, wb_ref, g_ref, beta_ref, o_ref, *, alpha):
    f = wa_ref[...] * a_ref[...] + wb_ref[...] * b_ref[...]
    o_ref[...] = _layer_norm(alpha * x_ref[...] + f, g_ref[...], beta_ref[...])


def residual_ln(x2, a, b, wa, wb, g, beta, alpha, tm=ROW_TILE):
    t, d = x2.shape
    row = pl.BlockSpec((tm, d), lambda i: (i, 0))
    col = pl.BlockSpec((tm, 1), lambda i: (i, 0))
    vec = pl.BlockSpec((1, d), lambda i: (0, 0))
    return pl.pallas_call(
        functools.partial(_residual_ln_kernel, alpha=alpha),
        out_shape=jax.ShapeDtypeStruct((t, d), F32),
        grid=(t // tm,),
        in_specs=[row, row, row, col, col, vec, vec],
        out_specs=row,
        compiler_params=_cparams("parallel"),
        name="residual_ln",
    )(x2, a, b, wa, wb, g.reshape(1, -1), beta.reshape(1, -1))


def moe_swiglu_ln(x2, x2_bf16, routed, w1, w3, w2, first_expert, g, beta, alpha, tm):
    t, d = x2.shape
    n_exp = N_EXPERTS
    idx = routed[:, 0:2].astype(jnp.int32)
    wts = routed[:, 2:4]
    e_flat = idx.T.reshape(-1)
    onehot = (e_flat[:, None] == jnp.arange(n_exp, dtype=jnp.int32)[None, :]).astype(jnp.int32)
    running = jnp.cumsum(onehot, axis=0)
    counts = running[-1]
    rank = jnp.sum(onehot * running, axis=1) - 1
    padded = ((counts + tm - 1) // tm) * tm
    pad_end = jnp.cumsum(padded)
    pad_start = pad_end - padded
    start = jnp.cumsum(counts) - counts
    pos = pad_start[e_flat] + rank
    n_rows = 2 * t + n_exp * tm
    n_tiles = n_rows // tm
    tile_start = jnp.arange(n_tiles, dtype=jnp.int32) * tm
    tile_expert = jnp.minimum(jnp.sum(tile_start[:, None] >= pad_end[None, :], axis=1), n_exp - 1).astype(jnp.int32)
    n_used = (pad_end[-1] // tm).astype(jnp.int32).reshape(1)
    order = jnp.argsort(e_flat, stable=True)
    row = jnp.arange(n_rows, dtype=jnp.int32)
    row_expert = jnp.repeat(tile_expert, tm)
    slot = jnp.minimum(start[row_expert] + (row - pad_start[row_expert]), 2 * t - 1)
    src = order[slot] % t
    xs = x2_bf16[src]
    ys = grouped_swiglu(xs, tile_expert + first_expert, n_used, w1, w3, w2, tm, _pick_tf(w1.shape[2]), BF16)
    return residual_ln(x2, ys[pos[:t]], ys[pos[t:]], wts[:, 0:1], wts[:, 1:2], g, beta, alpha)


def dense_swiglu_ln(x2, w1, w3, w2, layer, g, beta, alpha, tm):
    t = x2.shape[0]
    tile_expert = jnp.full((t // tm,), layer, jnp.int32)
    n_used = jnp.full((1,), t // tm, jnp.int32)
    return grouped_swiglu(x2, tile_expert, n_used, w1, w3, w2, tm, _pick_tf(w1.shape[2], 1408),
                          F32, ln=(g, beta), alpha=alpha)


def _ple_kernel(x_ref, p_ref, wg_ref, bg_ref, wp_ref, g_ref, beta_ref, o_ref, *, alpha):
    x = x_ref[...]
    gate = _sigmoid(jnp.dot(x.astype(BF16), wg_ref[...], preferred_element_type=F32) + bg_ref[...])
    e = jnp.dot(p_ref[...].astype(BF16), wp_ref[...], preferred_element_type=F32) * gate
    o_ref[...] = _layer_norm(alpha * x + e, g_ref[...], beta_ref[...])


def ple_ln(x2, p_all, layer, gate_w, gate_b, proj, g, beta, alpha, tm=ROW_TILE):
    t, d = x2.shape
    pd = p_all.shape[1]
    first = layer * (t // tm)
    vec = pl.BlockSpec((1, d), lambda i: (0, 0))
    return pl.pallas_call(
        functools.partial(_ple_kernel, alpha=alpha),
        out_shape=jax.ShapeDtypeStruct((t, d), F32),
        grid=(t // tm,),
        in_specs=[pl.BlockSpec((tm, d), lambda i: (i, 0)), pl.BlockSpec((tm, pd), lambda i: (first + i, 0)),
                  pl.BlockSpec((d, d), lambda i: (0, 0)), vec,
                  pl.BlockSpec((pd, d), lambda i: (0, 0)), vec, vec],
        out_specs=pl.BlockSpec((tm, d), lambda i: (i, 0)),
        compiler_params=_cparams("parallel"),
        name="ple_ln",
    )(x2, p_all, gate_w.astype(BF16), gate_b.reshape(1, -1), proj.astype(BF16), g.reshape(1, -1), beta.reshape(1, -1))


def hybrid_mixer_ln(x2, bsz, seq_len, tables, w_in, w_out, cmp_weights, conv_params, s5_params,
                    g, beta, alpha, router=None):
    (q, ksel, kwin, vsel, vwin, kv_cmp, gates, u_conv, zs) = in_proj(x2, _widen_w_in(w_in), tables, seq_len)
    b3 = lambda a: a.reshape(bsz, seq_len, a.shape[1])
    kvc = compress(kv_cmp.reshape(2, bsz, seq_len, LANES), _compress_weights(*cmp_weights))
    o_nsa = nsa_attention(b3(q), b3(ksel), b3(vsel), b3(kwin), b3(vwin), kvc, b3(gates))
    o_conv = conformer_conv(b3(u_conv), *conv_params)
    (a_re, a_im, log_dt, b_re, b_im, c_re, c_im, d_skip, glu_w, glu_b) = s5_params
    ops = _s5_operators(a_re, a_im, log_dt, b_re, b_im, c_re, c_im, seq_len // S5_CHUNK)
    o_s5 = s5_layer(b3(zs), ops, d_skip, glu_w, glu_b)
    return out_proj_ln(x2, o_nsa.reshape(-1, NSA_WIDTH), o_conv.reshape(-1, CONV_WIDTH), o_s5, w_out,
                       g, beta, alpha, router)


def kernel(x, p, positions, w_in, w_out, cmp_pe, cmp_w1, cmp_b1, cmp_w2, cmp_b2, conv_w, conv_b, conv_ln_g, conv_ln_b, s5_a_re, s5_a_im, s5_log_dt, s5_b_re, s5_b_im, s5_c_re, s5_c_im, s5_d, s5_glu_w, s5_glu_b, ffn_w1, ffn_w3, ffn_w2, moe_router, moe_w1, moe_w3, moe_w2, ple_gate_w, ple_gate_b, ple_proj, ln_g, ln_b):
    bsz, seq_len, d_model = x.shape
    depth = w_in.shape[0]
    alpha = (2 * depth) ** 0.25
    t = bsz * seq_len
    tables = rope_tables(positions)
    x2 = x.reshape(t, d_model)
    dense_tm = ROW_TILE_SMALL
    moe_tm = ROW_TILE if t % ROW_TILE == 0 else ROW_TILE_SMALL
    stack = lambda w: w.reshape((-1,) + w.shape[-2:])
    ffn_w = tuple(stack(w).astype(BF16) for w in (ffn_w1, ffn_w3, ffn_w2))
    moe_w = tuple(stack(w) for w in (moe_w1, moe_w3, moe_w2))
    for i in range(depth):
        j = i // 2
        routed_layer = i % 2 == 1
        mixed = hybrid_mixer_ln(
            x2, bsz, seq_len, tables, w_in[i], w_out[i],
            (cmp_pe[i], cmp_w1[i], cmp_b1[i], cmp_w2[i], cmp_b2[i]),
            (conv_w[i], conv_b[i], conv_ln_g[i], conv_ln_b[i]),
            (s5_a_re[i], s5_a_im[i], s5_log_dt[i], s5_b_re[i], s5_b_im[i], s5_c_re[i], s5_c_im[i],
             s5_d[i], s5_glu_w[i], s5_glu_b[i]),
            ln_g[i, 0], ln_b[i, 0], alpha, moe_router[j] if routed_layer else None)
        if routed_layer:
            x2, routed, x2_bf16 = mixed
            x2 = moe_swiglu_ln(x2, x2_bf16, routed, *moe_w, j * N_EXPERTS, ln_g[i, 1], ln_b[i, 1], alpha, moe_tm)
        else:
            x2 = dense_swiglu_ln(mixed, *ffn_w, j, ln_g[i, 1], ln_b[i, 1], alpha, dense_tm)
        x2 = ple_ln(x2, p.reshape(depth * t, -1), i, ple_gate_w[i], ple_gate_b[i], ple_proj[i],
                    ln_g[i, 2], ln_b[i, 2], alpha)
    return x2.reshape(bsz, seq_len, d_model)
```

```python
import functools
import math

import numpy as np
import jax
import jax.numpy as jnp
from jax import lax
from jax.experimental import pallas as pl
from jax.experimental.pallas import tpu as pltpu

F32 = jnp.float32
BF16 = jnp.bfloat16
HIGHEST = lax.Precision.HIGHEST

LANES = 128
VMEM_LIMIT = 56 * 1024 * 1024

ROW_TILE = 1024
ROW_TILE_SMALL = 512
CONV_ROWS = 256
CONV_HALO = 32

HEAD_DIM = 64
HALF_DIM = HEAD_DIM // 2
N_Q_HEADS = 8
N_KV_HEADS = 2
GQA = N_Q_HEADS // N_KV_HEADS
CMP_LEN = 32
CMP_STRIDE = 16
CMP_HIDDEN = 256
SEL_LEN = 64
SEL_TOPK = 16
WINDOW = 512
ROPE_THETA = 10000.0
FORCED_SCORE = 1e9
NEG = -1e30
CONV_WIDTH = 256
CONV_LEN = 31
S5_WIDTH = 256
S5_GROUP_CH = 16
S5_GROUPS = 16
S5_STATE = 64
S5_CHUNK = 8
S5_HALVES = 2
N_EXPERTS = 8
LN_EPS = 1e-5
NSA_WIDTH = N_Q_HEADS * HEAD_DIM

C_Q = 0
C_KV = C_Q + NSA_WIDTH
C_GATE = C_KV + 6 * N_KV_HEADS * HEAD_DIM
C_CONVA = C_GATE + N_KV_HEADS * LANES
C_CONVB = C_CONVA + CONV_WIDTH
C_S5 = C_CONVB + CONV_WIDTH
C_TOTAL = C_S5 + S5_WIDTH
Q_SCALE = HEAD_DIM ** -0.5 * math.log2(math.e)


def _cparams(*sem):
    return pltpu.CompilerParams(dimension_semantics=sem, vmem_limit_bytes=VMEM_LIMIT)


def _layer_norm(v, g, b):
    mu = jnp.mean(v, axis=-1, keepdims=True)
    d = v - mu
    var = jnp.mean(d * d, axis=-1, keepdims=True)
    return d * lax.rsqrt(var + LN_EPS) * g + b


def _gelu_tanh(x):
    return 0.5 * x * (1.0 + jnp.tanh(math.sqrt(2.0 / math.pi) * (x + 0.044715 * (x * x * x))))


def _sigmoid(x):
    return 1.0 / (1.0 + jnp.exp(-x))


def _top_bits(x):
    return pltpu.bitcast(pltpu.bitcast(x, jnp.uint32) & jnp.uint32(0xFFFF0000), F32)


def _rope_table_kernel(pos_ref, freq_ref, tab_ref):
    ang = pos_ref[...] * freq_ref[...]
    c = jnp.cos(ang)
    s = jnp.sin(ang)
    lane = lax.broadcasted_iota(jnp.int32, ang.shape, 1)
    first_half = (lane % HEAD_DIM) < HALF_DIM
    tab_ref[:, 0:LANES] = c
    tab_ref[:, LANES:2 * LANES] = jnp.where(first_half, -s, 0.0)
    tab_ref[:, 2 * LANES:3 * LANES] = jnp.where(first_half, 0.0, s)


def rope_tables(positions, tm=ROW_TILE_SMALL):
    t = positions.size
    pos = positions.reshape(t, 1).astype(F32)
    inv_freq = ROPE_THETA ** (-jnp.arange(0, HEAD_DIM, 2, dtype=F32) / HEAD_DIM)
    freq = jnp.tile(inv_freq, LANES // HALF_DIM).reshape(1, LANES)
    return pl.pallas_call(
        _rope_table_kernel,
        out_shape=jax.ShapeDtypeStruct((t, 3 * LANES), F32),
        grid=(t // tm,),
        in_specs=[pl.BlockSpec((tm, 1), lambda i: (i, 0)),
                  pl.BlockSpec((1, LANES), lambda i: (0, 0))],
        out_specs=pl.BlockSpec((tm, 3 * LANES), lambda i: (i, 0)),
        compiler_params=_cparams("parallel"),
        name="rope_tables",
    )(pos, freq)


def _in_proj_kernel(x_ref, w_ref, tab_ref, q_ref, ksel_ref, kwin_ref, vsel_ref, vwin_ref,
                    kvcmp_ref, gate_ref, u_ref, zs_ref, *, seq_len):
    tm = x_ref.shape[0]
    xb = x_ref[...].astype(BF16)
    cos = tab_ref[:, 0:LANES]
    sin_a = tab_ref[:, LANES:2 * LANES]
    sin_b = tab_ref[:, 2 * LANES:3 * LANES]

    def proj(c0):
        z = jnp.dot(xb, w_ref[:, c0:c0 + 2 * LANES], preferred_element_type=F32)
        return z[:, 0:LANES], z[:, LANES:2 * LANES]

    def rope(z):
        return (z * cos + pltpu.roll(z, LANES - HALF_DIM, 1) * sin_a
                + pltpu.roll(z, HALF_DIM, 1) * sin_b)

    lane = lax.broadcasted_iota(jnp.int32, (tm, LANES), 1)
    row = lax.broadcasted_iota(jnp.int32, (tm, LANES), 0)
    low = lane < HEAD_DIM
    t_seq = (pl.program_id(0) * tm) % seq_len + row
    blk_onehot = jnp.where(lane == HEAD_DIM + t_seq // SEL_LEN, 1.0, 0.0)
    ones_lane = jnp.where(lane == HEAD_DIM, 1.0, 0.0)

    def spread(z, extra, out_ref, c0):
        out_ref[:, c0:c0 + LANES] = (jnp.where(low, z, 0.0) + extra).astype(BF16)
        out_ref[:, c0 + LANES:c0 + 2 * LANES] = (jnp.where(low, pltpu.roll(z, HEAD_DIM, 1), 0.0) + extra).astype(BF16)

    for pair in range(N_Q_HEADS // 4):
        for n, z in enumerate(proj(C_Q + pair * 2 * LANES)):
            spread(rope(z) * Q_SCALE, 0.0, q_ref, (2 * pair + n) * 2 * LANES)
    k_cmp, v_cmp = proj(C_KV)
    kvcmp_ref[0] = rope(k_cmp)
    kvcmp_ref[1] = v_cmp
    k_sel, v_sel = proj(C_KV + 2 * LANES)
    spread(rope(k_sel), blk_onehot, ksel_ref, 0)
    spread(v_sel, ones_lane, vsel_ref, 0)
    k_win, v_win = proj(C_KV + 4 * LANES)
    spread(rope(k_win), 0.0, kwin_ref, 0)
    spread(v_win, ones_lane, vwin_ref, 0)
    g0, g1 = proj(C_GATE)
    gate_ref[:, 0:LANES] = _sigmoid(g0)
    gate_ref[:, LANES:2 * LANES] = _sigmoid(g1)
    a0, a1 = proj(C_CONVA)
    b0, b1 = proj(C_CONVB)
    u_ref[:, 0:LANES] = a0 * _sigmoid(b0)
    u_ref[:, LANES:2 * LANES] = a1 * _sigmoid(b1)
    s0, s1 = proj(C_S5)
    zs_ref[:, 0:LANES] = s0
    zs_ref[:, LANES:2 * LANES] = s1


def _widen_w_in(w_in):
    d = w_in.shape[0]
    o_gate = NSA_WIDTH + 6 * N_KV_HEADS * HEAD_DIM
    o_conv = o_gate + 3 * N_Q_HEADS
    per_group = 3 * GQA
    cols = [w_in[:, :o_gate]]
    for h in range(N_KV_HEADS):
        cols += [w_in[:, o_gate + h * per_group:o_gate + (h + 1) * per_group],
                 jnp.zeros((d, LANES - per_group), w_in.dtype)]
    cols += [w_in[:, o_conv:]]
    w = jnp.concatenate(cols, axis=1)
    assert w.shape[1] == C_TOTAL
    return w.astype(BF16)


def in_proj(x2, w_wide, tables, seq_len, tm=ROW_TILE):
    t, d = x2.shape
    row = lambda width: pl.BlockSpec((tm, width), lambda i: (i, 0))
    out_shape = (
        jax.ShapeDtypeStruct((t, N_Q_HEADS * LANES), BF16),
        jax.ShapeDtypeStruct((t, N_KV_HEADS * LANES), BF16),
        jax.ShapeDtypeStruct((t, N_KV_HEADS * LANES), BF16),
        jax.ShapeDtypeStruct((t, N_KV_HEADS * LANES), BF16),
        jax.ShapeDtypeStruct((t, N_KV_HEADS * LANES), BF16),
        jax.ShapeDtypeStruct((2, t, LANES), F32),
        jax.ShapeDtypeStruct((t, 2 * LANES), F32),
        jax.ShapeDtypeStruct((t, CONV_WIDTH), F32),
        jax.ShapeDtypeStruct((t, S5_WIDTH), F32),
    )
    out_spec = lambda s: (row(s.shape[1]) if len(s.shape) == 2
                          else pl.BlockSpec((s.shape[0], tm, s.shape[2]), lambda i: (0, i, 0)))
    return pl.pallas_call(
        functools.partial(_in_proj_kernel, seq_len=seq_len),
        out_shape=out_shape,
        grid=(t // tm,),
        in_specs=[row(d),
                  pl.BlockSpec((d, C_TOTAL), lambda i: (0, 0)),
                  row(3 * LANES)],
        out_specs=tuple(out_spec(s) for s in out_shape),
        compiler_params=_cparams("parallel"),
        name="in_proj",
    )(x2, w_wide, tables)


def _compress_kernel(x_ref, pe_ref, w1_ref, b1_ref, w2_ref, b2_ref, o_ref, shift_ref):
    nb = x_ref.shape[2] // CMP_STRIDE
    x = jnp.concatenate([x_ref[0, 0, pl.ds(s, nb, stride=CMP_STRIDE), :] for s in range(CMP_STRIDE)],
                        axis=1)
    xa = (x + pe_ref[0, 0:1, :]).astype(BF16)
    xb = (x + pe_ref[0, 1:2, :]).astype(BF16)
    ha = jnp.dot(xa, w1_ref[0, 0], preferred_element_type=F32)
    hb = jnp.dot(xb, w1_ref[0, 1], preferred_element_type=F32)
    shift_ref[0:nb, :] = hb
    shift_ref[nb:nb + 8, :] = jnp.zeros((8, hb.shape[1]), F32)
    h = ha + shift_ref[1:nb + 1, :] + b1_ref[0]
    g = _gelu_tanh(h).astype(BF16)
    for hd in range(N_KV_HEADS):
        gh = g[:, hd * CMP_HIDDEN:(hd + 1) * CMP_HIDDEN]
        o_ref[0, 0, hd] = (jnp.dot(gh, w2_ref[0], preferred_element_type=F32) + b2_ref[0]).astype(BF16)


def _compress_weights(cmp_pe, cmp_w1, cmp_b1, cmp_w2, cmp_b2):
    half = CMP_LEN // 2
    pe = cmp_pe.reshape(2, 2, half, 1, HEAD_DIM)
    pe = jnp.broadcast_to(pe, (2, 2, half, N_KV_HEADS, HEAD_DIM)).reshape(2, 2, half * N_KV_HEADS * HEAD_DIM)
    w1 = cmp_w1.reshape(2, 2, half, HEAD_DIM, CMP_HIDDEN)
    z = jnp.zeros_like(w1)
    w_h0 = jnp.stack([w1, z], axis=3)
    w_h1 = jnp.stack([z, w1], axis=3)
    w1w = jnp.concatenate([w_h0, w_h1], axis=-1)
    w1w = w1w.reshape(2, 2, half * N_KV_HEADS * HEAD_DIM, N_KV_HEADS * CMP_HIDDEN).astype(BF16)
    b1 = jnp.tile(cmp_b1, (1, N_KV_HEADS)).reshape(2, 1, N_KV_HEADS * CMP_HIDDEN)
    w2 = jnp.pad(cmp_w2, ((0, 0), (0, 0), (0, LANES - HEAD_DIM))).astype(BF16)
    b2 = jnp.pad(cmp_b2, ((0, 0), (0, LANES - HEAD_DIM))).reshape(2, 1, LANES)
    return pe, w1w, b1, w2, b2


def compress(kv_cmp, weights):
    pe, w1w, b1, w2, b2 = weights
    _, bsz, seq_len, _ = kv_cmp.shape
    nb = seq_len // CMP_STRIDE
    cw = CMP_STRIDE * LANES
    hid = N_KV_HEADS * CMP_HIDDEN
    return pl.pallas_call(
        _compress_kernel,
        out_shape=jax.ShapeDtypeStruct((2, bsz, N_KV_HEADS, nb, LANES), BF16),
        grid=(2, bsz),
        in_specs=[pl.BlockSpec((1, 1, seq_len, LANES), lambda j, b: (j, b, 0, 0)),
                  pl.BlockSpec((1, 2, cw), lambda j, b: (j, 0, 0)),
                  pl.BlockSpec((1, 2, cw, hid), lambda j, b: (j, 0, 0, 0)),
                  pl.BlockSpec((1, 1, hid), lambda j, b: (j, 0, 0)),
                  pl.BlockSpec((1, CMP_HIDDEN, LANES), lambda j, b: (j, 0, 0)),
                  pl.BlockSpec((1, 1, LANES), lambda j, b: (j, 0, 0))],
        out_specs=pl.BlockSpec((1, 1, N_KV_HEADS, nb, LANES), lambda j, b: (j, b, 0, 0, 0)),
        scratch_shapes=[pltpu.VMEM((nb + 8, hid), F32)],
        compiler_params=_cparams("parallel", "parallel"),
        name="compress",
    )(kv_cmp, pe, w1w, b1, w2, b2)


SEL_TK = 512
NSA_TQ = 128
NSA_SUBS = 4
N_SEL_BLOCKS = LANES - HEAD_DIM


def _nsa_kernel(q_ref, ksel_ref, vsel_ref, kwin_ref, vwin_ref, kc_ref, vc_ref, gate_ref, ovt_ref,
                o_ref, m_ref, acc_ref):
    tq = NSA_TQ
    subs = range(q_ref.shape[1] // tq)
    units = [(a, g) for a in subs for g in range(GQA)]
    i = pl.program_id(2)
    q0 = i * q_ref.shape[1]
    qh = {(a, g): q_ref[0, a * tq:(a + 1) * tq, g * LANES:(g + 1) * LANES] for a, g in units}
    t_col = [q0 + a * tq + lax.broadcasted_iota(jnp.int32, (tq, 1), 0) for a in subs]
    t_row = [q0 + a * tq + lax.broadcasted_iota(jnp.int32, (1, tq), 1) for a in subs]
    contract_last = (((1,), (1,)), ((), ()))

    ncb = kc_ref.shape[3]
    kc = kc_ref[0, 0, 0]
    vc = vc_ref[0, 0, 0]
    s_cmp = {u: lax.dot_general(qh[u], kc, contract_last, preferred_element_type=F32) for u in units}
    wn = tq + WINDOW
    ks = [pl.multiple_of(jnp.maximum(q0 + a * tq - WINDOW, 0), tq) for a in subs]
    kw = [kwin_ref[0, pl.ds(ks[a], wn), :] for a in subs]
    vw = [vwin_ref[0, pl.ds(ks[a], wn), :] for a in subs]
    s_win = {(a, g): lax.dot_general(qh[a, g], kw[a], contract_last, preferred_element_type=F32)
             for a, g in units}

    cmp_end = lax.broadcasted_iota(jnp.int32, (1, ncb), 1) * CMP_STRIDE + (CMP_LEN - 1)
    o_cmp = {}
    p_sum = [None for _ in subs]
    for a, g in units:
        valid = cmp_end <= t_col[a]
        s = jnp.where(valid, s_cmp[a, g], NEG)
        e = jnp.where(valid, jnp.exp2(s - jnp.max(s, axis=-1, keepdims=True)), 0.0)
        p = e / jnp.maximum(jnp.sum(e, axis=-1, keepdims=True), 1e-30)
        o_cmp[a, g] = jnp.dot(p.astype(BF16), vc, preferred_element_type=F32)
        p_sum[a] = p if p_sum[a] is None else p_sum[a] + p

    o_win = {}
    inside = []
    for a in subs:
        back = t_col[a] - (ks[a] + lax.broadcasted_iota(jnp.int32, (1, wn), 1))
        inside.append(pltpu.bitcast(back, jnp.uint32) < jnp.uint32(WINDOW))
    for a, g in units:
        sw = jnp.where(inside[a], s_win[a, g], NEG)
        pw = jnp.exp2(sw - jnp.max(sw, axis=-1, keepdims=True))
        accw = jnp.dot(pw.astype(BF16), vw[a], preferred_element_type=F32)
        o_win[a, g] = accw / accw[:, HEAD_DIM:HEAD_DIM + 1]

    ovt = ovt_ref[...]
    nblk = N_SEL_BLOCKS
    q_sel = {}
    for a in subs:
        t1 = _top_bits(p_sum[a])
        r1 = p_sum[a] - t1
        t2 = _top_bits(r1)
        p1, p2, p3 = t1.astype(BF16), t2.astype(BF16), (r1 - t2).astype(BF16)
        imp_t = (lax.dot_general(ovt, p1, contract_last, preferred_element_type=F32)
                 + lax.dot_general(ovt, p2, contract_last, preferred_element_type=F32)
                 + lax.dot_general(ovt, p3, contract_last, preferred_element_type=F32))

        blk = lax.broadcasted_iota(jnp.int32, (nblk, tq), 0)
        qblk = t_row[a] // SEL_LEN
        causal_blk = blk <= qblk
        forced = (blk == 0) | (blk == qblk) | (blk == qblk - 1)
        score_t = jnp.where(forced, FORCED_SCORE, jnp.where(causal_blk, imp_t, NEG))
        groups = [score_t[8 * r:8 * r + 8] for r in range(nblk // 8)]
        sub = lax.broadcasted_iota(jnp.int32, (8, tq), 0)
        later = [jnp.where(sub > s, 1.0, 0.0) for s in range(8)]
        counts = [jnp.zeros((8, tq), F32) for _ in groups]
        for b in range(nblk):
            row = score_t[b:b + 1]
            for r, grp in enumerate(groups):
                if 8 * r > b:
                    counts[r] = counts[r] + jnp.where(row >= grp, 1.0, 0.0)
                elif 8 * r + 7 <= b:
                    counts[r] = counts[r] + jnp.where(row > grp, 1.0, 0.0)
                else:
                    counts[r] = (counts[r] + jnp.where(row > grp, 1.0, 0.0)
                                 + jnp.where(row == grp, later[b - 8 * r], 0.0))
        keep_t = (jnp.concatenate(counts, axis=0) < float(SEL_TOPK)) & causal_blk
        bias_t = jnp.where(keep_t, 0.0, NEG)
        bias = jnp.concatenate([jnp.zeros((LANES - nblk, tq), F32), bias_t], axis=0).T.astype(BF16)
        for g in range(GQA):
            q_sel[a, g] = qh[a, g] + bias

    tk = SEL_TK
    m_ref[...] = jnp.full(m_ref.shape, NEG, F32)
    acc_ref[...] = jnp.zeros(acc_ref.shape, F32)

    def sel_tile(j, carry):
        k0 = pl.multiple_of(j * tk, tk)
        k = ksel_ref[0, pl.ds(k0, tk), :]
        v = vsel_ref[0, pl.ds(k0, tk), :]
        scores = {u: lax.dot_general(q_sel[u], k, contract_last, preferred_element_type=F32)
                  for u in units}
        kpos = k0 + lax.broadcasted_iota(jnp.int32, (1, tk), 1)
        for n, (a, g) in enumerate(units):
            rs = slice(n * tq, (n + 1) * tq)
            sc = jnp.where(kpos <= t_col[a], scores[a, g], NEG)
            m_old = m_ref[rs, :]
            m_new = jnp.maximum(m_old, jnp.max(sc, axis=-1, keepdims=True))
            p = jnp.exp2(sc - jnp.concatenate([m_new] * (tk // LANES), axis=1))
            acc_ref[rs, :] = (jnp.exp2(m_old - m_new) * acc_ref[rs, :]
                              + jnp.dot(p.astype(BF16), v, preferred_element_type=F32))
            m_ref[rs, :] = m_new
        return carry

    lax.fori_loop(0, q0 // tk + 1, sel_tile, 0)

    lane = lax.broadcasted_iota(jnp.int32, (tq, LANES), 1)
    low = lane < HEAD_DIM
    for a in subs:
        gates = gate_ref[0, a * tq:(a + 1) * tq, :]
        heads = []
        for g in range(GQA):
            n = a * GQA + g
            acc = acc_ref[n * tq:(n + 1) * tq, :]
            o_sel = acc / acc[:, HEAD_DIM:HEAD_DIM + 1]
            heads.append(gates[:, 3 * g:3 * g + 1] * o_cmp[a, g] + gates[:, 3 * g + 1:3 * g + 2] * o_sel
                         + gates[:, 3 * g + 2:3 * g + 3] * o_win[a, g])
        pair0 = jnp.where(low, heads[0], pltpu.roll(heads[1], HEAD_DIM, 1))
        pair1 = jnp.where(low, heads[2], pltpu.roll(heads[3], HEAD_DIM, 1))
        o_ref[0, a * tq:(a + 1) * tq, :] = jnp.concatenate([pair0, pair1], axis=1).astype(BF16)


def _overlap_matrix_t(n_cmp):
    n = np.arange(n_cmp)[None, :]
    s = np.arange(N_SEL_BLOCKS)[:, None]
    c_start = n * CMP_STRIDE
    s_start = s * SEL_LEN
    ov = (c_start < s_start + SEL_LEN) & (c_start + CMP_LEN > s_start)
    return jnp.asarray(ov.astype(np.float32), BF16)


def nsa_attention(q, ksel, vsel, kwin, vwin, kvc, gates):
    bsz, seq_len, _ = q.shape
    ncb = kvc.shape[3]
    tq = NSA_SUBS * NSA_TQ
    assert seq_len // SEL_LEN <= N_SEL_BLOCKS and seq_len % SEL_TK == 0 and SEL_TK % tq == 0
    assert seq_len >= NSA_TQ + WINDOW
    kv_spec = pl.BlockSpec((1, seq_len, LANES), lambda b, h, i: (b, 0, h))
    rows = GQA * tq
    return pl.pallas_call(
        _nsa_kernel,
        out_shape=jax.ShapeDtypeStruct((bsz, seq_len, NSA_WIDTH), BF16),
        grid=(bsz, N_KV_HEADS, seq_len // tq),
        in_specs=[pl.BlockSpec((1, tq, GQA * LANES), lambda b, h, i: (b, i, h)),
                  kv_spec, kv_spec, kv_spec, kv_spec,
                  pl.BlockSpec((1, 1, 1, ncb, LANES), lambda b, h, i: (0, b, h, 0, 0)),
                  pl.BlockSpec((1, 1, 1, ncb, LANES), lambda b, h, i: (1, b, h, 0, 0)),
                  pl.BlockSpec((1, tq, LANES), lambda b, h, i: (b, i, h)),
                  pl.BlockSpec((N_SEL_BLOCKS, ncb), lambda b, h, i: (0, 0))],
        out_specs=pl.BlockSpec((1, tq, GQA * HEAD_DIM), lambda b, h, i: (b, i, h)),
        scratch_shapes=[pltpu.VMEM((rows, LANES), F32), pltpu.VMEM((rows, LANES), F32)],
        compiler_params=_cparams("parallel", "parallel", "arbitrary"),
        name="nsa_attention",
    )(q, ksel, vsel, kwin, vwin, kvc, kvc, gates, _overlap_matrix_t(ncb))


def _conv_kernel(u_ref, w_ref, cb_ref, g_ref, b_ref, o_ref, pad_ref, *, rows):
    seq_len = u_ref.shape[1]
    halo = CONV_HALO
    pad_ref[0:halo, :] = jnp.zeros((halo, CONV_WIDTH), F32)
    pad_ref[halo:halo + seq_len, :] = u_ref[0]

    first = halo - CONV_LEN + 1

    def body(c, carry):
        r0 = pl.multiple_of(c * rows, rows)
        win = pad_ref[pl.ds(r0, rows + halo), :]
        acc = jnp.zeros((rows, CONV_WIDTH), F32)
        for sub in range(8):
            shifted = win if sub == 0 else pltpu.roll(win, rows + halo - sub, 0)
            for j in range(CONV_LEN):
                if (first + j) % 8 == sub:
                    a0 = first + j - sub
                    acc = acc + w_ref[j:j + 1, :] * shifted[a0:a0 + rows]
        y = _layer_norm(acc + cb_ref[...], g_ref[...], b_ref[...])
        o_ref[0, pl.ds(r0, rows), :] = (y * _sigmoid(y)).astype(BF16)
        return carry

    lax.fori_loop(0, seq_len // rows, body, 0)


def conformer_conv(u, conv_w, conv_b, ln_g, ln_b, rows=CONV_ROWS):
    bsz, seq_len, _ = u.shape
    vec = pl.BlockSpec((1, CONV_WIDTH), lambda b: (0, 0))
    return pl.pallas_call(
        functools.partial(_conv_kernel, rows=rows),
        out_shape=jax.ShapeDtypeStruct((bsz, seq_len, CONV_WIDTH), BF16),
        grid=(bsz,),
        in_specs=[pl.BlockSpec((1, seq_len, CONV_WIDTH), lambda b: (b, 0, 0)),
                  pl.BlockSpec((CONV_LEN, CONV_WIDTH), lambda b: (0, 0)), vec, vec, vec],
        out_specs=pl.BlockSpec((1, seq_len, CONV_WIDTH), lambda b: (b, 0, 0)),
        scratch_shapes=[pltpu.VMEM((seq_len + CONV_HALO, CONV_WIDTH), F32)],
        compiler_params=_cparams("parallel"),
        name="conformer_conv",
    )(u, conv_w, conv_b.reshape(1, -1), ln_g.reshape(1, -1), ln_b.reshape(1, -1))


def _s5_operators(a_re, a_im, log_dt, b_re, b_im, c_re, c_im, n_chunks):
    tc = S5_CHUNK
    dt = jnp.exp(log_dt.astype(F32))[:, None]
    lr = a_re.astype(F32) * dt
    li = a_im.astype(F32) * dt
    mag = jnp.exp(lr)
    ab_re = mag * jnp.cos(li)
    ab_im = mag * jnp.sin(li)
    den = a_re * a_re + a_im * a_im
    coef_re = ((ab_re - 1.0) * a_re + ab_im * a_im) / den
    coef_im = (ab_im * a_re - (ab_re - 1.0) * a_im) / den
    bb_re = coef_re[..., None] * b_re - coef_im[..., None] * b_im
    bb_im = coef_re[..., None] * b_im + coef_im[..., None] * b_re

    def power(tau):
        tau = jnp.asarray(tau, F32)
        m = jnp.exp(lr[..., None] * tau)
        return m * jnp.cos(li[..., None] * tau), m * jnp.sin(li[..., None] * tau)

    lag_re, lag_im = power(jnp.arange(tc + 1))
    ca_re = c_re[..., None] * lag_re[:, None] - c_im[..., None] * lag_im[:, None]
    ca_im = c_re[..., None] * lag_im[:, None] + c_im[..., None] * lag_re[:, None]
    kern = (jnp.einsum('gcnt,gnd->gtcd', ca_re[..., :tc], bb_re, precision=HIGHEST)
            - jnp.einsum('gcnt,gnd->gtcd', ca_im[..., :tc], bb_im, precision=HIGHEST))
    s_idx = np.arange(tc)[:, None]
    i_idx = np.arange(tc)[None, :]
    lag = np.clip(i_idx - s_idx, 0, tc - 1)
    toep = kern[:, lag]
    toep = jnp.where(jnp.asarray(i_idx >= s_idx)[None, :, :, None, None], toep, 0.0)
    rev_re, rev_im = lag_re[..., tc - 1::-1], lag_im[..., tc - 1::-1]
    bop_re = (rev_re[..., None] * bb_re[:, :, None] - rev_im[..., None] * bb_im[:, :, None])
    bop_im = (rev_re[..., None] * bb_im[:, :, None] + rev_im[..., None] * bb_re[:, :, None])
    bop_re = bop_re.transpose(0, 2, 3, 1)
    bop_im = bop_im.transpose(0, 2, 3, 1)
    cop_re = ca_re[..., 1:].transpose(0, 2, 3, 1)
    cop_im = -ca_im[..., 1:].transpose(0, 2, 3, 1)

    gh = S5_GROUPS // S5_HALVES
    eye = jnp.eye(gh, dtype=F32)
    hw = tc * gh * S5_GROUP_CH
    sw = gh * S5_STATE
    split = lambda x: x.reshape((S5_HALVES, gh) + x.shape[1:])
    m_nat = jnp.einsum('hgsiod,gk->hsgdiko', split(toep), eye).reshape(S5_HALVES, hw, hw)
    b_nat = jnp.concatenate(
        [jnp.einsum('hgsdn,gk->hsgdkn', split(bop), eye).reshape(S5_HALVES, hw, sw) for bop in (bop_re, bop_im)],
        axis=2)
    c_nat = jnp.concatenate(
        [jnp.einsum('hgnio,gk->hgniko', split(cop), eye).reshape(S5_HALVES, sw, hw) for cop in (cop_re, cop_im)],
        axis=1)
    levels = max(1, int(math.log2(n_chunks)))
    lv_re, lv_im = power(tc * (2.0 ** jnp.arange(levels)))
    lanes = lambda x: split(x).transpose(0, 3, 1, 2).reshape(S5_HALVES, levels, sw)
    a_lv = jnp.concatenate([lanes(lv_re), lanes(lv_im)], axis=-1)
    return m_nat.astype(BF16), b_nat.astype(BF16), c_nat.astype(BF16), a_lv


def _s5_kernel(u0_ref, u1_ref, m_ref, b_ref, c_ref, a_ref, d_ref, gw_ref, gb_ref, o0_ref, o1_ref,
               sre_ref, sim_ref):
    tc = S5_CHUNK
    nc = u0_ref.shape[1] // tc
    hl = S5_WIDTH // S5_HALVES
    sw = sre_ref.shape[1]
    u_step = [[u_ref[0, pl.ds(s, nc, stride=tc), :] for s in range(tc)] for u_ref in (u0_ref, u1_ref)]
    y_half = []
    for h in range(S5_HALVES):
        uh = jnp.concatenate(u_step[h], axis=1).astype(BF16)
        v = jnp.dot(uh, b_ref[h], preferred_element_type=F32)
        zeros = jnp.zeros((nc, sw), F32)
        sre_ref[0:nc, :] = zeros
        sim_ref[0:nc, :] = zeros
        sre_ref[nc:2 * nc, :] = v[:, 0:sw]
        sim_ref[nc:2 * nc, :] = v[:, sw:2 * sw]
        for lv in range(a_ref.shape[1]):
            d = 1 << lv
            ar = a_ref[h, lv:lv + 1, 0:sw]
            ai = a_ref[h, lv:lv + 1, sw:2 * sw]
            pr = sre_ref[nc - d:2 * nc - d, :]
            pi = sim_ref[nc - d:2 * nc - d, :]
            cr = sre_ref[nc:2 * nc, :]
            ci = sim_ref[nc:2 * nc, :]
            sre_ref[nc:2 * nc, :] = cr + ar * pr - ai * pi
            sim_ref[nc:2 * nc, :] = ci + ar * pi + ai * pr
        prev_re = sre_ref[nc - 1:2 * nc - 1, :].astype(BF16)
        prev_im = sim_ref[nc - 1:2 * nc - 1, :].astype(BF16)
        y_half.append(jnp.dot(uh, m_ref[h], preferred_element_type=F32)
                      + jnp.dot(prev_re, c_ref[h, 0:sw, :], preferred_element_type=F32)
                      + jnp.dot(prev_im, c_ref[h, sw:2 * sw, :], preferred_element_type=F32))
    for i in range(tc):
        y = jnp.concatenate([yh[:, i * hl:(i + 1) * hl] for yh in y_half], axis=1)
        z = _gelu_tanh(y + d_ref[...] * jnp.concatenate([u_step[0][i], u_step[1][i]], axis=1))
        gate = jnp.dot(z.astype(BF16), gw_ref[...], preferred_element_type=F32) + gb_ref[...]
        out = z * _sigmoid(gate)
        o0_ref[0, pl.ds(i, nc, stride=tc), :] = out[:, 0:hl]
        o1_ref[0, pl.ds(i, nc, stride=tc), :] = out[:, hl:2 * hl]


def s5_layer(zs, operators, d_skip, glu_w, glu_b):
    m_nat, b_nat, c_nat, a_lv = operators
    bsz, seq_len, _ = zs.shape
    tc = S5_CHUNK
    nc = seq_len // tc
    assert nc & (nc - 1) == 0 and a_lv.shape[1] == int(math.log2(nc))
    hl = S5_WIDTH // S5_HALVES
    sw = a_lv.shape[2] // 2
    full = lambda a: pl.BlockSpec(a.shape, lambda b: (0,) * a.ndim)
    vec = pl.BlockSpec((1, S5_WIDTH), lambda b: (0, 0))
    half = lambda h: pl.BlockSpec((1, seq_len, hl), lambda b: (b, 0, h))
    glu_wb = glu_w.astype(BF16)
    out = pl.pallas_call(
        _s5_kernel,
        out_shape=(jax.ShapeDtypeStruct((bsz, seq_len, hl), F32),) * S5_HALVES,
        grid=(bsz,),
        in_specs=[half(0), half(1), full(m_nat), full(b_nat), full(c_nat), full(a_lv), vec, full(glu_wb), vec],
        out_specs=(half(0),) * S5_HALVES,
        scratch_shapes=[pltpu.VMEM((2 * nc, sw), F32), pltpu.VMEM((2 * nc, sw), F32)],
        compiler_params=_cparams("parallel"),
        name="s5_layer",
    )(zs, zs, m_nat, b_nat, c_nat, a_lv, d_skip.reshape(1, -1), glu_wb, glu_b.reshape(1, -1))
    return tuple(o.reshape(bsz * seq_len, hl) for o in out)


def _route_top2(logits):
    lane = lax.broadcasted_iota(jnp.int32, logits.shape, 1)
    lane_f = lane.astype(F32)
    logits = jnp.where(lane < N_EXPERTS, logits, -jnp.inf)
    v1 = jnp.max(logits, axis=-1, keepdims=True)
    i1 = jnp.min(jnp.where(logits == v1, lane_f, 1e9), axis=-1, keepdims=True)
    rest = jnp.where(lane_f == i1, -jnp.inf, logits)
    v2 = jnp.max(rest, axis=-1, keepdims=True)
    i2 = jnp.min(jnp.where(rest == v2, lane_f, 1e9), axis=-1, keepdims=True)
    e2 = jnp.exp(v2 - v1)
    den = 1.0 + e2
    out = jnp.where(lane == 0, i1, jnp.where(lane == 1, i2, jnp.where(lane == 2, 1.0 / den, e2 / den)))
    return jnp.where(lane < 4, out, 0.0)


def _out_proj_kernel(x_ref, a_ref, c_ref, s0_ref, s1_ref, w_ref, g_ref, b_ref, *rest, alpha):
    o_ref = rest[0] if len(rest) == 1 else rest[1]
    o_s5 = jnp.concatenate([s0_ref[...], s1_ref[...]], axis=1).astype(BF16)
    h = jnp.dot(a_ref[...], w_ref[0:NSA_WIDTH, :], preferred_element_type=F32)
    h = h + jnp.dot(c_ref[...], w_ref[NSA_WIDTH:NSA_WIDTH + CONV_WIDTH, :], preferred_element_type=F32)
    h = h + jnp.dot(o_s5, w_ref[NSA_WIDTH + CONV_WIDTH:, :], preferred_element_type=F32)
    y = _layer_norm(alpha * x_ref[...] + h, g_ref[...], b_ref[...])
    o_ref[...] = y
    if len(rest) > 1:
        router_ref, _, routed_ref, ybf_ref = rest
        ybf_ref[...] = y.astype(BF16)
        y_top = _top_bits(y)
        y_hi = y_top.astype(BF16)
        y_lo = (y - y_top).astype(BF16)
        rows = y.shape[0]
        prod = jnp.dot(jnp.concatenate([y_hi, y_lo], axis=0), router_ref[...],
                       preferred_element_type=F32)
        logits = (prod[0:rows, 0:LANES] + prod[0:rows, LANES:2 * LANES]
                  + prod[rows:2 * rows, 0:LANES] + prod[rows:2 * rows, LANES:2 * LANES])
        routed_ref[...] = _route_top2(logits)


def out_proj_ln(x2, o_nsa, o_conv, o_s5, w_out, g, b, alpha, router=None, tm=ROW_TILE_SMALL):
    t, d = x2.shape
    row = lambda width: pl.BlockSpec((tm, width), lambda i: (i, 0))
    vec = pl.BlockSpec((1, d), lambda i: (0, 0))
    in_specs = [row(d), row(NSA_WIDTH), row(CONV_WIDTH), row(S5_WIDTH // S5_HALVES), row(S5_WIDTH // S5_HALVES),
                pl.BlockSpec(w_out.shape, lambda i: (0, 0)), vec, vec]
    args = [x2, o_nsa, o_conv, *o_s5, w_out.astype(BF16), g.reshape(1, -1), b.reshape(1, -1)]
    out_shape = jax.ShapeDtypeStruct((t, d), F32)
    out_specs = row(d)
    if router is not None:
        w = jnp.pad(router, ((0, 0), (0, LANES - router.shape[1])))
        w_top = lax.bitcast_convert_type(
            lax.bitcast_convert_type(w, jnp.uint32) & jnp.uint32(0xFFFF0000), F32)
        w_hi = w_top.astype(BF16)
        w_lo = (w - w_top).astype(BF16)
        in_specs.append(pl.BlockSpec((d, 2 * LANES), lambda i: (0, 0)))
        args.append(jnp.concatenate([w_hi, w_lo], axis=1))
        out_shape = (out_shape, jax.ShapeDtypeStruct((t, LANES), F32), jax.ShapeDtypeStruct((t, d), BF16))
        out_specs = (out_specs, row(LANES), row(d))
    return pl.pallas_call(
        functools.partial(_out_proj_kernel, alpha=alpha),
        out_shape=out_shape,
        grid=(t // tm,),
        in_specs=in_specs,
        out_specs=out_specs,
        compiler_params=_cparams("parallel"),
        name="out_proj_ln",
    )(*args)


def _ffn_kernel(te_ref, nu_ref, x_ref, w1_ref, w3_ref, w2_ref, *rest, alpha):
    ln_refs, (o_ref, acc_ref) = rest[:-2], rest[-2:]
    i = pl.program_id(0)
    f = pl.program_id(1)
    last = pl.num_programs(1) - 1
    used = i < nu_ref[0]

    @pl.when(used)
    def _():
        xb = x_ref[...].astype(BF16)
        h1 = jnp.dot(xb, w1_ref[0].astype(BF16), preferred_element_type=F32)
        h3 = jnp.dot(xb, w3_ref[0].astype(BF16), preferred_element_type=F32)
        h = (h1 * _sigmoid(h1) * h3).astype(BF16)
        part = jnp.dot(h, w2_ref[0].astype(BF16), preferred_element_type=F32)

        @pl.when(f == 0)
        def _():
            acc_ref[...] = part

        @pl.when(f > 0)
        def _():
            acc_ref[...] = acc_ref[...] + part

        @pl.when(f == last)
        def _():
            if ln_refs:
                o_ref[...] = _layer_norm(alpha * x_ref[...] + acc_ref[...], ln_refs[0][...], ln_refs[1][...])
            else:
                o_ref[...] = acc_ref[...].astype(o_ref.dtype)

    @pl.when(jnp.logical_not(used) & (f == last))
    def _():
        o_ref[...] = jnp.zeros(o_ref.shape, o_ref.dtype)


def grouped_swiglu(xs, tile_expert, n_used, w1, w3, w2, tm, tf, out_dtype, ln=None, alpha=1.0):
    p, d = xs.shape
    ff = w1.shape[2]
    assert p % tm == 0 and ff % tf == 0
    vec = pl.BlockSpec((1, d), lambda i, f, te, nu: (0, 0))
    ln_args = () if ln is None else (ln[0].reshape(1, d), ln[1].reshape(1, d))
    grid_spec = pltpu.PrefetchScalarGridSpec(
        num_scalar_prefetch=2,
        grid=(p // tm, ff // tf),
        in_specs=[pl.BlockSpec((tm, d), lambda i, f, te, nu: (i, 0)),
                  pl.BlockSpec((1, d, tf), lambda i, f, te, nu: (te[i], 0, f)),
                  pl.BlockSpec((1, d, tf), lambda i, f, te, nu: (te[i], 0, f)),
                  pl.BlockSpec((1, tf, d), lambda i, f, te, nu: (te[i], f, 0))] + [vec] * len(ln_args),
        out_specs=pl.BlockSpec((tm, d), lambda i, f, te, nu: (i, 0)),
        scratch_shapes=[pltpu.VMEM((tm, d), F32)],
    )
    return pl.pallas_call(
        functools.partial(_ffn_kernel, alpha=alpha),
        out_shape=jax.ShapeDtypeStruct((p, d), out_dtype),
        grid_spec=grid_spec,
        compiler_params=_cparams("arbitrary", "arbitrary"),
        name="grouped_swiglu",
    )(tile_expert, n_used, xs, w1, w3, w2, *ln_args)


def _pick_tf(ff, target=1024):
    best = LANES
    for cand in range(LANES, ff + 1, LANES):
        if ff % cand == 0 and cand <= target:
            best = cand
    return best


def _residual_ln_kernel(x_ref, a_ref, b_ref, wa_ref, wb_ref, g_ref, beta_ref, o_ref, *, alpha):
    f = wa_ref[...] * a_ref[...] + wb_ref[...] * b_ref[...]
    o_ref[...] = _layer_norm(alpha * x_ref[...] + f, g_ref[...], beta_ref[...])


def residual_ln(x2, ab, wa, wb, g, beta, alpha, tm=ROW_TILE):
    t, d = x2.shape
    row = pl.BlockSpec((tm, d), lambda i: (i, 0))
    row_b = pl.BlockSpec((tm, d), lambda i: (i + t // tm, 0))
    col = pl.BlockSpec((tm, 1), lambda i: (i, 0))
    vec = pl.BlockSpec((1, d), lambda i: (0, 0))
    return pl.pallas_call(
        functools.partial(_residual_ln_kernel, alpha=alpha),
        out_shape=jax.ShapeDtypeStruct((t, d), F32),
        grid=(t // tm,),
        in_specs=[row, row, row_b, col, col, vec, vec],
        out_specs=row,
        compiler_params=_cparams("parallel"),
        name="residual_ln",
    )(x2, ab, ab, wa, wb, g.reshape(1, -1), beta.reshape(1, -1))


def moe_swiglu_ln(x2, x2_bf16, routed, w1, w3, w2, first_expert, g, beta, alpha, tm):
    t, d = x2.shape
    n_exp = N_EXPERTS
    idx = routed[:, 0:2].astype(jnp.int32)
    wts = routed[:, 2:4]
    e_flat = idx.T.reshape(-1)
    onehot = (e_flat[:, None] == jnp.arange(n_exp, dtype=jnp.int32)[None, :]).astype(jnp.int32)
    running = jnp.cumsum(onehot, axis=0)
    counts = running[-1]
    rank = jnp.sum(onehot * running, axis=1) - 1
    padded = ((counts + tm - 1) // tm) * tm
    pad_end = jnp.cumsum(padded)
    pad_start = pad_end - padded
    start = jnp.cumsum(counts) - counts
    pos = pad_start[e_flat] + rank
    n_rows = 2 * t + n_exp * tm
    n_tiles = n_rows // tm
    tile_start = jnp.arange(n_tiles, dtype=jnp.int32) * tm
    tile_expert = jnp.minimum(jnp.sum(tile_start[:, None] >= pad_end[None, :], axis=1), n_exp - 1).astype(jnp.int32)
    n_used = (pad_end[-1] // tm).astype(jnp.int32).reshape(1)
    order = jnp.argsort(e_flat, stable=True)
    row = jnp.arange(n_rows, dtype=jnp.int32)
    row_expert = jnp.repeat(tile_expert, tm)
    slot = jnp.minimum(start[row_expert] + (row - pad_start[row_expert]), 2 * t - 1)
    src = order[slot] % t
    xs = x2_bf16[src]
    ys = grouped_swiglu(xs, tile_expert + first_expert, n_used, w1, w3, w2, tm, _pick_tf(w1.shape[2]), BF16)
    return residual_ln(x2, ys[pos], wts[:, 0:1], wts[:, 1:2], g, beta, alpha)


def dense_swiglu_ln(x2, w1, w3, w2, layer, g, beta, alpha, tm):
    t = x2.shape[0]
    tile_expert = jnp.full((t // tm,), layer, jnp.int32)
    n_used = jnp.full((1,), t // tm, jnp.int32)
    return grouped_swiglu(x2, tile_expert, n_used, w1, w3, w2, tm, _pick_tf(w1.shape[2], 1408),
                          F32, ln=(g, beta), alpha=alpha)


def _ple_kernel(x_ref, p_ref, wg_ref, bg_ref, wp_ref, g_ref, beta_ref, o_ref, *, alpha):
    x = x_ref[...]
    gate = _sigmoid(jnp.dot(x.astype(BF16), wg_ref[...], preferred_element_type=F32) + bg_ref[...])
    e = jnp.dot(p_ref[...].astype(BF16), wp_ref[...], preferred_element_type=F32) * gate
    o_ref[...] = _layer_norm(alpha * x + e, g_ref[...], beta_ref[...])


def ple_ln(x2, p_all, layer, gate_w, gate_b, proj, g, beta, alpha, tm=ROW_TILE):
    t, d = x2.shape
    pd = p_all.shape[1]
    first = layer * (t // tm)
    vec = pl.BlockSpec((1, d), lambda i: (0, 0))
    return pl.pallas_call(
        functools.partial(_ple_kernel, alpha=alpha),
        out_shape=jax.ShapeDtypeStruct((t, d), F32),
        grid=(t // tm,),
        in_specs=[pl.BlockSpec((tm, d), lambda i: (i, 0)), pl.BlockSpec((tm, pd), lambda i: (first + i, 0)),
                  pl.BlockSpec((d, d), lambda i: (0, 0)), vec,
                  pl.BlockSpec((pd, d), lambda i: (0, 0)), vec, vec],
        out_specs=pl.BlockSpec((tm, d), lambda i: (i, 0)),
        compiler_params=_cparams("parallel"),
        name="ple_ln",
    )(x2, p_all, gate_w.astype(BF16), gate_b.reshape(1, -1), proj.astype(BF16), g.reshape(1, -1), beta.reshape(1, -1))


def hybrid_mixer_ln(x2, bsz, seq_len, tables, w_in, w_out, cmp_weights, conv_params, s5_params,
                    g, beta, alpha, router=None):
    (q, ksel, kwin, vsel, vwin, kv_cmp, gates, u_conv, zs) = in_proj(x2, _widen_w_in(w_in), tables, seq_len)
    b3 = lambda a: a.reshape(bsz, seq_len, a.shape[1])
    kvc = compress(kv_cmp.reshape(2, bsz, seq_len, LANES), _compress_weights(*cmp_weights))
    o_nsa = nsa_attention(b3(q), b3(ksel), b3(vsel), b3(kwin), b3(vwin), kvc, b3(gates))
    o_conv = conformer_conv(b3(u_conv), *conv_params)
    (a_re, a_im, log_dt, b_re, b_im, c_re, c_im, d_skip, glu_w, glu_b) = s5_params
    ops = _s5_operators(a_re, a_im, log_dt, b_re, b_im, c_re, c_im, seq_len // S5_CHUNK)
    o_s5 = s5_layer(b3(zs), ops, d_skip, glu_w, glu_b)
    return out_proj_ln(x2, o_nsa.reshape(-1, NSA_WIDTH), o_conv.reshape(-1, CONV_WIDTH), o_s5, w_out,
                       g, beta, alpha, router)


def kernel(x, p, positions, w_in, w_out, cmp_pe, cmp_w1, cmp_b1, cmp_w2, cmp_b2, conv_w, conv_b, conv_ln_g, conv_ln_b, s5_a_re, s5_a_im, s5_log_dt, s5_b_re, s5_b_im, s5_c_re, s5_c_im, s5_d, s5_glu_w, s5_glu_b, ffn_w1, ffn_w3, ffn_w2, moe_router, moe_w1, moe_w3, moe_w2, ple_gate_w, ple_gate_b, ple_proj, ln_g, ln_b):
    bsz, seq_len, d_model = x.shape
    depth = w_in.shape[0]
    alpha = (2 * depth) ** 0.25
    t = bsz * seq_len
    tables = rope_tables(positions)
    x2 = x.reshape(t, d_model)
    dense_tm = ROW_TILE_SMALL
    moe_tm = ROW_TILE if t % ROW_TILE == 0 else ROW_TILE_SMALL
    stack = lambda w: w.reshape((-1,) + w.shape[-2:])
    ffn_w = tuple(stack(w).astype(BF16) for w in (ffn_w1, ffn_w3, ffn_w2))
    moe_w = tuple(stack(w) for w in (moe_w1, moe_w3, moe_w2))
    for i in range(depth):
        j = i // 2
        routed_layer = i % 2 == 1
        mixed = hybrid_mixer_ln(
            x2, bsz, seq_len, tables, w_in[i], w_out[i],
            (cmp_pe[i], cmp_w1[i], cmp_b1[i], cmp_w2[i], cmp_b2[i]),
            (conv_w[i], conv_b[i], conv_ln_g[i], conv_ln_b[i]),
            (s5_a_re[i], s5_a_im[i], s5_log_dt[i], s5_b_re[i], s5_b_im[i], s5_c_re[i], s5_c_im[i],
             s5_d[i], s5_glu_w[i], s5_glu_b[i]),
            ln_g[i, 0], ln_b[i, 0], alpha, moe_router[j] if routed_layer else None)
        if routed_layer:
            x2, routed, x2_bf16 = mixed
            x2 = moe_swiglu_ln(x2, x2_bf16, routed, *moe_w, j * N_EXPERTS, ln_g[i, 1], ln_b[i, 1], alpha, moe_tm)
        else:
            x2 = dense_swiglu_ln(mixed, *ffn_w, j, ln_g[i, 1], ln_b[i, 1], alpha, dense_tm)
        x2 = ple_ln(x2, p.reshape(depth * t, -1), i, ple_gate_w[i], ple_gate_b[i], ple_proj[i],
                    ln_g[i, 2], ln_b[i, 2], alpha)
    return x2.reshape(bsz, seq_len, d_model)
```

```python
import functools
import math

import numpy as np
import jax
import jax.numpy as jnp
from jax import lax
from jax.experimental import pallas as pl
from jax.experimental.pallas import tpu as pltpu

F32 = jnp.float32
BF16 = jnp.bfloat16
HIGHEST = lax.Precision.HIGHEST

LANES = 128
VMEM_LIMIT = 56 * 1024 * 1024

ROW_TILE = 1024
ROW_TILE_SMALL = 512
CONV_ROWS = 256
CONV_HALO = 32

HEAD_DIM = 64
HALF_DIM = HEAD_DIM // 2
N_Q_HEADS = 8
N_KV_HEADS = 2
GQA = N_Q_HEADS // N_KV_HEADS
CMP_LEN = 32
CMP_STRIDE = 16
CMP_HIDDEN = 256
SEL_LEN = 64
SEL_TOPK = 16
WINDOW = 512
ROPE_THETA = 10000.0
FORCED_SCORE = 1e9
NEG = -1e30
CONV_WIDTH = 256
CONV_LEN = 31
S5_WIDTH = 256
S5_GROUP_CH = 16
S5_GROUPS = 16
S5_STATE = 64
S5_CHUNK = 8
S5_HALVES = 2
N_EXPERTS = 8
LN_EPS = 1e-5
NSA_WIDTH = N_Q_HEADS * HEAD_DIM

C_Q = 0
C_KV = C_Q + NSA_WIDTH
C_GATE = C_KV + 6 * N_KV_HEADS * HEAD_DIM
C_CONVA = C_GATE + N_KV_HEADS * LANES
C_CONVB = C_CONVA + CONV_WIDTH
C_S5 = C_CONVB + CONV_WIDTH
C_TOTAL = C_S5 + S5_WIDTH
Q_SCALE = HEAD_DIM ** -0.5 * math.log2(math.e)


def _cparams(*sem):
    return pltpu.CompilerParams(dimension_semantics=sem, vmem_limit_bytes=VMEM_LIMIT)


def _layer_norm(v, g, b):
    mu = jnp.mean(v, axis=-1, keepdims=True)
    d = v - mu
    var = jnp.mean(d * d, axis=-1, keepdims=True)
    return d * lax.rsqrt(var + LN_EPS) * g + b


def _gelu_tanh(x):
    return 0.5 * x * (1.0 + jnp.tanh(math.sqrt(2.0 / math.pi) * (x + 0.044715 * (x * x * x))))


def _sigmoid(x):
    return 1.0 / (1.0 + jnp.exp(-x))


def _top_bits(x):
    return pltpu.bitcast(pltpu.bitcast(x, jnp.uint32) & jnp.uint32(0xFFFF0000), F32)


def _rope_table_kernel(pos_ref, freq_ref, tab_ref):
    ang = pos_ref[...] * freq_ref[...]
    c = jnp.cos(ang)
    s = jnp.sin(ang)
    lane = lax.broadcasted_iota(jnp.int32, ang.shape, 1)
    first_half = (lane % HEAD_DIM) < HALF_DIM
    tab_ref[:, 0:LANES] = c
    tab_ref[:, LANES:2 * LANES] = jnp.where(first_half, -s, 0.0)
    tab_ref[:, 2 * LANES:3 * LANES] = jnp.where(first_half, 0.0, s)


def rope_tables(positions, tm=ROW_TILE_SMALL):
    t = positions.size
    pos = positions.reshape(t, 1).astype(F32)
    inv_freq = ROPE_THETA ** (-jnp.arange(0, HEAD_DIM, 2, dtype=F32) / HEAD_DIM)
    freq = jnp.tile(inv_freq, LANES // HALF_DIM).reshape(1, LANES)
    return pl.pallas_call(
        _rope_table_kernel,
        out_shape=jax.ShapeDtypeStruct((t, 3 * LANES), F32),
        grid=(t // tm,),
        in_specs=[pl.BlockSpec((tm, 1), lambda i: (i, 0)),
                  pl.BlockSpec((1, LANES), lambda i: (0, 0))],
        out_specs=pl.BlockSpec((tm, 3 * LANES), lambda i: (i, 0)),
        compiler_params=_cparams("parallel"),
        name="rope_tables",
    )(pos, freq)


def _in_proj_kernel(x_ref, w_ref, tab_ref, q_ref, ksel_ref, kwin_ref, vsel_ref, vwin_ref,
                    kvcmp_ref, gate_ref, u_ref, zs_ref, *, seq_len):
    tm = x_ref.shape[0]
    xb = x_ref[...].astype(BF16)
    cos = tab_ref[:, 0:LANES]
    sin_a = tab_ref[:, LANES:2 * LANES]
    sin_b = tab_ref[:, 2 * LANES:3 * LANES]

    def proj(c0):
        z = jnp.dot(xb, w_ref[:, c0:c0 + 2 * LANES], preferred_element_type=F32)
        return z[:, 0:LANES], z[:, LANES:2 * LANES]

    def rope(z):
        return (z * cos + pltpu.roll(z, LANES - HALF_DIM, 1) * sin_a
                + pltpu.roll(z, HALF_DIM, 1) * sin_b)

    lane = lax.broadcasted_iota(jnp.int32, (tm, LANES), 1)
    row = lax.broadcasted_iota(jnp.int32, (tm, LANES), 0)
    low = lane < HEAD_DIM
    t_seq = (pl.program_id(0) * tm) % seq_len + row
    blk_onehot = jnp.where(lane == HEAD_DIM + t_seq // SEL_LEN, 1.0, 0.0)
    ones_lane = jnp.where(lane == HEAD_DIM, 1.0, 0.0)

    def spread(z, extra, out_ref, c0):
        out_ref[:, c0:c0 + LANES] = (jnp.where(low, z, 0.0) + extra).astype(BF16)
        out_ref[:, c0 + LANES:c0 + 2 * LANES] = (jnp.where(low, pltpu.roll(z, HEAD_DIM, 1), 0.0) + extra).astype(BF16)

    for pair in range(N_Q_HEADS // 4):
        for n, z in enumerate(proj(C_Q + pair * 2 * LANES)):
            spread(rope(z) * Q_SCALE, 0.0, q_ref, (2 * pair + n) * 2 * LANES)
    k_cmp, v_cmp = proj(C_KV)
    kvcmp_ref[0] = rope(k_cmp)
    kvcmp_ref[1] = v_cmp
    k_sel, v_sel = proj(C_KV + 2 * LANES)
    spread(rope(k_sel), blk_onehot, ksel_ref, 0)
    spread(v_sel, ones_lane, vsel_ref, 0)
    k_win, v_win = proj(C_KV + 4 * LANES)
    spread(rope(k_win), 0.0, kwin_ref, 0)
    spread(v_win, ones_lane, vwin_ref, 0)
    g0, g1 = proj(C_GATE)
    gate_ref[:, 0:LANES] = _sigmoid(g0)
    gate_ref[:, LANES:2 * LANES] = _sigmoid(g1)
    a0, a1 = proj(C_CONVA)
    b0, b1 = proj(C_CONVB)
    u_ref[:, 0:LANES] = a0 * _sigmoid(b0)
    u_ref[:, LANES:2 * LANES] = a1 * _sigmoid(b1)
    s0, s1 = proj(C_S5)
    zs_ref[:, 0:LANES] = s0
    zs_ref[:, LANES:2 * LANES] = s1


def _widen_w_in(w_in):
    d = w_in.shape[0]
    o_gate = NSA_WIDTH + 6 * N_KV_HEADS * HEAD_DIM
    o_conv = o_gate + 3 * N_Q_HEADS
    per_group = 3 * GQA
    cols = [w_in[:, :o_gate]]
    for h in range(N_KV_HEADS):
        cols += [w_in[:, o_gate + h * per_group:o_gate + (h + 1) * per_group],
                 jnp.zeros((d, LANES - per_group), w_in.dtype)]
    cols += [w_in[:, o_conv:]]
    w = jnp.concatenate(cols, axis=1)
    assert w.shape[1] == C_TOTAL
    return w.astype(BF16)


def in_proj(x2, w_wide, tables, seq_len, tm=ROW_TILE):
    t, d = x2.shape
    row = lambda width: pl.BlockSpec((tm, width), lambda i: (i, 0))
    out_shape = (
        jax.ShapeDtypeStruct((t, N_Q_HEADS * LANES), BF16),
        jax.ShapeDtypeStruct((t, N_KV_HEADS * LANES), BF16),
        jax.ShapeDtypeStruct((t, N_KV_HEADS * LANES), BF16),
        jax.ShapeDtypeStruct((t, N_KV_HEADS * LANES), BF16),
        jax.ShapeDtypeStruct((t, N_KV_HEADS * LANES), BF16),
        jax.ShapeDtypeStruct((2, t, LANES), F32),
        jax.ShapeDtypeStruct((t, 2 * LANES), F32),
        jax.ShapeDtypeStruct((t, CONV_WIDTH), F32),
        jax.ShapeDtypeStruct((t, S5_WIDTH), F32),
    )
    out_spec = lambda s: (row(s.shape[1]) if len(s.shape) == 2
                          else pl.BlockSpec((s.shape[0], tm, s.shape[2]), lambda i: (0, i, 0)))
    return pl.pallas_call(
        functools.partial(_in_proj_kernel, seq_len=seq_len),
        out_shape=out_shape,
        grid=(t // tm,),
        in_specs=[row(d),
                  pl.BlockSpec((d, C_TOTAL), lambda i: (0, 0)),
                  row(3 * LANES)],
        out_specs=tuple(out_spec(s) for s in out_shape),
        compiler_params=_cparams("parallel"),
        name="in_proj",
    )(x2, w_wide, tables)


def _compress_kernel(x_ref, pe_ref, w1_ref, b1_ref, w2_ref, b2_ref, o_ref, shift_ref):
    nb = x_ref.shape[2] // CMP_STRIDE
    x = jnp.concatenate([x_ref[0, 0, pl.ds(s, nb, stride=CMP_STRIDE), :] for s in range(CMP_STRIDE)],
                        axis=1)
    xa = (x + pe_ref[0, 0:1, :]).astype(BF16)
    xb = (x + pe_ref[0, 1:2, :]).astype(BF16)
    ha = jnp.dot(xa, w1_ref[0, 0], preferred_element_type=F32)
    hb = jnp.dot(xb, w1_ref[0, 1], preferred_element_type=F32)
    shift_ref[0:nb, :] = hb
    shift_ref[nb:nb + 8, :] = jnp.zeros((8, hb.shape[1]), F32)
    h = ha + shift_ref[1:nb + 1, :] + b1_ref[0]
    g = _gelu_tanh(h).astype(BF16)
    for hd in range(N_KV_HEADS):
        gh = g[:, hd * CMP_HIDDEN:(hd + 1) * CMP_HIDDEN]
        o_ref[0, 0, hd] = (jnp.dot(gh, w2_ref[0], preferred_element_type=F32) + b2_ref[0]).astype(BF16)


def _compress_weights(cmp_pe, cmp_w1, cmp_b1, cmp_w2, cmp_b2):
    half = CMP_LEN // 2
    pe = cmp_pe.reshape(2, 2, half, 1, HEAD_DIM)
    pe = jnp.broadcast_to(pe, (2, 2, half, N_KV_HEADS, HEAD_DIM)).reshape(2, 2, half * N_KV_HEADS * HEAD_DIM)
    w1 = cmp_w1.reshape(2, 2, half, HEAD_DIM, CMP_HIDDEN)
    z = jnp.zeros_like(w1)
    w_h0 = jnp.stack([w1, z], axis=3)
    w_h1 = jnp.stack([z, w1], axis=3)
    w1w = jnp.concatenate([w_h0, w_h1], axis=-1)
    w1w = w1w.reshape(2, 2, half * N_KV_HEADS * HEAD_DIM, N_KV_HEADS * CMP_HIDDEN).astype(BF16)
    b1 = jnp.tile(cmp_b1, (1, N_KV_HEADS)).reshape(2, 1, N_KV_HEADS * CMP_HIDDEN)
    w2 = jnp.pad(cmp_w2, ((0, 0), (0, 0), (0, LANES - HEAD_DIM))).astype(BF16)
    b2 = jnp.pad(cmp_b2, ((0, 0), (0, LANES - HEAD_DIM))).reshape(2, 1, LANES)
    return pe, w1w, b1, w2, b2


def compress(kv_cmp, weights):
    pe, w1w, b1, w2, b2 = weights
    _, bsz, seq_len, _ = kv_cmp.shape
    nb = seq_len // CMP_STRIDE
    cw = CMP_STRIDE * LANES
    hid = N_KV_HEADS * CMP_HIDDEN
    return pl.pallas_call(
        _compress_kernel,
        out_shape=jax.ShapeDtypeStruct((2, bsz, N_KV_HEADS, nb, LANES), BF16),
        grid=(2, bsz),
        in_specs=[pl.BlockSpec((1, 1, seq_len, LANES), lambda j, b: (j, b, 0, 0)),
                  pl.BlockSpec((1, 2, cw), lambda j, b: (j, 0, 0)),
                  pl.BlockSpec((1, 2, cw, hid), lambda j, b: (j, 0, 0, 0)),
                  pl.BlockSpec((1, 1, hid), lambda j, b: (j, 0, 0)),
                  pl.BlockSpec((1, CMP_HIDDEN, LANES), lambda j, b: (j, 0, 0)),
                  pl.BlockSpec((1, 1, LANES), lambda j, b: (j, 0, 0))],
        out_specs=pl.BlockSpec((1, 1, N_KV_HEADS, nb, LANES), lambda j, b: (j, b, 0, 0, 0)),
        scratch_shapes=[pltpu.VMEM((nb + 8, hid), F32)],
        compiler_params=_cparams("parallel", "parallel"),
        name="compress",
    )(kv_cmp, pe, w1w, b1, w2, b2)


SEL_TK = 512
NSA_TQ = 128
NSA_SUBS = 4
N_SEL_BLOCKS = LANES - HEAD_DIM


def _nsa_kernel(q_ref, ksel_ref, vsel_ref, kwin_ref, vwin_ref, kc_ref, vc_ref, gate_ref, ovt_ref,
                o_ref, m_ref, acc_ref):
    tq = NSA_TQ
    subs = range(q_ref.shape[1] // tq)
    units = [(a, g) for a in subs for g in range(GQA)]
    i = pl.program_id(2)
    q0 = i * q_ref.shape[1]
    qh = {(a, g): q_ref[0, a * tq:(a + 1) * tq, g * LANES:(g + 1) * LANES] for a, g in units}
    t_col = [q0 + a * tq + lax.broadcasted_iota(jnp.int32, (tq, 1), 0) for a in subs]
    t_row = [q0 + a * tq + lax.broadcasted_iota(jnp.int32, (1, tq), 1) for a in subs]
    contract_last = (((1,), (1,)), ((), ()))

    ncb = kc_ref.shape[3]
    kc = kc_ref[0, 0, 0]
    vc = vc_ref[0, 0, 0]
    s_cmp = {u: lax.dot_general(qh[u], kc, contract_last, preferred_element_type=F32) for u in units}
    wn = tq + WINDOW
    ks = [pl.multiple_of(jnp.maximum(q0 + a * tq - WINDOW, 0), tq) for a in subs]
    kw = [kwin_ref[0, pl.ds(ks[a], wn), :] for a in subs]
    vw = [vwin_ref[0, pl.ds(ks[a], wn), :] for a in subs]
    s_win = {(a, g): lax.dot_general(qh[a, g], kw[a], contract_last, preferred_element_type=F32)
             for a, g in units}

    cmp_end = lax.broadcasted_iota(jnp.int32, (1, ncb), 1) * CMP_STRIDE + (CMP_LEN - 1)
    o_cmp = {}
    p_sum = [None for _ in subs]
    for a, g in units:
        valid = cmp_end <= t_col[a]
        s = jnp.where(valid, s_cmp[a, g], NEG)
        e = jnp.where(valid, jnp.exp2(s - jnp.max(s, axis=-1, keepdims=True)), 0.0)
        p = e / jnp.maximum(jnp.sum(e, axis=-1, keepdims=True), 1e-30)
        o_cmp[a, g] = jnp.dot(p.astype(BF16), vc, preferred_element_type=F32)
        p_sum[a] = p if p_sum[a] is None else p_sum[a] + p

    o_win = {}
    inside = []
    for a in subs:
        back = t_col[a] - (ks[a] + lax.broadcasted_iota(jnp.int32, (1, wn), 1))
        inside.append(pltpu.bitcast(back, jnp.uint32) < jnp.uint32(WINDOW))
    for a, g in units:
        sw = jnp.where(inside[a], s_win[a, g], NEG)
        pw = jnp.exp2(sw - jnp.max(sw, axis=-1, keepdims=True))
        accw = jnp.dot(pw.astype(BF16), vw[a], preferred_element_type=F32)
        o_win[a, g] = accw / accw[:, HEAD_DIM:HEAD_DIM + 1]

    ovt = ovt_ref[...]
    nblk = N_SEL_BLOCKS
    q_sel = {}
    for a in subs:
        t1 = _top_bits(p_sum[a])
        r1 = p_sum[a] - t1
        t2 = _top_bits(r1)
        p1, p2, p3 = t1.astype(BF16), t2.astype(BF16), (r1 - t2).astype(BF16)
        imp_t = (lax.dot_general(ovt, p1, contract_last, preferred_element_type=F32)
                 + lax.dot_general(ovt, p2, contract_last, preferred_element_type=F32)
                 + lax.dot_general(ovt, p3, contract_last, preferred_element_type=F32))

        blk = lax.broadcasted_iota(jnp.int32, (nblk, tq), 0)
        qblk = t_row[a] // SEL_LEN
        causal_blk = blk <= qblk
        forced = (blk == 0) | (blk == qblk) | (blk == qblk - 1)
        score_t = jnp.where(forced, FORCED_SCORE, jnp.where(causal_blk, imp_t, NEG))
        groups = [score_t[8 * r:8 * r + 8] for r in range(nblk // 8)]
        sub = lax.broadcasted_iota(jnp.int32, (8, tq), 0)
        later = [jnp.where(sub > s, 1.0, 0.0) for s in range(8)]
        counts = [jnp.zeros((8, tq), F32) for _ in groups]
        for b in range(nblk):
            row = score_t[b:b + 1]
            for r, grp in enumerate(groups):
                if 8 * r > b:
                    counts[r] = counts[r] + jnp.where(row >= grp, 1.0, 0.0)
                elif 8 * r + 7 <= b:
                    counts[r] = counts[r] + jnp.where(row > grp, 1.0, 0.0)
                else:
                    counts[r] = (counts[r] + jnp.where(row > grp, 1.0, 0.0)
                                 + jnp.where(row == grp, later[b - 8 * r], 0.0))
        keep_t = (jnp.concatenate(counts, axis=0) < float(SEL_TOPK)) & causal_blk
        bias_t = jnp.where(keep_t, 0.0, NEG)
        bias = jnp.concatenate([jnp.zeros((LANES - nblk, tq), F32), bias_t], axis=0).T.astype(BF16)
        for g in range(GQA):
            q_sel[a, g] = qh[a, g] + bias

    tk = SEL_TK
    m_ref[...] = jnp.full(m_ref.shape, NEG, F32)
    acc_ref[...] = jnp.zeros(acc_ref.shape, F32)

    def sel_tile(j, carry):
        k0 = pl.multiple_of(j * tk, tk)
        k = ksel_ref[0, pl.ds(k0, tk), :]
        v = vsel_ref[0, pl.ds(k0, tk), :]
        scores = {u: lax.dot_general(q_sel[u], k, contract_last, preferred_element_type=F32)
                  for u in units}
        kpos = k0 + lax.broadcasted_iota(jnp.int32, (1, tk), 1)
        for n, (a, g) in enumerate(units):
            rs = slice(n * tq, (n + 1) * tq)
            sc = jnp.where(kpos <= t_col[a], scores[a, g], NEG)
            m_old = m_ref[rs, :]
            m_new = jnp.maximum(m_old, jnp.max(sc, axis=-1, keepdims=True))
            p = jnp.exp2(sc - jnp.concatenate([m_new] * (tk // LANES), axis=1))
            acc_ref[rs, :] = (jnp.exp2(m_old - m_new) * acc_ref[rs, :]
                              + jnp.dot(p.astype(BF16), v, preferred_element_type=F32))
            m_ref[rs, :] = m_new
        return carry

    lax.fori_loop(0, q0 // tk + 1, sel_tile, 0)

    lane = lax.broadcasted_iota(jnp.int32, (tq, LANES), 1)
    low = lane < HEAD_DIM
    for a in subs:
        gates = gate_ref[0, a * tq:(a + 1) * tq, :]
        heads = []
        for g in range(GQA):
            n = a * GQA + g
            acc = acc_ref[n * tq:(n + 1) * tq, :]
            o_sel = acc / acc[:, HEAD_DIM:HEAD_DIM + 1]
            heads.append(gates[:, 3 * g:3 * g + 1] * o_cmp[a, g] + gates[:, 3 * g + 1:3 * g + 2] * o_sel
                         + gates[:, 3 * g + 2:3 * g + 3] * o_win[a, g])
        pair0 = jnp.where(low, heads[0], pltpu.roll(heads[1], HEAD_DIM, 1))
        pair1 = jnp.where(low, heads[2], pltpu.roll(heads[3], HEAD_DIM, 1))
        o_ref[0, a * tq:(a + 1) * tq, :] = jnp.concatenate([pair0, pair1], axis=1).astype(BF16)


def _overlap_matrix_t(n_cmp):
    n = np.arange(n_cmp)[None, :]
    s = np.arange(N_SEL_BLOCKS)[:, None]
    c_start = n * CMP_STRIDE
    s_start = s * SEL_LEN
    ov = (c_start < s_start + SEL_LEN) & (c_start + CMP_LEN > s_start)
    return jnp.asarray(ov.astype(np.float32), BF16)


def nsa_attention(q, ksel, vsel, kwin, vwin, kvc, gates):
    bsz, seq_len, _ = q.shape
    ncb = kvc.shape[3]
    tq = NSA_SUBS * NSA_TQ
    assert seq_len // SEL_LEN <= N_SEL_BLOCKS and seq_len % SEL_TK == 0 and SEL_TK % tq == 0
    assert seq_len >= NSA_TQ + WINDOW
    kv_spec = pl.BlockSpec((1, seq_len, LANES), lambda b, h, i: (b, 0, h))
    rows = GQA * tq
    return pl.pallas_call(
        _nsa_kernel,
        out_shape=jax.ShapeDtypeStruct((bsz, seq_len, NSA_WIDTH), BF16),
        grid=(bsz, N_KV_HEADS, seq_len // tq),
        in_specs=[pl.BlockSpec((1, tq, GQA * LANES), lambda b, h, i: (b, i, h)),
                  kv_spec, kv_spec, kv_spec, kv_spec,
                  pl.BlockSpec((1, 1, 1, ncb, LANES), lambda b, h, i: (0, b, h, 0, 0)),
                  pl.BlockSpec((1, 1, 1, ncb, LANES), lambda b, h, i: (1, b, h, 0, 0)),
                  pl.BlockSpec((1, tq, LANES), lambda b, h, i: (b, i, h)),
                  pl.BlockSpec((N_SEL_BLOCKS, ncb), lambda b, h, i: (0, 0))],
        out_specs=pl.BlockSpec((1, tq, GQA * HEAD_DIM), lambda b, h, i: (b, i, h)),
        scratch_shapes=[pltpu.VMEM((rows, LANES), F32), pltpu.VMEM((rows, LANES), F32)],
        compiler_params=_cparams("parallel", "parallel", "arbitrary"),
        name="nsa_attention",
    )(q, ksel, vsel, kwin, vwin, kvc, kvc, gates, _overlap_matrix_t(ncb))


def _conv_kernel(u_ref, w_ref, cb_ref, g_ref, b_ref, o_ref, pad_ref, *, rows):
    seq_len = u_ref.shape[1]
    halo = CONV_HALO
    pad_ref[0:halo, :] = jnp.zeros((halo, CONV_WIDTH), F32)
    pad_ref[halo:halo + seq_len, :] = u_ref[0]

    first = halo - CONV_LEN + 1

    def body(c, carry):
        r0 = pl.multiple_of(c * rows, rows)
        win = pad_ref[pl.ds(r0, rows + halo), :]
        acc = jnp.zeros((rows, CONV_WIDTH), F32)
        for sub in range(8):
            shifted = win if sub == 0 else pltpu.roll(win, rows + halo - sub, 0)
            for j in range(CONV_LEN):
                if (first + j) % 8 == sub:
                    a0 = first + j - sub
                    acc = acc + w_ref[j:j + 1, :] * shifted[a0:a0 + rows]
        y = _layer_norm(acc + cb_ref[...], g_ref[...], b_ref[...])
        o_ref[0, pl.ds(r0, rows), :] = (y * _sigmoid(y)).astype(BF16)
        return carry

    lax.fori_loop(0, seq_len // rows, body, 0)


def conformer_conv(u, conv_w, conv_b, ln_g, ln_b, rows=CONV_ROWS):
    bsz, seq_len, _ = u.shape
    vec = pl.BlockSpec((1, CONV_WIDTH), lambda b: (0, 0))
    return pl.pallas_call(
        functools.partial(_conv_kernel, rows=rows),
        out_shape=jax.ShapeDtypeStruct((bsz, seq_len, CONV_WIDTH), BF16),
        grid=(bsz,),
        in_specs=[pl.BlockSpec((1, seq_len, CONV_WIDTH), lambda b: (b, 0, 0)),
                  pl.BlockSpec((CONV_LEN, CONV_WIDTH), lambda b: (0, 0)), vec, vec, vec],
        out_specs=pl.BlockSpec((1, seq_len, CONV_WIDTH), lambda b: (b, 0, 0)),
        scratch_shapes=[pltpu.VMEM((seq_len + CONV_HALO, CONV_WIDTH), F32)],
        compiler_params=_cparams("parallel"),
        name="conformer_conv",
    )(u, conv_w, conv_b.reshape(1, -1), ln_g.reshape(1, -1), ln_b.reshape(1, -1))


def _s5_operators(a_re, a_im, log_dt, b_re, b_im, c_re, c_im, n_chunks):
    tc = S5_CHUNK
    dt = jnp.exp(log_dt.astype(F32))[:, None]
    lr = a_re.astype(F32) * dt
    li = a_im.astype(F32) * dt
    mag = jnp.exp(lr)
    ab_re = mag * jnp.cos(li)
    ab_im = mag * jnp.sin(li)
    den = a_re * a_re + a_im * a_im
    coef_re = ((ab_re - 1.0) * a_re + ab_im * a_im) / den
    coef_im = (ab_im * a_re - (ab_re - 1.0) * a_im) / den
    bb_re = coef_re[..., None] * b_re - coef_im[..., None] * b_im
    bb_im = coef_re[..., None] * b_im + coef_im[..., None] * b_re

    def power(tau):
        tau = jnp.asarray(tau, F32)
        m = jnp.exp(lr[..., None] * tau)
        return m * jnp.cos(li[..., None] * tau), m * jnp.sin(li[..., None] * tau)

    lag_re, lag_im = power(jnp.arange(tc + 1))
    ca_re = c_re[..., None] * lag_re[:, None] - c_im[..., None] * lag_im[:, None]
    ca_im = c_re[..., None] * lag_im[:, None] + c_im[..., None] * lag_re[:, None]
    kern = (jnp.einsum('gcnt,gnd->gtcd', ca_re[..., :tc], bb_re, precision=HIGHEST)
            - jnp.einsum('gcnt,gnd->gtcd', ca_im[..., :tc], bb_im, precision=HIGHEST))
    s_idx = np.arange(tc)[:, None]
    i_idx = np.arange(tc)[None, :]
    lag = np.clip(i_idx - s_idx, 0, tc - 1)
    toep = kern[:, lag]
    toep = jnp.where(jnp.asarray(i_idx >= s_idx)[None, :, :, None, None], toep, 0.0)
    rev_re, rev_im = lag_re[..., tc - 1::-1], lag_im[..., tc - 1::-1]
    bop_re = (rev_re[..., None] * bb_re[:, :, None] - rev_im[..., None] * bb_im[:, :, None])
    bop_im = (rev_re[..., None] * bb_im[:, :, None] + rev_im[..., None] * bb_re[:, :, None])
    bop_re = bop_re.transpose(0, 2, 3, 1)
    bop_im = bop_im.transpose(0, 2, 3, 1)
    cop_re = ca_re[..., 1:].transpose(0, 2, 3, 1)
    cop_im = -ca_im[..., 1:].transpose(0, 2, 3, 1)

    gh = S5_GROUPS // S5_HALVES
    eye = jnp.eye(gh, dtype=F32)
    hw = tc * gh * S5_GROUP_CH
    sw = gh * S5_STATE
    split = lambda x: x.reshape((S5_HALVES, gh) + x.shape[1:])
    m_nat = jnp.einsum('hgsiod,gk->hsgdiko', split(toep), eye).reshape(S5_HALVES, hw, hw)
    b_nat = jnp.concatenate(
        [jnp.einsum('hgsdn,gk->hsgdkn', split(bop), eye).reshape(S5_HALVES, hw, sw) for bop in (bop_re, bop_im)],
        axis=2)
    c_nat = jnp.concatenate(
        [jnp.einsum('hgnio,gk->hgniko', split(cop), eye).reshape(S5_HALVES, sw, hw) for cop in (cop_re, cop_im)],
        axis=1)
    levels = max(1, int(math.log2(n_chunks)))
    lv_re, lv_im = power(tc * (2.0 ** jnp.arange(levels)))
    lanes = lambda x: split(x).transpose(0, 3, 1, 2).reshape(S5_HALVES, levels, sw)
    a_lv = jnp.concatenate([lanes(lv_re), lanes(lv_im)], axis=-1)
    return m_nat.astype(BF16), b_nat.astype(BF16), c_nat.astype(BF16), a_lv


def _s5_kernel(u0_ref, u1_ref, m_ref, b_ref, c_ref, a_ref, d_ref, gw_ref, gb_ref, o0_ref, o1_ref,
               sre_ref, sim_ref):
    tc = S5_CHUNK
    nc = u0_ref.shape[1] // tc
    hl = S5_WIDTH // S5_HALVES
    sw = sre_ref.shape[1]
    u_step = [[u_ref[0, pl.ds(s, nc, stride=tc), :] for s in range(tc)] for u_ref in (u0_ref, u1_ref)]
    y_half = []
    for h in range(S5_HALVES):
        uh = jnp.concatenate(u_step[h], axis=1).astype(BF16)
        v = jnp.dot(uh, b_ref[h], preferred_element_type=F32)
        zeros = jnp.zeros((nc, sw), F32)
        sre_ref[0:nc, :] = zeros
        sim_ref[0:nc, :] = zeros
        sre_ref[nc:2 * nc, :] = v[:, 0:sw]
        sim_ref[nc:2 * nc, :] = v[:, sw:2 * sw]
        for lv in range(a_ref.shape[1]):
            d = 1 << lv
            ar = a_ref[h, lv:lv + 1, 0:sw]
            ai = a_ref[h, lv:lv + 1, sw:2 * sw]
            pr = sre_ref[nc - d:2 * nc - d, :]
            pi = sim_ref[nc - d:2 * nc - d, :]
            cr = sre_ref[nc:2 * nc, :]
            ci = sim_ref[nc:2 * nc, :]
            sre_ref[nc:2 * nc, :] = cr + ar * pr - ai * pi
            sim_ref[nc:2 * nc, :] = ci + ar * pi + ai * pr
        prev_re = sre_ref[nc - 1:2 * nc - 1, :].astype(BF16)
        prev_im = sim_ref[nc - 1:2 * nc - 1, :].astype(BF16)
        y_half.append(jnp.dot(uh, m_ref[h], preferred_element_type=F32)
                      + jnp.dot(prev_re, c_ref[h, 0:sw, :], preferred_element_type=F32)
                      + jnp.dot(prev_im, c_ref[h, sw:2 * sw, :], preferred_element_type=F32))
    for i in range(tc):
        y = jnp.concatenate([yh[:, i * hl:(i + 1) * hl] for yh in y_half], axis=1)
        z = _gelu_tanh(y + d_ref[...] * jnp.concatenate([u_step[0][i], u_step[1][i]], axis=1))
        gate = jnp.dot(z.astype(BF16), gw_ref[...], preferred_element_type=F32) + gb_ref[...]
        out = z * _sigmoid(gate)
        o0_ref[0, pl.ds(i, nc, stride=tc), :] = out[:, 0:hl]
        o1_ref[0, pl.ds(i, nc, stride=tc), :] = out[:, hl:2 * hl]


def s5_layer(zs, operators, d_skip, glu_w, glu_b):
    m_nat, b_nat, c_nat, a_lv = operators
    bsz, seq_len, _ = zs.shape
    tc = S5_CHUNK
    nc = seq_len // tc
    assert nc & (nc - 1) == 0 and a_lv.shape[1] == int(math.log2(nc))
    hl = S5_WIDTH // S5_HALVES
    sw = a_lv.shape[2] // 2
    full = lambda a: pl.BlockSpec(a.shape, lambda b: (0,) * a.ndim)
    vec = pl.BlockSpec((1, S5_WIDTH), lambda b: (0, 0))
    half = lambda h: pl.BlockSpec((1, seq_len, hl), lambda b: (b, 0, h))
    glu_wb = glu_w.astype(BF16)
    out = pl.pallas_call(
        _s5_kernel,
        out_shape=(jax.ShapeDtypeStruct((bsz, seq_len, hl), F32),) * S5_HALVES,
        grid=(bsz,),
        in_specs=[half(0), half(1), full(m_nat), full(b_nat), full(c_nat), full(a_lv), vec, full(glu_wb), vec],
        out_specs=(half(0),) * S5_HALVES,
        scratch_shapes=[pltpu.VMEM((2 * nc, sw), F32), pltpu.VMEM((2 * nc, sw), F32)],
        compiler_params=_cparams("parallel"),
        name="s5_layer",
    )(zs, zs, m_nat, b_nat, c_nat, a_lv, d_skip.reshape(1, -1), glu_wb, glu_b.reshape(1, -1))
    return tuple(o.reshape(bsz * seq_len, hl) for o in out)


def _route_top2(logits):
    lane = lax.broadcasted_iota(jnp.int32, logits.shape, 1)
    lane_f = lane.astype(F32)
    logits = jnp.where(lane < N_EXPERTS, logits, -jnp.inf)
    v1 = jnp.max(logits, axis=-1, keepdims=True)
    i1 = jnp.min(jnp.where(logits == v1, lane_f, 1e9), axis=-1, keepdims=True)
    rest = jnp.where(lane_f == i1, -jnp.inf, logits)
    v2 = jnp.max(rest, axis=-1, keepdims=True)
    i2 = jnp.min(jnp.where(rest == v2, lane_f, 1e9), axis=-1, keepdims=True)
    e2 = jnp.exp(v2 - v1)
    den = 1.0 + e2
    out = jnp.where(lane == 0, i1, jnp.where(lane == 1, i2, jnp.where(lane == 2, 1.0 / den, e2 / den)))
    return jnp.where(lane < 4, out, 0.0)


def _out_proj_kernel(x_ref, a_ref, c_ref, s0_ref, s1_ref, w_ref, g_ref, b_ref, *rest, alpha):
    o_ref = rest[0] if len(rest) == 1 else rest[1]
    o_s5 = jnp.concatenate([s0_ref[...], s1_ref[...]], axis=1).astype(BF16)
    h = jnp.dot(a_ref[...], w_ref[0:NSA_WIDTH, :], preferred_element_type=F32)
    h = h + jnp.dot(c_ref[...], w_ref[NSA_WIDTH:NSA_WIDTH + CONV_WIDTH, :], preferred_element_type=F32)
    h = h + jnp.dot(o_s5, w_ref[NSA_WIDTH + CONV_WIDTH:, :], preferred_element_type=F32)
    y = _layer_norm(alpha * x_ref[...] + h, g_ref[...], b_ref[...])
    o_ref[...] = y
    if len(rest) > 1:
        router_ref, _, routed_ref, ybf_ref = rest
        ybf_ref[...] = y.astype(BF16)
        y_top = _top_bits(y)
        y_hi = y_top.astype(BF16)
        y_lo = (y - y_top).astype(BF16)
        rows = y.shape[0]
        prod = jnp.dot(jnp.concatenate([y_hi, y_lo], axis=0), router_ref[...],
                       preferred_element_type=F32)
        logits = (prod[0:rows, 0:LANES] + prod[0:rows, LANES:2 * LANES]
                  + prod[rows:2 * rows, 0:LANES] + prod[rows:2 * rows, LANES:2 * LANES])
        routed_ref[...] = _route_top2(logits)


def out_proj_ln(x2, o_nsa, o_conv, o_s5, w_out, g, b, alpha, router=None, tm=ROW_TILE):
    t, d = x2.shape
    row = lambda width: pl.BlockSpec((tm, width), lambda i: (i, 0))
    vec = pl.BlockSpec((1, d), lambda i: (0, 0))
    in_specs = [row(d), row(NSA_WIDTH), row(CONV_WIDTH), row(S5_WIDTH // S5_HALVES), row(S5_WIDTH // S5_HALVES),
                pl.BlockSpec(w_out.shape, lambda i: (0, 0), pipeline_mode=pl.Buffered(1)), vec, vec]
    args = [x2, o_nsa, o_conv, *o_s5, w_out.astype(BF16), g.reshape(1, -1), b.reshape(1, -1)]
    out_shape = jax.ShapeDtypeStruct((t, d), F32)
    out_specs = row(d)
    if router is not None:
        w = jnp.pad(router, ((0, 0), (0, LANES - router.shape[1])))
        w_top = lax.bitcast_convert_type(
            lax.bitcast_convert_type(w, jnp.uint32) & jnp.uint32(0xFFFF0000), F32)
        w_hi = w_top.astype(BF16)
        w_lo = (w - w_top).astype(BF16)
        in_specs.append(pl.BlockSpec((d, 2 * LANES), lambda i: (0, 0)))
        args.append(jnp.concatenate([w_hi, w_lo], axis=1))
        out_shape = (out_shape, jax.ShapeDtypeStruct((t, LANES), F32), jax.ShapeDtypeStruct((t, d), BF16))
        out_specs = (out_specs, row(LANES), row(d))
    return pl.pallas_call(
        functools.partial(_out_proj_kernel, alpha=alpha),
        out_shape=out_shape,
        grid=(t // tm,),
        in_specs=in_specs,
        out_specs=out_specs,
        compiler_params=_cparams("parallel"),
        name="out_proj_ln",
    )(*args)


def _ffn_kernel(te_ref, nu_ref, x_ref, w1_ref, w3_ref, w2_ref, *rest, alpha):
    ln_refs, (o_ref, acc_ref) = rest[:-2], rest[-2:]
    i = pl.program_id(0)
    f = pl.program_id(1)
    last = pl.num_programs(1) - 1
    used = i < nu_ref[0]

    @pl.when(used)
    def _():
        xb = x_ref[...].astype(BF16)
        h1 = jnp.dot(xb, w1_ref[0].astype(BF16), preferred_element_type=F32)
        h3 = jnp.dot(xb, w3_ref[0].astype(BF16), preferred_element_type=F32)
        h = (h1 * _sigmoid(h1) * h3).astype(BF16)
        part = jnp.dot(h, w2_ref[0].astype(BF16), preferred_element_type=F32)

        @pl.when(f == 0)
        def _():
            acc_ref[...] = part

        @pl.when(f > 0)
        def _():
            acc_ref[...] = acc_ref[...] + part

        @pl.when(f == last)
        def _():
            if ln_refs:
                o_ref[...] = _layer_norm(alpha * x_ref[...] + acc_ref[...], ln_refs[0][...], ln_refs[1][...])
            else:
                o_ref[...] = acc_ref[...].astype(o_ref.dtype)

    @pl.when(jnp.logical_not(used) & (f == last))
    def _():
        o_ref[...] = jnp.zeros(o_ref.shape, o_ref.dtype)


def grouped_swiglu(xs, tile_expert, n_used, w1, w3, w2, tm, tf, out_dtype, ln=None, alpha=1.0):
    p, d = xs.shape
    ff = w1.shape[2]
    assert p % tm == 0 and ff % tf == 0
    vec = pl.BlockSpec((1, d), lambda i, f, te, nu: (0, 0))
    ln_args = () if ln is None else (ln[0].reshape(1, d), ln[1].reshape(1, d))
    grid_spec = pltpu.PrefetchScalarGridSpec(
        num_scalar_prefetch=2,
        grid=(p // tm, ff // tf),
        in_specs=[pl.BlockSpec((tm, d), lambda i, f, te, nu: (i, 0)),
                  pl.BlockSpec((1, d, tf), lambda i, f, te, nu: (te[i], 0, f)),
                  pl.BlockSpec((1, d, tf), lambda i, f, te, nu: (te[i], 0, f)),
                  pl.BlockSpec((1, tf, d), lambda i, f, te, nu: (te[i], f, 0))] + [vec] * len(ln_args),
        out_specs=pl.BlockSpec((tm, d), lambda i, f, te, nu: (i, 0)),
        scratch_shapes=[pltpu.VMEM((tm, d), F32)],
    )
    return pl.pallas_call(
        functools.partial(_ffn_kernel, alpha=alpha),
        out_shape=jax.ShapeDtypeStruct((p, d), out_dtype),
        grid_spec=grid_spec,
        compiler_params=_cparams("arbitrary", "arbitrary"),
        name="grouped_swiglu",
    )(tile_expert, n_used, xs, w1, w3, w2, *ln_args)


def _pick_tf(ff, target=1024):
    best = LANES
    for cand in range(LANES, ff + 1, LANES):
        if ff % cand == 0 and cand <= target:
            best = cand
    return best


def _residual_ln_kernel(x_ref, a_ref, b_ref, wa_ref, wb_ref, g_ref, beta_ref, o_ref, *, alpha):
    f = wa_ref[...] * a_ref[...] + wb_ref[...] * b_ref[...]
    o_ref[...] = _layer_norm(alpha * x_ref[...] + f, g_ref[...], beta_ref[...])


def residual_ln(x2, ab, wa, wb, g, beta, alpha, tm=ROW_TILE):
    t, d = x2.shape
    row = pl.BlockSpec((tm, d), lambda i: (i, 0))
    row_b = pl.BlockSpec((tm, d), lambda i: (i + t // tm, 0))
    col = pl.BlockSpec((tm, 1), lambda i: (i, 0))
    vec = pl.BlockSpec((1, d), lambda i: (0, 0))
    return pl.pallas_call(
        functools.partial(_residual_ln_kernel, alpha=alpha),
        out_shape=jax.ShapeDtypeStruct((t, d), F32),
        grid=(t // tm,),
        in_specs=[row, row, row_b, col, col, vec, vec],
        out_specs=row,
        compiler_params=_cparams("parallel"),
        name="residual_ln",
    )(x2, ab, ab, wa, wb, g.reshape(1, -1), beta.reshape(1, -1))


def moe_swiglu_ln(x2, x2_bf16, routed, w1, w3, w2, first_expert, g, beta, alpha, tm):
    t, d = x2.shape
    n_exp = N_EXPERTS
    idx = routed[:, 0:2].astype(jnp.int32)
    wts = routed[:, 2:4]
    e_flat = idx.T.reshape(-1)
    onehot = (e_flat[:, None] == jnp.arange(n_exp, dtype=jnp.int32)[None, :]).astype(jnp.int32)
    running = jnp.cumsum(onehot, axis=0)
    counts = running[-1]
    rank = jnp.sum(onehot * running, axis=1) - 1
    padded = ((counts + tm - 1) // tm) * tm
    pad_end = jnp.cumsum(padded)
    pad_start = pad_end - padded
    start = jnp.cumsum(counts) - counts
    pos = pad_start[e_flat] + rank
    n_rows = 2 * t + n_exp * tm
    n_tiles = n_rows // tm
    tile_start = jnp.arange(n_tiles, dtype=jnp.int32) * tm
    tile_expert = jnp.minimum(jnp.sum(tile_start[:, None] >= pad_end[None, :], axis=1), n_exp - 1).astype(jnp.int32)
    n_used = (pad_end[-1] // tm).astype(jnp.int32).reshape(1)
    order = jnp.argsort(e_flat, stable=True)
    row = jnp.arange(n_rows, dtype=jnp.int32)
    row_expert = jnp.repeat(tile_expert, tm)
    slot = jnp.minimum(start[row_expert] + (row - pad_start[row_expert]), 2 * t - 1)
    src = order[slot] % t
    xs = x2_bf16[src]
    ys = grouped_swiglu(xs, tile_expert + first_expert, n_used, w1, w3, w2, tm, _pick_tf(w1.shape[2]), BF16)
    return residual_ln(x2, ys[pos], wts[:, 0:1], wts[:, 1:2], g, beta, alpha)


def dense_swiglu_ln(x2, w1, w3, w2, layer, g, beta, alpha, tm):
    t = x2.shape[0]
    tile_expert = jnp.full((t // tm,), layer, jnp.int32)
    n_used = jnp.full((1,), t // tm, jnp.int32)
    return grouped_swiglu(x2, tile_expert, n_used, w1, w3, w2, tm, _pick_tf(w1.shape[2], 1408),
                          F32, ln=(g, beta), alpha=alpha)


def _ple_kernel(x_ref, p_ref, wg_ref, bg_ref, wp_ref, g_ref, beta_ref, o_ref, *, alpha):
    x = x_ref[...]
    gate = _sigmoid(jnp.dot(x.astype(BF16), wg_ref[...], preferred_element_type=F32) + bg_ref[...])
    e = jnp.dot(p_ref[...].astype(BF16), wp_ref[...], preferred_element_type=F32) * gate
    o_ref[...] = _layer_norm(alpha * x + e, g_ref[...], beta_ref[...])


def ple_ln(x2, p_all, layer, gate_w, gate_b, proj, g, beta, alpha, tm=ROW_TILE):
    t, d = x2.shape
    pd = p_all.shape[1]
    first = layer * (t // tm)
    vec = pl.BlockSpec((1, d), lambda i: (0, 0))
    return pl.pallas_call(
        functools.partial(_ple_kernel, alpha=alpha),
        out_shape=jax.ShapeDtypeStruct((t, d), F32),
        grid=(t // tm,),
        in_specs=[pl.BlockSpec((tm, d), lambda i: (i, 0)), pl.BlockSpec((tm, pd), lambda i: (first + i, 0)),
                  pl.BlockSpec((d, d), lambda i: (0, 0)), vec,
                  pl.BlockSpec((pd, d), lambda i: (0, 0)), vec, vec],
        out_specs=pl.BlockSpec((tm, d), lambda i: (i, 0)),
        compiler_params=_cparams("parallel"),
        name="ple_ln",
    )(x2, p_all, gate_w.astype(BF16), gate_b.reshape(1, -1), proj.astype(BF16), g.reshape(1, -1), beta.reshape(1, -1))


def hybrid_mixer_ln(x2, bsz, seq_len, tables, w_in, w_out, cmp_weights, conv_params, s5_params,
                    g, beta, alpha, router=None):
    (q, ksel, kwin, vsel, vwin, kv_cmp, gates, u_conv, zs) = in_proj(x2, _widen_w_in(w_in), tables, seq_len)
    b3 = lambda a: a.reshape(bsz, seq_len, a.shape[1])
    kvc = compress(kv_cmp.reshape(2, bsz, seq_len, LANES), _compress_weights(*cmp_weights))
    o_nsa = nsa_attention(b3(q), b3(ksel), b3(vsel), b3(kwin), b3(vwin), kvc, b3(gates))
    o_conv = conformer_conv(b3(u_conv), *conv_params)
    (a_re, a_im, log_dt, b_re, b_im, c_re, c_im, d_skip, glu_w, glu_b) = s5_params
    ops = _s5_operators(a_re, a_im, log_dt, b_re, b_im, c_re, c_im, seq_len // S5_CHUNK)
    o_s5 = s5_layer(b3(zs), ops, d_skip, glu_w, glu_b)
    return out_proj_ln(x2, o_nsa.reshape(-1, NSA_WIDTH), o_conv.reshape(-1, CONV_WIDTH), o_s5, w_out,
                       g, beta, alpha, router)


def kernel(x, p, positions, w_in, w_out, cmp_pe, cmp_w1, cmp_b1, cmp_w2, cmp_b2, conv_w, conv_b, conv_ln_g, conv_ln_b, s5_a_re, s5_a_im, s5_log_dt, s5_b_re, s5_b_im, s5_c_re, s5_c_im, s5_d, s5_glu_w, s5_glu_b, ffn_w1, ffn_w3, ffn_w2, moe_router, moe_w1, moe_w3, moe_w2, ple_gate_w, ple_gate_b, ple_proj, ln_g, ln_b):
    bsz, seq_len, d_model = x.shape
    depth = w_in.shape[0]
    alpha = (2 * depth) ** 0.25
    t = bsz * seq_len
    tables = rope_tables(positions)
    x2 = x.reshape(t, d_model)
    dense_tm = ROW_TILE_SMALL
    moe_tm = ROW_TILE if t % ROW_TILE == 0 else ROW_TILE_SMALL
    stack = lambda w: w.reshape((-1,) + w.shape[-2:])
    ffn_w = tuple(stack(w).astype(BF16) for w in (ffn_w1, ffn_w3, ffn_w2))
    moe_w = tuple(stack(w) for w in (moe_w1, moe_w3, moe_w2))
    for i in range(depth):
        j = i // 2
        routed_layer = i % 2 == 1
        mixed = hybrid_mixer_ln(
            x2, bsz, seq_len, tables, w_in[i], w_out[i],
            (cmp_pe[i], cmp_w1[i], cmp_b1[i], cmp_w2[i], cmp_b2[i]),
            (conv_w[i], conv_b[i], conv_ln_g[i], conv_ln_b[i]),
            (s5_a_re[i], s5_a_im[i], s5_log_dt[i], s5_b_re[i], s5_b_im[i], s5_c_re[i], s5_c_im[i],
             s5_d[i], s5_glu_w[i], s5_glu_b[i]),
            ln_g[i, 0], ln_b[i, 0], alpha, moe_router[j] if routed_layer else None)
        if routed_layer:
            x2, routed, x2_bf16 = mixed
            x2 = moe_swiglu_ln(x2, x2_bf16, routed, *moe_w, j * N_EXPERTS, ln_g[i, 1], ln_b[i, 1], alpha, moe_tm)
        else:
            x2 = dense_swiglu_ln(mixed, *ffn_w, j, ln_g[i, 1], ln_b[i, 1], alpha, dense_tm)
        x2 = ple_ln(x2, p.reshape(depth * t, -1), i, ple_gate_w[i], ple_gate_b[i], ple_proj[i],
                    ln_g[i, 2], ln_b[i, 2], alpha)
    return x2.reshape(bsz, seq_len, d_model)
```
